```python
import jax
import jax.numpy as jnp
from jax import lax
import numpy as np

D_MODEL = 1024
BATCH = 16
SEQ = 2048
DEPTH = 2

N_MIXERS = 2
EPS = 1e-6
NEG = -1e30
BIG = 1e9
ROPE_THETA = 500000.0

NSA_HEADS = 16
NSA_KV_GROUPS = 2
NSA_HEADS_PER_GROUP = NSA_HEADS // NSA_KV_GROUPS
HEAD_DIM = D_MODEL // NSA_HEADS
ROT_DIM = HEAD_DIM // 4
CMP_BLOCK = 32
CMP_STRIDE = 16
CMP_HIDDEN = 256
SEL_BLOCK = 64
SEL_TOPN = 8
WINDOW = 512
NSA_Q_BLOCK = 64
NSA_BRANCHES = 3
NSA_Q_COLS = NSA_HEADS * HEAD_DIM
NSA_KV_COLS = NSA_KV_GROUPS * HEAD_DIM
NSA_GATE_COLS = NSA_BRANCHES * NSA_HEADS
NSA_IN_COLS = NSA_Q_COLS + 2 * NSA_BRANCHES * NSA_KV_COLS + NSA_GATE_COLS

MLSTM_HEADS = 4
MLSTM_V_DIM = D_MODEL // MLSTM_HEADS
MLSTM_QK_DIM = MLSTM_V_DIM // 2
MLSTM_CHUNK = 64
GATE_SOFTCAP = 15.0
MLSTM_IN_COLS = 2 * MLSTM_HEADS * MLSTM_QK_DIM + 2 * MLSTM_HEADS * MLSTM_V_DIM + 2 * MLSTM_HEADS

N_EXPERTS = 32
N_GROUPS = 4
EXPERTS_PER_GROUP = N_EXPERTS // N_GROUPS
TOP_K = 2
D_EXPERT = 512
MOE_BLOCK = 128

kernel_name = 'hybrid_nsa_mlstm_grouped_moe_adaln'


def rms_norm(x, gain):
    xf = x.astype(jnp.float32)
    y = xf * lax.rsqrt(jnp.mean(xf * xf, axis=-1, keepdims=True) + EPS)
    return (y * gain.astype(jnp.float32)).astype(x.dtype)


def softcap(a):
    return GATE_SOFTCAP * jnp.tanh(a / GATE_SOFTCAP)


def partial_rope(x, pos):
    half = ROT_DIM // 2
    inv_freq = ROPE_THETA ** (-jnp.arange(half, dtype=jnp.float32) / half)
    ang = pos.astype(jnp.float32)[:, None] * inv_freq[None, :]
    cos = jnp.cos(ang)[:, None, :]
    sin = jnp.sin(ang)[:, None, :]
    xr = x[..., :ROT_DIM].astype(jnp.float32)
    x1, x2 = xr[..., :half], xr[..., half:]
    rot = jnp.concatenate([x1 * cos - x2 * sin, x2 * cos + x1 * sin], axis=-1).astype(x.dtype)
    return jnp.concatenate([rot, x[..., ROT_DIM:]], axis=-1)


def split_cols(a, sizes):
    return jnp.split(a, np.cumsum(sizes)[:-1].tolist(), axis=-1)


def to_group_major(a):
    return jnp.transpose(a, (0, 2, 1, 3))


def nsa_compress(a, pe, w1, b1, w2, b2):
    B, S, G, dh = a.shape
    nc = (S - CMP_BLOCK) // CMP_STRIDE + 1
    idx = np.arange(nc)[:, None] * CMP_STRIDE + np.arange(CMP_BLOCK)[None, :]
    blocks = a[:, idx] + pe[None, None, :, None, :]
    blocks = blocks.transpose(0, 1, 3, 2, 4).reshape(B, nc, G, CMP_BLOCK * dh)
    return jax.nn.gelu(blocks @ w1 + b1) @ w2 + b2


def nsa_mixer(h, w_in, w_out, q_gain, k_gain, cmp_pe, cmp_w1, cmp_b1, cmp_w2, cmp_b2):
    B, S, _ = h.shape
    G, R, dh, QB = NSA_KV_GROUPS, NSA_HEADS_PER_GROUP, HEAD_DIM, NSA_Q_BLOCK
    f32 = jnp.float32
    pos = jnp.arange(S, dtype=jnp.int32)
    q, kc, vc, ks, vs, kw, vw, gates = split_cols(h @ w_in, [NSA_Q_COLS] + [NSA_KV_COLS] * 6 + [NSA_GATE_COLS])
    q = partial_rope(rms_norm(q.reshape(B, S, NSA_HEADS, dh), q_gain), pos)
    ks = partial_rope(rms_norm(ks.reshape(B, S, G, dh), k_gain[1]), pos)
    kw = partial_rope(rms_norm(kw.reshape(B, S, G, dh), k_gain[2]), pos)
    kc = nsa_compress(kc.reshape(B, S, G, dh), cmp_pe[0], cmp_w1[0], cmp_b1[0], cmp_w2[0], cmp_b2[0])
    vc = nsa_compress(vc.reshape(B, S, G, dh), cmp_pe[1], cmp_w1[1], cmp_b1[1], cmp_w2[1], cmp_b2[1])
    nc = kc.shape[1]
    cmp_pos = jnp.arange(nc, dtype=jnp.int32) * CMP_STRIDE + (CMP_BLOCK - 1)
    kc = to_group_major(partial_rope(rms_norm(kc, k_gain[0]), cmp_pos))
    vc = to_group_major(vc)
    ns = S // SEL_BLOCK
    ks_blk = to_group_major(ks).reshape(B, G, ns, SEL_BLOCK, dh)
    vs_blk = to_group_major(vs.reshape(B, S, G, dh)).reshape(B, G, ns, SEL_BLOCK, dh)
    pad = ((0, 0), (0, 0), (WINDOW, 0), (0, 0))
    kw_pad = jnp.pad(to_group_major(kw), pad)
    vw_pad = jnp.pad(to_group_major(vw.reshape(B, S, G, dh)), pad)
    r, u = SEL_BLOCK // CMP_STRIDE, CMP_BLOCK // CMP_STRIDE
    c_idx = (r * np.arange(ns)[:, None, None] + np.arange(r)[None, :, None] + np.arange(u)[None, None, :]).reshape(ns, -1)
    cmp_to_sel = jnp.asarray((c_idx[:, :, None] == np.arange(nc)[None, None, :]).sum(1).T.astype(np.float32))
    n_top = min(SEL_TOPN, ns)
    scale = HEAD_DIM ** -0.5
    blk = jnp.arange(ns, dtype=jnp.int32)
    bi = jnp.arange(B)[:, None, None, None]
    gi = jnp.arange(G)[None, :, None, None]

    def attend_block(args):
        qb, gb, s0 = args
        t = s0 + jnp.arange(QB, dtype=jnp.int32)
        sc = jnp.einsum('bgrqd,bgcd->bgrqc', qb, kc).astype(f32) * scale
        valid_c = cmp_pos[None, :] <= t[:, None]
        p_c = jnp.where(valid_c, jax.nn.softmax(jnp.where(valid_c, sc, NEG), axis=-1), 0.0)
        o_c = jnp.einsum('bgrqc,bgcd->bgrqd', p_c.astype(vc.dtype), vc)
        imp = jnp.einsum('bgrqc,cj->bgqj', p_c, cmp_to_sel)
        cur = t // SEL_BLOCK
        forced = (blk[None, :] == 0) | (blk[None, :] == cur[:, None]) | (blk[None, :] == cur[:, None] - 1)
        allowed = blk[None, :] <= cur[:, None]
        imp = jnp.where(allowed, jnp.where(forced, BIG, imp), -BIG)
        _, sel = lax.top_k(imp, n_top)
        k_g = ks_blk[bi, gi, sel].reshape(B, G, QB, n_top * SEL_BLOCK, dh)
        v_g = vs_blk[bi, gi, sel].reshape(B, G, QB, n_top * SEL_BLOCK, dh)
        kpos = (sel[..., None] * SEL_BLOCK + jnp.arange(SEL_BLOCK, dtype=jnp.int32)).reshape(B, G, QB, n_top * SEL_BLOCK)
        valid_s = kpos[:, :, None] <= t[:, None]
        ss = jnp.einsum('bgrqd,bgqkd->bgrqk', qb, k_g).astype(f32) * scale
        p_s = jax.nn.softmax(jnp.where(valid_s, ss, NEG), axis=-1)
        o_s = jnp.einsum('bgrqk,bgqkd->bgrqd', p_s.astype(v_g.dtype), v_g)
        kwb = lax.dynamic_slice_in_dim(kw_pad, s0, WINDOW + QB, axis=2)
        vwb = lax.dynamic_slice_in_dim(vw_pad, s0, WINDOW + QB, axis=2)
        wpos = s0 - WINDOW + jnp.arange(WINDOW + QB, dtype=jnp.int32)
        valid_w = (wpos[None, :] <= t[:, None]) & (wpos[None, :] > t[:, None] - WINDOW) & (wpos[None, :] >= 0)
        sw = jnp.einsum('bgrqd,bgkd->bgrqk', qb, kwb).astype(f32) * scale
        p_w = jax.nn.softmax(jnp.where(valid_w, sw, NEG), axis=-1)
        o_w = jnp.einsum('bgrqk,bgkd->bgrqd', p_w.astype(vwb.dtype), vwb)
        return gb[:, 0][..., None] * o_c + gb[:, 1][..., None] * o_s + gb[:, 2][..., None] * o_w

    nq = S // QB
    q_blocks = q.reshape(B, nq, QB, G, R, dh).transpose(1, 0, 3, 4, 2, 5)
    g_blocks = jax.nn.sigmoid(gates).reshape(B, nq, QB, NSA_BRANCHES, G, R).transpose(1, 0, 3, 4, 5, 2)
    starts = jnp.arange(nq, dtype=jnp.int32) * QB
    o = lax.map(attend_block, (q_blocks, g_blocks, starts))
    o = o.transpose(1, 0, 4, 2, 3, 5).reshape(B, S, NSA_Q_COLS)
    return o @ w_out


def mlstm_mixer(h, w_in, w_out, b_igate, b_fgate, norm_gain):
    B, S, _ = h.shape
    NH, dqk, dv, L = MLSTM_HEADS, MLSTM_QK_DIM, MLSTM_V_DIM, MLSTM_CHUNK
    f32 = jnp.float32
    q, k, v, o, ig, fg = split_cols(h @ w_in, [NH * dqk, NH * dqk, NH * dv, NH * dv, NH, NH])

    def to_heads(a, d):
        return a.reshape(B, S, NH, d).transpose(0, 2, 1, 3).astype(f32)

    q = to_heads(q, dqk)
    k = to_heads(k, dqk) * (dqk ** -0.5)
    v = to_heads(v, dv)
    li = softcap(ig.astype(f32) + b_igate.astype(f32)).transpose(0, 2, 1)
    lf = jax.nn.log_sigmoid(softcap(fg.astype(f32) + b_fgate.astype(f32))).transpose(0, 2, 1)
    nch = S // L

    def to_chunks(a):
        return jnp.moveaxis(a.reshape(B, NH, nch, L, *a.shape[3:]), 2, 0)

    causal = np.tril(np.ones((L, L), dtype=bool))

    def step(carry, xs):
        C, n, m = carry
        qc, kc, vc, lic, lfc = xs
        b = jnp.cumsum(lfc, axis=-1)
        D = jnp.where(causal, b[..., :, None] - b[..., None, :] + lic[..., None, :], NEG)
        m_inter = b + m[..., None]
        m_t = jnp.maximum(m_inter, D.max(-1))
        A = jnp.exp(D - m_t[..., None]) * jnp.einsum('bhqd,bhkd->bhqk', qc, kc)
        inter = jnp.exp(m_inter - m_t)
        num = jnp.einsum('bhqk,bhkv->bhqv', A, vc) + inter[..., None] * jnp.einsum('bhqd,bhdv->bhqv', qc, C)
        den = A.sum(-1) + inter * jnp.einsum('bhqd,bhd->bhq', qc, n)
        h_out = num / jnp.maximum(jnp.abs(den), jnp.exp(-m_t))[..., None]
        m_new = m_t[..., -1]
        w = jnp.exp(b[..., -1:] - b + lic - m_new[..., None])
        decay = jnp.exp(m_inter[..., -1] - m_new)
        C_new = decay[..., None, None] * C + jnp.einsum('bhs,bhsd,bhsv->bhdv', w, kc, vc)
        n_new = decay[..., None] * n + jnp.einsum('bhs,bhsd->bhd', w, kc)
        return (C_new, n_new, m_new), h_out

    init = (jnp.zeros((B, NH, dqk, dv), f32), jnp.zeros((B, NH, dqk), f32), jnp.zeros((B, NH), f32))
    _, hs = lax.scan(step, init, (to_chunks(q), to_chunks(k), to_chunks(v), to_chunks(li), to_chunks(lf)))
    hs = jnp.moveaxis(hs, 0, 2).reshape(B, NH, S, dv).transpose(0, 2, 1, 3)
    hs = rms_norm(hs, norm_gain).reshape(B, S, NH * dv).astype(h.dtype)
    return (jax.nn.sigmoid(o) * hs) @ w_out


def moe_ffn(h, router_w, router_b, w_gate, w_up, w_down):
    B, S, D = h.shape
    T = B * S
    f32 = jnp.float32
    xf = h.reshape(T, D)
    affinity = jax.nn.sigmoid(xf.astype(f32) @ router_w.astype(f32))
    choice = (affinity + router_b.astype(f32)).reshape(T, N_GROUPS, EXPERTS_PER_GROUP)
    group_score = lax.top_k(choice, TOP_K)[0].sum(-1)
    group = jnp.argmax(group_score, axis=-1).astype(jnp.int32)
    in_group = jnp.take_along_axis(choice, group[:, None, None], axis=1)[:, 0]
    _, local = lax.top_k(in_group, TOP_K)
    expert = group[:, None] * EXPERTS_PER_GROUP + local
    aff = jnp.take_along_axis(affinity, expert, axis=1)
    weight = (aff / aff.sum(-1, keepdims=True)).astype(h.dtype)
    A = T * TOP_K
    e_flat = expert.reshape(A)
    tok_flat = jnp.repeat(jnp.arange(T, dtype=jnp.int32), TOP_K)
    w_flat = weight.reshape(A)
    order = jnp.argsort(e_flat)
    e_s, tok_s, w_s = e_flat[order], tok_flat[order], w_flat[order]
    counts = jax.ops.segment_sum(jnp.ones((A,), jnp.int32), e_flat, num_segments=N_EXPERTS)
    starts = jnp.cumsum(counts) - counts
    padded = (counts + MOE_BLOCK - 1) // MOE_BLOCK * MOE_BLOCK
    p_ends = jnp.cumsum(padded)
    p_starts = p_ends - padded
    dest = p_starts[e_s] + (jnp.arange(A, dtype=jnp.int32) - starts[e_s])
    nb = -(-A // MOE_BLOCK) + N_EXPERTS
    P = nb * MOE_BLOCK
    buf_tok = jnp.full((P,), T, jnp.int32).at[dest].set(tok_s)
    buf_w = jnp.zeros((P,), h.dtype).at[dest].set(w_s)
    block_start = jnp.arange(nb, dtype=jnp.int32) * MOE_BLOCK
    block_expert = jnp.minimum(jnp.searchsorted(p_ends, block_start, side='right'), N_EXPERTS - 1).astype(jnp.int32)
    x_pad = jnp.concatenate([xf, jnp.zeros((1, D), h.dtype)], axis=0)

    def expert_block(args):
        e, toks, ws = args
        xb = x_pad[toks]
        hid = jax.nn.silu(xb @ w_gate[e]) * (xb @ w_up[e])
        return (hid @ w_down[e]) * ws[:, None]

    out = lax.map(expert_block, (block_expert, buf_tok.reshape(nb, MOE_BLOCK), buf_w.reshape(nb, MOE_BLOCK)))
    y = jnp.zeros((T + 1, D), h.dtype).at[buf_tok].add(out.reshape(P, D))
    return y[:T].reshape(B, S, D)


def setup_inputs(seed: int = 0) -> dict:
    key = jax.random.key(seed)
    ks = jax.random.split(key, 32)
    D = D_MODEL
    n_nsa = (DEPTH + 1) // 2
    n_ml = DEPTH // 2

    def nrm(k, shape, s):
        return jax.random.normal(k, shape, jnp.float32) * s

    return {
        'x': nrm(ks[0], (BATCH, SEQ, D), 1.0),
        'c': nrm(ks[1], (BATCH, D), 1.0),
        'ada_w': nrm(ks[2], (DEPTH, D, 6 * D), 0.5 * D ** -0.5),
        'ada_b': nrm(ks[3], (DEPTH, 6 * D), 0.02),
        'norm_mix_gain': 1.0 + nrm(ks[4], (DEPTH, D), 0.05),
        'norm_ffn_gain': 1.0 + nrm(ks[5], (DEPTH, D), 0.05),
        'nsa_w_in': nrm(ks[6], (n_nsa, D, NSA_IN_COLS), D ** -0.5),
        'nsa_w_out': nrm(ks[7], (n_nsa, NSA_Q_COLS, D), NSA_Q_COLS ** -0.5),
        'nsa_q_gain': 1.0 + nrm(ks[8], (n_nsa, HEAD_DIM), 0.05),
        'nsa_k_gain': 1.0 + nrm(ks[9], (n_nsa, NSA_BRANCHES, HEAD_DIM), 0.05),
        'nsa_cmp_pe': nrm(ks[10], (n_nsa, 2, CMP_BLOCK, HEAD_DIM), 0.1),
        'nsa_cmp_w1': nrm(ks[11], (n_nsa, 2, CMP_BLOCK * HEAD_DIM, CMP_HIDDEN), (CMP_BLOCK * HEAD_DIM) ** -0.5),
        'nsa_cmp_b1': nrm(ks[12], (n_nsa, 2, CMP_HIDDEN), 0.02),
        'nsa_cmp_w2': nrm(ks[13], (n_nsa, 2, CMP_HIDDEN, HEAD_DIM), CMP_HIDDEN ** -0.5),
        'nsa_cmp_b2': nrm(ks[14], (n_nsa, 2, HEAD_DIM), 0.02),
        'mlstm_w_in': nrm(ks[15], (n_ml, D, MLSTM_IN_COLS), D ** -0.5),
        'mlstm_b_igate': nrm(ks[16], (n_ml, MLSTM_HEADS), 0.1),
        'mlstm_b_fgate': 3.0 + nrm(ks[17], (n_ml, MLSTM_HEADS), 0.5),
        'mlstm_norm_gain': 1.0 + nrm(ks[18], (n_ml, MLSTM_HEADS, MLSTM_V_DIM), 0.05),
        'mlstm_w_out': nrm(ks[19], (n_ml, MLSTM_HEADS * MLSTM_V_DIM, D), (MLSTM_HEADS * MLSTM_V_DIM) ** -0.5),
        'router_w': nrm(ks[20], (D, N_EXPERTS), D ** -0.5),
        'router_b': nrm(ks[21], (N_EXPERTS,), 0.01),
        'moe_w_gate': nrm(ks[22], (DEPTH, N_EXPERTS, D, D_EXPERT), D ** -0.5),
        'moe_w_up': nrm(ks[23], (DEPTH, N_EXPERTS, D, D_EXPERT), D ** -0.5),
        'moe_w_down': nrm(ks[24], (DEPTH, N_EXPERTS, D_EXPERT, D), D_EXPERT ** -0.5),
    }


def reference(x, c, ada_w, ada_b, norm_mix_gain, norm_ffn_gain,
              nsa_w_in, nsa_w_out, nsa_q_gain, nsa_k_gain, nsa_cmp_pe, nsa_cmp_w1, nsa_cmp_b1, nsa_cmp_w2, nsa_cmp_b2,
              mlstm_w_in, mlstm_b_igate, mlstm_b_fgate, mlstm_norm_gain, mlstm_w_out,
              router_w, router_b, moe_w_gate, moe_w_up, moe_w_down):
    cond = jax.nn.silu(c)
    for i in range(DEPTH):
        mod = cond @ ada_w[i] + ada_b[i]
        sh_m, sc_m, g_m, sh_f, sc_f, g_f = jnp.split(mod[:, None, :], 6, axis=-1)
        hm = rms_norm(x, norm_mix_gain[i]) * (1.0 + sc_m) + sh_m
        j = i // N_MIXERS
        if i % N_MIXERS == 0:
            y = nsa_mixer(hm, nsa_w_in[j], nsa_w_out[j], nsa_q_gain[j], nsa_k_gain[j], nsa_cmp_pe[j],
                          nsa_cmp_w1[j], nsa_cmp_b1[j], nsa_cmp_w2[j], nsa_cmp_b2[j])
        else:
            y = mlstm_mixer(hm, mlstm_w_in[j], mlstm_w_out[j], mlstm_b_igate[j], mlstm_b_fgate[j], mlstm_norm_gain[j])
        x = x + g_m * y
        hf = rms_norm(x, norm_ffn_gain[i]) * (1.0 + sc_f) + sh_f
        x = x + g_f * moe_ffn(hf, router_w, router_b, moe_w_gate[i], moe_w_up[i], moe_w_down[i])
    return x
```

```python
import functools

import numpy as np
import jax
import jax.numpy as jnp
from jax import lax
from jax.experimental import pallas as pl
from jax.experimental.pallas import tpu as pltpu

F32 = jnp.float32
BF16 = jnp.bfloat16
HIGHEST = lax.Precision.HIGHEST

EPS = 1e-6
NEG = -1e30
BIG = 1e9
ROPE_THETA = 500000.0

NSA_HEADS = 16
NSA_KV_GROUPS = 2
NSA_HEADS_PER_GROUP = NSA_HEADS // NSA_KV_GROUPS
HEAD_DIM = 64
ROT_DIM = HEAD_DIM // 4
CMP_BLOCK = 32
CMP_STRIDE = 16
SEL_BLOCK = 64
SEL_TOPN = 8
WINDOW = 512
NSA_Q_BLOCK = 64
NSA_BRANCHES = 3

MLSTM_HEADS = 4
MLSTM_CHUNK = 64
GATE_SOFTCAP = 15.0

N_EXPERTS = 32
N_GROUPS = 4
EXPERTS_PER_GROUP = N_EXPERTS // N_GROUPS
TOP_K = 2
MOE_ROWS = 256

LANES = 128
VMEM_LIMIT = 48 * 1024 * 1024


def _cparams(n_axes):
    return pltpu.CompilerParams(dimension_semantics=("arbitrary",) * n_axes,
                                vmem_limit_bytes=VMEM_LIMIT)


def _dot(a, b):
    return jnp.dot(a, b, preferred_element_type=F32)


def _dot_nt(a, b):
    return lax.dot_general(a, b, (((1,), (1,)), ((), ())), preferred_element_type=F32)


def _dot_tn(a, b):
    return lax.dot_general(a, b, (((0,), (0,)), ((), ())), preferred_element_type=F32)


def _norm_mod(x, gain, sc, sh):
    y = x * lax.rsqrt(jnp.mean(x * x, axis=-1, keepdims=True) + EPS) * gain
    return y * (1.0 + sc) + sh


def _half_norm_rope(x, gain, cos, sin):
    lane = lax.broadcasted_iota(jnp.int32, x.shape, x.ndim - 1)
    left = lane < HEAD_DIM
    x2 = x * x
    ss_l = jnp.sum(jnp.where(left, x2, 0.0), axis=-1, keepdims=True)
    ss_r = jnp.sum(jnp.where(left, 0.0, x2), axis=-1, keepdims=True)
    ms = jnp.where(left, ss_l, ss_r) * (1.0 / HEAD_DIM)
    y = x * lax.rsqrt(ms + EPS) * gain
    half = ROT_DIM // 2
    nd = x.ndim - 1
    partner = jnp.where((lane % HEAD_DIM) < half,
                        pltpu.roll(y, LANES - half, nd), pltpu.roll(y, half, nd))
    return y * cos + partner * sin


def _rope_tables(pos):
    half = ROT_DIM // 2
    inv_freq = ROPE_THETA ** (-jnp.arange(half, dtype=F32) / half)
    ang = pos.astype(F32)[:, None] * inv_freq[None, :]
    cos, sin = jnp.cos(ang), jnp.sin(ang)
    n = pos.shape[0]
    one = jnp.ones((n, HEAD_DIM - ROT_DIM), F32)
    cos_h = jnp.concatenate([cos, cos, one], axis=-1)
    sin_h = jnp.concatenate([-sin, sin, 0.0 * one], axis=-1)
    return jnp.tile(cos_h, (1, 2)), jnp.tile(sin_h, (1, 2))


def _mod_kernel(c_ref, w_ref, b_ref, o_ref):
    c = c_ref[...]
    cond = c * jax.nn.sigmoid(c)
    o_ref[0] = jnp.dot(cond, w_ref[0], preferred_element_type=F32, precision=HIGHEST) + b_ref[0]


def _mod_call(c, ada_w, ada_b):
    depth, d, n = ada_w.shape
    b = c.shape[0]
    tn = n // 4
    return pl.pallas_call(
        _mod_kernel,
        out_shape=jax.ShapeDtypeStruct((depth, b, n), F32),
        grid=(depth, n // tn),
        in_specs=[pl.BlockSpec((b, d), lambda i, j: (0, 0)),
                  pl.BlockSpec((1, d, tn), lambda i, j: (i, 0, j)),
                  pl.BlockSpec((1, 1, tn), lambda i, j: (i, 0, j))],
        out_specs=pl.BlockSpec((1, b, tn), lambda i, j: (i, 0, j)),
        compiler_params=_cparams(2),
        name="adaln_mod",
    )(c, ada_w, ada_b.reshape(depth, 1, n))


def _nsa_in_kernel(x_ref, gain_ref, sc_ref, sh_ref, wq_ref, wkv_ref, wg_ref, kg_ref, cos_ref, sin_ref,
                   q_ref, kc_ref, vc_ref, ks_ref, vs_ref, kw_ref, vw_ref, g_ref):
    h = _norm_mod(x_ref[0], gain_ref[...], sc_ref[0], sh_ref[0]).astype(BF16)
    q_ref[0] = _dot(h, wq_ref[...])
    g_ref[0] = _dot(h, wg_ref[...])
    kv = _dot(h, wkv_ref[...])
    cos, sin = cos_ref[...], sin_ref[...]
    kc_ref[0] = kv[:, 0 * LANES:1 * LANES]
    vc_ref[0] = kv[:, 1 * LANES:2 * LANES]
    ks_ref[0] = _half_norm_rope(kv[:, 2 * LANES:3 * LANES], kg_ref[1:2, :], cos, sin).astype(BF16)
    vs_ref[0] = kv[:, 3 * LANES:4 * LANES].astype(BF16)
    kw_ref[0] = _half_norm_rope(kv[:, 4 * LANES:5 * LANES], kg_ref[2:3, :], cos, sin).astype(BF16)
    vw_ref[0] = kv[:, 5 * LANES:6 * LANES].astype(BF16)


def _nsa_in_call(x, gain, sc, sh, w_in, k_gain, cos, sin, tm):
    b, s, d = x.shape
    nq = NSA_HEADS * HEAD_DIM
    nkv = 6 * LANES
    wq = w_in[:, :nq].astype(BF16)
    wkv = w_in[:, nq:nq + nkv].astype(BF16)
    ng = NSA_BRANCHES * NSA_HEADS
    wg = jnp.pad(w_in[:, nq + nkv:], ((0, 0), (0, LANES - ng))).astype(BF16)
    kg = jnp.tile(k_gain, (1, 2))
    row = lambda i, j: (i, j, 0)
    per_b = lambda i, j: (i, 0, 0)
    const = lambda i, j: (0, 0)
    kv_out = lambda dt: jax.ShapeDtypeStruct((b, s, LANES), dt)
    return pl.pallas_call(
        _nsa_in_kernel,
        out_shape=(jax.ShapeDtypeStruct((b, s, nq), F32), kv_out(F32), kv_out(F32),
                   kv_out(BF16), kv_out(BF16), kv_out(BF16), kv_out(BF16), kv_out(F32)),
        grid=(b, s // tm),
        in_specs=[pl.BlockSpec((1, tm, d), row),
                  pl.BlockSpec((1, d), const),
                  pl.BlockSpec((1, 1, d), per_b),
                  pl.BlockSpec((1, 1, d), per_b),
                  pl.BlockSpec((d, nq), const),
                  pl.BlockSpec((d, nkv), const),
                  pl.BlockSpec((d, LANES), const),
                  pl.BlockSpec((3, LANES), const),
                  pl.BlockSpec((tm, LANES), lambda i, j: (j, 0)),
                  pl.BlockSpec((tm, LANES), lambda i, j: (j, 0))],
        out_specs=(pl.BlockSpec((1, tm, nq), row),) + (pl.BlockSpec((1, tm, LANES), row),) * 7,
        compiler_params=_cparams(2),
        name="nsa_in_proj",
    )(x, gain.reshape(1, d), sc, sh, wq, wkv, wg, kg, cos, sin)


def _compress_kernel(blk_ref, pe_ref, w1_ref, b1_ref, w2_ref, b2_ref, kg_ref, cos_ref, sin_ref, o_ref):
    is_key = pl.program_id(0) == 0
    blk = (blk_ref[0] + pe_ref[0]).astype(BF16)
    hid = _dot(blk, w1_ref[0]) + b1_ref[0]
    hid = 0.5 * hid * (1.0 + jnp.tanh(np.sqrt(2.0 / np.pi) * (hid + 0.044715 * hid * hid * hid)))
    out = _dot(hid.astype(BF16), w2_ref[0]) + b2_ref[0]
    normed = _half_norm_rope(out, kg_ref[...], cos_ref[...], sin_ref[...])
    o_ref[0] = jnp.where(is_key, normed, out)[:, :HEAD_DIM].astype(o_ref.dtype)


def _compress_call(blocks, pe, w1, b1, w2, b2, k_gain0, cos, sin, rows):
    _, r, kdim = blocks.shape
    hid = w1.shape[-1]
    w2p = jnp.pad(w2, ((0, 0), (0, 0), (0, LANES - HEAD_DIM))).astype(BF16)
    b2p = jnp.pad(b2, ((0, 0), (0, LANES - HEAD_DIM))).reshape(2, 1, LANES)
    kg = jnp.pad(k_gain0, (0, LANES - HEAD_DIM)).reshape(1, LANES)
    sel = lambda i, j: (i, 0, 0)
    const = lambda i, j: (0, 0)
    return pl.pallas_call(
        _compress_kernel,
        out_shape=jax.ShapeDtypeStruct((2, r, HEAD_DIM), BF16),
        grid=(2, r // rows),
        in_specs=[pl.BlockSpec((1, rows, kdim), lambda i, j: (i, j, 0)),
                  pl.BlockSpec((1, 1, kdim), sel),
                  pl.BlockSpec((1, kdim, hid), sel),
                  pl.BlockSpec((1, 1, hid), sel),
                  pl.BlockSpec((1, hid, LANES), sel),
                  pl.BlockSpec((1, 1, LANES), sel),
                  pl.BlockSpec((1, LANES), const),
                  pl.BlockSpec((rows, LANES), const),
                  pl.BlockSpec((rows, LANES), const)],
        out_specs=pl.BlockSpec((1, rows, HEAD_DIM), lambda i, j: (i, j, 0)),
        compiler_params=_cparams(2),
        name="nsa_compress",
    )(blocks, pe.reshape(2, 1, kdim), w1.astype(BF16), b1.reshape(2, 1, hid), w2p, b2p, kg, cos, sin)


def _softmax_rows(s):
    m = jnp.max(s, axis=-1, keepdims=True)
    e = jnp.exp(s - m)
    return e / jnp.sum(e, axis=-1, keepdims=True)


def _nsa_attn_kernel(q_ref, g_ref, qg_ref, cos_ref, sin_ref, kc_ref, vc_ref, ks_ref, vs_ref, kw_ref, vw_ref,
                     c2s_ref, o_ref, *, seq, n_cmp, n_top, win_keys, sel_chunk):
    R, QB = NSA_HEADS_PER_GROUP, NSA_Q_BLOCK
    rows = R * QB
    qi = pl.program_id(1)
    s0 = qi * QB
    qt = q_ref[0]
    gt = jax.nn.sigmoid(g_ref[0])
    lane = lax.broadcasted_iota(jnp.int32, (QB, LANES), 1)
    tq = s0 + lax.broadcasted_iota(jnp.int32, (rows, 1), 0) % QB
    tq3 = tq.reshape(R, QB, 1)
    cos = jnp.concatenate([cos_ref[...]] * R, axis=0)
    sin = jnp.concatenate([sin_ref[...]] * R, axis=0)
    blocks_per_chunk = sel_chunk // SEL_BLOCK

    for g in range(NSA_KV_GROUPS):
        in_g = (lane // HEAD_DIM) == g
        slabs = []
        for r in range(R):
            pair = (g * R + r) // 2
            slab = qt[:, pair * LANES:(pair + 1) * LANES]
            if r % 2 != g:
                slab = pltpu.roll(slab, HEAD_DIM, 1)
            slabs.append(jnp.where(in_g, slab, 0.0))
        q2 = jnp.concatenate(slabs, axis=0)
        q2 = _half_norm_rope(q2, qg_ref[...], cos, sin) * (HEAD_DIM ** -0.5)
        qb = q2.astype(BF16)

        sc = _dot_nt(qb, kc_ref[0])
        cpos = lax.broadcasted_iota(jnp.int32, (1, LANES), 1) * CMP_STRIDE + (CMP_BLOCK - 1)
        valid_c = (cpos <= tq) & (lax.broadcasted_iota(jnp.int32, (1, LANES), 1) < n_cmp)
        p_c = jnp.where(valid_c, _softmax_rows(jnp.where(valid_c, sc, NEG)), 0.0)
        o_c = _dot(p_c.astype(BF16), vc_ref[0])

        psum = jnp.sum(p_c.reshape(R, QB, LANES), axis=0)
        imp = jnp.dot(psum, c2s_ref[...], preferred_element_type=F32, precision=HIGHEST)
        forced = (lane == 0) | (lane == qi) | (lane == qi - 1)
        imp = jnp.where(lane <= qi, jnp.where(forced, BIG, imp), -BIG)
        imp = jnp.where(lane < seq // SEL_BLOCK, imp, -jnp.inf)
        chosen = jnp.zeros((QB, LANES), F32)
        for _ in range(n_top):
            mx = jnp.max(imp, axis=-1, keepdims=True)
            first = jnp.min(jnp.where(imp == mx, lane, LANES), axis=-1, keepdims=True)
            pick = lane == first
            chosen = jnp.where(pick, 1.0, chosen)
            imp = jnp.where(pick, -jnp.inf, imp)
        chosen_b = chosen.astype(BF16)

        def sel_body(c, carry):
            m, l, acc = carry
            k0 = pl.multiple_of(c * sel_chunk, sel_chunk)
            k = ks_ref[0, pl.ds(k0, sel_chunk), :]
            v = vs_ref[0, pl.ds(k0, sel_chunk), :]
            s = _dot_nt(qb, k).reshape(R, QB, sel_chunk)
            blk_row = lax.broadcasted_iota(jnp.int32, (LANES, sel_chunk), 0)
            blk_key = c * blocks_per_chunk + lax.broadcasted_iota(jnp.int32, (LANES, sel_chunk), 1) // SEL_BLOCK
            expand = jnp.where(blk_row == blk_key, 1.0, 0.0).astype(BF16)
            picked = _dot(chosen_b, expand)
            kpos = k0 + lax.broadcasted_iota(jnp.int32, (1, 1, sel_chunk), 2)
            ok = (picked[None] > 0.5) & (kpos <= tq3)
            s = jnp.where(ok, s, NEG).reshape(rows, sel_chunk)
            m_new = jnp.maximum(m, jnp.max(s, axis=-1, keepdims=True))
            alpha = jnp.exp(m - m_new)
            p = jnp.exp(s - m_new)
            l = alpha * l + jnp.sum(p, axis=-1, keepdims=True)
            acc = alpha * acc + _dot(p.astype(BF16), v)
            return m_new, l, acc

        n_chunks = (s0 + QB + sel_chunk - 1) // sel_chunk
        init = (jnp.full((rows, 1), NEG, F32), jnp.zeros((rows, 1), F32), jnp.zeros((rows, LANES), F32))
        _, l_s, acc_s = lax.fori_loop(0, n_chunks, sel_body, init)
        o_s = acc_s / l_s

        w0 = pl.multiple_of(jnp.maximum(s0 + QB - win_keys, 0), SEL_BLOCK)
        kwin = kw_ref[0, pl.ds(w0, win_keys), :]
        vwin = vw_ref[0, pl.ds(w0, win_keys), :]
        wpos = w0 + lax.broadcasted_iota(jnp.int32, (1, win_keys), 1)
        ok_w = (wpos <= tq) & (wpos > tq - WINDOW)
        p_w = _softmax_rows(jnp.where(ok_w, _dot_nt(qb, kwin), NEG))
        o_w = _dot(p_w.astype(BF16), vwin)

        heads = []
        for r in range(R):
            h = g * R + r
            rs = slice(r * QB, (r + 1) * QB)
            o_h = (gt[:, h:h + 1] * o_c[rs]
                   + gt[:, NSA_HEADS + h:NSA_HEADS + h + 1] * o_s[rs]
                   + gt[:, 2 * NSA_HEADS + h:2 * NSA_HEADS + h + 1] * o_w[rs])
            if r % 2 != g:
                o_h = pltpu.roll(o_h, HEAD_DIM, 1)
            heads.append(o_h)
        for k in range(R // 2):
            pair = g * (R // 2) + k
            slab = jnp.where(lane < HEAD_DIM, heads[2 * k], heads[2 * k + 1])
            o_ref[0, :, pair * LANES:(pair + 1) * LANES] = slab.astype(o_ref.dtype)


def _nsa_attn_call(q, gates, q_gain, cos, sin, kc, vc, ks, vs, kw, vw, cmp_to_sel, n_cmp):
    b, s, nq = q.shape
    qb = NSA_Q_BLOCK
    n_top = min(SEL_TOPN, s // SEL_BLOCK)
    win_keys = min(WINDOW + 2 * qb, s)
    sel_chunk = min(256, s)
    qg = jnp.tile(q_gain, 2).reshape(1, LANES)
    row = lambda i, j: (i, j, 0)
    per_b = lambda i, j: (i, 0, 0)
    const = lambda i, j: (0, 0)
    kern = functools.partial(_nsa_attn_kernel, seq=s, n_cmp=n_cmp, n_top=n_top, win_keys=win_keys,
                             sel_chunk=sel_chunk)
    return pl.pallas_call(
        kern,
        out_shape=jax.ShapeDtypeStruct((b, s, nq), BF16),
        grid=(b, s // qb),
        in_specs=[pl.BlockSpec((1, qb, nq), row),
                  pl.BlockSpec((1, qb, LANES), row),
                  pl.BlockSpec((1, LANES), const),
                  pl.BlockSpec((qb, LANES), lambda i, j: (j, 0)),
                  pl.BlockSpec((qb, LANES), lambda i, j: (j, 0)),
                  pl.BlockSpec((1, LANES, LANES), per_b),
                  pl.BlockSpec((1, LANES, LANES), per_b),
                  pl.BlockSpec((1, s, LANES), per_b),
                  pl.BlockSpec((1, s, LANES), per_b),
                  pl.BlockSpec((1, s, LANES), per_b),
                  pl.BlockSpec((1, s, LANES), per_b),
                  pl.BlockSpec((LANES, LANES), const)],
        out_specs=pl.BlockSpec((1, qb, nq), row),
        compiler_params=_cparams(2),
        name="nsa_attention",
    )(q, gates, qg, cos, sin, kc, vc, ks, vs, kw, vw, cmp_to_sel)


def _nsa_out_kernel(a_ref, w_ref, x_ref, g_ref, o_ref):
    o_ref[0] = x_ref[0] + g_ref[0] * _dot(a_ref[0], w_ref[...])


def _mlstm_out_kernel(a_ref, og_ref, w_ref, x_ref, g_ref, o_ref):
    lhs = (jax.nn.sigmoid(og_ref[0]) * a_ref[0]).astype(BF16)
    o_ref[0] = x_ref[0] + g_ref[0] * _dot(lhs, w_ref[...])


def _out_proj_call(a, og, w_out, x, g, tm):
    b, s, d = x.shape
    k = a.shape[-1]
    row = lambda i, j: (i, j, 0)
    per_b = lambda i, j: (i, 0, 0)
    a_spec = pl.BlockSpec((1, tm, k), row)
    tail = [pl.BlockSpec((k, d), lambda i, j: (0, 0)), pl.BlockSpec((1, tm, d), row),
            pl.BlockSpec((1, 1, d), per_b)]
    if og is None:
        kern, ins, args = _nsa_out_kernel, [a_spec] + tail, (a, w_out.astype(BF16), x, g)
    else:
        kern, ins, args = _mlstm_out_kernel, [a_spec, a_spec] + tail, (a, og, w_out.astype(BF16), x, g)
    return pl.pallas_call(
        kern,
        out_shape=jax.ShapeDtypeStruct((b, s, d), F32),
        grid=(b, s // tm),
        in_specs=ins,
        out_specs=pl.BlockSpec((1, tm, d), row),
        compiler_params=_cparams(2),
        name="mixer_out_proj",
    )(*args)


def _mlstm_in_kernel(x_ref, gain_ref, sc_ref, sh_ref, wq_ref, wk_ref, wv_ref, wo_ref, wg_ref,
                     q_ref, k_ref, v_ref, o_ref, g_ref):
    h = _norm_mod(x_ref[0], gain_ref[...], sc_ref[0], sh_ref[0]).astype(BF16)
    q_ref[0] = _dot(h, wq_ref[...])
    k_ref[0] = _dot(h, wk_ref[...])
    v_ref[0] = _dot(h, wv_ref[...])
    o_ref[0] = _dot(h, wo_ref[...])
    g_ref[0] = _dot(h, wg_ref[...])


def _mlstm_in_call(x, gain, sc, sh, w_in, dqk, dv, tm):
    b, s, d = x.shape
    nh = MLSTM_HEADS
    sizes = [nh * dqk, nh * dqk, nh * dv, nh * dv]
    offs = np.cumsum([0] + sizes)
    ws = [w_in[:, offs[i]:offs[i + 1]].astype(BF16) for i in range(4)]
    wg = jnp.pad(w_in[:, offs[4]:], ((0, 0), (0, LANES - 2 * nh))).astype(BF16)
    row = lambda i, j: (i, j, 0)
    per_b = lambda i, j: (i, 0, 0)
    const = lambda i, j: (0, 0)
    widths = sizes + [LANES]
    return pl.pallas_call(
        _mlstm_in_kernel,
        out_shape=tuple(jax.ShapeDtypeStruct((b, s, n), F32) for n in widths),
        grid=(b, s // tm),
        in_specs=[pl.BlockSpec((1, tm, d), row),
                  pl.BlockSpec((1, d), const),
                  pl.BlockSpec((1, 1, d), per_b),
                  pl.BlockSpec((1, 1, d), per_b)] + [pl.BlockSpec((d, n), const) for n in widths],
        out_specs=tuple(pl.BlockSpec((1, tm, n), row) for n in widths),
        compiler_params=_cparams(2),
        name="mlstm_in_proj",
    )(x, gain.reshape(1, d), sc, sh, *ws, wg)


def _softcap(a):
    return GATE_SOFTCAP * jnp.tanh(a / GATE_SOFTCAP)


def _mlstm_kernel(bi_ref, bf_ref, q_ref, k_ref, v_ref, ig_ref, fg_ref, gain_ref, o_ref,
                  li_s, b_s, c_s, n_s, m_s, *, n_chunks, dqk):
    L = MLSTM_CHUNK
    h = pl.program_id(1)
    li_s[...] = _softcap(ig_ref[0, 0] + bi_ref[h])
    fa = _softcap(fg_ref[0, 0] + bf_ref[h])
    lf = jnp.minimum(fa, 0.0) - jnp.log1p(jnp.exp(-jnp.abs(fa)))
    r_i = lax.broadcasted_iota(jnp.int32, (L, L), 0)
    c_i = lax.broadcasted_iota(jnp.int32, (L, L), 1)
    upper = jnp.where(r_i <= c_i, 1.0, 0.0)
    b_s[...] = jnp.dot(lf, upper, preferred_element_type=F32, precision=HIGHEST)
    c_s[...] = jnp.zeros_like(c_s)
    n_s[...] = jnp.zeros_like(n_s)
    m_s[...] = jnp.zeros_like(m_s)
    eye = r_i == c_i
    causal = r_i >= c_i
    gain = gain_ref[0]
    k_scale = dqk ** -0.5

    def to_col(row):
        return jnp.sum(jnp.where(eye, jnp.broadcast_to(row, (L, L)), 0.0), axis=1, keepdims=True)

    def body(c, carry):
        r0 = pl.multiple_of(c * L, L)
        qc = q_ref[0, pl.ds(r0, L), :]
        kc = k_ref[0, pl.ds(r0, L), :] * k_scale
        vc = v_ref[0, pl.ds(r0, L), :].astype(BF16)
        b_row = b_s[pl.ds(c, 1), :]
        li_row = li_s[pl.ds(c, 1), :]
        b_col, li_col = to_col(b_row), to_col(li_row)
        m_prev = m_s[...]
        state = c_s[...]
        n_row = n_s[...]
        dmat = jnp.where(causal, b_col - b_row + li_row, NEG)
        m_inter = b_col + m_prev
        m_t = jnp.maximum(m_inter, jnp.max(dmat, axis=-1, keepdims=True))
        qcb = qc.astype(BF16)
        a = jnp.exp(dmat - m_t) * _dot_nt(qcb, kc.astype(BF16))
        inter = jnp.exp(m_inter - m_t)
        num = _dot(a.astype(BF16), vc) + inter * _dot(qcb, state.astype(BF16))
        den = jnp.sum(a, axis=-1, keepdims=True) + inter * jnp.sum(qc * n_row, axis=-1, keepdims=True)
        h_out = num / jnp.maximum(jnp.abs(den), jnp.exp(-m_t))
        hs = h_out * lax.rsqrt(jnp.mean(h_out * h_out, axis=-1, keepdims=True) + EPS) * gain
        o_ref[0, pl.ds(r0, L), :] = hs.astype(o_ref.dtype)
        m_new = m_t[L - 1:L, :]
        w_col = jnp.exp(b_row[:, L - 1:L] - b_col + li_col - m_new)
        decay = jnp.exp(m_inter[L - 1:L, :] - m_new)
        kw = kc * w_col
        c_s[...] = decay * state + _dot_tn(kw.astype(BF16), vc)
        n_s[...] = decay * n_row + jnp.sum(kw, axis=0, keepdims=True)
        m_s[...] = m_new
        return carry

    lax.fori_loop(0, n_chunks, body, 0)


def _mlstm_call(q, k, v, gates, b_igate, b_fgate, norm_gain):
    b, s, _ = q.shape
    nh = MLSTM_HEADS
    dqk, dv = q.shape[-1] // nh, v.shape[-1] // nh
    L = MLSTM_CHUNK
    nch = s // L
    g = jnp.transpose(gates[..., :2 * nh], (0, 2, 1)).reshape(b, 2 * nh, nch, L)
    smem = pl.BlockSpec(memory_space=pltpu.SMEM)
    head_cols = lambda n: pl.BlockSpec((1, s, n), lambda i, j: (i, 0, j))
    kern = functools.partial(_mlstm_kernel, n_chunks=nch, dqk=dqk)
    return pl.pallas_call(
        kern,
        out_shape=jax.ShapeDtypeStruct((b, s, nh * dv), BF16),
        grid=(b, nh),
        in_specs=[smem, smem, head_cols(dqk), head_cols(dqk), head_cols(dv),
                  pl.BlockSpec((1, 1, nch, L), lambda i, j: (i, j, 0, 0)),
                  pl.BlockSpec((1, 1, nch, L), lambda i, j: (i, j + nh, 0, 0)),
                  pl.BlockSpec((1, 1, dv), lambda i, j: (j, 0, 0))],
        out_specs=head_cols(dv),
        scratch_shapes=[pltpu.VMEM((nch, L), F32), pltpu.VMEM((nch, L), F32),
                        pltpu.VMEM((dqk, dv), F32), pltpu.VMEM((1, dqk), F32), pltpu.VMEM((1, 1), F32)],
        compiler_params=_cparams(2),
        name="mlstm_chunk_scan",
    )(b_igate, b_fgate, q, k, v, g, g, norm_gain.reshape(nh, 1, dv))


def _router_kernel(x_ref, gain_ref, sc_ref, sh_ref, wh_ref, wl_ref, h_ref, a_ref):
    h = _norm_mod(x_ref[0], gain_ref[...], sc_ref[0], sh_ref[0])
    hi = h.astype(BF16)
    lo = (h - hi.astype(F32)).astype(BF16)
    logits = _dot(hi, wh_ref[...]) + (_dot(lo, wh_ref[...]) + _dot(hi, wl_ref[...]))
    h_ref[0] = hi
    a_ref[0] = jax.nn.sigmoid(logits)


def _router_call(x, gain, sc, sh, router_w, tm):
    b, s, d = x.shape
    wp = jnp.pad(router_w, ((0, 0), (0, LANES - N_EXPERTS)))
    wh = wp.astype(BF16)
    wl = (wp - wh.astype(F32)).astype(BF16)
    row = lambda i, j: (i, j, 0)
    per_b = lambda i, j: (i, 0, 0)
    const = lambda i, j: (0, 0)
    return pl.pallas_call(
        _router_kernel,
        out_shape=(jax.ShapeDtypeStruct((b, s, d), BF16), jax.ShapeDtypeStruct((b, s, LANES), F32)),
        grid=(b, s // tm),
        in_specs=[pl.BlockSpec((1, tm, d), row),
                  pl.BlockSpec((1, d), const),
                  pl.BlockSpec((1, 1, d), per_b),
                  pl.BlockSpec((1, 1, d), per_b),
                  pl.BlockSpec((d, LANES), const),
                  pl.BlockSpec((d, LANES), const)],
        out_specs=(pl.BlockSpec((1, tm, d), row), pl.BlockSpec((1, tm, LANES), row)),
        compiler_params=_cparams(2),
        name="moe_router",
    )(x, gain.reshape(1, d), sc, sh, wh, wl)


def _expert_kernel(be_ref, live_ref, x_ref, ws_ref, wg_ref, wu_ref, wd_ref, o_ref, wg_s, wu_s, wd_s):
    i = pl.program_id(0)
    fresh = (i == 0) | (be_ref[i] != be_ref[jnp.maximum(i - 1, 0)])

    @pl.when(fresh)
    def _():
        wg_s[...] = wg_ref[0, 0].astype(BF16)
        wu_s[...] = wu_ref[0, 0].astype(BF16)
        wd_s[...] = wd_ref[0, 0].astype(BF16)

    @pl.when(live_ref[i] == 1)
    def _():
        xb = x_ref[...]
        gate = _dot(xb, wg_s[...])
        hid = gate * jax.nn.sigmoid(gate) * _dot(xb, wu_s[...])
        o_ref[...] = _dot(hid.astype(BF16), wd_s[...]) * ws_ref[...]

    @pl.when(live_ref[i] == 0)
    def _():
        o_ref[...] = jnp.zeros_like(o_ref)


def _expert_call(block_expert, block_live, xs, ws, w_gate, w_up, w_down, layer):
    p, d = xs.shape
    de = w_gate.shape[-1]
    nb = p // MOE_ROWS
    grid_spec = pltpu.PrefetchScalarGridSpec(
        num_scalar_prefetch=2,
        grid=(nb,),
        in_specs=[pl.BlockSpec((MOE_ROWS, d), lambda i, be, lv: (i, 0)),
                  pl.BlockSpec((MOE_ROWS, 1), lambda i, be, lv: (i, 0)),
                  pl.BlockSpec((1, 1, d, de), lambda i, be, lv: (layer, be[i], 0, 0)),
                  pl.BlockSpec((1, 1, d, de), lambda i, be, lv: (layer, be[i], 0, 0)),
                  pl.BlockSpec((1, 1, de, d), lambda i, be, lv: (layer, be[i], 0, 0))],
        out_specs=pl.BlockSpec((MOE_ROWS, d), lambda i, be, lv: (i, 0)),
        scratch_shapes=[pltpu.VMEM((d, de), BF16), pltpu.VMEM((d, de), BF16), pltpu.VMEM((de, d), BF16)],
    )
    return pl.pallas_call(
        _expert_kernel,
        out_shape=jax.ShapeDtypeStruct((p, d), F32),
        grid_spec=grid_spec,
        compiler_params=_cparams(1),
        name="moe_experts",
    )(block_expert, block_live, xs, ws, w_gate, w_up, w_down)


def _combine_kernel(x_ref, g_ref, ya_ref, yb_ref, o_ref):
    o_ref[0] = x_ref[0] + g_ref[0] * (ya_ref[0] + yb_ref[0])


def _combine_call(x, g, ya, yb, tm):
    b, s, d = x.shape
    row = lambda i, j: (i, j, 0)
    spec = pl.BlockSpec((1, tm, d), row)
    return pl.pallas_call(
        _combine_kernel,
        out_shape=jax.ShapeDtypeStruct((b, s, d), F32),
        grid=(b, s // tm),
        in_specs=[spec, pl.BlockSpec((1, 1, d), lambda i, j: (i, 0, 0)), spec, spec],
        out_specs=spec,
        compiler_params=_cparams(2),
        name="moe_combine",
    )(x, g, ya, yb)


def _route(affinity, router_b):
    t = affinity.shape[0]
    choice = (affinity + router_b.astype(F32)).reshape(t, N_GROUPS, EXPERTS_PER_GROUP)
    group_score = lax.top_k(choice, TOP_K)[0].sum(-1)
    group = jnp.argmax(group_score, axis=-1).astype(jnp.int32)
    in_group = jnp.take_along_axis(choice, group[:, None, None], axis=1)[:, 0]
    _, local = lax.top_k(in_group, TOP_K)
    expert = group[:, None] * EXPERTS_PER_GROUP + local
    aff = jnp.take_along_axis(affinity, expert, axis=1)
    weight = aff / aff.sum(-1, keepdims=True)

    n_assign = t * TOP_K
    e_flat = expert.reshape(n_assign)
    onehot = (e_flat[:, None] == jnp.arange(N_EXPERTS, dtype=jnp.int32)[None, :]).astype(jnp.int32)
    running = jnp.cumsum(onehot, axis=0)
    rank = jnp.take_along_axis(running, e_flat[:, None], axis=1)[:, 0] - 1
    counts = running[-1]
    padded = (counts + MOE_ROWS - 1) // MOE_ROWS * MOE_ROWS
    p_ends = jnp.cumsum(padded)
    p_starts = p_ends - padded
    dest = p_starts[e_flat] + rank
    nb = n_assign // MOE_ROWS + N_EXPERTS
    p = nb * MOE_ROWS
    tok_flat = jnp.repeat(jnp.arange(t, dtype=jnp.int32), TOP_K)
    buf_tok = jnp.zeros((p,), jnp.int32).at[dest].set(tok_flat)
    buf_w = jnp.zeros((p,), F32).at[dest].set(weight.reshape(n_assign))
    block_start = jnp.arange(nb, dtype=jnp.int32) * MOE_ROWS
    block_expert = jnp.minimum(jnp.searchsorted(p_ends, block_start, side='right'),
                               N_EXPERTS - 1).astype(jnp.int32)
    block_live = (block_start < p_ends[-1]).astype(jnp.int32)
    return buf_tok, buf_w, dest.reshape(t, TOP_K), block_expert, block_live


def _moe_layer(x, gain, sc, sh, g, router_w, router_b, w_gate, w_up, w_down, layer, tm):
    b, s, d = x.shape
    t = b * s
    hf, aff = _router_call(x, gain, sc, sh, router_w, tm)
    buf_tok, buf_w, dest, block_expert, block_live = _route(aff.reshape(t, LANES)[:, :N_EXPERTS], router_b)
    xs = jnp.take(hf.reshape(t, d), buf_tok, axis=0)
    out = _expert_call(block_expert, block_live, xs, buf_w[:, None], w_gate, w_up, w_down, layer)
    ya = jnp.take(out, dest[:, 0], axis=0).reshape(b, s, d)
    yb = jnp.take(out, dest[:, 1], axis=0).reshape(b, s, d)
    return _combine_call(x, g, ya, yb, tm)


def _nsa_layer(x, gain, sc, sh, g, w_in, w_out, q_gain, k_gain, cmp_pe, cmp_w1, cmp_b1, cmp_w2, cmp_b2, tm):
    b, s, d = x.shape
    G, dh = NSA_KV_GROUPS, HEAD_DIM
    cos, sin = _rope_tables(jnp.arange(s, dtype=jnp.int32))
    q, kc_raw, vc_raw, ks, vs, kw, vw, gates = _nsa_in_call(x, gain, sc, sh, w_in, k_gain, cos, sin, tm)

    n_cmp = (s - CMP_BLOCK) // CMP_STRIDE + 1
    n_str = s // CMP_STRIDE
    per_stride = CMP_STRIDE * dh

    def to_blocks(a):
        a = a.reshape(b, n_str, CMP_STRIDE, G, dh).transpose(0, 1, 3, 2, 4).reshape(b, n_str, G, per_stride)
        nxt = jnp.concatenate([a[:, 1:], jnp.zeros_like(a[:, :1])], axis=1)
        return jnp.concatenate([a, nxt], axis=-1).reshape(b * n_str * G, 2 * per_stride)

    blocks = jnp.stack([to_blocks(kc_raw), to_blocks(vc_raw)], axis=0)
    rows = n_str * G
    cmp_pos = (jnp.arange(rows, dtype=jnp.int32) // G) * CMP_STRIDE + (CMP_BLOCK - 1)
    ccos, csin = _rope_tables(cmp_pos)
    cmp = _compress_call(blocks, cmp_pe.reshape(2, CMP_BLOCK * dh), cmp_w1, cmp_b1, cmp_w2, cmp_b2,
                         k_gain[0], ccos, csin, rows)
    cmp = cmp.reshape(2, b, n_str, G * dh)
    cmp = jnp.pad(cmp, ((0, 0), (0, 0), (0, LANES - n_str), (0, 0)))
    kc, vc = cmp[0], cmp[1]

    ns = s // SEL_BLOCK
    r_, u_ = SEL_BLOCK // CMP_STRIDE, CMP_BLOCK // CMP_STRIDE
    c_idx = (r_ * np.arange(ns)[:, None, None] + np.arange(r_)[None, :, None]
             + np.arange(u_)[None, None, :]).reshape(ns, -1)
    c2s = (c_idx[:, :, None] == np.arange(n_cmp)[None, None, :]).sum(1).T.astype(np.float32)
    c2s = jnp.asarray(np.pad(c2s, ((0, LANES - n_cmp), (0, LANES - ns))))

    o = _nsa_attn_call(q, gates, q_gain, cos, sin, kc, vc, ks, vs, kw, vw, c2s, n_cmp)
    return _out_proj_call(o, None, w_out, x, g, tm)


def _mlstm_layer(x, gain, sc, sh, g, w_in, w_out, b_igate, b_fgate, norm_gain, tm):
    nh = MLSTM_HEADS
    dv = norm_gain.shape[-1]
    dqk = (w_in.shape[-1] - 2 * nh - 2 * nh * dv) // (2 * nh)
    q, k, v, og, gates = _mlstm_in_call(x, gain, sc, sh, w_in, dqk, dv, tm)
    hs = _mlstm_call(q, k, v, gates, b_igate, b_fgate, norm_gain)
    return _out_proj_call(hs, og, w_out, x, g, tm)


def kernel(x, c, ada_w, ada_b, norm_mix_gain, norm_ffn_gain, nsa_w_in, nsa_w_out, nsa_q_gain, nsa_k_gain, nsa_cmp_pe, nsa_cmp_w1, nsa_cmp_b1, nsa_cmp_w2, nsa_cmp_b2, mlstm_w_in, mlstm_b_igate, mlstm_b_fgate, mlstm_norm_gain, mlstm_w_out, router_w, router_b, moe_w_gate, moe_w_up, moe_w_down):
    b, s, d = x.shape
    depth = ada_w.shape[0]
    tm = min(512, s)
    mod = _mod_call(c, ada_w, ada_b)
    for i in range(depth):
        sh_m, sc_m, g_m, sh_f, sc_f, g_f = [mod[i, :, None, k * d:(k + 1) * d] for k in range(6)]
        j = i // 2
        if i % 2 == 0:
            x = _nsa_layer(x, norm_mix_gain[i], sc_m, sh_m, g_m, nsa_w_in[j], nsa_w_out[j], nsa_q_gain[j],
                           nsa_k_gain[j], nsa_cmp_pe[j], nsa_cmp_w1[j], nsa_cmp_b1[j], nsa_cmp_w2[j],
                           nsa_cmp_b2[j], tm)
        else:
            x = _mlstm_layer(x, norm_mix_gain[i], sc_m, sh_m, g_m, mlstm_w_in[j], mlstm_w_out[j],
                             mlstm_b_igate[j], mlstm_b_fgate[j], mlstm_norm_gain[j], tm)
        x = _moe_layer(x, norm_ffn_gain[i], sc_f, sh_f, g_f, router_w, router_b,
                       moe_w_gate, moe_w_up, moe_w_down, i, tm)
    return x
```

```python
import functools

import numpy as np
import jax
import jax.numpy as jnp
from jax import lax
from jax.experimental import pallas as pl
from jax.experimental.pallas import tpu as pltpu

F32 = jnp.float32
BF16 = jnp.bfloat16
HIGHEST = lax.Precision.HIGHEST

EPS = 1e-6
NEG = -1e30
BIG = 1e9
ROPE_THETA = 500000.0
LOG2E = 1.4426950408889634

NSA_HEADS = 16
NSA_KV_GROUPS = 2
NSA_HEADS_PER_GROUP = NSA_HEADS // NSA_KV_GROUPS
HEAD_DIM = 64
ROT_DIM = HEAD_DIM // 4
CMP_BLOCK = 32
CMP_STRIDE = 16
SEL_BLOCK = 64
SEL_TOPN = 8
WINDOW = 512
NSA_Q_BLOCK = 64
NSA_BRANCHES = 3

MLSTM_HEADS = 4
MLSTM_CHUNK = 64
GATE_SOFTCAP = 15.0

N_EXPERTS = 32
N_GROUPS = 4
EXPERTS_PER_GROUP = N_EXPERTS // N_GROUPS
TOP_K = 2
MOE_ROWS = 256

LANES = 128
VMEM_LIMIT = 48 * 1024 * 1024


def _cparams(n_axes):
    return pltpu.CompilerParams(dimension_semantics=("arbitrary",) * n_axes,
                                vmem_limit_bytes=VMEM_LIMIT)


def _dot(a, b):
    return jnp.dot(a, b, preferred_element_type=F32)


def _dot_nt(a, b):
    return lax.dot_general(a, b, (((1,), (1,)), ((), ())), preferred_element_type=F32)


def _dot_tn(a, b):
    return lax.dot_general(a, b, (((0,), (0,)), ((), ())), preferred_element_type=F32)


def _norm_mod(x, gain, sc, sh):
    y = x * lax.rsqrt(jnp.mean(x * x, axis=-1, keepdims=True) + EPS) * gain
    return y * (1.0 + sc) + sh


def _half_norm_rope(x, gain, cos, sin):
    lane = lax.broadcasted_iota(jnp.int32, x.shape, x.ndim - 1)
    left = lane < HEAD_DIM
    x2 = x * x
    ss_l = jnp.sum(jnp.where(left, x2, 0.0), axis=-1, keepdims=True)
    ss_r = jnp.sum(jnp.where(left, 0.0, x2), axis=-1, keepdims=True)
    ms = jnp.where(left, ss_l, ss_r) * (1.0 / HEAD_DIM)
    y = x * lax.rsqrt(ms + EPS) * gain
    half = ROT_DIM // 2
    nd = x.ndim - 1
    partner = jnp.where((lane % HEAD_DIM) < half,
                        pltpu.roll(y, LANES - half, nd), pltpu.roll(y, half, nd))
    return y * cos + partner * sin


def _rope_tables(pos):
    half = ROT_DIM // 2
    inv_freq = ROPE_THETA ** (-jnp.arange(half, dtype=F32) / half)
    ang = pos.astype(F32)[:, None] * inv_freq[None, :]
    cos, sin = jnp.cos(ang), jnp.sin(ang)
    n = pos.shape[0]
    one = jnp.ones((n, HEAD_DIM - ROT_DIM), F32)
    cos_h = jnp.concatenate([cos, cos, one], axis=-1)
    sin_h = jnp.concatenate([-sin, sin, 0.0 * one], axis=-1)
    return jnp.tile(cos_h, (1, 2)), jnp.tile(sin_h, (1, 2))


def _mod_kernel(c_ref, w_ref, b_ref, o_ref):
    c = c_ref[...]
    cond = c * jax.nn.sigmoid(c)
    o_ref[0] = jnp.dot(cond, w_ref[0], preferred_element_type=F32, precision=HIGHEST) + b_ref[0]


def _mod_call(c, ada_w, ada_b):
    depth, d, n = ada_w.shape
    b = c.shape[0]
    tn = n // 4
    return pl.pallas_call(
        _mod_kernel,
        out_shape=jax.ShapeDtypeStruct((depth, b, n), F32),
        grid=(depth, n // tn),
        in_specs=[pl.BlockSpec((b, d), lambda i, j: (0, 0)),
                  pl.BlockSpec((1, d, tn), lambda i, j: (i, 0, j)),
                  pl.BlockSpec((1, 1, tn), lambda i, j: (i, 0, j))],
        out_specs=pl.BlockSpec((1, b, tn), lambda i, j: (i, 0, j)),
        compiler_params=_cparams(2),
        name="adaln_mod",
    )(c, ada_w, ada_b.reshape(depth, 1, n))


def _nsa_in_kernel(x_ref, gain_ref, sc_ref, sh_ref, wq_ref, wkv_ref, wg_ref, kg_ref, cos_ref, sin_ref,
                   q_ref, kc_ref, vc_ref, ks_ref, vs_ref, kw_ref, vw_ref, g_ref):
    h = _norm_mod(x_ref[0], gain_ref[...], sc_ref[0], sh_ref[0]).astype(BF16)
    q_ref[0] = _dot(h, wq_ref[...])
    g_ref[0] = _dot(h, wg_ref[...])
    kv = _dot(h, wkv_ref[...])
    cos, sin = cos_ref[...], sin_ref[...]
    kc_ref[0] = kv[:, 0 * LANES:1 * LANES]
    vc_ref[0] = kv[:, 1 * LANES:2 * LANES]
    ks_ref[0] = _half_norm_rope(kv[:, 2 * LANES:3 * LANES], kg_ref[1:2, :], cos, sin).astype(BF16)
    vs_ref[0] = kv[:, 3 * LANES:4 * LANES].astype(BF16)
    kw_ref[0] = _half_norm_rope(kv[:, 4 * LANES:5 * LANES], kg_ref[2:3, :], cos, sin).astype(BF16)
    vw_ref[0] = kv[:, 5 * LANES:6 * LANES].astype(BF16)


def _nsa_in_call(x, gain, sc, sh, w_in, k_gain, cos, sin, tm):
    b, s, d = x.shape
    nq = NSA_HEADS * HEAD_DIM
    nkv = 6 * LANES
    wq = w_in[:, :nq].astype(BF16)
    wkv = w_in[:, nq:nq + nkv].astype(BF16)
    ng = NSA_BRANCHES * NSA_HEADS
    wg = jnp.pad(w_in[:, nq + nkv:], ((0, 0), (0, LANES - ng))).astype(BF16)
    kg = jnp.tile(k_gain, (1, 2))
    row = lambda i, j: (i, j, 0)
    per_b = lambda i, j: (i, 0, 0)
    const = lambda i, j: (0, 0)
    kv_out = lambda dt: jax.ShapeDtypeStruct((b, s, LANES), dt)
    return pl.pallas_call(
        _nsa_in_kernel,
        out_shape=(jax.ShapeDtypeStruct((b, s, nq), F32), kv_out(F32), kv_out(F32),
                   kv_out(BF16), kv_out(BF16), kv_out(BF16), kv_out(BF16), kv_out(F32)),
        grid=(b, s // tm),
        in_specs=[pl.BlockSpec((1, tm, d), row),
                  pl.BlockSpec((1, d), const),
                  pl.BlockSpec((1, 1, d), per_b),
                  pl.BlockSpec((1, 1, d), per_b),
                  pl.BlockSpec((d, nq), const),
                  pl.BlockSpec((d, nkv), const),
                  pl.BlockSpec((d, LANES), const),
                  pl.BlockSpec((3, LANES), const),
                  pl.BlockSpec((tm, LANES), lambda i, j: (j, 0)),
                  pl.BlockSpec((tm, LANES), lambda i, j: (j, 0))],
        out_specs=(pl.BlockSpec((1, tm, nq), row),) + (pl.BlockSpec((1, tm, LANES), row),) * 7,
        compiler_params=_cparams(2),
        name="nsa_in_proj",
    )(x, gain.reshape(1, d), sc, sh, wq, wkv, wg, kg, cos, sin)


def _compress_kernel(blk_ref, pe_ref, w1_ref, b1_ref, w2_ref, b2_ref, kg_ref, cos_ref, sin_ref, o_ref):
    is_key = pl.program_id(0) == 0
    blk = (blk_ref[0] + pe_ref[0]).astype(BF16)
    hid = _dot(blk, w1_ref[0]) + b1_ref[0]
    hid = 0.5 * hid * (1.0 + jnp.tanh(np.sqrt(2.0 / np.pi) * (hid + 0.044715 * hid * hid * hid)))
    out = _dot(hid.astype(BF16), w2_ref[0]) + b2_ref[0]
    normed = _half_norm_rope(out, kg_ref[...], cos_ref[...], sin_ref[...])
    o_ref[0] = jnp.where(is_key, normed, out)[:, :HEAD_DIM].astype(o_ref.dtype)


def _compress_call(blocks, pe, w1, b1, w2, b2, k_gain0, cos, sin, rows):
    _, r, kdim = blocks.shape
    hid = w1.shape[-1]
    w2p = jnp.pad(w2, ((0, 0), (0, 0), (0, LANES - HEAD_DIM))).astype(BF16)
    b2p = jnp.pad(b2, ((0, 0), (0, LANES - HEAD_DIM))).reshape(2, 1, LANES)
    kg = jnp.pad(k_gain0, (0, LANES - HEAD_DIM)).reshape(1, LANES)
    sel = lambda i, j: (i, 0, 0)
    const = lambda i, j: (0, 0)
    return pl.pallas_call(
        _compress_kernel,
        out_shape=jax.ShapeDtypeStruct((2, r, HEAD_DIM), BF16),
        grid=(2, r // rows),
        in_specs=[pl.BlockSpec((1, rows, kdim), lambda i, j: (i, j, 0)),
                  pl.BlockSpec((1, 1, kdim), sel),
                  pl.BlockSpec((1, kdim, hid), sel),
                  pl.BlockSpec((1, 1, hid), sel),
                  pl.BlockSpec((1, hid, LANES), sel),
                  pl.BlockSpec((1, 1, LANES), sel),
                  pl.BlockSpec((1, LANES), const),
                  pl.BlockSpec((rows, LANES), const),
                  pl.BlockSpec((rows, LANES), const)],
        out_specs=pl.BlockSpec((1, rows, HEAD_DIM), lambda i, j: (i, j, 0)),
        compiler_params=_cparams(2),
        name="nsa_compress",
    )(blocks, pe.reshape(2, 1, kdim), w1.astype(BF16), b1.reshape(2, 1, hid), w2p, b2p, kg, cos, sin)


def _attend(qb, k, v, g, bias):
    n_keys = k.shape[0]
    rows = qb.shape[0]
    s = _dot_nt(qb, k).reshape(rows // NSA_Q_BLOCK, NSA_Q_BLOCK, n_keys) + bias[None]
    s = s.reshape(rows, n_keys)
    p = jnp.exp2(s - jnp.max(s, axis=-1, keepdims=True)).astype(BF16)
    v_lane = lax.broadcasted_iota(jnp.int32, v.shape, 1)
    v_aug = jnp.where((v_lane // HEAD_DIM) == g, v, jnp.ones_like(v))
    return _dot(p, v_aug)


def _normalise(acc, g):
    c = (1 - g) * HEAD_DIM
    return acc / acc[:, c:c + 1]


def _nsa_attn_kernel(q_ref, g_ref, qg_ref, cos_ref, sin_ref, kc_ref, vc_ref, ks_ref, vs_ref, kw_ref, vw_ref,
                     c2s_ref, exp_ref, o_ref, sel_s, *, seq, n_cmp, n_top, win_keys, sel_span):
    R, QB = NSA_HEADS_PER_GROUP, NSA_Q_BLOCK
    rows = R * QB
    n_sel = seq // SEL_BLOCK
    qi = pl.program_id(1)
    s0 = qi * QB
    qt = q_ref[0]
    gt = jax.nn.sigmoid(g_ref[0])
    lane = lax.broadcasted_iota(jnp.int32, (QB, LANES), 1)
    tq = s0 + lax.broadcasted_iota(jnp.int32, (rows, 1), 0) % QB
    tq1 = s0 + lax.broadcasted_iota(jnp.int32, (QB, 1), 0)
    cos = jnp.concatenate([cos_ref[...]] * R, axis=0)
    sin = jnp.concatenate([sin_ref[...]] * R, axis=0)

    w0 = pl.multiple_of(jnp.maximum(s0 + QB - win_keys, 0), SEL_BLOCK)
    wpos = w0 + lax.broadcasted_iota(jnp.int32, (1, win_keys), 1)
    bias_w = jnp.where((wpos <= tq1) & (wpos > tq1 - WINDOW), 0.0, NEG)

    qbs, chosen_bs, o_cs, o_ws = [], [], [], []
    for g in range(NSA_KV_GROUPS):
        in_g = (lane // HEAD_DIM) == g
        slabs = []
        for r in range(R):
            pair = (g * R + r) // 2
            slab = qt[:, pair * LANES:(pair + 1) * LANES]
            if r % 2 != g:
                slab = pltpu.roll(slab, HEAD_DIM, 1)
            slabs.append(jnp.where(in_g, slab, 0.0))
        q2 = jnp.concatenate(slabs, axis=0)
        q2 = _half_norm_rope(q2, qg_ref[...], cos, sin) * (HEAD_DIM ** -0.5 * LOG2E)
        qb = q2.astype(BF16)

        sc = _dot_nt(qb, kc_ref[0])
        cpos = lax.broadcasted_iota(jnp.int32, (1, LANES), 1) * CMP_STRIDE + (CMP_BLOCK - 1)
        valid_c = (cpos <= tq) & (lax.broadcasted_iota(jnp.int32, (1, LANES), 1) < n_cmp)
        sc = jnp.where(valid_c, sc, NEG)
        e_c = jnp.exp2(sc - jnp.max(sc, axis=-1, keepdims=True))
        p_c = jnp.where(valid_c, e_c / jnp.sum(e_c, axis=-1, keepdims=True), 0.0)
        o_c = _dot(p_c.astype(BF16), vc_ref[0])

        psum = jnp.sum(p_c.reshape(R, QB, LANES), axis=0)
        imp = jnp.dot(psum, c2s_ref[...], preferred_element_type=F32, precision=HIGHEST)
        forced = (lane == 0) | (lane == qi) | (lane == qi - 1)
        imp = jnp.where(lane <= qi, jnp.where(forced, BIG, imp), -BIG)
        rep = jnp.where(lane < n_sel, imp, 0.0)
        span = n_sel
        while span < LANES:
            rep = rep + pltpu.roll(rep, span, 1)
            span *= 2
        j_idx = lane % n_sel
        beaten = jnp.zeros((QB, LANES), F32)
        for k in range(1, n_sel):
            other = pltpu.roll(rep, k, 1)
            beats = (other > rep) | ((j_idx >= k) & (other == rep))
            beaten = beaten + jnp.where(beats, 1.0, 0.0)
        chosen_bs.append(jnp.where((beaten < n_top) & (lane < n_sel), 1.0, 0.0).astype(BF16))

        kwin = kw_ref[0, pl.ds(w0, win_keys), :]
        vwin = vw_ref[0, pl.ds(w0, win_keys), :]
        o_ws.append(_normalise(_attend(qb, kwin, vwin, g, bias_w), g))
        qbs.append(qb)
        o_cs.append(o_c)

    n_spans = (s0 + QB + sel_span - 1) // sel_span
    for n in range(1, seq // sel_span + 1):
        @pl.when(n_spans == n)
        def _():
            n_keys = n * sel_span
            kpos = lax.broadcasted_iota(jnp.int32, (1, n_keys), 1)
            for g in range(NSA_KV_GROUPS):
                picked = _dot(chosen_bs[g], exp_ref[:, :n_keys])
                bias_s = jnp.where((picked > 0.5) & (kpos <= tq1), 0.0, NEG)
                sel_s[g] = _attend(qbs[g], ks_ref[0, :n_keys, :], vs_ref[0, :n_keys, :], g, bias_s)

    for g in range(NSA_KV_GROUPS):
        o_c, o_w = o_cs[g], o_ws[g]
        o_s = _normalise(sel_s[g], g)
        heads = []
        for r in range(R):
            h = g * R + r
            rs = slice(r * QB, (r + 1) * QB)
            o_h = (gt[:, h:h + 1] * o_c[rs]
                   + gt[:, NSA_HEADS + h:NSA_HEADS + h + 1] * o_s[rs]
                   + gt[:, 2 * NSA_HEADS + h:2 * NSA_HEADS + h + 1] * o_w[rs])
            if r % 2 != g:
                o_h = pltpu.roll(o_h, HEAD_DIM, 1)
            heads.append(o_h)
        for k in range(R // 2):
            pair = g * (R // 2) + k
            slab = jnp.where(lane < HEAD_DIM, heads[2 * k], heads[2 * k + 1])
            o_ref[0, :, pair * LANES:(pair + 1) * LANES] = slab.astype(o_ref.dtype)


def _nsa_attn_call(q, gates, q_gain, cos, sin, kc, vc, ks, vs, kw, vw, cmp_to_sel, n_cmp):
    b, s, nq = q.shape
    qb = NSA_Q_BLOCK
    n_top = min(SEL_TOPN, s // SEL_BLOCK)
    win_keys = min(WINDOW + 2 * qb, s)
    sel_span = min(512, s)
    qg = jnp.tile(q_gain, 2).reshape(1, LANES)
    expand = jnp.asarray(np.arange(LANES)[:, None] == (np.arange(s)[None, :] // SEL_BLOCK), BF16)
    row = lambda i, j: (i, j, 0)
    per_b = lambda i, j: (i, 0, 0)
    const = lambda i, j: (0, 0)
    kern = functools.partial(_nsa_attn_kernel, seq=s, n_cmp=n_cmp, n_top=n_top, win_keys=win_keys,
                             sel_span=sel_span)
    return pl.pallas_call(
        kern,
        out_shape=jax.ShapeDtypeStruct((b, s, nq), BF16),
        grid=(b, s // qb),
        in_specs=[pl.BlockSpec((1, qb, nq), row),
                  pl.BlockSpec((1, qb, LANES), row),
                  pl.BlockSpec((1, LANES), const),
                  pl.BlockSpec((qb, LANES), lambda i, j: (j, 0)),
                  pl.BlockSpec((qb, LANES), lambda i, j: (j, 0)),
                  pl.BlockSpec((1, LANES, LANES), per_b),
                  pl.BlockSpec((1, LANES, LANES), per_b),
                  pl.BlockSpec((1, s, LANES), per_b),
                  pl.BlockSpec((1, s, LANES), per_b),
                  pl.BlockSpec((1, s, LANES), per_b),
                  pl.BlockSpec((1, s, LANES), per_b),
                  pl.BlockSpec((LANES, LANES), const),
                  pl.BlockSpec((LANES, s), const)],
        out_specs=pl.BlockSpec((1, qb, nq), row),
        scratch_shapes=[pltpu.VMEM((NSA_KV_GROUPS, NSA_HEADS_PER_GROUP * qb, LANES), F32)],
        compiler_params=_cparams(2),
        name="nsa_attention",
    )(q, gates, qg, cos, sin, kc, vc, ks, vs, kw, vw, cmp_to_sel, expand)


def _nsa_out_kernel(a_ref, w_ref, x_ref, g_ref, o_ref):
    o_ref[0] = x_ref[0] + g_ref[0] * _dot(a_ref[0], w_ref[...])


def _mlstm_out_kernel(a_ref, og_ref, w_ref, x_ref, g_ref, o_ref):
    lhs = (jax.nn.sigmoid(og_ref[0]) * a_ref[0]).astype(BF16)
    o_ref[0] = x_ref[0] + g_ref[0] * _dot(lhs, w_ref[...])


def _out_proj_call(a, og, w_out, x, g, tm):
    b, s, d = x.shape
    k = a.shape[-1]
    row = lambda i, j: (i, j, 0)
    per_b = lambda i, j: (i, 0, 0)
    a_spec = pl.BlockSpec((1, tm, k), row)
    tail = [pl.BlockSpec((k, d), lambda i, j: (0, 0)), pl.BlockSpec((1, tm, d), row),
            pl.BlockSpec((1, 1, d), per_b)]
    if og is None:
        kern, ins, args = _nsa_out_kernel, [a_spec] + tail, (a, w_out.astype(BF16), x, g)
    else:
        kern, ins, args = _mlstm_out_kernel, [a_spec, a_spec] + tail, (a, og, w_out.astype(BF16), x, g)
    return pl.pallas_call(
        kern,
        out_shape=jax.ShapeDtypeStruct((b, s, d), F32),
        grid=(b, s // tm),
        in_specs=ins,
        out_specs=pl.BlockSpec((1, tm, d), row),
        compiler_params=_cparams(2),
        name="mixer_out_proj",
    )(*args)


def _mlstm_in_kernel(x_ref, gain_ref, sc_ref, sh_ref, wq_ref, wk_ref, wv_ref, wo_ref, wg_ref,
                     q_ref, k_ref, v_ref, o_ref, g_ref):
    h = _norm_mod(x_ref[0], gain_ref[...], sc_ref[0], sh_ref[0]).astype(BF16)
    q_ref[0] = _dot(h, wq_ref[...])
    k_ref[0] = _dot(h, wk_ref[...])
    v_ref[0] = _dot(h, wv_ref[...])
    o_ref[0] = _dot(h, wo_ref[...])
    g_ref[0] = _dot(h, wg_ref[...])


def _mlstm_in_call(x, gain, sc, sh, w_in, dqk, dv, tm):
    b, s, d = x.shape
    nh = MLSTM_HEADS
    sizes = [nh * dqk, nh * dqk, nh * dv, nh * dv]
    offs = np.cumsum([0] + sizes)
    ws = [w_in[:, offs[i]:offs[i + 1]].astype(BF16) for i in range(4)]
    wg = jnp.pad(w_in[:, offs[4]:], ((0, 0), (0, LANES - 2 * nh))).astype(BF16)
    row = lambda i, j: (i, j, 0)
    per_b = lambda i, j: (i, 0, 0)
    const = lambda i, j: (0, 0)
    widths = sizes + [LANES]
    return pl.pallas_call(
        _mlstm_in_kernel,
        out_shape=tuple(jax.ShapeDtypeStruct((b, s, n), F32) for n in widths),
        grid=(b, s // tm),
        in_specs=[pl.BlockSpec((1, tm, d), row),
                  pl.BlockSpec((1, d), const),
                  pl.BlockSpec((1, 1, d), per_b),
                  pl.BlockSpec((1, 1, d), per_b)] + [pl.BlockSpec((d, n), const) for n in widths],
        out_specs=tuple(pl.BlockSpec((1, tm, n), row) for n in widths),
        compiler_params=_cparams(2),
        name="mlstm_in_proj",
    )(x, gain.reshape(1, d), sc, sh, *ws, wg)


def _softcap(a):
    return GATE_SOFTCAP * jnp.tanh(a / GATE_SOFTCAP)


def _mlstm_kernel(bi_ref, bf_ref, q_ref, k_ref, v_ref, ig_ref, fg_ref, gain_ref, o_ref,
                  li_s, b_s, c_s, n_s, m_s, *, n_chunks, dqk):
    L = MLSTM_CHUNK
    h = pl.program_id(1)
    li_s[...] = _softcap(ig_ref[0, 0] + bi_ref[h])
    fa = _softcap(fg_ref[0, 0] + bf_ref[h])
    lf = jnp.minimum(fa, 0.0) - jnp.log1p(jnp.exp(-jnp.abs(fa)))
    r_i = lax.broadcasted_iota(jnp.int32, (L, L), 0)
    c_i = lax.broadcasted_iota(jnp.int32, (L, L), 1)
    upper = jnp.where(r_i <= c_i, 1.0, 0.0)
    b_s[...] = jnp.dot(lf, upper, preferred_element_type=F32, precision=HIGHEST)
    c_s[...] = jnp.zeros_like(c_s)
    n_s[...] = jnp.zeros_like(n_s)
    m_s[...] = jnp.zeros_like(m_s)
    eye = r_i == c_i
    causal = r_i >= c_i
    gain = gain_ref[0]
    k_scale = dqk ** -0.5

    def to_col(row):
        return jnp.sum(jnp.where(eye, jnp.broadcast_to(row, (L, L)), 0.0), axis=1, keepdims=True)

    def body(c, carry):
        r0 = pl.multiple_of(c * L, L)
        qc = q_ref[0, pl.ds(r0, L), :]
        kc = k_ref[0, pl.ds(r0, L), :] * k_scale
        vc = v_ref[0, pl.ds(r0, L), :].astype(BF16)
        b_row = b_s[pl.ds(c, 1), :]
        li_row = li_s[pl.ds(c, 1), :]
        b_col, li_col = to_col(b_row), to_col(li_row)
        m_prev = m_s[...]
        state = c_s[...]
        n_row = n_s[...]
        dmat = jnp.where(causal, b_col - b_row + li_row, NEG)
        m_inter = b_col + m_prev
        m_t = jnp.maximum(m_inter, jnp.max(dmat, axis=-1, keepdims=True))
        qcb = qc.astype(BF16)
        a = jnp.exp(dmat - m_t) * _dot_nt(qcb, kc.astype(BF16))
        inter = jnp.exp(m_inter - m_t)
        num = _dot(a.astype(BF16), vc) + inter * _dot(qcb, state.astype(BF16))
        den = jnp.sum(a, axis=-1, keepdims=True) + inter * jnp.sum(qc * n_row, axis=-1, keepdims=True)
        h_out = num / jnp.maximum(jnp.abs(den), jnp.exp(-m_t))
        hs = h_out * lax.rsqrt(jnp.mean(h_out * h_out, axis=-1, keepdims=True) + EPS) * gain
        o_ref[0, pl.ds(r0, L), :] = hs.astype(o_ref.dtype)
        m_new = m_t[L - 1:L, :]
        w_col = jnp.exp(b_row[:, L - 1:L] - b_col + li_col - m_new)
        decay = jnp.exp(m_inter[L - 1:L, :] - m_new)
        kw = kc * w_col
        c_s[...] = decay * state + _dot_tn(kw.astype(BF16), vc)
        n_s[...] = decay * n_row + jnp.sum(kw, axis=0, keepdims=True)
        m_s[...] = m_new
        return carry

    lax.fori_loop(0, n_chunks, body, 0)


def _mlstm_call(q, k, v, gates, b_igate, b_fgate, norm_gain):
    b, s, _ = q.shape
    nh = MLSTM_HEADS
    dqk, dv = q.shape[-1] // nh, v.shape[-1] // nh
    L = MLSTM_CHUNK
    nch = s // L
    g = jnp.transpose(gates[..., :2 * nh], (0, 2, 1)).reshape(b, 2 * nh, nch, L)
    smem = pl.BlockSpec(memory_space=pltpu.SMEM)
    head_cols = lambda n: pl.BlockSpec((1, s, n), lambda i, j: (i, 0, j))
    kern = functools.partial(_mlstm_kernel, n_chunks=nch, dqk=dqk)
    return pl.pallas_call(
        kern,
        out_shape=jax.ShapeDtypeStruct((b, s, nh * dv), BF16),
        grid=(b, nh),
        in_specs=[smem, smem, head_cols(dqk), head_cols(dqk), head_cols(dv),
                  pl.BlockSpec((1, 1, nch, L), lambda i, j: (i, j, 0, 0)),
                  pl.BlockSpec((1, 1, nch, L), lambda i, j: (i, j + nh, 0, 0)),
                  pl.BlockSpec((1, 1, dv), lambda i, j: (j, 0, 0))],
        out_specs=head_cols(dv),
        scratch_shapes=[pltpu.VMEM((nch, L), F32), pltpu.VMEM((nch, L), F32),
                        pltpu.VMEM((dqk, dv), F32), pltpu.VMEM((1, dqk), F32), pltpu.VMEM((1, 1), F32)],
        compiler_params=_cparams(2),
        name="mlstm_chunk_scan",
    )(b_igate, b_fgate, q, k, v, g, g, norm_gain.reshape(nh, 1, dv))


def _router_kernel(x_ref, gain_ref, sc_ref, sh_ref, wh_ref, wl_ref, rb_ref, h_ref, ri_ref, rw_ref, cnt_ref):
    h = _norm_mod(x_ref[0], gain_ref[...], sc_ref[0], sh_ref[0])
    hi = h.astype(BF16)
    lo = (h - hi.astype(F32)).astype(BF16)
    logits = _dot(hi, wh_ref[...]) + (_dot(lo, wh_ref[...]) + _dot(hi, wl_ref[...]))
    h_ref[0] = hi
    aff = jax.nn.sigmoid(logits)
    tm = aff.shape[0]
    lane = lax.broadcasted_iota(jnp.int32, (tm, LANES), 1)
    choice = jnp.where(lane < N_EXPERTS, aff + rb_ref[...], -jnp.inf)

    def first_max(v):
        m = jnp.max(v, axis=-1, keepdims=True)
        return m, jnp.min(jnp.where(v == m, lane, LANES), axis=-1, keepdims=True)

    best = e0 = e1 = None
    for g in range(N_GROUPS):
        cg = jnp.where((lane // EXPERTS_PER_GROUP) == g, choice, -jnp.inf)
        m1, i1 = first_max(cg)
        m2, i2 = first_max(jnp.where(lane == i1, -jnp.inf, cg))
        score = m1 + m2
        if g == 0:
            best, e0, e1 = score, i1, i2
        else:
            better = score > best
            best = jnp.where(better, score, best)
            e0 = jnp.where(better, i1, e0)
            e1 = jnp.where(better, i2, e1)
    is0, is1 = lane == e0, lane == e1
    a0 = jnp.sum(jnp.where(is0, aff, 0.0), axis=-1, keepdims=True)
    a1 = jnp.sum(jnp.where(is1, aff, 0.0), axis=-1, keepdims=True)
    tot = a0 + a1
    rw_ref[0] = jnp.where(lane == 0, a0 / tot, jnp.where(lane == 1, a1 / tot, 0.0))
    onehot = jnp.where(is0 | is1, 1.0, 0.0).astype(BF16)
    r_i = lax.broadcasted_iota(jnp.int32, (tm, tm), 0)
    c_i = lax.broadcasted_iota(jnp.int32, (tm, tm), 1)
    running = _dot(jnp.where(r_i >= c_i, 1.0, 0.0).astype(BF16), onehot)
    r0 = jnp.sum(jnp.where(is0, running, 0.0), axis=-1, keepdims=True) - 1.0
    r1 = jnp.sum(jnp.where(is1, running, 0.0), axis=-1, keepdims=True) - 1.0
    ri_ref[0] = jnp.where(lane == 0, e0, jnp.where(lane == 1, e1, jnp.where(
        lane == 2, r0.astype(jnp.int32), jnp.where(lane == 3, r1.astype(jnp.int32), 0))))
    cnt_ref[0] = jnp.broadcast_to(running[tm - 1:tm, :], (8, LANES))


def _router_call(x, gain, sc, sh, router_w, router_b, tm):
    b, s, d = x.shape
    nt = s // tm
    wp = jnp.pad(router_w, ((0, 0), (0, LANES - N_EXPERTS)))
    wh = wp.astype(BF16)
    wl = (wp - wh.astype(F32)).astype(BF16)
    rb = jnp.pad(router_b.astype(F32), (0, LANES - N_EXPERTS)).reshape(1, LANES)
    row = lambda i, j: (i, j, 0)
    per_b = lambda i, j: (i, 0, 0)
    const = lambda i, j: (0, 0)
    return pl.pallas_call(
        _router_kernel,
        out_shape=(jax.ShapeDtypeStruct((b, s, d), BF16), jax.ShapeDtypeStruct((b, s, LANES), jnp.int32),
                   jax.ShapeDtypeStruct((b, s, LANES), F32), jax.ShapeDtypeStruct((b * nt, 8, LANES), F32)),
        grid=(b, nt),
        in_specs=[pl.BlockSpec((1, tm, d), row),
                  pl.BlockSpec((1, d), const),
                  pl.BlockSpec((1, 1, d), per_b),
                  pl.BlockSpec((1, 1, d), per_b),
                  pl.BlockSpec((d, LANES), const),
                  pl.BlockSpec((d, LANES), const),
                  pl.BlockSpec((1, LANES), const)],
        out_specs=(pl.BlockSpec((1, tm, d), row), pl.BlockSpec((1, tm, LANES), row),
                   pl.BlockSpec((1, tm, LANES), row), pl.BlockSpec((1, 8, LANES), lambda i, j: (i * nt + j, 0, 0))),
        compiler_params=_cparams(2),
        name="moe_router",
    )(x, gain.reshape(1, d), sc, sh, wh, wl, rb)


def _expert_kernel(be_ref, live_ref, x_ref, wg_ref, wu_ref, wd_ref, o_ref, wg_s, wu_s, wd_s):
    i = pl.program_id(0)
    fresh = (i == 0) | (be_ref[i] != be_ref[jnp.maximum(i - 1, 0)])

    @pl.when(fresh)
    def _():
        wg_s[...] = wg_ref[0, 0].astype(BF16)
        wu_s[...] = wu_ref[0, 0].astype(BF16)
        wd_s[...] = wd_ref[0, 0].astype(BF16)

    @pl.when(live_ref[i] == 1)
    def _():
        xb = x_ref[...]
        gate = _dot(xb, wg_s[...])
        hid = gate * jax.nn.sigmoid(gate) * _dot(xb, wu_s[...])
        o_ref[...] = _dot(hid.astype(BF16), wd_s[...]).astype(o_ref.dtype)

    @pl.when(live_ref[i] == 0)
    def _():
        o_ref[...] = jnp.zeros_like(o_ref)


def _expert_call(block_expert, block_live, xs, w_gate, w_up, w_down, layer):
    p, d = xs.shape
    de = w_gate.shape[-1]
    nb = p // MOE_ROWS
    grid_spec = pltpu.PrefetchScalarGridSpec(
        num_scalar_prefetch=2,
        grid=(nb,),
        in_specs=[pl.BlockSpec((MOE_ROWS, d), lambda i, be, lv: (i, 0)),
                  pl.BlockSpec((1, 1, d, de), lambda i, be, lv: (layer, be[i], 0, 0)),
                  pl.BlockSpec((1, 1, d, de), lambda i, be, lv: (layer, be[i], 0, 0)),
                  pl.BlockSpec((1, 1, de, d), lambda i, be, lv: (layer, be[i], 0, 0))],
        out_specs=pl.BlockSpec((MOE_ROWS, d), lambda i, be, lv: (i, 0)),
        scratch_shapes=[pltpu.VMEM((d, de), BF16), pltpu.VMEM((d, de), BF16), pltpu.VMEM((de, d), BF16)],
    )
    return pl.pallas_call(
        _expert_kernel,
        out_shape=jax.ShapeDtypeStruct((p, d), BF16),
        grid_spec=grid_spec,
        compiler_params=_cparams(1),
        name="moe_experts",
    )(block_expert, block_live, xs, w_gate, w_up, w_down)


def _combine_kernel(x_ref, g_ref, rw_ref, ya_ref, yb_ref, o_ref):
    rw = rw_ref[0]
    y = rw[:, 0:1] * ya_ref[0].astype(F32) + rw[:, 1:2] * yb_ref[0].astype(F32)
    o_ref[0] = x_ref[0] + g_ref[0] * y


def _combine_call(x, g, rw, ya, yb, tm):
    b, s, d = x.shape
    row = lambda i, j: (i, j, 0)
    spec = pl.BlockSpec((1, tm, d), row)
    return pl.pallas_call(
        _combine_kernel,
        out_shape=jax.ShapeDtypeStruct((b, s, d), F32),
        grid=(b, s // tm),
        in_specs=[spec, pl.BlockSpec((1, 1, d), lambda i, j: (i, 0, 0)),
                  pl.BlockSpec((1, tm, LANES), row), spec, spec],
        out_specs=spec,
        compiler_params=_cparams(2),
        name="moe_combine",
    )(x, g, rw, ya, yb)


def _row_layout(ri, cnt, tm):
    t = ri.shape[0]
    nt = t // tm
    lanes = jnp.arange(N_EXPERTS, dtype=jnp.int32)
    tile_cnt = cnt[:, 0, :N_EXPERTS].astype(jnp.int32)
    tile_off = jnp.cumsum(tile_cnt, axis=0) - tile_cnt
    counts = jnp.sum(tile_cnt, axis=0)
    padded = (counts + MOE_ROWS - 1) // MOE_ROWS * MOE_ROWS
    p_ends = jnp.cumsum(padded)
    base = (p_ends - padded)[None, :] + tile_off
    e = ri[:, 0:TOP_K].reshape(nt, tm, TOP_K)
    rank = ri[:, TOP_K:2 * TOP_K].reshape(nt, tm, TOP_K)
    dest = jnp.sum(jnp.where(e[..., None] == lanes, base[:, None, None, :], 0), axis=-1) + rank
    dest = dest.reshape(t, TOP_K)
    nb = (t * TOP_K) // MOE_ROWS + N_EXPERTS
    tok = jnp.repeat(jnp.arange(t, dtype=jnp.int32), TOP_K)
    buf_tok = jnp.zeros((nb * MOE_ROWS,), jnp.int32).at[dest.reshape(-1)].set(tok)
    block_start = jnp.arange(nb, dtype=jnp.int32) * MOE_ROWS
    block_expert = jnp.minimum(jnp.sum((p_ends[None, :] <= block_start[:, None]).astype(jnp.int32), axis=-1),
                               N_EXPERTS - 1)
    block_live = (block_start < p_ends[-1]).astype(jnp.int32)
    return dest, buf_tok, block_expert, block_live


def _moe_layer(x, gain, sc, sh, g, router_w, router_b, w_gate, w_up, w_down, layer, tm):
    b, s, d = x.shape
    t = b * s
    hf, ri, rw, cnt = _router_call(x, gain, sc, sh, router_w, router_b, tm)
    dest, buf_tok, block_expert, block_live = _row_layout(ri.reshape(t, LANES), cnt, tm)
    xs = jnp.take(hf.reshape(t, d), buf_tok, axis=0)
    out = _expert_call(block_expert, block_live, xs, w_gate, w_up, w_down, layer)
    ya = jnp.take(out, dest[:, 0], axis=0).reshape(b, s, d)
    yb = jnp.take(out, dest[:, 1], axis=0).reshape(b, s, d)
    return _combine_call(x, g, rw, ya, yb, tm)


def _nsa_layer(x, gain, sc, sh, g, w_in, w_out, q_gain, k_gain, cmp_pe, cmp_w1, cmp_b1, cmp_w2, cmp_b2, tm):
    b, s, d = x.shape
    G, dh = NSA_KV_GROUPS, HEAD_DIM
    cos, sin = _rope_tables(jnp.arange(s, dtype=jnp.int32))
    q, kc_raw, vc_raw, ks, vs, kw, vw, gates = _nsa_in_call(x, gain, sc, sh, w_in, k_gain, cos, sin, tm)

    n_cmp = (s - CMP_BLOCK) // CMP_STRIDE + 1
    n_str = s // CMP_STRIDE
    per_stride = CMP_STRIDE * dh

    def to_blocks(a):
        a = a.reshape(b, n_str, CMP_STRIDE, G, dh).transpose(0, 1, 3, 2, 4).reshape(b, n_str, G, per_stride)
        nxt = jnp.concatenate([a[:, 1:], jnp.zeros_like(a[:, :1])], axis=1)
        return jnp.concatenate([a, nxt], axis=-1).reshape(b * n_str * G, 2 * per_stride)

    blocks = jnp.stack([to_blocks(kc_raw), to_blocks(vc_raw)], axis=0)
    rows = n_str * G
    cmp_pos = (jnp.arange(rows, dtype=jnp.int32) // G) * CMP_STRIDE + (CMP_BLOCK - 1)
    ccos, csin = _rope_tables(cmp_pos)
    cmp = _compress_call(blocks, cmp_pe.reshape(2, CMP_BLOCK * dh), cmp_w1, cmp_b1, cmp_w2, cmp_b2,
                         k_gain[0], ccos, csin, rows)
    cmp = cmp.reshape(2, b, n_str, G * dh)
    cmp = jnp.pad(cmp, ((0, 0), (0, 0), (0, LANES - n_str), (0, 0)))
    kc, vc = cmp[0], cmp[1]

    ns = s // SEL_BLOCK
    r_, u_ = SEL_BLOCK // CMP_STRIDE, CMP_BLOCK // CMP_STRIDE
    c_idx = (r_ * np.arange(ns)[:, None, None] + np.arange(r_)[None, :, None]
             + np.arange(u_)[None, None, :]).reshape(ns, -1)
    c2s = (c_idx[:, :, None] == np.arange(n_cmp)[None, None, :]).sum(1).T.astype(np.float32)
    c2s = jnp.asarray(np.pad(c2s, ((0, LANES - n_cmp), (0, LANES - ns))))

    o = _nsa_attn_call(q, gates, q_gain, cos, sin, kc, vc, ks, vs, kw, vw, c2s, n_cmp)
    return _out_proj_call(o, None, w_out, x, g, tm)


def _mlstm_layer(x, gain, sc, sh, g, w_in, w_out, b_igate, b_fgate, norm_gain, tm):
    nh = MLSTM_HEADS
    dv = norm_gain.shape[-1]
    dqk = (w_in.shape[-1] - 2 * nh - 2 * nh * dv) // (2 * nh)
    q, k, v, og, gates = _mlstm_in_call(x, gain, sc, sh, w_in, dqk, dv, tm)
    hs = _mlstm_call(q, k, v, gates, b_igate, b_fgate, norm_gain)
    return _out_proj_call(hs, og, w_out, x, g, tm)


def kernel(x, c, ada_w, ada_b, norm_mix_gain, norm_ffn_gain, nsa_w_in, nsa_w_out, nsa_q_gain, nsa_k_gain, nsa_cmp_pe, nsa_cmp_w1, nsa_cmp_b1, nsa_cmp_w2, nsa_cmp_b2, mlstm_w_in, mlstm_b_igate, mlstm_b_fgate, mlstm_norm_gain, mlstm_w_out, router_w, router_b, moe_w_gate, moe_w_up, moe_w_down):
    b, s, d = x.shape
    depth = ada_w.shape[0]
    tm = min(512, s)
    mod = _mod_call(c, ada_w, ada_b)
    for i in range(depth):
        sh_m, sc_m, g_m, sh_f, sc_f, g_f = [mod[i, :, None, k * d:(k + 1) * d] for k in range(6)]
        j = i // 2
        if i % 2 == 0:
            x = _nsa_layer(x, norm_mix_gain[i], sc_m, sh_m, g_m, nsa_w_in[j], nsa_w_out[j], nsa_q_gain[j],
                           nsa_k_gain[j], nsa_cmp_pe[j], nsa_cmp_w1[j], nsa_cmp_b1[j], nsa_cmp_w2[j],
                           nsa_cmp_b2[j], tm)
        else:
            x = _mlstm_layer(x, norm_mix_gain[i], sc_m, sh_m, g_m, mlstm_w_in[j], mlstm_w_out[j],
                             mlstm_b_igate[j], mlstm_b_fgate[j], mlstm_norm_gain[j], tm)
        x = _moe_layer(x, norm_ffn_gain[i], sc_f, sh_f, g_f, router_w, router_b,
                       moe_w_gate, moe_w_up, moe_w_down, i, tm)
    return x
```

```python
import functools

import numpy as np
import jax
import jax.numpy as jnp
from jax import lax
from jax.experimental import pallas as pl
from jax.experimental.pallas import tpu as pltpu

F32 = jnp.float32
BF16 = jnp.bfloat16
HIGHEST = lax.Precision.HIGHEST

EPS = 1e-6
NEG = -1e30
BIG = 1e9
ROPE_THETA = 500000.0
LOG2E = 1.4426950408889634

NSA_HEADS = 16
NSA_KV_GROUPS = 2
NSA_HEADS_PER_GROUP = NSA_HEADS // NSA_KV_GROUPS
HEAD_DIM = 64
ROT_DIM = HEAD_DIM // 4
CMP_BLOCK = 32
CMP_STRIDE = 16
SEL_BLOCK = 64
SEL_TOPN = 8
WINDOW = 512
NSA_Q_BLOCK = 64
NSA_BRANCHES = 3

MLSTM_HEADS = 4
MLSTM_CHUNK = 64
MLSTM_UNROLL = 4
GATE_SOFTCAP = 15.0

N_EXPERTS = 32
N_GROUPS = 4
EXPERTS_PER_GROUP = N_EXPERTS // N_GROUPS
TOP_K = 2
MOE_ROWS = 256

LANES = 128
VMEM_LIMIT = 48 * 1024 * 1024


def _cparams(n_axes):
    return pltpu.CompilerParams(dimension_semantics=("arbitrary",) * n_axes,
                                vmem_limit_bytes=VMEM_LIMIT)


def _dot(a, b):
    return jnp.dot(a, b, preferred_element_type=F32)


def _dot_nt(a, b):
    return lax.dot_general(a, b, (((1,), (1,)), ((), ())), preferred_element_type=F32)


def _dot_tn(a, b):
    return lax.dot_general(a, b, (((0,), (0,)), ((), ())), preferred_element_type=F32)


def _norm_mod(x, gain, sc, sh):
    y = x * lax.rsqrt(jnp.mean(x * x, axis=-1, keepdims=True) + EPS) * gain
    return y * (1.0 + sc) + sh


def _half_norm_rope(x, gain, cos, sin):
    lane = lax.broadcasted_iota(jnp.int32, x.shape, x.ndim - 1)
    left = lane < HEAD_DIM
    x2 = x * x
    ss_l = jnp.sum(jnp.where(left, x2, 0.0), axis=-1, keepdims=True)
    ss_r = jnp.sum(jnp.where(left, 0.0, x2), axis=-1, keepdims=True)
    ms = jnp.where(left, ss_l, ss_r) * (1.0 / HEAD_DIM)
    y = x * lax.rsqrt(ms + EPS) * gain
    half = ROT_DIM // 2
    nd = x.ndim - 1
    partner = jnp.where((lane % HEAD_DIM) < half,
                        pltpu.roll(y, LANES - half, nd), pltpu.roll(y, half, nd))
    return y * cos + partner * sin


def _rope_tables(pos):
    half = ROT_DIM // 2
    inv_freq = ROPE_THETA ** (-jnp.arange(half, dtype=F32) / half)
    ang = pos.astype(F32)[:, None] * inv_freq[None, :]
    cos, sin = jnp.cos(ang), jnp.sin(ang)
    n = pos.shape[0]
    one = jnp.ones((n, HEAD_DIM - ROT_DIM), F32)
    cos_h = jnp.concatenate([cos, cos, one], axis=-1)
    sin_h = jnp.concatenate([-sin, sin, 0.0 * one], axis=-1)
    return jnp.tile(cos_h, (1, 2)), jnp.tile(sin_h, (1, 2))


def _mod_kernel(c_ref, w_ref, b_ref, o_ref):
    c = c_ref[...]
    cond = c * jax.nn.sigmoid(c)
    o_ref[0] = jnp.dot(cond, w_ref[0], preferred_element_type=F32, precision=HIGHEST) + b_ref[0]


def _mod_call(c, ada_w, ada_b):
    depth, d, n = ada_w.shape
    b = c.shape[0]
    tn = n // 4
    return pl.pallas_call(
        _mod_kernel,
        out_shape=jax.ShapeDtypeStruct((depth, b, n), F32),
        grid=(depth, n // tn),
        in_specs=[pl.BlockSpec((b, d), lambda i, j: (0, 0)),
                  pl.BlockSpec((1, d, tn), lambda i, j: (i, 0, j)),
                  pl.BlockSpec((1, 1, tn), lambda i, j: (i, 0, j))],
        out_specs=pl.BlockSpec((1, b, tn), lambda i, j: (i, 0, j)),
        compiler_params=_cparams(2),
        name="adaln_mod",
    )(c, ada_w, ada_b.reshape(depth, 1, n))


def _nsa_in_kernel(x_ref, gain_ref, sc_ref, sh_ref, wq_ref, wkv_ref, wg_ref, kg_ref, cos_ref, sin_ref,
                   q_ref, kc_ref, vc_ref, ks_ref, vs_ref, kw_ref, vw_ref, g_ref):
    h = _norm_mod(x_ref[0], gain_ref[...], sc_ref[0], sh_ref[0]).astype(BF16)
    q_ref[0] = _dot(h, wq_ref[...])
    g_ref[0] = _dot(h, wg_ref[...])
    kv = _dot(h, wkv_ref[...])
    cos, sin = cos_ref[...], sin_ref[...]
    kc_ref[0] = kv[:, 0 * LANES:1 * LANES]
    vc_ref[0] = kv[:, 1 * LANES:2 * LANES]
    ks_ref[0] = _half_norm_rope(kv[:, 2 * LANES:3 * LANES], kg_ref[1:2, :], cos, sin).astype(BF16)
    vs_ref[0] = kv[:, 3 * LANES:4 * LANES].astype(BF16)
    kw_ref[0] = _half_norm_rope(kv[:, 4 * LANES:5 * LANES], kg_ref[2:3, :], cos, sin).astype(BF16)
    vw_ref[0] = kv[:, 5 * LANES:6 * LANES].astype(BF16)


def _nsa_in_call(x, gain, sc, sh, w_in, k_gain, cos, sin, tm):
    b, s, d = x.shape
    nq = NSA_HEADS * HEAD_DIM
    nkv = 6 * LANES
    wq = w_in[:, :nq].astype(BF16)
    wkv = w_in[:, nq:nq + nkv].astype(BF16)
    ng = NSA_BRANCHES * NSA_HEADS
    wg = jnp.pad(w_in[:, nq + nkv:], ((0, 0), (0, LANES - ng))).astype(BF16)
    kg = jnp.tile(k_gain, (1, 2))
    row = lambda i, j: (i, j, 0)
    per_b = lambda i, j: (i, 0, 0)
    const = lambda i, j: (0, 0)
    kv_out = lambda dt: jax.ShapeDtypeStruct((b, s, LANES), dt)
    return pl.pallas_call(
        _nsa_in_kernel,
        out_shape=(jax.ShapeDtypeStruct((b, s, nq), F32), kv_out(F32), kv_out(F32),
                   kv_out(BF16), kv_out(BF16), kv_out(BF16), kv_out(BF16), kv_out(F32)),
        grid=(b, s // tm),
        in_specs=[pl.BlockSpec((1, tm, d), row),
                  pl.BlockSpec((1, d), const),
                  pl.BlockSpec((1, 1, d), per_b),
                  pl.BlockSpec((1, 1, d), per_b),
                  pl.BlockSpec((d, nq), const),
                  pl.BlockSpec((d, nkv), const),
                  pl.BlockSpec((d, LANES), const),
                  pl.BlockSpec((3, LANES), const),
                  pl.BlockSpec((tm, LANES), lambda i, j: (j, 0)),
                  pl.BlockSpec((tm, LANES), lambda i, j: (j, 0))],
        out_specs=(pl.BlockSpec((1, tm, nq), row),) + (pl.BlockSpec((1, tm, LANES), row),) * 7,
        compiler_params=_cparams(2),
        name="nsa_in_proj",
    )(x, gain.reshape(1, d), sc, sh, wq, wkv, wg, kg, cos, sin)


def _compress_kernel(blk_ref, pe_ref, w1_ref, b1_ref, w2_ref, b2_ref, kg_ref, cos_ref, sin_ref, o_ref):
    is_key = pl.program_id(0) == 0
    blk = (blk_ref[0] + pe_ref[0]).astype(BF16)
    hid = _dot(blk, w1_ref[0]) + b1_ref[0]
    hid = 0.5 * hid * (1.0 + jnp.tanh(np.sqrt(2.0 / np.pi) * (hid + 0.044715 * hid * hid * hid)))
    out = _dot(hid.astype(BF16), w2_ref[0]) + b2_ref[0]
    normed = _half_norm_rope(out, kg_ref[...], cos_ref[...], sin_ref[...])
    o_ref[0] = jnp.where(is_key, normed, out)[:, :HEAD_DIM].astype(o_ref.dtype)


def _compress_call(blocks, pe, w1, b1, w2, b2, k_gain0, cos, sin, rows):
    _, r, kdim = blocks.shape
    hid = w1.shape[-1]
    w2p = jnp.pad(w2, ((0, 0), (0, 0), (0, LANES - HEAD_DIM))).astype(BF16)
    b2p = jnp.pad(b2, ((0, 0), (0, LANES - HEAD_DIM))).reshape(2, 1, LANES)
    kg = jnp.pad(k_gain0, (0, LANES - HEAD_DIM)).reshape(1, LANES)
    sel = lambda i, j: (i, 0, 0)
    const = lambda i, j: (0, 0)
    return pl.pallas_call(
        _compress_kernel,
        out_shape=jax.ShapeDtypeStruct((2, r, HEAD_DIM), BF16),
        grid=(2, r // rows),
        in_specs=[pl.BlockSpec((1, rows, kdim), lambda i, j: (i, j, 0)),
                  pl.BlockSpec((1, 1, kdim), sel),
                  pl.BlockSpec((1, kdim, hid), sel),
                  pl.BlockSpec((1, 1, hid), sel),
                  pl.BlockSpec((1, hid, LANES), sel),
                  pl.BlockSpec((1, 1, LANES), sel),
                  pl.BlockSpec((1, LANES), const),
                  pl.BlockSpec((rows, LANES), const),
                  pl.BlockSpec((rows, LANES), const)],
        out_specs=pl.BlockSpec((1, rows, HEAD_DIM), lambda i, j: (i, j, 0)),
        compiler_params=_cparams(2),
        name="nsa_compress",
    )(blocks, pe.reshape(2, 1, kdim), w1.astype(BF16), b1.reshape(2, 1, hid), w2p, b2p, kg, cos, sin)


def _attend(qb, k_ref, v_ref, k0, spans, g, bias):
    rows = qb.shape[0]
    m = acc = None
    for off, size in spans:
        k = k_ref[0, pl.ds(k0 + off, size), :]
        v = v_ref[0, pl.ds(k0 + off, size), :]
        s = _dot_nt(qb, k).reshape(rows // NSA_Q_BLOCK, NSA_Q_BLOCK, size) + bias[None, :, off:off + size]
        s = s.reshape(rows, size)
        m_span = jnp.max(s, axis=-1, keepdims=True)
        m_new = m_span if m is None else jnp.maximum(m, m_span)
        p = jnp.exp2(s - m_new).astype(BF16)
        v_lane = lax.broadcasted_iota(jnp.int32, v.shape, 1)
        pv = _dot(p, jnp.where((v_lane // HEAD_DIM) == g, v, jnp.ones_like(v)))
        acc = pv if m is None else acc * jnp.exp2(m - m_new) + pv
        m = m_new
    return acc


def _normalise(acc, g):
    c = (1 - g) * HEAD_DIM
    return acc / acc[:, c:c + 1]


def _nsa_attn_kernel(q_ref, g_ref, qg_ref, cos_ref, sin_ref, kc_ref, vc_ref, ks_ref, vs_ref, kw_ref, vw_ref,
                     c2s_ref, exp_ref, o_ref, sel_s, *, seq, n_cmp, n_top, win_keys, sel_span):
    R, QB = NSA_HEADS_PER_GROUP, NSA_Q_BLOCK
    rows = R * QB
    n_sel = seq // SEL_BLOCK
    qi = pl.program_id(1)
    s0 = qi * QB
    qt = q_ref[0]
    gt = jax.nn.sigmoid(g_ref[0])
    lane = lax.broadcasted_iota(jnp.int32, (QB, LANES), 1)
    tq = s0 + lax.broadcasted_iota(jnp.int32, (rows, 1), 0) % QB
    tq1 = s0 + lax.broadcasted_iota(jnp.int32, (QB, 1), 0)
    cos = jnp.concatenate([cos_ref[...]] * R, axis=0)
    sin = jnp.concatenate([sin_ref[...]] * R, axis=0)

    w0 = pl.multiple_of(jnp.maximum(s0 + QB - win_keys, 0), SEL_BLOCK)
    wpos = w0 + lax.broadcasted_iota(jnp.int32, (1, win_keys), 1)
    bias_w = jnp.where((wpos <= tq1) & (wpos > tq1 - WINDOW), 0.0, NEG)
    win_spans = [(off, min(2 * LANES, win_keys - off)) for off in range(0, win_keys, 2 * LANES)]

    qbs, chosen_bs, o_cs, o_ws = [], [], [], []
    for g in range(NSA_KV_GROUPS):
        in_g = (lane // HEAD_DIM) == g
        slabs = []
        for r in range(R):
            pair = (g * R + r) // 2
            slab = qt[:, pair * LANES:(pair + 1) * LANES]
            if r % 2 != g:
                slab = pltpu.roll(slab, HEAD_DIM, 1)
            slabs.append(jnp.where(in_g, slab, 0.0))
        q2 = jnp.concatenate(slabs, axis=0)
        q2 = _half_norm_rope(q2, qg_ref[...], cos, sin) * (HEAD_DIM ** -0.5 * LOG2E)
        qb = q2.astype(BF16)

        sc = _dot_nt(qb, kc_ref[0])
        cpos = lax.broadcasted_iota(jnp.int32, (1, LANES), 1) * CMP_STRIDE + (CMP_BLOCK - 1)
        valid_c = (cpos <= tq) & (lax.broadcasted_iota(jnp.int32, (1, LANES), 1) < n_cmp)
        sc = jnp.where(valid_c, sc, NEG)
        e_c = jnp.exp2(sc - jnp.max(sc, axis=-1, keepdims=True))
        p_c = jnp.where(valid_c, e_c / jnp.sum(e_c, axis=-1, keepdims=True), 0.0)
        o_c = _dot(p_c.astype(BF16), vc_ref[0])

        psum = jnp.sum(p_c.reshape(R, QB, LANES), axis=0)
        imp = jnp.dot(psum, c2s_ref[...], preferred_element_type=F32, precision=HIGHEST)
        forced = (lane == 0) | (lane == qi) | (lane == qi - 1)
        imp = jnp.where(lane <= qi, jnp.where(forced, BIG, imp), -BIG)
        rep = jnp.where(lane < n_sel, imp, 0.0)
        span = n_sel
        while span < LANES:
            rep = rep + pltpu.roll(rep, span, 1)
            span *= 2
        j_idx = lane % n_sel
        beaten = jnp.zeros((QB, LANES), F32)
        for k in range(1, n_sel):
            other = pltpu.roll(rep, k, 1)
            beats = (other > rep) | ((j_idx >= k) & (other == rep))
            beaten = beaten + jnp.where(beats, 1.0, 0.0)
        chosen_bs.append(jnp.where((beaten < n_top) & (lane < n_sel), 1.0, 0.0).astype(BF16))

        o_ws.append(_normalise(_attend(qb, kw_ref, vw_ref, w0, win_spans, g, bias_w), g))
        qbs.append(qb)
        o_cs.append(o_c)

    n_spans = (s0 + QB + sel_span - 1) // sel_span
    for n in range(1, seq // sel_span + 1):
        @pl.when(n_spans == n)
        def _():
            n_keys = n * sel_span
            kpos = lax.broadcasted_iota(jnp.int32, (1, n_keys), 1)
            for g in range(NSA_KV_GROUPS):
                picked = _dot(chosen_bs[g], exp_ref[:, :n_keys])
                bias_s = jnp.where((picked > 0.5) & (kpos <= tq1), 0.0, NEG)
                spans = [(c * sel_span, sel_span) for c in range(n)]
                sel_s[g] = _attend(qbs[g], ks_ref, vs_ref, 0, spans, g, bias_s)

    for g in range(NSA_KV_GROUPS):
        o_c, o_w = o_cs[g], o_ws[g]
        o_s = _normalise(sel_s[g], g)
        heads = []
        for r in range(R):
            h = g * R + r
            rs = slice(r * QB, (r + 1) * QB)
            o_h = (gt[:, h:h + 1] * o_c[rs]
                   + gt[:, NSA_HEADS + h:NSA_HEADS + h + 1] * o_s[rs]
                   + gt[:, 2 * NSA_HEADS + h:2 * NSA_HEADS + h + 1] * o_w[rs])
            if r % 2 != g:
                o_h = pltpu.roll(o_h, HEAD_DIM, 1)
            heads.append(o_h)
        for k in range(R // 2):
            pair = g * (R // 2) + k
            slab = jnp.where(lane < HEAD_DIM, heads[2 * k], heads[2 * k + 1])
            o_ref[0, :, pair * LANES:(pair + 1) * LANES] = slab.astype(o_ref.dtype)


def _nsa_attn_call(q, gates, q_gain, cos, sin, kc, vc, ks, vs, kw, vw, cmp_to_sel, n_cmp):
    b, s, nq = q.shape
    qb = NSA_Q_BLOCK
    n_top = min(SEL_TOPN, s // SEL_BLOCK)
    win_keys = min(WINDOW + 2 * qb, s)
    sel_span = min(512, s)
    qg = jnp.tile(q_gain, 2).reshape(1, LANES)
    expand = jnp.asarray(np.arange(LANES)[:, None] == (np.arange(s)[None, :] // SEL_BLOCK), BF16)
    row = lambda i, j: (i, j, 0)
    per_b = lambda i, j: (i, 0, 0)
    const = lambda i, j: (0, 0)
    kern = functools.partial(_nsa_attn_kernel, seq=s, n_cmp=n_cmp, n_top=n_top, win_keys=win_keys,
                             sel_span=sel_span)
    return pl.pallas_call(
        kern,
        out_shape=jax.ShapeDtypeStruct((b, s, nq), BF16),
        grid=(b, s // qb),
        in_specs=[pl.BlockSpec((1, qb, nq), row),
                  pl.BlockSpec((1, qb, LANES), row),
                  pl.BlockSpec((1, LANES), const),
                  pl.BlockSpec((qb, LANES), lambda i, j: (j, 0)),
                  pl.BlockSpec((qb, LANES), lambda i, j: (j, 0)),
                  pl.BlockSpec((1, LANES, LANES), per_b),
                  pl.BlockSpec((1, LANES, LANES), per_b),
                  pl.BlockSpec((1, s, LANES), per_b),
                  pl.BlockSpec((1, s, LANES), per_b),
                  pl.BlockSpec((1, s, LANES), per_b),
                  pl.BlockSpec((1, s, LANES), per_b),
                  pl.BlockSpec((LANES, LANES), const),
                  pl.BlockSpec((LANES, s), const)],
        out_specs=pl.BlockSpec((1, qb, nq), row),
        scratch_shapes=[pltpu.VMEM((NSA_KV_GROUPS, NSA_HEADS_PER_GROUP * qb, LANES), F32)],
        compiler_params=_cparams(2),
        name="nsa_attention",
    )(q, gates, qg, cos, sin, kc, vc, ks, vs, kw, vw, cmp_to_sel, expand)


def _nsa_out_kernel(a_ref, w_ref, x_ref, g_ref, o_ref):
    o_ref[0] = x_ref[0] + g_ref[0] * _dot(a_ref[0], w_ref[...])


def _mlstm_out_kernel(a_ref, og_ref, w_ref, x_ref, g_ref, o_ref):
    lhs = (jax.nn.sigmoid(og_ref[0].astype(F32)) * a_ref[0].astype(F32)).astype(BF16)
    o_ref[0] = x_ref[0] + g_ref[0] * _dot(lhs, w_ref[...])


def _out_proj_call(a, og, w_out, x, g, tm):
    b, s, d = x.shape
    k = a.shape[-1]
    row = lambda i, j: (i, j, 0)
    per_b = lambda i, j: (i, 0, 0)
    a_spec = pl.BlockSpec((1, tm, k), row)
    tail = [pl.BlockSpec((k, d), lambda i, j: (0, 0)), pl.BlockSpec((1, tm, d), row),
            pl.BlockSpec((1, 1, d), per_b)]
    if og is None:
        kern, ins, args = _nsa_out_kernel, [a_spec] + tail, (a, w_out.astype(BF16), x, g)
    else:
        kern, ins, args = _mlstm_out_kernel, [a_spec, a_spec] + tail, (a, og, w_out.astype(BF16), x, g)
    return pl.pallas_call(
        kern,
        out_shape=jax.ShapeDtypeStruct((b, s, d), F32),
        grid=(b, s // tm),
        in_specs=ins,
        out_specs=pl.BlockSpec((1, tm, d), row),
        compiler_params=_cparams(2),
        name="mixer_out_proj",
    )(*args)


def _mlstm_in_kernel(x_ref, gain_ref, sc_ref, sh_ref, wq_ref, wk_ref, wv_ref, wo_ref, wg_ref,
                     q_ref, k_ref, v_ref, o_ref, g_ref):
    h = _norm_mod(x_ref[0], gain_ref[...], sc_ref[0], sh_ref[0]).astype(BF16)
    q_ref[0] = _dot(h, wq_ref[...]).astype(BF16)
    k_ref[0] = _dot(h, wk_ref[...]).astype(BF16)
    v_ref[0] = _dot(h, wv_ref[...]).astype(BF16)
    o_ref[0] = _dot(h, wo_ref[...]).astype(BF16)
    g_ref[0] = _dot(h, wg_ref[...])


def _mlstm_in_call(x, gain, sc, sh, w_in, dqk, dv, tm):
    b, s, d = x.shape
    nh = MLSTM_HEADS
    sizes = [nh * dqk, nh * dqk, nh * dv, nh * dv]
    offs = np.cumsum([0] + sizes)
    ws = [w_in[:, offs[i]:offs[i + 1]].astype(BF16) for i in range(4)]
    wg = jnp.pad(w_in[:, offs[4]:], ((0, 0), (0, LANES - 2 * nh))).astype(BF16)
    row = lambda i, j: (i, j, 0)
    per_b = lambda i, j: (i, 0, 0)
    const = lambda i, j: (0, 0)
    widths = sizes + [LANES]
    return pl.pallas_call(
        _mlstm_in_kernel,
        out_shape=tuple(jax.ShapeDtypeStruct((b, s, n), BF16) for n in sizes)
        + (jax.ShapeDtypeStruct((b, s, LANES), F32),),
        grid=(b, s // tm),
        in_specs=[pl.BlockSpec((1, tm, d), row),
                  pl.BlockSpec((1, d), const),
                  pl.BlockSpec((1, 1, d), per_b),
                  pl.BlockSpec((1, 1, d), per_b)] + [pl.BlockSpec((d, n), const) for n in widths],
        out_specs=tuple(pl.BlockSpec((1, tm, n), row) for n in widths),
        compiler_params=_cparams(2),
        name="mlstm_in_proj",
    )(x, gain.reshape(1, d), sc, sh, *ws, wg)


def _softcap(a):
    return GATE_SOFTCAP * jnp.tanh(a / GATE_SOFTCAP)


def _mlstm_kernel(bi_ref, bf_ref, q_ref, k_ref, v_ref, ig_ref, fg_ref, gain_ref, o_ref,
                  li_s, b_s, *, n_chunks, dqk, dv):
    L = MLSTM_CHUNK
    h = pl.program_id(1)
    li_s[...] = _softcap(ig_ref[0, 0] + bi_ref[h])
    fa = _softcap(fg_ref[0, 0] + bf_ref[h])
    lf = jnp.minimum(fa, 0.0) - jnp.log1p(jnp.exp(-jnp.abs(fa)))
    r_i = lax.broadcasted_iota(jnp.int32, (L, L), 0)
    c_i = lax.broadcasted_iota(jnp.int32, (L, L), 1)
    upper = jnp.where(r_i <= c_i, 1.0, 0.0)
    b_s[...] = jnp.dot(lf, upper, preferred_element_type=F32, precision=HIGHEST)
    eye = r_i == c_i
    causal = r_i >= c_i
    gain = gain_ref[0]
    k_scale = dqk ** -0.5

    def to_col(row):
        return jnp.sum(jnp.where(eye, jnp.broadcast_to(row, (L, L)), 0.0), axis=1, keepdims=True)

    def body(c, carry):
        state, n_row, m_prev = carry
        r0 = pl.multiple_of(c * L, L)
        qcb = q_ref[0, pl.ds(r0, L), :]
        kc = k_ref[0, pl.ds(r0, L), :].astype(F32) * k_scale
        vc = v_ref[0, pl.ds(r0, L), :]
        b_row = b_s[pl.ds(c, 1), :]
        li_row = li_s[pl.ds(c, 1), :]
        b_col, li_col = to_col(b_row), to_col(li_row)
        b_last = b_row[:, L - 1:L]
        dmat = jnp.where(causal, b_col - b_row + li_row, NEG)
        m_loc = jnp.max(dmat, axis=-1, keepdims=True)
        a_loc = jnp.exp(dmat - m_loc) * _dot_nt(qcb, kc.astype(BF16))
        num_loc = _dot(a_loc.astype(BF16), vc)
        den_loc = jnp.sum(a_loc, axis=-1, keepdims=True)
        g_max = m_loc[L - 1:L, :]
        kw = kc * jnp.exp(b_last - b_col + li_col - g_max)
        kv = _dot_tn(kw.astype(BF16), vc)
        kn = jnp.sum(kw, axis=0, keepdims=True)
        m_inter = b_col + m_prev
        m_t = jnp.maximum(m_inter, m_loc)
        intra = jnp.exp(m_loc - m_t)
        inter = jnp.exp(m_inter - m_t)
        num = intra * num_loc + inter * _dot(qcb, state.astype(BF16))
        den = intra * den_loc + inter * jnp.sum(qcb.astype(F32) * n_row, axis=-1, keepdims=True)
        h_out = num / jnp.maximum(jnp.abs(den), jnp.exp(-m_t))
        hs = h_out * lax.rsqrt(jnp.mean(h_out * h_out, axis=-1, keepdims=True) + EPS) * gain
        o_ref[0, pl.ds(r0, L), :] = hs.astype(o_ref.dtype)
        m_new = jnp.maximum(b_last + m_prev, g_max)
        decay = jnp.exp(b_last + m_prev - m_new)
        grow = jnp.exp(g_max - m_new)
        return decay * state + grow * kv, decay * n_row + grow * kn, m_new

    init = (jnp.zeros((dqk, dv), F32), jnp.zeros((1, dqk), F32), jnp.zeros((1, 1), F32))
    lax.fori_loop(0, n_chunks, body, init, unroll=MLSTM_UNROLL)


def _mlstm_call(q, k, v, gates, b_igate, b_fgate, norm_gain):
    b, s, _ = q.shape
    nh = MLSTM_HEADS
    dqk, dv = q.shape[-1] // nh, v.shape[-1] // nh
    L = MLSTM_CHUNK
    nch = s // L
    g = jnp.transpose(gates[..., :2 * nh], (0, 2, 1)).reshape(b, 2 * nh, nch, L)
    smem = pl.BlockSpec(memory_space=pltpu.SMEM)
    head_cols = lambda n: pl.BlockSpec((1, s, n), lambda i, j: (i, 0, j))
    kern = functools.partial(_mlstm_kernel, n_chunks=nch, dqk=dqk, dv=dv)
    return pl.pallas_call(
        kern,
        out_shape=jax.ShapeDtypeStruct((b, s, nh * dv), BF16),
        grid=(b, nh),
        in_specs=[smem, smem, head_cols(dqk), head_cols(dqk), head_cols(dv),
                  pl.BlockSpec((1, 1, nch, L), lambda i, j: (i, j, 0, 0)),
                  pl.BlockSpec((1, 1, nch, L), lambda i, j: (i, j + nh, 0, 0)),
                  pl.BlockSpec((1, 1, dv), lambda i, j: (j, 0, 0))],
        out_specs=head_cols(dv),
        scratch_shapes=[pltpu.VMEM((nch, L), F32), pltpu.VMEM((nch, L), F32)],
        compiler_params=_cparams(2),
        name="mlstm_chunk_scan",
    )(b_igate, b_fgate, q, k, v, g, g, norm_gain.reshape(nh, 1, dv))


def _router_kernel(x_ref, gain_ref, sc_ref, sh_ref, wh_ref, wl_ref, rb_ref, h_ref, ri_ref, rw_ref, cnt_ref):
    h = _norm_mod(x_ref[0], gain_ref[...], sc_ref[0], sh_ref[0])
    hi = h.astype(BF16)
    lo = (h - hi.astype(F32)).astype(BF16)
    logits = _dot(hi, wh_ref[...]) + (_dot(lo, wh_ref[...]) + _dot(hi, wl_ref[...]))
    h_ref[0] = hi
    aff = jax.nn.sigmoid(logits)
    tm = aff.shape[0]
    lane = lax.broadcasted_iota(jnp.int32, (tm, LANES), 1)
    choice = jnp.where(lane < N_EXPERTS, aff + rb_ref[...], -jnp.inf)

    def first_max(v):
        m = jnp.max(v, axis=-1, keepdims=True)
        return m, jnp.min(jnp.where(v == m, lane, LANES), axis=-1, keepdims=True)

    best = e0 = e1 = None
    for g in range(N_GROUPS):
        cg = jnp.where((lane // EXPERTS_PER_GROUP) == g, choice, -jnp.inf)
        m1, i1 = first_max(cg)
        m2, i2 = first_max(jnp.where(lane == i1, -jnp.inf, cg))
        score = m1 + m2
        if g == 0:
            best, e0, e1 = score, i1, i2
        else:
            better = score > best
            best = jnp.where(better, score, best)
            e0 = jnp.where(better, i1, e0)
            e1 = jnp.where(better, i2, e1)
    is0, is1 = lane == e0, lane == e1
    a0 = jnp.sum(jnp.where(is0, aff, 0.0), axis=-1, keepdims=True)
    a1 = jnp.sum(jnp.where(is1, aff, 0.0), axis=-1, keepdims=True)
    tot = a0 + a1
    rw_ref[0] = jnp.where(lane == 0, a0 / tot, jnp.where(lane == 1, a1 / tot, 0.0))
    onehot = jnp.where(is0 | is1, 1.0, 0.0).astype(BF16)
    r_i = lax.broadcasted_iota(jnp.int32, (tm, tm), 0)
    c_i = lax.broadcasted_iota(jnp.int32, (tm, tm), 1)
    running = _dot(jnp.where(r_i >= c_i, 1.0, 0.0).astype(BF16), onehot)
    r0 = jnp.sum(jnp.where(is0, running, 0.0), axis=-1, keepdims=True) - 1.0
    r1 = jnp.sum(jnp.where(is1, running, 0.0), axis=-1, keepdims=True) - 1.0
    ri_ref[0] = jnp.where(lane == 0, e0, jnp.where(lane == 1, e1, jnp.where(
        lane == 2, r0.astype(jnp.int32), jnp.where(lane == 3, r1.astype(jnp.int32), 0))))
    cnt_ref[0] = jnp.broadcast_to(running[tm - 1:tm, :], (8, LANES))


def _router_call(x, gain, sc, sh, router_w, router_b, tm):
    b, s, d = x.shape
    nt = s // tm
    wp = jnp.pad(router_w, ((0, 0), (0, LANES - N_EXPERTS)))
    wh = wp.astype(BF16)
    wl = (wp - wh.astype(F32)).astype(BF16)
    rb = jnp.pad(router_b.astype(F32), (0, LANES - N_EXPERTS)).reshape(1, LANES)
    row = lambda i, j: (i, j, 0)
    per_b = lambda i, j: (i, 0, 0)
    const = lambda i, j: (0, 0)
    return pl.pallas_call(
        _router_kernel,
        out_shape=(jax.ShapeDtypeStruct((b, s, d), BF16), jax.ShapeDtypeStruct((b, s, LANES), jnp.int32),
                   jax.ShapeDtypeStruct((b, s, LANES), F32), jax.ShapeDtypeStruct((b * nt, 8, LANES), F32)),
        grid=(b, nt),
        in_specs=[pl.BlockSpec((1, tm, d), row),
                  pl.BlockSpec((1, d), const),
                  pl.BlockSpec((1, 1, d), per_b),
                  pl.BlockSpec((1, 1, d), per_b),
                  pl.BlockSpec((d, LANES), const),
                  pl.BlockSpec((d, LANES), const),
                  pl.BlockSpec((1, LANES), const)],
        out_specs=(pl.BlockSpec((1, tm, d), row), pl.BlockSpec((1, tm, LANES), row),
                   pl.BlockSpec((1, tm, LANES), row), pl.BlockSpec((1, 8, LANES), lambda i, j: (i * nt + j, 0, 0))),
        compiler_params=_cparams(2),
        name="moe_router",
    )(x, gain.reshape(1, d), sc, sh, wh, wl, rb)


def _expert_kernel(be_ref, live_ref, x_ref, wg_ref, wu_ref, wd_ref, o_ref, wg_s, wu_s, wd_s):
    i = pl.program_id(0)
    fresh = (i == 0) | (be_ref[i] != be_ref[jnp.maximum(i - 1, 0)])

    @pl.when(fresh)
    def _():
        wg_s[...] = wg_ref[0, 0].astype(BF16)
        wu_s[...] = wu_ref[0, 0].astype(BF16)
        wd_s[...] = wd_ref[0, 0].astype(BF16)

    @pl.when(live_ref[i] == 1)
    def _():
        xb = x_ref[...]
        gate = _dot(xb, wg_s[...])
        hid = gate * jax.nn.sigmoid(gate) * _dot(xb, wu_s[...])
        o_ref[...] = _dot(hid.astype(BF16), wd_s[...]).astype(o_ref.dtype)

    @pl.when(live_ref[i] == 0)
    def _():
        o_ref[...] = jnp.zeros_like(o_ref)


def _expert_call(block_expert, block_live, xs, w_gate, w_up, w_down, layer):
    p, d = xs.shape
    de = w_gate.shape[-1]
    nb = p // MOE_ROWS
    grid_spec = pltpu.PrefetchScalarGridSpec(
        num_scalar_prefetch=2,
        grid=(nb,),
        in_specs=[pl.BlockSpec((MOE_ROWS, d), lambda i, be, lv: (i, 0)),
                  pl.BlockSpec((1, 1, d, de), lambda i, be, lv: (layer, be[i], 0, 0)),
                  pl.BlockSpec((1, 1, d, de), lambda i, be, lv: (layer, be[i], 0, 0)),
                  pl.BlockSpec((1, 1, de, d), lambda i, be, lv: (layer, be[i], 0, 0))],
        out_specs=pl.BlockSpec((MOE_ROWS, d), lambda i, be, lv: (i, 0)),
        scratch_shapes=[pltpu.VMEM((d, de), BF16), pltpu.VMEM((d, de), BF16), pltpu.VMEM((de, d), BF16)],
    )
    return pl.pallas_call(
        _expert_kernel,
        out_shape=jax.ShapeDtypeStruct((p, d), BF16),
        grid_spec=grid_spec,
        compiler_params=_cparams(1),
        name="moe_experts",
    )(block_expert, block_live, xs, w_gate, w_up, w_down)


def _combine_kernel(x_ref, g_ref, rw_ref, ya_ref, yb_ref, o_ref):
    rw = rw_ref[0]
    y = rw[:, 0:1] * ya_ref[0].astype(F32) + rw[:, 1:2] * yb_ref[0].astype(F32)
    o_ref[0] = x_ref[0] + g_ref[0] * y


def _combine_call(x, g, rw, ya, yb, tm):
    b, s, d = x.shape
    row = lambda i, j: (i, j, 0)
    spec = pl.BlockSpec((1, tm, d), row)
    return pl.pallas_call(
        _combine_kernel,
        out_shape=jax.ShapeDtypeStruct((b, s, d), F32),
        grid=(b, s // tm),
        in_specs=[spec, pl.BlockSpec((1, 1, d), lambda i, j: (i, 0, 0)),
                  pl.BlockSpec((1, tm, LANES), row), spec, spec],
        out_specs=spec,
        compiler_params=_cparams(2),
        name="moe_combine",
    )(x, g, rw, ya, yb)


def _row_layout(ri, cnt, tm):
    t = ri.shape[0]
    nt = t // tm
    lanes = jnp.arange(N_EXPERTS, dtype=jnp.int32)
    tile_cnt = cnt[:, 0, :N_EXPERTS].astype(jnp.int32)
    tile_off = jnp.cumsum(tile_cnt, axis=0) - tile_cnt
    counts = jnp.sum(tile_cnt, axis=0)
    padded = (counts + MOE_ROWS - 1) // MOE_ROWS * MOE_ROWS
    p_ends = jnp.cumsum(padded)
    base = (p_ends - padded)[None, :] + tile_off
    e = ri[:, 0:TOP_K].reshape(nt, tm, TOP_K)
    rank = ri[:, TOP_K:2 * TOP_K].reshape(nt, tm, TOP_K)
    dest = jnp.sum(jnp.where(e[..., None] == lanes, base[:, None, None, :], 0), axis=-1) + rank
    dest = dest.reshape(t, TOP_K)
    nb = (t * TOP_K) // MOE_ROWS + N_EXPERTS
    tok = jnp.repeat(jnp.arange(t, dtype=jnp.int32), TOP_K)
    buf_tok = jnp.zeros((nb * MOE_ROWS,), jnp.int32).at[dest.reshape(-1)].set(tok)
    block_start = jnp.arange(nb, dtype=jnp.int32) * MOE_ROWS
    block_expert = jnp.minimum(jnp.sum((p_ends[None, :] <= block_start[:, None]).astype(jnp.int32), axis=-1),
                               N_EXPERTS - 1)
    block_live = (block_start < p_ends[-1]).astype(jnp.int32)
    return dest, buf_tok, block_expert, block_live


def _take_rows(a, idx):
    n, d = a.shape
    words = lax.bitcast_convert_type(a.reshape(n, d // 2, 2), jnp.uint32)
    return lax.bitcast_convert_type(jnp.take(words, idx, axis=0), BF16).reshape(idx.shape[0], d)


def _moe_layer(x, gain, sc, sh, g, router_w, router_b, w_gate, w_up, w_down, layer, tm):
    b, s, d = x.shape
    t = b * s
    hf, ri, rw, cnt = _router_call(x, gain, sc, sh, router_w, router_b, tm)
    dest, buf_tok, block_expert, block_live = _row_layout(ri.reshape(t, LANES), cnt, tm)
    xs = _take_rows(hf.reshape(t, d), buf_tok)
    out = _expert_call(block_expert, block_live, xs, w_gate, w_up, w_down, layer)
    ya = _take_rows(out, dest[:, 0]).reshape(b, s, d)
    yb = _take_rows(out, dest[:, 1]).reshape(b, s, d)
    return _combine_call(x, g, rw, ya, yb, tm)


def _nsa_layer(x, gain, sc, sh, g, w_in, w_out, q_gain, k_gain, cmp_pe, cmp_w1, cmp_b1, cmp_w2, cmp_b2, tm):
    b, s, d = x.shape
    G, dh = NSA_KV_GROUPS, HEAD_DIM
    cos, sin = _rope_tables(jnp.arange(s, dtype=jnp.int32))
    q, kc_raw, vc_raw, ks, vs, kw, vw, gates = _nsa_in_call(x, gain, sc, sh, w_in, k_gain, cos, sin, tm)

    n_cmp = (s - CMP_BLOCK) // CMP_STRIDE + 1
    n_str = s // CMP_STRIDE
    per_stride = CMP_STRIDE * dh

    def to_blocks(a):
        a = a.reshape(b, n_str, CMP_STRIDE, G, dh).transpose(0, 1, 3, 2, 4).reshape(b, n_str, G, per_stride)
        nxt = jnp.concatenate([a[:, 1:], jnp.zeros_like(a[:, :1])], axis=1)
        return jnp.concatenate([a, nxt], axis=-1).reshape(b * n_str * G, 2 * per_stride)

    blocks = jnp.stack([to_blocks(kc_raw), to_blocks(vc_raw)], axis=0)
    rows = n_str * G
    cmp_pos = (jnp.arange(rows, dtype=jnp.int32) // G) * CMP_STRIDE + (CMP_BLOCK - 1)
    ccos, csin = _rope_tables(cmp_pos)
    cmp = _compress_call(blocks, cmp_pe.reshape(2, CMP_BLOCK * dh), cmp_w1, cmp_b1, cmp_w2, cmp_b2,
                         k_gain[0], ccos, csin, rows)
    cmp = cmp.reshape(2, b, n_str, G * dh)
    cmp = jnp.pad(cmp, ((0, 0), (0, 0), (0, LANES - n_str), (0, 0)))
    kc, vc = cmp[0], cmp[1]

    ns = s // SEL_BLOCK
    r_, u_ = SEL_BLOCK // CMP_STRIDE, CMP_BLOCK // CMP_STRIDE
    c_idx = (r_ * np.arange(ns)[:, None, None] + np.arange(r_)[None, :, None]
             + np.arange(u_)[None, None, :]).reshape(ns, -1)
    c2s = (c_idx[:, :, None] == np.arange(n_cmp)[None, None, :]).sum(1).T.astype(np.float32)
    c2s = jnp.asarray(np.pad(c2s, ((0, LANES - n_cmp), (0, LANES - ns))))

    o = _nsa_attn_call(q, gates, q_gain, cos, sin, kc, vc, ks, vs, kw, vw, c2s, n_cmp)
    return _out_proj_call(o, None, w_out, x, g, tm)


def _mlstm_layer(x, gain, sc, sh, g, w_in, w_out, b_igate, b_fgate, norm_gain, tm):
    nh = MLSTM_HEADS
    dv = norm_gain.shape[-1]
    dqk = (w_in.shape[-1] - 2 * nh - 2 * nh * dv) // (2 * nh)
    q, k, v, og, gates = _mlstm_in_call(x, gain, sc, sh, w_in, dqk, dv, tm)
    hs = _mlstm_call(q, k, v, gates, b_igate, b_fgate, norm_gain)
    return _out_proj_call(hs, og, w_out, x, g, tm)


def kernel(x, c, ada_w, ada_b, norm_mix_gain, norm_ffn_gain, nsa_w_in, nsa_w_out, nsa_q_gain, nsa_k_gain, nsa_cmp_pe, nsa_cmp_w1, nsa_cmp_b1, nsa_cmp_w2, nsa_cmp_b2, mlstm_w_in, mlstm_b_igate, mlstm_b_fgate, mlstm_norm_gain, mlstm_w_out, router_w, router_b, moe_w_gate, moe_w_up, moe_w_down):
    b, s, d = x.shape
    depth = ada_w.shape[0]
    tm = min(512, s)
    mod = _mod_call(c, ada_w, ada_b)
    for i in range(depth):
        sh_m, sc_m, g_m, sh_f, sc_f, g_f = [mod[i, :, None, k * d:(k + 1) * d] for k in range(6)]
        j = i // 2
        if i % 2 == 0:
            x = _nsa_layer(x, norm_mix_gain[i], sc_m, sh_m, g_m, nsa_w_in[j], nsa_w_out[j], nsa_q_gain[j],
                           nsa_k_gain[j], nsa_cmp_pe[j], nsa_cmp_w1[j], nsa_cmp_b1[j], nsa_cmp_w2[j],
                           nsa_cmp_b2[j], tm)
        else:
            x = _mlstm_layer(x, norm_mix_gain[i], sc_m, sh_m, g_m, mlstm_w_in[j], mlstm_w_out[j],
                             mlstm_b_igate[j], mlstm_b_fgate[j], mlstm_norm_gain[j], tm)
        x = _moe_layer(x, norm_ffn_gain[i], sc_f, sh_f, g_f, router_w, router_b,
                       moe_w_gate, moe_w_up, moe_w_down, i, tm)
    return x
```

```python
import functools

import numpy as np
import jax
import jax.numpy as jnp
from jax import lax
from jax.experimental import pallas as pl
from jax.experimental.pallas import tpu as pltpu

F32 = jnp.float32
BF16 = jnp.bfloat16
HIGHEST = lax.Precision.HIGHEST

EPS = 1e-6
NEG = -1e30
BIG = 1e9
ROPE_THETA = 500000.0
LOG2E = 1.4426950408889634

NSA_HEADS = 16
NSA_KV_GROUPS = 2
NSA_HEADS_PER_GROUP = NSA_HEADS // NSA_KV_GROUPS
HEAD_DIM = 64
ROT_DIM = HEAD_DIM // 4
CMP_BLOCK = 32
CMP_STRIDE = 16
SEL_BLOCK = 64
SEL_TOPN = 8
WINDOW = 512
NSA_Q_BLOCK = 64
NSA_BRANCHES = 3

MLSTM_HEADS = 4
MLSTM_CHUNK = 64
MLSTM_UNROLL = 4
GATE_SOFTCAP = 15.0

N_EXPERTS = 32
N_GROUPS = 4
EXPERTS_PER_GROUP = N_EXPERTS // N_GROUPS
TOP_K = 2
MOE_ROWS = 256

LANES = 128
VMEM_LIMIT = 48 * 1024 * 1024


def _cparams(n_axes):
    return pltpu.CompilerParams(dimension_semantics=("arbitrary",) * n_axes,
                                vmem_limit_bytes=VMEM_LIMIT)


def _dot(a, b):
    return jnp.dot(a, b, preferred_element_type=F32)


def _dot_nt(a, b):
    return lax.dot_general(a, b, (((1,), (1,)), ((), ())), preferred_element_type=F32)


def _dot_tn(a, b):
    return lax.dot_general(a, b, (((0,), (0,)), ((), ())), preferred_element_type=F32)


def _norm_mod(x, gain, sc, sh):
    y = x * lax.rsqrt(jnp.mean(x * x, axis=-1, keepdims=True) + EPS) * gain
    return y * (1.0 + sc) + sh


def _half_norm_rope(x, gain, cos, sin, other_half_zero=False):
    lane = lax.broadcasted_iota(jnp.int32, x.shape, x.ndim - 1)
    x2 = x * x
    if other_half_zero:
        ms = jnp.sum(x2, axis=-1, keepdims=True) * (1.0 / HEAD_DIM)
    else:
        left = lane < HEAD_DIM
        ss_l = jnp.sum(jnp.where(left, x2, 0.0), axis=-1, keepdims=True)
        ss_r = jnp.sum(jnp.where(left, 0.0, x2), axis=-1, keepdims=True)
        ms = jnp.where(left, ss_l, ss_r) * (1.0 / HEAD_DIM)
    y = x * lax.rsqrt(ms + EPS) * gain
    half = ROT_DIM // 2
    nd = x.ndim - 1
    partner = jnp.where((lane % HEAD_DIM) < half,
                        pltpu.roll(y, LANES - half, nd), pltpu.roll(y, half, nd))
    return y * cos + partner * sin


def _rope_tables(pos):
    half = ROT_DIM // 2
    inv_freq = ROPE_THETA ** (-jnp.arange(half, dtype=F32) / half)
    ang = pos.astype(F32)[:, None] * inv_freq[None, :]
    cos, sin = jnp.cos(ang), jnp.sin(ang)
    n = pos.shape[0]
    one = jnp.ones((n, HEAD_DIM - ROT_DIM), F32)
    cos_h = jnp.concatenate([cos, cos, one], axis=-1)
    sin_h = jnp.concatenate([-sin, sin, 0.0 * one], axis=-1)
    return jnp.tile(cos_h, (1, 2)), jnp.tile(sin_h, (1, 2))


def _mod_kernel(c_ref, w_ref, b_ref, o_ref):
    c = c_ref[...]
    cond = c * jax.nn.sigmoid(c)
    o_ref[0] = jnp.dot(cond, w_ref[0], preferred_element_type=F32, precision=HIGHEST) + b_ref[0]


def _mod_call(c, ada_w, ada_b):
    depth, d, n = ada_w.shape
    b = c.shape[0]
    tn = n // 4
    return pl.pallas_call(
        _mod_kernel,
        out_shape=jax.ShapeDtypeStruct((depth, b, n), F32),
        grid=(depth, n // tn),
        in_specs=[pl.BlockSpec((b, d), lambda i, j: (0, 0)),
                  pl.BlockSpec((1, d, tn), lambda i, j: (i, 0, j)),
                  pl.BlockSpec((1, 1, tn), lambda i, j: (i, 0, j))],
        out_specs=pl.BlockSpec((1, b, tn), lambda i, j: (i, 0, j)),
        compiler_params=_cparams(2),
        name="adaln_mod",
    )(c, ada_w, ada_b.reshape(depth, 1, n))


def _nsa_in_kernel(x_ref, gain_ref, sc_ref, sh_ref, wq_ref, wkv_ref, wg_ref, kg_ref, cos_ref, sin_ref,
                   q_ref, kc_ref, vc_ref, ks_ref, vs_ref, kw_ref, vw_ref, g_ref):
    h = _norm_mod(x_ref[0], gain_ref[...], sc_ref[0], sh_ref[0]).astype(BF16)
    q_ref[0] = _dot(h, wq_ref[...])
    g_ref[0] = _dot(h, wg_ref[...])
    kv = _dot(h, wkv_ref[...])
    cos, sin = cos_ref[...], sin_ref[...]
    kc_ref[0] = kv[:, 0 * LANES:1 * LANES]
    vc_ref[0] = kv[:, 1 * LANES:2 * LANES]
    ks_ref[0] = _half_norm_rope(kv[:, 2 * LANES:3 * LANES], kg_ref[1:2, :], cos, sin).astype(BF16)
    vs_ref[0] = kv[:, 3 * LANES:4 * LANES].astype(BF16)
    kw_ref[0] = _half_norm_rope(kv[:, 4 * LANES:5 * LANES], kg_ref[2:3, :], cos, sin).astype(BF16)
    vw_ref[0] = kv[:, 5 * LANES:6 * LANES].astype(BF16)


def _nsa_in_call(x, gain, sc, sh, w_in, k_gain, cos, sin, tm):
    b, s, d = x.shape
    nq = NSA_HEADS * HEAD_DIM
    nkv = 6 * LANES
    wq = w_in[:, :nq].astype(BF16)
    wkv = w_in[:, nq:nq + nkv].astype(BF16)
    ng = NSA_BRANCHES * NSA_HEADS
    wg = jnp.pad(w_in[:, nq + nkv:], ((0, 0), (0, LANES - ng))).astype(BF16)
    kg = jnp.tile(k_gain, (1, 2))
    row = lambda i, j: (i, j, 0)
    per_b = lambda i, j: (i, 0, 0)
    const = lambda i, j: (0, 0)
    kv_out = lambda dt: jax.ShapeDtypeStruct((b, s, LANES), dt)
    return pl.pallas_call(
        _nsa_in_kernel,
        out_shape=(jax.ShapeDtypeStruct((b, s, nq), F32), kv_out(F32), kv_out(F32),
                   kv_out(BF16), kv_out(BF16), kv_out(BF16), kv_out(BF16), kv_out(F32)),
        grid=(b, s // tm),
        in_specs=[pl.BlockSpec((1, tm, d), row),
                  pl.BlockSpec((1, d), const),
                  pl.BlockSpec((1, 1, d), per_b),
                  pl.BlockSpec((1, 1, d), per_b),
                  pl.BlockSpec((d, nq), const),
                  pl.BlockSpec((d, nkv), const),
                  pl.BlockSpec((d, LANES), const),
                  pl.BlockSpec((3, LANES), const),
                  pl.BlockSpec((tm, LANES), lambda i, j: (j, 0)),
                  pl.BlockSpec((tm, LANES), lambda i, j: (j, 0))],
        out_specs=(pl.BlockSpec((1, tm, nq), row),) + (pl.BlockSpec((1, tm, LANES), row),) * 7,
        compiler_params=_cparams(2),
        name="nsa_in_proj",
    )(x, gain.reshape(1, d), sc, sh, wq, wkv, wg, kg, cos, sin)


def _compress_kernel(blk_ref, pe_ref, w1_ref, b1_ref, w2_ref, b2_ref, kg_ref, cos_ref, sin_ref, o_ref):
    is_key = pl.program_id(0) == 0
    blk = (blk_ref[0] + pe_ref[0]).astype(BF16)
    hid = _dot(blk, w1_ref[0]) + b1_ref[0]
    hid = 0.5 * hid * (1.0 + jnp.tanh(np.sqrt(2.0 / np.pi) * (hid + 0.044715 * hid * hid * hid)))
    out = _dot(hid.astype(BF16), w2_ref[0]) + b2_ref[0]
    normed = _half_norm_rope(out, kg_ref[...], cos_ref[...], sin_ref[...])
    o_ref[0] = jnp.where(is_key, normed, out)[:, :HEAD_DIM].astype(o_ref.dtype)


def _compress_call(blocks, pe, w1, b1, w2, b2, k_gain0, cos, sin, rows):
    _, r, kdim = blocks.shape
    hid = w1.shape[-1]
    w2p = jnp.pad(w2, ((0, 0), (0, 0), (0, LANES - HEAD_DIM))).astype(BF16)
    b2p = jnp.pad(b2, ((0, 0), (0, LANES - HEAD_DIM))).reshape(2, 1, LANES)
    kg = jnp.pad(k_gain0, (0, LANES - HEAD_DIM)).reshape(1, LANES)
    sel = lambda i, j: (i, 0, 0)
    const = lambda i, j: (0, 0)
    return pl.pallas_call(
        _compress_kernel,
        out_shape=jax.ShapeDtypeStruct((2, r, HEAD_DIM), BF16),
        grid=(2, r // rows),
        in_specs=[pl.BlockSpec((1, rows, kdim), lambda i, j: (i, j, 0)),
                  pl.BlockSpec((1, 1, kdim), sel),
                  pl.BlockSpec((1, kdim, hid), sel),
                  pl.BlockSpec((1, 1, hid), sel),
                  pl.BlockSpec((1, hid, LANES), sel),
                  pl.BlockSpec((1, 1, LANES), sel),
                  pl.BlockSpec((1, LANES), const),
                  pl.BlockSpec((rows, LANES), const),
                  pl.BlockSpec((rows, LANES), const)],
        out_specs=pl.BlockSpec((1, rows, HEAD_DIM), lambda i, j: (i, j, 0)),
        compiler_params=_cparams(2),
        name="nsa_compress",
    )(blocks, pe.reshape(2, 1, kdim), w1.astype(BF16), b1.reshape(2, 1, hid), w2p, b2p, kg, cos, sin)


def _attend(qb, k_ref, v_ref, k0, spans, g, bias):
    rows = qb.shape[0]
    m = acc = None
    for off, size in spans:
        k = k_ref[0, pl.ds(k0 + off, size), :]
        v = v_ref[0, pl.ds(k0 + off, size), :]
        s = _dot_nt(qb, k).reshape(rows // NSA_Q_BLOCK, NSA_Q_BLOCK, size) + bias[None, :, off:off + size]
        s = s.reshape(rows, size)
        m_span = jnp.max(s, axis=-1, keepdims=True)
        m_new = m_span if m is None else jnp.maximum(m, m_span)
        p = jnp.exp2(s - m_new).astype(BF16)
        v_lane = lax.broadcasted_iota(jnp.int32, v.shape, 1)
        pv = _dot(p, jnp.where((v_lane // HEAD_DIM) == g, v, jnp.ones_like(v)))
        acc = pv if m is None else acc * jnp.exp2(m - m_new) + pv
        m = m_new
    return acc


def _normalise(acc, g):
    c = (1 - g) * HEAD_DIM
    return acc / acc[:, c:c + 1]


def _nsa_attn_kernel(q_ref, g_ref, qg_ref, cos_ref, sin_ref, kc_ref, vc_ref, ks_ref, vs_ref, kw_ref, vw_ref,
                     c2s_ref, exp_ref, o_ref, *, seq, n_cmp, n_top, win_keys, sel_span, n_spans, q_block0):
    R, QB = NSA_HEADS_PER_GROUP, NSA_Q_BLOCK
    rows = R * QB
    n_sel = seq // SEL_BLOCK
    qi = pl.program_id(1) + q_block0
    s0 = qi * QB
    qt = q_ref[0]
    gt = jax.nn.sigmoid(g_ref[0])
    lane = lax.broadcasted_iota(jnp.int32, (QB, LANES), 1)
    tq = s0 + lax.broadcasted_iota(jnp.int32, (rows, 1), 0) % QB
    tq1 = s0 + lax.broadcasted_iota(jnp.int32, (QB, 1), 0)
    cos = jnp.concatenate([cos_ref[...]] * R, axis=0)
    sin = jnp.concatenate([sin_ref[...]] * R, axis=0)

    w0 = pl.multiple_of(jnp.maximum(s0 + QB - win_keys, 0), SEL_BLOCK)
    wpos = w0 + lax.broadcasted_iota(jnp.int32, (1, win_keys), 1)
    bias_w = jnp.where((wpos <= tq1) & (wpos > tq1 - WINDOW), 0.0, NEG)
    win_spans = [(off, min(2 * LANES, win_keys - off)) for off in range(0, win_keys, 2 * LANES)]

    qbs, psums, o_cs, o_ws = [], [], [], []
    for g in range(NSA_KV_GROUPS):
        in_g = (lane // HEAD_DIM) == g
        slabs = []
        for r in range(R):
            pair = (g * R + r) // 2
            slab = qt[:, pair * LANES:(pair + 1) * LANES]
            if r % 2 != g:
                slab = pltpu.roll(slab, HEAD_DIM, 1)
            slabs.append(jnp.where(in_g, slab, 0.0))
        q2 = jnp.concatenate(slabs, axis=0)
        q2 = _half_norm_rope(q2, qg_ref[...], cos, sin, other_half_zero=True) * (HEAD_DIM ** -0.5 * LOG2E)
        qb = q2.astype(BF16)

        sc = _dot_nt(qb, kc_ref[0])
        cpos = lax.broadcasted_iota(jnp.int32, (1, LANES), 1) * CMP_STRIDE + (CMP_BLOCK - 1)
        valid_c = (cpos <= tq) & (lax.broadcasted_iota(jnp.int32, (1, LANES), 1) < n_cmp)
        sc = jnp.where(valid_c, sc, NEG)
        e_c = jnp.exp2(sc - jnp.max(sc, axis=-1, keepdims=True))
        p_c = jnp.where(valid_c, e_c / jnp.sum(e_c, axis=-1, keepdims=True), 0.0)
        o_c = _dot(p_c.astype(BF16), vc_ref[0])

        psums.append(jnp.sum(p_c.reshape(R, QB, LANES), axis=0))

        o_ws.append(_normalise(_attend(qb, kw_ref, vw_ref, w0, win_spans, g, bias_w), g))
        qbs.append(qb)
        o_cs.append(o_c)

    imp = lax.dot_general(c2s_ref[...], jnp.concatenate(psums, axis=0), (((1,), (1,)), ((), ())),
                          preferred_element_type=F32, precision=HIGHEST)
    blk = lax.broadcasted_iota(jnp.int32, (n_sel, NSA_KV_GROUPS * QB), 0)
    forced = (blk == 0) | (blk == qi) | (blk == qi - 1)
    imp = jnp.where(blk <= qi, jnp.where(forced, BIG, imp), -BIG)
    beaten = jnp.zeros(imp.shape, F32)
    for k in range(1, n_sel):
        other = pltpu.roll(imp, k, 0)
        beats = (other > imp) | ((blk >= k) & (other == imp))
        beaten = beaten + jnp.where(beats, 1.0, 0.0)
    chosen = jnp.where(beaten < n_top, 1.0, 0.0).astype(BF16)
    n_keys = n_spans * sel_span
    picked = _dot_tn(chosen, exp_ref[:, :n_keys])
    kpos = lax.broadcasted_iota(jnp.int32, (1, n_keys), 1)
    sel_spans = [(c * sel_span, sel_span) for c in range(n_spans)]

    for g in range(NSA_KV_GROUPS):
        bias_s = jnp.where((picked[g * QB:(g + 1) * QB] > 0.5) & (kpos <= tq1), 0.0, NEG)
        o_s = _normalise(_attend(qbs[g], ks_ref, vs_ref, 0, sel_spans, g, bias_s), g)
        o_c, o_w = o_cs[g], o_ws[g]
        heads = []
        for r in range(R):
            h = g * R + r
            rs = slice(r * QB, (r + 1) * QB)
            o_h = (gt[:, h:h + 1] * o_c[rs]
                   + gt[:, NSA_HEADS + h:NSA_HEADS + h + 1] * o_s[rs]
                   + gt[:, 2 * NSA_HEADS + h:2 * NSA_HEADS + h + 1] * o_w[rs])
            if r % 2 != g:
                o_h = pltpu.roll(o_h, HEAD_DIM, 1)
            heads.append(o_h)
        for k in range(R // 2):
            pair = g * (R // 2) + k
            slab = jnp.where(lane < HEAD_DIM, heads[2 * k], heads[2 * k + 1])
            o_ref[0, :, pair * LANES:(pair + 1) * LANES] = slab.astype(o_ref.dtype)


def _nsa_attn_call(q, gates, q_gain, cos, sin, kc, vc, ks, vs, kw, vw, cmp_to_sel, n_cmp):
    b, s, nq = q.shape
    qb = NSA_Q_BLOCK
    n_top = min(SEL_TOPN, s // SEL_BLOCK)
    win_keys = min(WINDOW + 2 * qb, s)
    sel_span = min(512, s)
    n_sel = s // SEL_BLOCK
    qg = jnp.tile(q_gain, 2).reshape(1, LANES)
    expand = jnp.asarray(np.arange(n_sel)[:, None] == (np.arange(s)[None, :] // SEL_BLOCK), BF16)
    per_b = lambda i, j: (i, 0, 0)
    const = lambda i, j: (0, 0)
    per_call = sel_span // qb
    outs = []
    for n in range(1, s // sel_span + 1):
        q0 = (n - 1) * per_call
        row = lambda i, j, q0=q0: (i, j + q0, 0)
        tab = lambda i, j, q0=q0: (j + q0, 0)
        kern = functools.partial(_nsa_attn_kernel, seq=s, n_cmp=n_cmp, n_top=n_top, win_keys=win_keys,
                                 sel_span=sel_span, n_spans=n, q_block0=q0)
        outs.append(pl.pallas_call(
            kern,
            out_shape=jax.ShapeDtypeStruct((b, sel_span, nq), BF16),
            grid=(b, per_call),
            in_specs=[pl.BlockSpec((1, qb, nq), row),
                      pl.BlockSpec((1, qb, LANES), row),
                      pl.BlockSpec((1, LANES), const),
                      pl.BlockSpec((qb, LANES), tab),
                      pl.BlockSpec((qb, LANES), tab),
                      pl.BlockSpec((1, LANES, LANES), per_b),
                      pl.BlockSpec((1, LANES, LANES), per_b),
                      pl.BlockSpec((1, n * sel_span, LANES), per_b),
                      pl.BlockSpec((1, n * sel_span, LANES), per_b),
                      pl.BlockSpec((1, s, LANES), per_b),
                      pl.BlockSpec((1, s, LANES), per_b),
                      pl.BlockSpec((n_sel, LANES), const),
                      pl.BlockSpec((n_sel, s), const)],
            out_specs=pl.BlockSpec((1, qb, nq), lambda i, j: (i, j, 0)),
            compiler_params=_cparams(2),
            name="nsa_attention",
        )(q, gates, qg, cos, sin, kc, vc, ks, vs, kw, vw, cmp_to_sel, expand))
    return jnp.concatenate(outs, axis=1)


def _nsa_out_kernel(a_ref, w_ref, x_ref, g_ref, o_ref):
    o_ref[0] = x_ref[0] + g_ref[0] * _dot(a_ref[0], w_ref[...])


def _mlstm_out_kernel(a_ref, og_ref, w_ref, x_ref, g_ref, o_ref):
    lhs = (jax.nn.sigmoid(og_ref[0].astype(F32)) * a_ref[0].astype(F32)).astype(BF16)
    o_ref[0] = x_ref[0] + g_ref[0] * _dot(lhs, w_ref[...])


def _out_proj_call(a, og, w_out, x, g, tm):
    b, s, d = x.shape
    k = a.shape[-1]
    row = lambda i, j: (i, j, 0)
    per_b = lambda i, j: (i, 0, 0)
    a_spec = pl.BlockSpec((1, tm, k), row)
    tail = [pl.BlockSpec((k, d), lambda i, j: (0, 0)), pl.BlockSpec((1, tm, d), row),
            pl.BlockSpec((1, 1, d), per_b)]
    if og is None:
        kern, ins, args = _nsa_out_kernel, [a_spec] + tail, (a, w_out.astype(BF16), x, g)
    else:
        kern, ins, args = _mlstm_out_kernel, [a_spec, a_spec] + tail, (a, og, w_out.astype(BF16), x, g)
    return pl.pallas_call(
        kern,
        out_shape=jax.ShapeDtypeStruct((b, s, d), F32),
        grid=(b, s // tm),
        in_specs=ins,
        out_specs=pl.BlockSpec((1, tm, d), row),
        compiler_params=_cparams(2),
        name="mixer_out_proj",
    )(*args)


def _mlstm_in_kernel(x_ref, gain_ref, sc_ref, sh_ref, wq_ref, wk_ref, wv_ref, wo_ref, wg_ref,
                     q_ref, k_ref, v_ref, o_ref, g_ref):
    h = _norm_mod(x_ref[0], gain_ref[...], sc_ref[0], sh_ref[0]).astype(BF16)
    q_ref[0] = _dot(h, wq_ref[...]).astype(BF16)
    k_ref[0] = _dot(h, wk_ref[...]).astype(BF16)
    v_ref[0] = _dot(h, wv_ref[...]).astype(BF16)
    o_ref[0] = _dot(h, wo_ref[...]).astype(BF16)
    g_ref[0] = _dot(h, wg_ref[...])


def _mlstm_in_call(x, gain, sc, sh, w_in, dqk, dv, tm):
    b, s, d = x.shape
    nh = MLSTM_HEADS
    sizes = [nh * dqk, nh * dqk, nh * dv, nh * dv]
    offs = np.cumsum([0] + sizes)
    ws = [w_in[:, offs[i]:offs[i + 1]].astype(BF16) for i in range(4)]
    wg = jnp.pad(w_in[:, offs[4]:], ((0, 0), (0, LANES - 2 * nh))).astype(BF16)
    row = lambda i, j: (i, j, 0)
    per_b = lambda i, j: (i, 0, 0)
    const = lambda i, j: (0, 0)
    widths = sizes + [LANES]
    return pl.pallas_call(
        _mlstm_in_kernel,
        out_shape=tuple(jax.ShapeDtypeStruct((b, s, n), BF16) for n in sizes)
        + (jax.ShapeDtypeStruct((b, s, LANES), F32),),
        grid=(b, s // tm),
        in_specs=[pl.BlockSpec((1, tm, d), row),
                  pl.BlockSpec((1, d), const),
                  pl.BlockSpec((1, 1, d), per_b),
                  pl.BlockSpec((1, 1, d), per_b)] + [pl.BlockSpec((d, n), const) for n in widths],
        out_specs=tuple(pl.BlockSpec((1, tm, n), row) for n in widths),
        compiler_params=_cparams(2),
        name="mlstm_in_proj",
    )(x, gain.reshape(1, d), sc, sh, *ws, wg)


def _softcap(a):
    return GATE_SOFTCAP * jnp.tanh(a / GATE_SOFTCAP)


def _mlstm_kernel(bi_ref, bf_ref, q_ref, k_ref, v_ref, ig_ref, fg_ref, gain_ref, o_ref,
                  li_s, b_s, *, n_chunks, dqk, dv):
    L = MLSTM_CHUNK
    h = pl.program_id(1)
    li_s[...] = _softcap(ig_ref[0, 0] + bi_ref[h])
    fa = _softcap(fg_ref[0, 0] + bf_ref[h])
    lf = jnp.minimum(fa, 0.0) - jnp.log1p(jnp.exp(-jnp.abs(fa)))
    r_i = lax.broadcasted_iota(jnp.int32, (L, L), 0)
    c_i = lax.broadcasted_iota(jnp.int32, (L, L), 1)
    upper = jnp.where(r_i <= c_i, 1.0, 0.0)
    b_s[...] = jnp.dot(lf, upper, preferred_element_type=F32, precision=HIGHEST)
    eye = r_i == c_i
    causal = r_i >= c_i
    gain = gain_ref[0]
    k_scale = dqk ** -0.5

    def to_col(row):
        return jnp.sum(jnp.where(eye, jnp.broadcast_to(row, (L, L)), 0.0), axis=1, keepdims=True)

    def body(c, carry):
        state, n_row, m_prev = carry
        r0 = pl.multiple_of(c * L, L)
        qcb = q_ref[0, pl.ds(r0, L), :]
        kc = k_ref[0, pl.ds(r0, L), :].astype(F32) * k_scale
        vc = v_ref[0, pl.ds(r0, L), :]
        b_row = b_s[pl.ds(c, 1), :]
        li_row = li_s[pl.ds(c, 1), :]
        b_col, li_col = to_col(b_row), to_col(li_row)
        b_last = b_row[:, L - 1:L]
        dmat = jnp.where(causal, b_col - b_row + li_row, NEG)
        m_loc = jnp.max(dmat, axis=-1, keepdims=True)
        a_loc = jnp.exp(dmat - m_loc) * _dot_nt(qcb, kc.astype(BF16))
        num_loc = _dot(a_loc.astype(BF16), vc)
        den_loc = jnp.sum(a_loc, axis=-1, keepdims=True)
        g_max = m_loc[L - 1:L, :]
        kw = kc * jnp.exp(b_last - b_col + li_col - g_max)
        kv = _dot_tn(kw.astype(BF16), vc)
        kn = jnp.sum(kw, axis=0, keepdims=True)
        m_inter = b_col + m_prev
        m_t = jnp.maximum(m_inter, m_loc)
        intra = jnp.exp(m_loc - m_t)
        inter = jnp.exp(m_inter - m_t)
        num = intra * num_loc + inter * _dot(qcb, state.astype(BF16))
        den = intra * den_loc + inter * jnp.sum(qcb.astype(F32) * n_row, axis=-1, keepdims=True)
        h_out = num / jnp.maximum(jnp.abs(den), jnp.exp(-m_t))
        hs = h_out * lax.rsqrt(jnp.mean(h_out * h_out, axis=-1, keepdims=True) + EPS) * gain
        o_ref[0, pl.ds(r0, L), :] = hs.astype(o_ref.dtype)
        m_new = jnp.maximum(b_last + m_prev, g_max)
        decay = jnp.exp(b_last + m_prev - m_new)
        grow = jnp.exp(g_max - m_new)
        return decay * state + grow * kv, decay * n_row + grow * kn, m_new

    init = (jnp.zeros((dqk, dv), F32), jnp.zeros((1, dqk), F32), jnp.zeros((1, 1), F32))
    lax.fori_loop(0, n_chunks, body, init, unroll=MLSTM_UNROLL)


def _mlstm_call(q, k, v, gates, b_igate, b_fgate, norm_gain):
    b, s, _ = q.shape
    nh = MLSTM_HEADS
    dqk, dv = q.shape[-1] // nh, v.shape[-1] // nh
    L = MLSTM_CHUNK
    nch = s // L
    g = jnp.transpose(gates[..., :2 * nh], (0, 2, 1)).reshape(b, 2 * nh, nch, L)
    smem = pl.BlockSpec(memory_space=pltpu.SMEM)
    head_cols = lambda n: pl.BlockSpec((1, s, n), lambda i, j: (i, 0, j))
    kern = functools.partial(_mlstm_kernel, n_chunks=nch, dqk=dqk, dv=dv)
    return pl.pallas_call(
        kern,
        out_shape=jax.ShapeDtypeStruct((b, s, nh * dv), BF16),
        grid=(b, nh),
        in_specs=[smem, smem, head_cols(dqk), head_cols(dqk), head_cols(dv),
                  pl.BlockSpec((1, 1, nch, L), lambda i, j: (i, j, 0, 0)),
                  pl.BlockSpec((1, 1, nch, L), lambda i, j: (i, j + nh, 0, 0)),
                  pl.BlockSpec((1, 1, dv), lambda i, j: (j, 0, 0))],
        out_specs=head_cols(dv),
        scratch_shapes=[pltpu.VMEM((nch, L), F32), pltpu.VMEM((nch, L), F32)],
        compiler_params=_cparams(2),
        name="mlstm_chunk_scan",
    )(b_igate, b_fgate, q, k, v, g, g, norm_gain.reshape(nh, 1, dv))


def _router_kernel(x_ref, gain_ref, sc_ref, sh_ref, wh_ref, wl_ref, rb_ref, h_ref, ri_ref, rw_ref, cnt_ref):
    h = _norm_mod(x_ref[0], gain_ref[...], sc_ref[0], sh_ref[0])
    hi = h.astype(BF16)
    lo = (h - hi.astype(F32)).astype(BF16)
    logits = _dot(hi, wh_ref[...]) + (_dot(lo, wh_ref[...]) + _dot(hi, wl_ref[...]))
    h_ref[0] = hi
    aff = jax.nn.sigmoid(logits)
    tm = aff.shape[0]
    lane = lax.broadcasted_iota(jnp.int32, (tm, LANES), 1)
    choice = jnp.where(lane < N_EXPERTS, aff + rb_ref[...], -jnp.inf)

    def first_max(v):
        m = jnp.max(v, axis=-1, keepdims=True)
        return m, jnp.min(jnp.where(v == m, lane, LANES), axis=-1, keepdims=True)

    best = e0 = e1 = None
    for g in range(N_GROUPS):
        cg = jnp.where((lane // EXPERTS_PER_GROUP) == g, choice, -jnp.inf)
        m1, i1 = first_max(cg)
        m2, i2 = first_max(jnp.where(lane == i1, -jnp.inf, cg))
        score = m1 + m2
        if g == 0:
            best, e0, e1 = score, i1, i2
        else:
            better = score > best
            best = jnp.where(better, score, best)
            e0 = jnp.where(better, i1, e0)
            e1 = jnp.where(better, i2, e1)
    is0, is1 = lane == e0, lane == e1
    a0 = jnp.sum(jnp.where(is0, aff, 0.0), axis=-1, keepdims=True)
    a1 = jnp.sum(jnp.where(is1, aff, 0.0), axis=-1, keepdims=True)
    tot = a0 + a1
    rw_ref[0] = jnp.where(lane == 0, a0 / tot, jnp.where(lane == 1, a1 / tot, 0.0))
    onehot = jnp.where(is0 | is1, 1.0, 0.0).astype(BF16)
    r_i = lax.broadcasted_iota(jnp.int32, (tm, tm), 0)
    c_i = lax.broadcasted_iota(jnp.int32, (tm, tm), 1)
    running = _dot(jnp.where(r_i >= c_i, 1.0, 0.0).astype(BF16), onehot)
    r0 = jnp.sum(jnp.where(is0, running, 0.0), axis=-1, keepdims=True) - 1.0
    r1 = jnp.sum(jnp.where(is1, running, 0.0), axis=-1, keepdims=True) - 1.0
    ri_ref[0] = jnp.where(lane == 0, e0, jnp.where(lane == 1, e1, jnp.where(
        lane == 2, r0.astype(jnp.int32), jnp.where(lane == 3, r1.astype(jnp.int32), 0))))
    cnt_ref[0] = jnp.broadcast_to(running[tm - 1:tm, :], (8, LANES))


def _router_call(x, gain, sc, sh, router_w, router_b, tm):
    b, s, d = x.shape
    nt = s // tm
    wp = jnp.pad(router_w, ((0, 0), (0, LANES - N_EXPERTS)))
    wh = wp.astype(BF16)
    wl = (wp - wh.astype(F32)).astype(BF16)
    rb = jnp.pad(router_b.astype(F32), (0, LANES - N_EXPERTS)).reshape(1, LANES)
    row = lambda i, j: (i, j, 0)
    per_b = lambda i, j: (i, 0, 0)
    const = lambda i, j: (0, 0)
    return pl.pallas_call(
        _router_kernel,
        out_shape=(jax.ShapeDtypeStruct((b, s, d), BF16), jax.ShapeDtypeStruct((b, s, LANES), jnp.int32),
                   jax.ShapeDtypeStruct((b, s, LANES), F32), jax.ShapeDtypeStruct((b * nt, 8, LANES), F32)),
        grid=(b, nt),
        in_specs=[pl.BlockSpec((1, tm, d), row),
                  pl.BlockSpec((1, d), const),
                  pl.BlockSpec((1, 1, d), per_b),
                  pl.BlockSpec((1, 1, d), per_b),
                  pl.BlockSpec((d, LANES), const),
                  pl.BlockSpec((d, LANES), const),
                  pl.BlockSpec((1, LANES), const)],
        out_specs=(pl.BlockSpec((1, tm, d), row), pl.BlockSpec((1, tm, LANES), row),
                   pl.BlockSpec((1, tm, LANES), row), pl.BlockSpec((1, 8, LANES), lambda i, j: (i * nt + j, 0, 0))),
        compiler_params=_cparams(2),
        name="moe_router",
    )(x, gain.reshape(1, d), sc, sh, wh, wl, rb)


def _expert_kernel(be_ref, live_ref, x_ref, wg_ref, wu_ref, wd_ref, o_ref, wg_s, wu_s, wd_s):
    i = pl.program_id(0)
    fresh = (i == 0) | (be_ref[i] != be_ref[jnp.maximum(i - 1, 0)])

    @pl.when(fresh)
    def _():
        wg_s[...] = wg_ref[0, 0].astype(BF16)
        wu_s[...] = wu_ref[0, 0].astype(BF16)
        wd_s[...] = wd_ref[0, 0].astype(BF16)

    @pl.when(live_ref[i] == 1)
    def _():
        xb = x_ref[...]
        gate = _dot(xb, wg_s[...])
        hid = gate * jax.nn.sigmoid(gate) * _dot(xb, wu_s[...])
        o_ref[...] = _dot(hid.astype(BF16), wd_s[...]).astype(o_ref.dtype)

    @pl.when(live_ref[i] == 0)
    def _():
        o_ref[...] = jnp.zeros_like(o_ref)


def _expert_call(block_expert, block_live, xs, w_gate, w_up, w_down, layer):
    p, d = xs.shape
    de = w_gate.shape[-1]
    nb = p // MOE_ROWS
    grid_spec = pltpu.PrefetchScalarGridSpec(
        num_scalar_prefetch=2,
        grid=(nb,),
        in_specs=[pl.BlockSpec((MOE_ROWS, d), lambda i, be, lv: (i, 0)),
                  pl.BlockSpec((1, 1, d, de), lambda i, be, lv: (layer, be[i], 0, 0)),
                  pl.BlockSpec((1, 1, d, de), lambda i, be, lv: (layer, be[i], 0, 0)),
                  pl.BlockSpec((1, 1, de, d), lambda i, be, lv: (layer, be[i], 0, 0))],
        out_specs=pl.BlockSpec((MOE_ROWS, d), lambda i, be, lv: (i, 0)),
        scratch_shapes=[pltpu.VMEM((d, de), BF16), pltpu.VMEM((d, de), BF16), pltpu.VMEM((de, d), BF16)],
    )
    return pl.pallas_call(
        _expert_kernel,
        out_shape=jax.ShapeDtypeStruct((p, d), F32),
        grid_spec=grid_spec,
        compiler_params=_cparams(1),
        name="moe_experts",
    )(block_expert, block_live, xs, w_gate, w_up, w_down)


def _combine_kernel(x_ref, g_ref, rw_ref, ya_ref, yb_ref, o_ref):
    rw = rw_ref[0]
    y = rw[:, 0:1] * ya_ref[0].astype(F32) + rw[:, 1:2] * yb_ref[0].astype(F32)
    o_ref[0] = x_ref[0] + g_ref[0] * y


def _combine_call(x, g, rw, ya, yb, tm):
    b, s, d = x.shape
    row = lambda i, j: (i, j, 0)
    spec = pl.BlockSpec((1, tm, d), row)
    return pl.pallas_call(
        _combine_kernel,
        out_shape=jax.ShapeDtypeStruct((b, s, d), F32),
        grid=(b, s // tm),
        in_specs=[spec, pl.BlockSpec((1, 1, d), lambda i, j: (i, 0, 0)),
                  pl.BlockSpec((1, tm, LANES), row), spec, spec],
        out_specs=spec,
        compiler_params=_cparams(2),
        name="moe_combine",
    )(x, g, rw, ya, yb)


def _row_layout(ri, cnt, tm):
    t = ri.shape[0]
    nt = t // tm
    lanes = jnp.arange(N_EXPERTS, dtype=jnp.int32)
    tile_cnt = cnt[:, 0, :N_EXPERTS].astype(jnp.int32)
    tile_off = jnp.cumsum(tile_cnt, axis=0) - tile_cnt
    counts = jnp.sum(tile_cnt, axis=0)
    padded = (counts + MOE_ROWS - 1) // MOE_ROWS * MOE_ROWS
    p_ends = jnp.cumsum(padded)
    base = (p_ends - padded)[None, :] + tile_off
    e = ri[:, 0:TOP_K].reshape(nt, tm, TOP_K)
    rank = ri[:, TOP_K:2 * TOP_K].reshape(nt, tm, TOP_K)
    dest = jnp.sum(jnp.where(e[..., None] == lanes, base[:, None, None, :], 0), axis=-1) + rank
    dest = dest.reshape(t, TOP_K)
    nb = (t * TOP_K) // MOE_ROWS + N_EXPERTS
    tok = jnp.repeat(jnp.arange(t, dtype=jnp.int32), TOP_K)
    buf_tok = jnp.zeros((nb * MOE_ROWS,), jnp.int32).at[dest.reshape(-1)].set(tok)
    block_start = jnp.arange(nb, dtype=jnp.int32) * MOE_ROWS
    block_expert = jnp.minimum(jnp.sum((p_ends[None, :] <= block_start[:, None]).astype(jnp.int32), axis=-1),
                               N_EXPERTS - 1)
    block_live = (block_start < p_ends[-1]).astype(jnp.int32)
    return dest, buf_tok, block_expert, block_live


def _moe_layer(x, gain, sc, sh, g, router_w, router_b, w_gate, w_up, w_down, layer, tm):
    b, s, d = x.shape
    t = b * s
    hf, ri, rw, cnt = _router_call(x, gain, sc, sh, router_w, router_b, tm)
    dest, buf_tok, block_expert, block_live = _row_layout(ri.reshape(t, LANES), cnt, tm)
    xs = jnp.take(hf.reshape(t, d), buf_tok, axis=0)
    out = _expert_call(block_expert, block_live, xs, w_gate, w_up, w_down, layer)
    ya = jnp.take(out, dest[:, 0], axis=0).reshape(b, s, d)
    yb = jnp.take(out, dest[:, 1], axis=0).reshape(b, s, d)
    return _combine_call(x, g, rw, ya, yb, tm)


def _nsa_layer(x, gain, sc, sh, g, w_in, w_out, q_gain, k_gain, cmp_pe, cmp_w1, cmp_b1, cmp_w2, cmp_b2, tm):
    b, s, d = x.shape
    G, dh = NSA_KV_GROUPS, HEAD_DIM
    cos, sin = _rope_tables(jnp.arange(s, dtype=jnp.int32))
    q, kc_raw, vc_raw, ks, vs, kw, vw, gates = _nsa_in_call(x, gain, sc, sh, w_in, k_gain, cos, sin, tm)

    n_cmp = (s - CMP_BLOCK) // CMP_STRIDE + 1
    n_str = s // CMP_STRIDE
    per_stride = CMP_STRIDE * dh

    def to_blocks(a):
        a = a.reshape(b, n_str, CMP_STRIDE, G, dh).transpose(0, 1, 3, 2, 4).reshape(b, n_str, G, per_stride)
        nxt = jnp.concatenate([a[:, 1:], jnp.zeros_like(a[:, :1])], axis=1)
        return jnp.concatenate([a, nxt], axis=-1).reshape(b * n_str * G, 2 * per_stride)

    blocks = jnp.stack([to_blocks(kc_raw), to_blocks(vc_raw)], axis=0)
    rows = n_str * G
    cmp_pos = (jnp.arange(rows, dtype=jnp.int32) // G) * CMP_STRIDE + (CMP_BLOCK - 1)
    ccos, csin = _rope_tables(cmp_pos)
    cmp = _compress_call(blocks, cmp_pe.reshape(2, CMP_BLOCK * dh), cmp_w1, cmp_b1, cmp_w2, cmp_b2,
                         k_gain[0], ccos, csin, rows)
    cmp = cmp.reshape(2, b, n_str, G * dh)
    cmp = jnp.pad(cmp, ((0, 0), (0, 0), (0, LANES - n_str), (0, 0)))
    kc, vc = cmp[0], cmp[1]

    ns = s // SEL_BLOCK
    r_, u_ = SEL_BLOCK // CMP_STRIDE, CMP_BLOCK // CMP_STRIDE
    c_idx = (r_ * np.arange(ns)[:, None, None] + np.arange(r_)[None, :, None]
             + np.arange(u_)[None, None, :]).reshape(ns, -1)
    c2s = (c_idx[:, :, None] == np.arange(n_cmp)[None, None, :]).sum(1).astype(np.float32)
    c2s = jnp.asarray(np.pad(c2s, ((0, 0), (0, LANES - n_cmp))))

    o = _nsa_attn_call(q, gates, q_gain, cos, sin, kc, vc, ks, vs, kw, vw, c2s, n_cmp)
    return _out_proj_call(o, None, w_out, x, g, tm)


def _mlstm_layer(x, gain, sc, sh, g, w_in, w_out, b_igate, b_fgate, norm_gain, tm):
    nh = MLSTM_HEADS
    dv = norm_gain.shape[-1]
    dqk = (w_in.shape[-1] - 2 * nh - 2 * nh * dv) // (2 * nh)
    q, k, v, og, gates = _mlstm_in_call(x, gain, sc, sh, w_in, dqk, dv, tm)
    hs = _mlstm_call(q, k, v, gates, b_igate, b_fgate, norm_gain)
    return _out_proj_call(hs, og, w_out, x, g, tm)


def kernel(x, c, ada_w, ada_b, norm_mix_gain, norm_ffn_gain, nsa_w_in, nsa_w_out, nsa_q_gain, nsa_k_gain, nsa_cmp_pe, nsa_cmp_w1, nsa_cmp_b1, nsa_cmp_w2, nsa_cmp_b2, mlstm_w_in, mlstm_b_igate, mlstm_b_fgate, mlstm_norm_gain, mlstm_w_out, router_w, router_b, moe_w_gate, moe_w_up, moe_w_down):
    b, s, d = x.shape
    depth = ada_w.shape[0]
    tm = min(512, s)
    mod = _mod_call(c, ada_w, ada_b)
    for i in range(depth):
        sh_m, sc_m, g_m, sh_f, sc_f, g_f = [mod[i, :, None, k * d:(k + 1) * d] for k in range(6)]
        j = i // 2
        if i % 2 == 0:
            x = _nsa_layer(x, norm_mix_gain[i], sc_m, sh_m, g_m, nsa_w_in[j], nsa_w_out[j], nsa_q_gain[j],
                           nsa_k_gain[j], nsa_cmp_pe[j], nsa_cmp_w1[j], nsa_cmp_b1[j], nsa_cmp_w2[j],
                           nsa_cmp_b2[j], tm)
        else:
            x = _mlstm_layer(x, norm_mix_gain[i], sc_m, sh_m, g_m, mlstm_w_in[j], mlstm_w_out[j],
                             mlstm_b_igate[j], mlstm_b_fgate[j], mlstm_norm_gain[j], tm)
        x = _moe_layer(x, norm_ffn_gain[i], sc_f, sh_f, g_f, router_w, router_b,
                       moe_w_gate, moe_w_up, moe_w_down, i, tm)
    return x
```

```python
import functools

import numpy as np
import jax
import jax.numpy as jnp
from jax import lax
from jax.experimental import pallas as pl
from jax.experimental.pallas import tpu as pltpu

F32 = jnp.float32
BF16 = jnp.bfloat16
HIGHEST = lax.Precision.HIGHEST

EPS = 1e-6
NEG = -1e30
BIG = 1e9
ROPE_THETA = 500000.0
LOG2E = 1.4426950408889634

NSA_HEADS = 16
NSA_KV_GROUPS = 2
NSA_HEADS_PER_GROUP = NSA_HEADS // NSA_KV_GROUPS
HEAD_DIM = 64
ROT_DIM = HEAD_DIM // 4
CMP_BLOCK = 32
CMP_STRIDE = 16
SEL_BLOCK = 64
SEL_TOPN = 8
WINDOW = 512
NSA_Q_BLOCK = 64
NSA_BRANCHES = 3

MLSTM_HEADS = 4
MLSTM_CHUNK = 64
MLSTM_GROUP = 4
GATE_SOFTCAP = 15.0

N_EXPERTS = 32
N_GROUPS = 4
EXPERTS_PER_GROUP = N_EXPERTS // N_GROUPS
TOP_K = 2
MOE_ROWS = 256

LANES = 128
VMEM_LIMIT = 48 * 1024 * 1024


def _cparams(n_axes):
    return pltpu.CompilerParams(dimension_semantics=("arbitrary",) * n_axes,
                                vmem_limit_bytes=VMEM_LIMIT)


def _dot(a, b):
    return jnp.dot(a, b, preferred_element_type=F32)


def _dot_nt(a, b):
    return lax.dot_general(a, b, (((1,), (1,)), ((), ())), preferred_element_type=F32)


def _dot_tn(a, b):
    return lax.dot_general(a, b, (((0,), (0,)), ((), ())), preferred_element_type=F32)


def _norm_mod(x, gain, sc, sh):
    y = x * lax.rsqrt(jnp.mean(x * x, axis=-1, keepdims=True) + EPS) * gain
    return y * (1.0 + sc) + sh


def _half_norm_rope(x, gain, cos, sin, other_half_zero=False):
    lane = lax.broadcasted_iota(jnp.int32, x.shape, x.ndim - 1)
    x2 = x * x
    if other_half_zero:
        ms = jnp.sum(x2, axis=-1, keepdims=True) * (1.0 / HEAD_DIM)
    else:
        left = lane < HEAD_DIM
        ss_l = jnp.sum(jnp.where(left, x2, 0.0), axis=-1, keepdims=True)
        ss_r = jnp.sum(jnp.where(left, 0.0, x2), axis=-1, keepdims=True)
        ms = jnp.where(left, ss_l, ss_r) * (1.0 / HEAD_DIM)
    y = x * lax.rsqrt(ms + EPS) * gain
    half = ROT_DIM // 2
    nd = x.ndim - 1
    partner = jnp.where((lane % HEAD_DIM) < half,
                        pltpu.roll(y, LANES - half, nd), pltpu.roll(y, half, nd))
    return y * cos + partner * sin


def _rope_tables(pos):
    half = ROT_DIM // 2
    inv_freq = ROPE_THETA ** (-jnp.arange(half, dtype=F32) / half)
    ang = pos.astype(F32)[:, None] * inv_freq[None, :]
    cos, sin = jnp.cos(ang), jnp.sin(ang)
    n = pos.shape[0]
    one = jnp.ones((n, HEAD_DIM - ROT_DIM), F32)
    cos_h = jnp.concatenate([cos, cos, one], axis=-1)
    sin_h = jnp.concatenate([-sin, sin, 0.0 * one], axis=-1)
    return jnp.tile(cos_h, (1, 2)), jnp.tile(sin_h, (1, 2))


def _mod_kernel(c_ref, w_ref, b_ref, o_ref):
    c = c_ref[...]
    cond = c * jax.nn.sigmoid(c)
    o_ref[0] = jnp.dot(cond, w_ref[0], preferred_element_type=F32, precision=HIGHEST) + b_ref[0]


def _mod_call(c, ada_w, ada_b):
    depth, d, n = ada_w.shape
    b = c.shape[0]
    tn = n // 4
    return pl.pallas_call(
        _mod_kernel,
        out_shape=jax.ShapeDtypeStruct((depth, b, n), F32),
        grid=(depth, n // tn),
        in_specs=[pl.BlockSpec((b, d), lambda i, j: (0, 0)),
                  pl.BlockSpec((1, d, tn), lambda i, j: (i, 0, j)),
                  pl.BlockSpec((1, 1, tn), lambda i, j: (i, 0, j))],
        out_specs=pl.BlockSpec((1, b, tn), lambda i, j: (i, 0, j)),
        compiler_params=_cparams(2),
        name="adaln_mod",
    )(c, ada_w, ada_b.reshape(depth, 1, n))


def _nsa_in_kernel(x_ref, gain_ref, sc_ref, sh_ref, wq_ref, wkv_ref, wg_ref, kg_ref, cos_ref, sin_ref,
                   q_ref, kc_ref, vc_ref, ks_ref, vs_ref, kw_ref, vw_ref, g_ref):
    h = _norm_mod(x_ref[0], gain_ref[...], sc_ref[0], sh_ref[0]).astype(BF16)
    q_ref[0] = _dot(h, wq_ref[...])
    g_ref[0] = _dot(h, wg_ref[...])
    kv = _dot(h, wkv_ref[...])
    cos, sin = cos_ref[...], sin_ref[...]
    kc_ref[0] = kv[:, 0 * LANES:1 * LANES]
    vc_ref[0] = kv[:, 1 * LANES:2 * LANES]
    ks_ref[0] = _half_norm_rope(kv[:, 2 * LANES:3 * LANES], kg_ref[1:2, :], cos, sin).astype(BF16)
    vs_ref[0] = kv[:, 3 * LANES:4 * LANES].astype(BF16)
    kw_ref[0] = _half_norm_rope(kv[:, 4 * LANES:5 * LANES], kg_ref[2:3, :], cos, sin).astype(BF16)
    vw_ref[0] = kv[:, 5 * LANES:6 * LANES].astype(BF16)


def _nsa_in_call(x, gain, sc, sh, w_in, k_gain, cos, sin, tm):
    b, s, d = x.shape
    nq = NSA_HEADS * HEAD_DIM
    nkv = 6 * LANES
    wq = w_in[:, :nq].astype(BF16)
    wkv = w_in[:, nq:nq + nkv].astype(BF16)
    ng = NSA_BRANCHES * NSA_HEADS
    wg = jnp.pad(w_in[:, nq + nkv:], ((0, 0), (0, LANES - ng))).astype(BF16)
    kg = jnp.tile(k_gain, (1, 2))
    row = lambda i, j: (i, j, 0)
    per_b = lambda i, j: (i, 0, 0)
    const = lambda i, j: (0, 0)
    kv_out = lambda dt: jax.ShapeDtypeStruct((b, s, LANES), dt)
    return pl.pallas_call(
        _nsa_in_kernel,
        out_shape=(jax.ShapeDtypeStruct((b, s, nq), F32), kv_out(F32), kv_out(F32),
                   kv_out(BF16), kv_out(BF16), kv_out(BF16), kv_out(BF16), kv_out(F32)),
        grid=(b, s // tm),
        in_specs=[pl.BlockSpec((1, tm, d), row),
                  pl.BlockSpec((1, d), const),
                  pl.BlockSpec((1, 1, d), per_b),
                  pl.BlockSpec((1, 1, d), per_b),
                  pl.BlockSpec((d, nq), const),
                  pl.BlockSpec((d, nkv), const),
                  pl.BlockSpec((d, LANES), const),
                  pl.BlockSpec((3, LANES), const),
                  pl.BlockSpec((tm, LANES), lambda i, j: (j, 0)),
                  pl.BlockSpec((tm, LANES), lambda i, j: (j, 0))],
        out_specs=(pl.BlockSpec((1, tm, nq), row),) + (pl.BlockSpec((1, tm, LANES), row),) * 7,
        compiler_params=_cparams(2),
        name="nsa_in_proj",
    )(x, gain.reshape(1, d), sc, sh, wq, wkv, wg, kg, cos, sin)


def _compress_kernel(blk_ref, pe_ref, w1_ref, b1_ref, w2_ref, b2_ref, kg_ref, cos_ref, sin_ref, o_ref):
    is_key = pl.program_id(0) == 0
    blk = (blk_ref[0] + pe_ref[0]).astype(BF16)
    hid = _dot(blk, w1_ref[0]) + b1_ref[0]
    hid = 0.5 * hid * (1.0 + jnp.tanh(np.sqrt(2.0 / np.pi) * (hid + 0.044715 * hid * hid * hid)))
    out = _dot(hid.astype(BF16), w2_ref[0]) + b2_ref[0]
    normed = _half_norm_rope(out, kg_ref[...], cos_ref[...], sin_ref[...])
    o_ref[0] = jnp.where(is_key, normed, out)[:, :HEAD_DIM].astype(o_ref.dtype)


def _compress_call(blocks, pe, w1, b1, w2, b2, k_gain0, cos, sin, rows):
    _, r, kdim = blocks.shape
    hid = w1.shape[-1]
    w2p = jnp.pad(w2, ((0, 0), (0, 0), (0, LANES - HEAD_DIM))).astype(BF16)
    b2p = jnp.pad(b2, ((0, 0), (0, LANES - HEAD_DIM))).reshape(2, 1, LANES)
    kg = jnp.pad(k_gain0, (0, LANES - HEAD_DIM)).reshape(1, LANES)
    sel = lambda i, j: (i, 0, 0)
    const = lambda i, j: (0, 0)
    return pl.pallas_call(
        _compress_kernel,
        out_shape=jax.ShapeDtypeStruct((2, r, HEAD_DIM), BF16),
        grid=(2, r // rows),
        in_specs=[pl.BlockSpec((1, rows, kdim), lambda i, j: (i, j, 0)),
                  pl.BlockSpec((1, 1, kdim), sel),
                  pl.BlockSpec((1, kdim, hid), sel),
                  pl.BlockSpec((1, 1, hid), sel),
                  pl.BlockSpec((1, hid, LANES), sel),
                  pl.BlockSpec((1, 1, LANES), sel),
                  pl.BlockSpec((1, LANES), const),
                  pl.BlockSpec((rows, LANES), const),
                  pl.BlockSpec((rows, LANES), const)],
        out_specs=pl.BlockSpec((1, rows, HEAD_DIM), lambda i, j: (i, j, 0)),
        compiler_params=_cparams(2),
        name="nsa_compress",
    )(blocks, pe.reshape(2, 1, kdim), w1.astype(BF16), b1.reshape(2, 1, hid), w2p, b2p, kg, cos, sin)


def _attend(qb, k_ref, v_ref, k0, spans, g, bias):
    rows = qb.shape[0]
    m = acc = None
    for off, size in spans:
        k = k_ref[0, pl.ds(k0 + off, size), :]
        v = v_ref[0, pl.ds(k0 + off, size), :]
        s = _dot_nt(qb, k).reshape(rows // NSA_Q_BLOCK, NSA_Q_BLOCK, size) + bias[None, :, off:off + size]
        s = s.reshape(rows, size)
        m_span = jnp.max(s, axis=-1, keepdims=True)
        m_new = m_span if m is None else jnp.maximum(m, m_span)
        p = jnp.exp2(s - m_new).astype(BF16)
        v_lane = lax.broadcasted_iota(jnp.int32, v.shape, 1)
        pv = _dot(p, jnp.where((v_lane // HEAD_DIM) == g, v, jnp.ones_like(v)))
        acc = pv if m is None else acc * jnp.exp2(m - m_new) + pv
        m = m_new
    return acc


def _normalise(acc, g):
    c = (1 - g) * HEAD_DIM
    return acc / acc[:, c:c + 1]


def _nsa_attn_kernel(q_ref, g_ref, qg_ref, cos_ref, sin_ref, kc_ref, vc_ref, ks_ref, vs_ref, kw_ref, vw_ref,
                     c2s_ref, exp_ref, o_ref, *, seq, n_cmp, n_top, win_keys, sel_span, n_spans, q_block0):
    R, QB = NSA_HEADS_PER_GROUP, NSA_Q_BLOCK
    rows = R * QB
    n_sel = seq // SEL_BLOCK
    qi = pl.program_id(1) + q_block0
    s0 = qi * QB
    qt = q_ref[0]
    gt = jax.nn.sigmoid(g_ref[0])
    lane = lax.broadcasted_iota(jnp.int32, (QB, LANES), 1)
    tq = s0 + lax.broadcasted_iota(jnp.int32, (rows, 1), 0) % QB
    tq1 = s0 + lax.broadcasted_iota(jnp.int32, (QB, 1), 0)
    cos = jnp.concatenate([cos_ref[...]] * R, axis=0)
    sin = jnp.concatenate([sin_ref[...]] * R, axis=0)

    w0 = pl.multiple_of(jnp.maximum(s0 + QB - win_keys, 0), SEL_BLOCK)
    wpos = w0 + lax.broadcasted_iota(jnp.int32, (1, win_keys), 1)
    bias_w = jnp.where((wpos <= tq1) & (wpos > tq1 - WINDOW), 0.0, NEG)
    win_spans = [(off, min(2 * LANES, win_keys - off)) for off in range(0, win_keys, 2 * LANES)]

    qbs, psums, o_cs, o_ws = [], [], [], []
    for g in range(NSA_KV_GROUPS):
        in_g = (lane // HEAD_DIM) == g
        slabs = []
        for r in range(R):
            pair = (g * R + r) // 2
            slab = qt[:, pair * LANES:(pair + 1) * LANES]
            if r % 2 != g:
                slab = pltpu.roll(slab, HEAD_DIM, 1)
            slabs.append(jnp.where(in_g, slab, 0.0))
        q2 = jnp.concatenate(slabs, axis=0)
        q2 = _half_norm_rope(q2, qg_ref[...], cos, sin, other_half_zero=True) * (HEAD_DIM ** -0.5 * LOG2E)
        qb = q2.astype(BF16)

        sc = _dot_nt(qb, kc_ref[0])
        cpos = lax.broadcasted_iota(jnp.int32, (1, LANES), 1) * CMP_STRIDE + (CMP_BLOCK - 1)
        valid_c = (cpos <= tq) & (lax.broadcasted_iota(jnp.int32, (1, LANES), 1) < n_cmp)
        sc = jnp.where(valid_c, sc, NEG)
        e_c = jnp.exp2(sc - jnp.max(sc, axis=-1, keepdims=True))
        p_c = jnp.where(valid_c, e_c / jnp.sum(e_c, axis=-1, keepdims=True), 0.0)
        o_c = _dot(p_c.astype(BF16), vc_ref[0])

        psums.append(jnp.sum(p_c.reshape(R, QB, LANES), axis=0))

        o_ws.append(_normalise(_attend(qb, kw_ref, vw_ref, w0, win_spans, g, bias_w), g))
        qbs.append(qb)
        o_cs.append(o_c)

    imp = lax.dot_general(c2s_ref[...], jnp.concatenate(psums, axis=0), (((1,), (1,)), ((), ())),
                          preferred_element_type=F32, precision=HIGHEST)
    blk = lax.broadcasted_iota(jnp.int32, (n_sel, NSA_KV_GROUPS * QB), 0)
    forced = (blk == 0) | (blk == qi) | (blk == qi - 1)
    imp = jnp.where(blk <= qi, jnp.where(forced, BIG, imp), -BIG)
    beaten = jnp.zeros(imp.shape, F32)
    for k in range(1, n_sel):
        other = pltpu.roll(imp, k, 0)
        beats = (other > imp) | ((blk >= k) & (other == imp))
        beaten = beaten + jnp.where(beats, 1.0, 0.0)
    chosen = jnp.where(beaten < n_top, 1.0, 0.0).astype(BF16)
    n_keys = n_spans * sel_span
    picked = _dot_tn(chosen, exp_ref[:, :n_keys])
    kpos = lax.broadcasted_iota(jnp.int32, (1, n_keys), 1)
    sel_spans = [(c * sel_span, sel_span) for c in range(n_spans)]

    for g in range(NSA_KV_GROUPS):
        bias_s = jnp.where((picked[g * QB:(g + 1) * QB] > 0.5) & (kpos <= tq1), 0.0, NEG)
        o_s = _normalise(_attend(qbs[g], ks_ref, vs_ref, 0, sel_spans, g, bias_s), g)
        o_c, o_w = o_cs[g], o_ws[g]
        heads = []
        for r in range(R):
            h = g * R + r
            rs = slice(r * QB, (r + 1) * QB)
            o_h = (gt[:, h:h + 1] * o_c[rs]
                   + gt[:, NSA_HEADS + h:NSA_HEADS + h + 1] * o_s[rs]
                   + gt[:, 2 * NSA_HEADS + h:2 * NSA_HEADS + h + 1] * o_w[rs])
            if r % 2 != g:
                o_h = pltpu.roll(o_h, HEAD_DIM, 1)
            heads.append(o_h)
        for k in range(R // 2):
            pair = g * (R // 2) + k
            slab = jnp.where(lane < HEAD_DIM, heads[2 * k], heads[2 * k + 1])
            o_ref[0, :, pair * LANES:(pair + 1) * LANES] = slab.astype(o_ref.dtype)


def _nsa_attn_call(q, gates, q_gain, cos, sin, kc, vc, ks, vs, kw, vw, cmp_to_sel, n_cmp):
    b, s, nq = q.shape
    qb = NSA_Q_BLOCK
    n_top = min(SEL_TOPN, s // SEL_BLOCK)
    win_keys = min(WINDOW + 2 * qb, s)
    sel_span = min(512, s)
    n_sel = s // SEL_BLOCK
    qg = jnp.tile(q_gain, 2).reshape(1, LANES)
    expand = jnp.asarray(np.arange(n_sel)[:, None] == (np.arange(s)[None, :] // SEL_BLOCK), BF16)
    per_b = lambda i, j: (i, 0, 0)
    const = lambda i, j: (0, 0)
    per_call = sel_span // qb
    outs = []
    for n in range(1, s // sel_span + 1):
        q0 = (n - 1) * per_call
        row = lambda i, j, q0=q0: (i, j + q0, 0)
        tab = lambda i, j, q0=q0: (j + q0, 0)
        kern = functools.partial(_nsa_attn_kernel, seq=s, n_cmp=n_cmp, n_top=n_top, win_keys=win_keys,
                                 sel_span=sel_span, n_spans=n, q_block0=q0)
        outs.append(pl.pallas_call(
            kern,
            out_shape=jax.ShapeDtypeStruct((b, sel_span, nq), BF16),
            grid=(b, per_call),
            in_specs=[pl.BlockSpec((1, qb, nq), row),
                      pl.BlockSpec((1, qb, LANES), row),
                      pl.BlockSpec((1, LANES), const),
                      pl.BlockSpec((qb, LANES), tab),
                      pl.BlockSpec((qb, LANES), tab),
                      pl.BlockSpec((1, LANES, LANES), per_b),
                      pl.BlockSpec((1, LANES, LANES), per_b),
                      pl.BlockSpec((1, n * sel_span, LANES), per_b),
                      pl.BlockSpec((1, n * sel_span, LANES), per_b),
                      pl.BlockSpec((1, s, LANES), per_b),
                      pl.BlockSpec((1, s, LANES), per_b),
                      pl.BlockSpec((n_sel, LANES), const),
                      pl.BlockSpec((n_sel, s), const)],
            out_specs=pl.BlockSpec((1, qb, nq), lambda i, j: (i, j, 0)),
            compiler_params=_cparams(2),
            name="nsa_attention",
        )(q, gates, qg, cos, sin, kc, vc, ks, vs, kw, vw, cmp_to_sel, expand))
    return jnp.concatenate(outs, axis=1)


def _nsa_out_kernel(a_ref, w_ref, x_ref, g_ref, o_ref):
    o_ref[0] = x_ref[0] + g_ref[0] * _dot(a_ref[0], w_ref[...])


def _mlstm_out_kernel(a_ref, og_ref, w_ref, x_ref, g_ref, o_ref):
    lhs = (jax.nn.sigmoid(og_ref[0].astype(F32)) * a_ref[0].astype(F32)).astype(BF16)
    o_ref[0] = x_ref[0] + g_ref[0] * _dot(lhs, w_ref[...])


def _out_proj_call(a, og, w_out, x, g, tm):
    b, s, d = x.shape
    k = a.shape[-1]
    row = lambda i, j: (i, j, 0)
    per_b = lambda i, j: (i, 0, 0)
    a_spec = pl.BlockSpec((1, tm, k), row)
    tail = [pl.BlockSpec((k, d), lambda i, j: (0, 0)), pl.BlockSpec((1, tm, d), row),
            pl.BlockSpec((1, 1, d), per_b)]
    if og is None:
        kern, ins, args = _nsa_out_kernel, [a_spec] + tail, (a, w_out.astype(BF16), x, g)
    else:
        kern, ins, args = _mlstm_out_kernel, [a_spec, a_spec] + tail, (a, og, w_out.astype(BF16), x, g)
    return pl.pallas_call(
        kern,
        out_shape=jax.ShapeDtypeStruct((b, s, d), F32),
        grid=(b, s // tm),
        in_specs=ins,
        out_specs=pl.BlockSpec((1, tm, d), row),
        compiler_params=_cparams(2),
        name="mixer_out_proj",
    )(*args)


def _mlstm_in_kernel(x_ref, gain_ref, sc_ref, sh_ref, wq_ref, wk_ref, wv_ref, wo_ref, wg_ref,
                     q_ref, k_ref, v_ref, o_ref, g_ref):
    h = _norm_mod(x_ref[0], gain_ref[...], sc_ref[0], sh_ref[0]).astype(BF16)
    q_ref[0] = _dot(h, wq_ref[...]).astype(BF16)
    k_ref[0] = _dot(h, wk_ref[...]).astype(BF16)
    v_ref[0] = _dot(h, wv_ref[...]).astype(BF16)
    o_ref[0] = _dot(h, wo_ref[...]).astype(BF16)
    g_ref[0] = _dot(h, wg_ref[...])


def _mlstm_in_call(x, gain, sc, sh, w_in, dqk, dv, tm):
    b, s, d = x.shape
    nh = MLSTM_HEADS
    sizes = [nh * dqk, nh * dqk, nh * dv, nh * dv]
    offs = np.cumsum([0] + sizes)
    ws = [w_in[:, offs[i]:offs[i + 1]].astype(BF16) for i in range(4)]
    wg = jnp.pad(w_in[:, offs[4]:], ((0, 0), (0, LANES - 2 * nh))).astype(BF16)
    row = lambda i, j: (i, j, 0)
    per_b = lambda i, j: (i, 0, 0)
    const = lambda i, j: (0, 0)
    widths = sizes + [LANES]
    return pl.pallas_call(
        _mlstm_in_kernel,
        out_shape=tuple(jax.ShapeDtypeStruct((b, s, n), BF16) for n in sizes)
        + (jax.ShapeDtypeStruct((b, s, LANES), F32),),
        grid=(b, s // tm),
        in_specs=[pl.BlockSpec((1, tm, d), row),
                  pl.BlockSpec((1, d), const),
                  pl.BlockSpec((1, 1, d), per_b),
                  pl.BlockSpec((1, 1, d), per_b)] + [pl.BlockSpec((d, n), const) for n in widths],
        out_specs=tuple(pl.BlockSpec((1, tm, n), row) for n in widths),
        compiler_params=_cparams(2),
        name="mlstm_in_proj",
    )(x, gain.reshape(1, d), sc, sh, *ws, wg)


def _softcap(a):
    return GATE_SOFTCAP * jnp.tanh(a / GATE_SOFTCAP)


def _mlstm_kernel(bi_ref, bf_ref, q_ref, k_ref, v_ref, ig_ref, fg_ref, gain_ref, o_ref,
                  li_s, b_s, *, n_chunks, dqk, dv):
    L = MLSTM_CHUNK
    h = pl.program_id(1)
    li_s[...] = _softcap(ig_ref[0, 0] + bi_ref[h])
    fa = _softcap(fg_ref[0, 0] + bf_ref[h])
    lf = jnp.minimum(fa, 0.0) - jnp.log1p(jnp.exp(-jnp.abs(fa)))
    r_i = lax.broadcasted_iota(jnp.int32, (L, L), 0)
    c_i = lax.broadcasted_iota(jnp.int32, (L, L), 1)
    upper = jnp.where(r_i <= c_i, 1.0, 0.0)
    b_s[...] = jnp.dot(lf, upper, preferred_element_type=F32, precision=HIGHEST)
    eye = r_i == c_i
    causal = r_i >= c_i
    gain = gain_ref[0]
    k_scale = dqk ** -0.5

    def to_col(row):
        return jnp.sum(jnp.where(eye, jnp.broadcast_to(row, (L, L)), 0.0), axis=1, keepdims=True)

    def local_part(r0, c):
        qcb = q_ref[0, pl.ds(r0, L), :]
        kc = k_ref[0, pl.ds(r0, L), :].astype(F32) * k_scale
        vc = v_ref[0, pl.ds(r0, L), :]
        b_row = b_s[pl.ds(c, 1), :]
        li_row = li_s[pl.ds(c, 1), :]
        b_col, li_col = to_col(b_row), to_col(li_row)
        b_last = b_row[:, L - 1:L]
        dmat = jnp.where(causal, b_col - b_row + li_row, NEG)
        m_loc = jnp.max(dmat, axis=-1, keepdims=True)
        a_loc = jnp.exp(dmat - m_loc) * _dot_nt(qcb, kc.astype(BF16))
        num_loc = _dot(a_loc.astype(BF16), vc)
        den_loc = jnp.sum(a_loc, axis=-1, keepdims=True)
        g_max = m_loc[L - 1:L, :]
        kw = kc * jnp.exp(b_last - b_col + li_col - g_max)
        kv = _dot_tn(kw.astype(BF16), vc)
        kn = jnp.sum(kw, axis=0, keepdims=True)
        return qcb, b_col, b_last, m_loc, num_loc, den_loc, g_max, kv, kn

    def body(grp, carry):
        state, n_row, m_prev = carry
        r0 = pl.multiple_of(grp * (MLSTM_GROUP * L), MLSTM_GROUP * L)
        parts = [local_part(r0 + j * L, grp * MLSTM_GROUP + j) for j in range(MLSTM_GROUP)]
        outs = []
        for qcb, b_col, b_last, m_loc, num_loc, den_loc, g_max, kv, kn in parts:
            m_inter = b_col + m_prev
            m_t = jnp.maximum(m_inter, m_loc)
            intra = jnp.exp(m_loc - m_t)
            inter = jnp.exp(m_inter - m_t)
            num = intra * num_loc + inter * _dot(qcb, state.astype(BF16))
            den = intra * den_loc + inter * jnp.sum(qcb.astype(F32) * n_row, axis=-1, keepdims=True)
            h_out = num / jnp.maximum(jnp.abs(den), jnp.exp(-m_t))
            hs = h_out * lax.rsqrt(jnp.mean(h_out * h_out, axis=-1, keepdims=True) + EPS) * gain
            outs.append(hs.astype(o_ref.dtype))
            m_new = jnp.maximum(b_last + m_prev, g_max)
            decay = jnp.exp(b_last + m_prev - m_new)
            grow = jnp.exp(g_max - m_new)
            state, n_row, m_prev = decay * state + grow * kv, decay * n_row + grow * kn, m_new
        o_ref[0, pl.ds(r0, MLSTM_GROUP * L), :] = jnp.concatenate(outs, axis=0)
        return state, n_row, m_prev

    init = (jnp.zeros((dqk, dv), F32), jnp.zeros((1, dqk), F32), jnp.zeros((1, 1), F32))
    lax.fori_loop(0, n_chunks // MLSTM_GROUP, body, init)


def _mlstm_call(q, k, v, gates, b_igate, b_fgate, norm_gain):
    b, s, _ = q.shape
    nh = MLSTM_HEADS
    dqk, dv = q.shape[-1] // nh, v.shape[-1] // nh
    L = MLSTM_CHUNK
    nch = s // L
    g = jnp.transpose(gates[..., :2 * nh], (0, 2, 1)).reshape(b, 2 * nh, nch, L)
    smem = pl.BlockSpec(memory_space=pltpu.SMEM)
    head_cols = lambda n: pl.BlockSpec((1, s, n), lambda i, j: (i, 0, j))
    kern = functools.partial(_mlstm_kernel, n_chunks=nch, dqk=dqk, dv=dv)
    return pl.pallas_call(
        kern,
        out_shape=jax.ShapeDtypeStruct((b, s, nh * dv), BF16),
        grid=(b, nh),
        in_specs=[smem, smem, head_cols(dqk), head_cols(dqk), head_cols(dv),
                  pl.BlockSpec((1, 1, nch, L), lambda i, j: (i, j, 0, 0)),
                  pl.BlockSpec((1, 1, nch, L), lambda i, j: (i, j + nh, 0, 0)),
                  pl.BlockSpec((1, 1, dv), lambda i, j: (j, 0, 0))],
        out_specs=head_cols(dv),
        scratch_shapes=[pltpu.VMEM((nch, L), F32), pltpu.VMEM((nch, L), F32)],
        compiler_params=_cparams(2),
        name="mlstm_chunk_scan",
    )(b_igate, b_fgate, q, k, v, g, g, norm_gain.reshape(nh, 1, dv))


def _router_kernel(x_ref, gain_ref, sc_ref, sh_ref, wh_ref, wl_ref, rb_ref, h_ref, ri_ref, rw_ref, cnt_ref):
    h = _norm_mod(x_ref[0], gain_ref[...], sc_ref[0], sh_ref[0])
    hi = h.astype(BF16)
    lo = (h - hi.astype(F32)).astype(BF16)
    logits = _dot(hi, wh_ref[...]) + (_dot(lo, wh_ref[...]) + _dot(hi, wl_ref[...]))
    h_ref[0] = hi
    aff = jax.nn.sigmoid(logits)
    tm = aff.shape[0]
    lane = lax.broadcasted_iota(jnp.int32, (tm, LANES), 1)
    choice = jnp.where(lane < N_EXPERTS, aff + rb_ref[...], -jnp.inf)

    def first_max(v):
        m = jnp.max(v, axis=-1, keepdims=True)
        return m, jnp.min(jnp.where(v == m, lane, LANES), axis=-1, keepdims=True)

    best = e0 = e1 = None
    for g in range(N_GROUPS):
        cg = jnp.where((lane // EXPERTS_PER_GROUP) == g, choice, -jnp.inf)
        m1, i1 = first_max(cg)
        m2, i2 = first_max(jnp.where(lane == i1, -jnp.inf, cg))
        score = m1 + m2
        if g == 0:
            best, e0, e1 = score, i1, i2
        else:
            better = score > best
            best = jnp.where(better, score, best)
            e0 = jnp.where(better, i1, e0)
            e1 = jnp.where(better, i2, e1)
    is0, is1 = lane == e0, lane == e1
    a0 = jnp.sum(jnp.where(is0, aff, 0.0), axis=-1, keepdims=True)
    a1 = jnp.sum(jnp.where(is1, aff, 0.0), axis=-1, keepdims=True)
    tot = a0 + a1
    rw_ref[0] = jnp.where(lane == 0, a0 / tot, jnp.where(lane == 1, a1 / tot, 0.0))
    onehot = jnp.where(is0 | is1, 1.0, 0.0).astype(BF16)
    r_i = lax.broadcasted_iota(jnp.int32, (tm, tm), 0)
    c_i = lax.broadcasted_iota(jnp.int32, (tm, tm), 1)
    running = _dot(jnp.where(r_i >= c_i, 1.0, 0.0).astype(BF16), onehot)
    r0 = jnp.sum(jnp.where(is0, running, 0.0), axis=-1, keepdims=True) - 1.0
    r1 = jnp.sum(jnp.where(is1, running, 0.0), axis=-1, keepdims=True) - 1.0
    ri_ref[0] = jnp.where(lane == 0, e0, jnp.where(lane == 1, e1, jnp.where(
        lane == 2, r0.astype(jnp.int32), jnp.where(lane == 3, r1.astype(jnp.int32), 0))))
    cnt_ref[0] = jnp.broadcast_to(running[tm - 1:tm, :], (8, LANES))


def _router_call(x, gain, sc, sh, router_w, router_b, tm):
    b, s, d = x.shape
    nt = s // tm
    wp = jnp.pad(router_w, ((0, 0), (0, LANES - N_EXPERTS)))
    wh = wp.astype(BF16)
    wl = (wp - wh.astype(F32)).astype(BF16)
    rb = jnp.pad(router_b.astype(F32), (0, LANES - N_EXPERTS)).reshape(1, LANES)
    row = lambda i, j: (i, j, 0)
    per_b = lambda i, j: (i, 0, 0)
    const = lambda i, j: (0, 0)
    return pl.pallas_call(
        _router_kernel,
        out_shape=(jax.ShapeDtypeStruct((b, s, d), BF16), jax.ShapeDtypeStruct((b, s, LANES), jnp.int32),
                   jax.ShapeDtypeStruct((b, s, LANES), F32), jax.ShapeDtypeStruct((b * nt, 8, LANES), F32)),
        grid=(b, nt),
        in_specs=[pl.BlockSpec((1, tm, d), row),
                  pl.BlockSpec((1, d), const),
                  pl.BlockSpec((1, 1, d), per_b),
                  pl.BlockSpec((1, 1, d), per_b),
                  pl.BlockSpec((d, LANES), const),
                  pl.BlockSpec((d, LANES), const),
                  pl.BlockSpec((1, LANES), const)],
        out_specs=(pl.BlockSpec((1, tm, d), row), pl.BlockSpec((1, tm, LANES), row),
                   pl.BlockSpec((1, tm, LANES), row), pl.BlockSpec((1, 8, LANES), lambda i, j: (i * nt + j, 0, 0))),
        compiler_params=_cparams(2),
        name="moe_router",
    )(x, gain.reshape(1, d), sc, sh, wh, wl, rb)


def _expert_kernel(be_ref, live_ref, x_ref, wg_ref, wu_ref, wd_ref, o_ref, wg_s, wu_s, wd_s):
    i = pl.program_id(0)
    fresh = (i == 0) | (be_ref[i] != be_ref[jnp.maximum(i - 1, 0)])

    @pl.when(fresh)
    def _():
        wg_s[...] = wg_ref[0, 0].astype(BF16)
        wu_s[...] = wu_ref[0, 0].astype(BF16)
        wd_s[...] = wd_ref[0, 0].astype(BF16)

    @pl.when(live_ref[i] == 1)
    def _():
        xb = x_ref[...]
        gate = _dot(xb, wg_s[...])
        hid = gate * jax.nn.sigmoid(gate) * _dot(xb, wu_s[...])
        o_ref[...] = _dot(hid.astype(BF16), wd_s[...]).astype(o_ref.dtype)

    @pl.when(live_ref[i] == 0)
    def _():
        o_ref[...] = jnp.zeros_like(o_ref)


def _expert_call(block_expert, block_live, xs, w_gate, w_up, w_down, layer):
    p, d = xs.shape
    de = w_gate.shape[-1]
    nb = p // MOE_ROWS
    grid_spec = pltpu.PrefetchScalarGridSpec(
        num_scalar_prefetch=2,
        grid=(nb,),
        in_specs=[pl.BlockSpec((MOE_ROWS, d), lambda i, be, lv: (i, 0)),
                  pl.BlockSpec((1, 1, d, de), lambda i, be, lv: (layer, be[i], 0, 0)),
                  pl.BlockSpec((1, 1, d, de), lambda i, be, lv: (layer, be[i], 0, 0)),
                  pl.BlockSpec((1, 1, de, d), lambda i, be, lv: (layer, be[i], 0, 0))],
        out_specs=pl.BlockSpec((MOE_ROWS, d), lambda i, be, lv: (i, 0)),
        scratch_shapes=[pltpu.VMEM((d, de), BF16), pltpu.VMEM((d, de), BF16), pltpu.VMEM((de, d), BF16)],
    )
    return pl.pallas_call(
        _expert_kernel,
        out_shape=jax.ShapeDtypeStruct((p, d), F32),
        grid_spec=grid_spec,
        compiler_params=_cparams(1),
        name="moe_experts",
    )(block_expert, block_live, xs, w_gate, w_up, w_down)


def _combine_kernel(x_ref, g_ref, rw_ref, ya_ref, yb_ref, o_ref):
    rw = rw_ref[0]
    y = rw[:, 0:1] * ya_ref[0].astype(F32) + rw[:, 1:2] * yb_ref[0].astype(F32)
    o_ref[0] = x_ref[0] + g_ref[0] * y


def _combine_call(x, g, rw, ya, yb, tm):
    b, s, d = x.shape
    row = lambda i, j: (i, j, 0)
    spec = pl.BlockSpec((1, tm, d), row)
    return pl.pallas_call(
        _combine_kernel,
        out_shape=jax.ShapeDtypeStruct((b, s, d), F32),
        grid=(b, s // tm),
        in_specs=[spec, pl.BlockSpec((1, 1, d), lambda i, j: (i, 0, 0)),
                  pl.BlockSpec((1, tm, LANES), row), spec, spec],
        out_specs=spec,
        compiler_params=_cparams(2),
        name="moe_combine",
    )(x, g, rw, ya, yb)


def _row_layout(ri, cnt, tm):
    t = ri.shape[0]
    nt = t // tm
    lanes = jnp.arange(N_EXPERTS, dtype=jnp.int32)
    tile_cnt = cnt[:, 0, :N_EXPERTS].astype(jnp.int32)
    tile_off = jnp.cumsum(tile_cnt, axis=0) - tile_cnt
    counts = jnp.sum(tile_cnt, axis=0)
    padded = (counts + MOE_ROWS - 1) // MOE_ROWS * MOE_ROWS
    p_ends = jnp.cumsum(padded)
    base = (p_ends - padded)[None, :] + tile_off
    e = ri[:, 0:TOP_K].reshape(nt, tm, TOP_K)
    rank = ri[:, TOP_K:2 * TOP_K].reshape(nt, tm, TOP_K)
    dest = jnp.sum(jnp.where(e[..., None] == lanes, base[:, None, None, :], 0), axis=-1) + rank
    dest = dest.reshape(t, TOP_K)
    nb = (t * TOP_K) // MOE_ROWS + N_EXPERTS
    tok = jnp.repeat(jnp.arange(t, dtype=jnp.int32), TOP_K)
    buf_tok = (jnp.arange(nb * MOE_ROWS, dtype=jnp.int32) % t).at[dest.reshape(-1)].set(tok)
    block_start = jnp.arange(nb, dtype=jnp.int32) * MOE_ROWS
    block_expert = jnp.minimum(jnp.sum((p_ends[None, :] <= block_start[:, None]).astype(jnp.int32), axis=-1),
                               N_EXPERTS - 1)
    block_live = (block_start < p_ends[-1]).astype(jnp.int32)
    return dest, buf_tok, block_expert, block_live


def _moe_layer(x, gain, sc, sh, g, router_w, router_b, w_gate, w_up, w_down, layer, tm):
    b, s, d = x.shape
    t = b * s
    hf, ri, rw, cnt = _router_call(x, gain, sc, sh, router_w, router_b, tm)
    dest, buf_tok, block_expert, block_live = _row_layout(ri.reshape(t, LANES), cnt, tm)
    xs = jnp.take(hf.reshape(t, d), buf_tok, axis=0)
    out = _expert_call(block_expert, block_live, xs, w_gate, w_up, w_down, layer)
    ya = jnp.take(out, dest[:, 0], axis=0).reshape(b, s, d)
    yb = jnp.take(out, dest[:, 1], axis=0).reshape(b, s, d)
    return _combine_call(x, g, rw, ya, yb, tm)


def _nsa_layer(x, gain, sc, sh, g, w_in, w_out, q_gain, k_gain, cmp_pe, cmp_w1, cmp_b1, cmp_w2, cmp_b2, tm):
    b, s, d = x.shape
    G, dh = NSA_KV_GROUPS, HEAD_DIM
    cos, sin = _rope_tables(jnp.arange(s, dtype=jnp.int32))
    q, kc_raw, vc_raw, ks, vs, kw, vw, gates = _nsa_in_call(x, gain, sc, sh, w_in, k_gain, cos, sin, tm)

    n_cmp = (s - CMP_BLOCK) // CMP_STRIDE + 1
    n_str = s // CMP_STRIDE
    per_stride = CMP_STRIDE * dh

    def to_blocks(a):
        a = a.reshape(b, n_str, CMP_STRIDE, G, dh).transpose(0, 1, 3, 2, 4).reshape(b, n_str, G, per_stride)
        nxt = jnp.concatenate([a[:, 1:], jnp.zeros_like(a[:, :1])], axis=1)
        return jnp.concatenate([a, nxt], axis=-1).reshape(b * n_str * G, 2 * per_stride)

    blocks = jnp.stack([to_blocks(kc_raw), to_blocks(vc_raw)], axis=0)
    rows = n_str * G
    cmp_pos = (jnp.arange(rows, dtype=jnp.int32) // G) * CMP_STRIDE + (CMP_BLOCK - 1)
    ccos, csin = _rope_tables(cmp_pos)
    cmp = _compress_call(blocks, cmp_pe.reshape(2, CMP_BLOCK * dh), cmp_w1, cmp_b1, cmp_w2, cmp_b2,
                         k_gain[0], ccos, csin, rows)
    cmp = cmp.reshape(2, b, n_str, G * dh)
    cmp = jnp.pad(cmp, ((0, 0), (0, 0), (0, LANES - n_str), (0, 0)))
    kc, vc = cmp[0], cmp[1]

    ns = s // SEL_BLOCK
    r_, u_ = SEL_BLOCK // CMP_STRIDE, CMP_BLOCK // CMP_STRIDE
    c_idx = (r_ * np.arange(ns)[:, None, None] + np.arange(r_)[None, :, None]
             + np.arange(u_)[None, None, :]).reshape(ns, -1)
    c2s = (c_idx[:, :, None] == np.arange(n_cmp)[None, None, :]).sum(1).astype(np.float32)
    c2s = jnp.asarray(np.pad(c2s, ((0, 0), (0, LANES - n_cmp))))

    o = _nsa_attn_call(q, gates, q_gain, cos, sin, kc, vc, ks, vs, kw, vw, c2s, n_cmp)
    return _out_proj_call(o, None, w_out, x, g, tm)


def _mlstm_layer(x, gain, sc, sh, g, w_in, w_out, b_igate, b_fgate, norm_gain, tm):
    nh = MLSTM_HEADS
    dv = norm_gain.shape[-1]
    dqk = (w_in.shape[-1] - 2 * nh - 2 * nh * dv) // (2 * nh)
    q, k, v, og, gates = _mlstm_in_call(x, gain, sc, sh, w_in, dqk, dv, tm)
    hs = _mlstm_call(q, k, v, gates, b_igate, b_fgate, norm_gain)
    return _out_proj_call(hs, og, w_out, x, g, tm)


def kernel(x, c, ada_w, ada_b, norm_mix_gain, norm_ffn_gain, nsa_w_in, nsa_w_out, nsa_q_gain, nsa_k_gain, nsa_cmp_pe, nsa_cmp_w1, nsa_cmp_b1, nsa_cmp_w2, nsa_cmp_b2, mlstm_w_in, mlstm_b_igate, mlstm_b_fgate, mlstm_norm_gain, mlstm_w_out, router_w, router_b, moe_w_gate, moe_w_up, moe_w_down):
    b, s, d = x.shape
    depth = ada_w.shape[0]
    tm = min(512, s)
    mod = _mod_call(c, ada_w, ada_b)
    for i in range(depth):
        sh_m, sc_m, g_m, sh_f, sc_f, g_f = [mod[i, :, None, k * d:(k + 1) * d] for k in range(6)]
        j = i // 2
        if i % 2 == 0:
            x = _nsa_layer(x, norm_mix_gain[i], sc_m, sh_m, g_m, nsa_w_in[j], nsa_w_out[j], nsa_q_gain[j],
                           nsa_k_gain[j], nsa_cmp_pe[j], nsa_cmp_w1[j], nsa_cmp_b1[j], nsa_cmp_w2[j],
                           nsa_cmp_b2[j], tm)
        else:
            x = _mlstm_layer(x, norm_mix_gain[i], sc_m, sh_m, g_m, mlstm_w_in[j], mlstm_w_out[j],
                             mlstm_b_igate[j], mlstm_b_fgate[j], mlstm_norm_gain[j], tm)
        x = _moe_layer(x, norm_ffn_gain[i], sc_f, sh_f, g_f, router_w, router_b,
                       moe_w_gate, moe_w_up, moe_w_down, i, tm)
    return x
```

```python
import functools

import numpy as np
import jax
import jax.numpy as jnp
from jax import lax
from jax.experimental import pallas as pl
from jax.experimental.pallas import tpu as pltpu

F32 = jnp.float32
BF16 = jnp.bfloat16
HIGHEST = lax.Precision.HIGHEST

EPS = 1e-6
NEG = -1e30
BIG = 1e9
ROPE_THETA = 500000.0
LOG2E = 1.4426950408889634

NSA_HEADS = 16
NSA_KV_GROUPS = 2
NSA_HEADS_PER_GROUP = NSA_HEADS // NSA_KV_GROUPS
HEAD_DIM = 64
ROT_DIM = HEAD_DIM // 4
CMP_BLOCK = 32
CMP_STRIDE = 16
SEL_BLOCK = 64
SEL_TOPN = 8
WINDOW = 512
NSA_Q_BLOCK = 64
NSA_BRANCHES = 3

MLSTM_HEADS = 4
MLSTM_CHUNK = 64
MLSTM_GROUP = 4
GATE_SOFTCAP = 15.0

N_EXPERTS = 32
N_GROUPS = 4
EXPERTS_PER_GROUP = N_EXPERTS // N_GROUPS
TOP_K = 2
MOE_ROWS = 256

LANES = 128
VMEM_LIMIT = 48 * 1024 * 1024


def _cparams(n_axes):
    return pltpu.CompilerParams(dimension_semantics=("arbitrary",) * n_axes,
                                vmem_limit_bytes=VMEM_LIMIT)


def _dot(a, b):
    return jnp.dot(a, b, preferred_element_type=F32)


def _dot_nt(a, b):
    return lax.dot_general(a, b, (((1,), (1,)), ((), ())), preferred_element_type=F32)


def _dot_tn(a, b):
    return lax.dot_general(a, b, (((0,), (0,)), ((), ())), preferred_element_type=F32)


def _norm_mod(x, gain, sc, sh):
    y = x * lax.rsqrt(jnp.mean(x * x, axis=-1, keepdims=True) + EPS) * gain
    return y * (1.0 + sc) + sh


def _half_norm_rope(x, gain, cos, sin, other_half_zero=False):
    lane = lax.broadcasted_iota(jnp.int32, x.shape, x.ndim - 1)
    x2 = x * x
    if other_half_zero:
        ms = jnp.sum(x2, axis=-1, keepdims=True) * (1.0 / HEAD_DIM)
    else:
        left = lane < HEAD_DIM
        ss_l = jnp.sum(jnp.where(left, x2, 0.0), axis=-1, keepdims=True)
        ss_r = jnp.sum(jnp.where(left, 0.0, x2), axis=-1, keepdims=True)
        ms = jnp.where(left, ss_l, ss_r) * (1.0 / HEAD_DIM)
    y = x * lax.rsqrt(ms + EPS) * gain
    half = ROT_DIM // 2
    nd = x.ndim - 1
    partner = jnp.where((lane % HEAD_DIM) < half,
                        pltpu.roll(y, LANES - half, nd), pltpu.roll(y, half, nd))
    return y * cos + partner * sin


def _rope_tables(pos):
    half = ROT_DIM // 2
    inv_freq = ROPE_THETA ** (-jnp.arange(half, dtype=F32) / half)
    ang = pos.astype(F32)[:, None] * inv_freq[None, :]
    cos, sin = jnp.cos(ang), jnp.sin(ang)
    n = pos.shape[0]
    one = jnp.ones((n, HEAD_DIM - ROT_DIM), F32)
    cos_h = jnp.concatenate([cos, cos, one], axis=-1)
    sin_h = jnp.concatenate([-sin, sin, 0.0 * one], axis=-1)
    return jnp.tile(cos_h, (1, 2)), jnp.tile(sin_h, (1, 2))


def _mod_kernel(c_ref, w_ref, b_ref, o_ref):
    c = c_ref[...]
    cond = c * jax.nn.sigmoid(c)
    o_ref[0] = jnp.dot(cond, w_ref[0], preferred_element_type=F32, precision=HIGHEST) + b_ref[0]


def _mod_call(c, ada_w, ada_b):
    depth, d, n = ada_w.shape
    b = c.shape[0]
    tn = n // 4
    return pl.pallas_call(
        _mod_kernel,
        out_shape=jax.ShapeDtypeStruct((depth, b, n), F32),
        grid=(depth, n // tn),
        in_specs=[pl.BlockSpec((b, d), lambda i, j: (0, 0)),
                  pl.BlockSpec((1, d, tn), lambda i, j: (i, 0, j)),
                  pl.BlockSpec((1, 1, tn), lambda i, j: (i, 0, j))],
        out_specs=pl.BlockSpec((1, b, tn), lambda i, j: (i, 0, j)),
        compiler_params=_cparams(2),
        name="adaln_mod",
    )(c, ada_w, ada_b.reshape(depth, 1, n))


def _nsa_in_kernel(x_ref, gain_ref, sc_ref, sh_ref, wq_ref, wkv_ref, wg_ref, kg_ref, cos_ref, sin_ref,
                   q_ref, cv_ref, ks_ref, vs_ref, kw_ref, vw_ref, g_ref):
    h = _norm_mod(x_ref[0], gain_ref[...], sc_ref[0], sh_ref[0]).astype(BF16)
    q_ref[0] = _dot(h, wq_ref[...])
    g_ref[0] = _dot(h, wg_ref[...])
    kv = _dot(h, wkv_ref[...])
    cos, sin = cos_ref[...], sin_ref[...]
    cv_ref[0] = kv[:, 0:2 * LANES]
    ks_ref[0] = _half_norm_rope(kv[:, 2 * LANES:3 * LANES], kg_ref[1:2, :], cos, sin).astype(BF16)
    vs_ref[0] = kv[:, 3 * LANES:4 * LANES].astype(BF16)
    kw_ref[0] = _half_norm_rope(kv[:, 4 * LANES:5 * LANES], kg_ref[2:3, :], cos, sin).astype(BF16)
    vw_ref[0] = kv[:, 5 * LANES:6 * LANES].astype(BF16)


def _nsa_in_call(x, gain, sc, sh, w_in, k_gain, cos, sin, tm):
    b, s, d = x.shape
    nq = NSA_HEADS * HEAD_DIM
    nkv = 6 * LANES
    wq = w_in[:, :nq].astype(BF16)
    wkv = w_in[:, nq:nq + nkv].astype(BF16)
    ng = NSA_BRANCHES * NSA_HEADS
    wg = jnp.pad(w_in[:, nq + nkv:], ((0, 0), (0, LANES - ng))).astype(BF16)
    kg = jnp.tile(k_gain, (1, 2))
    row = lambda i, j: (i, j, 0)
    per_b = lambda i, j: (i, 0, 0)
    const = lambda i, j: (0, 0)
    kv_out = lambda dt: jax.ShapeDtypeStruct((b, s, LANES), dt)
    return pl.pallas_call(
        _nsa_in_kernel,
        out_shape=(jax.ShapeDtypeStruct((b, s, nq), F32), jax.ShapeDtypeStruct((b, s, 2 * LANES), F32),
                   kv_out(BF16), kv_out(BF16), kv_out(BF16), kv_out(BF16), kv_out(F32)),
        grid=(b, s // tm),
        in_specs=[pl.BlockSpec((1, tm, d), row),
                  pl.BlockSpec((1, d), const),
                  pl.BlockSpec((1, 1, d), per_b),
                  pl.BlockSpec((1, 1, d), per_b),
                  pl.BlockSpec((d, nq), const),
                  pl.BlockSpec((d, nkv), const),
                  pl.BlockSpec((d, LANES), const),
                  pl.BlockSpec((3, LANES), const),
                  pl.BlockSpec((tm, LANES), lambda i, j: (j, 0)),
                  pl.BlockSpec((tm, LANES), lambda i, j: (j, 0))],
        out_specs=(pl.BlockSpec((1, tm, nq), row), pl.BlockSpec((1, tm, 2 * LANES), row))
        + (pl.BlockSpec((1, tm, LANES), row),) * 5,
        compiler_params=_cparams(2),
        name="nsa_in_proj",
    )(x, gain.reshape(1, d), sc, sh, wq, wkv, wg, kg, cos, sin)


def _compress_kernel(a_ref, pe_ref, w1_ref, b1_ref, w2_ref, b2_ref, kg_ref, cos_ref, sin_ref, o_ref, *, n_str):
    is_key = pl.program_id(0) == 0
    hid2 = w1_ref.shape[-1]
    first = jnp.zeros((n_str, hid2), F32)
    second = jnp.zeros((n_str, hid2), F32)
    pe_term = jnp.zeros((8, hid2), F32)
    for l in range(CMP_STRIDE):
        rows = a_ref[0, pl.ds(l, n_str, stride=CMP_STRIDE), :].astype(BF16)
        first = first + _dot(rows, w1_ref[0, l])
        second = second + _dot(rows, w1_ref[0, CMP_STRIDE + l])
    for l in range(CMP_BLOCK):
        pe_term = pe_term + _dot(pe_ref[0, l].astype(BF16), w1_ref[0, l])
    hid = first + pltpu.roll(second, n_str - 1, 0) + pe_term[0:1] + b1_ref[0]
    hid = 0.5 * hid * (1.0 + jnp.tanh(np.sqrt(2.0 / np.pi) * (hid + 0.044715 * hid * hid * hid)))
    out = _dot(hid.astype(BF16), w2_ref[0]) + b2_ref[0]
    normed = _half_norm_rope(out, kg_ref[...], cos_ref[...], sin_ref[...])
    o_ref[0, 0] = jnp.where(is_key, normed, out).astype(o_ref.dtype)


def _block_diag2(w):
    z = jnp.zeros_like(w)
    return jnp.concatenate([jnp.concatenate([w, z], axis=-1), jnp.concatenate([z, w], axis=-1)], axis=-2)


def _compress_call(cv, pe, w1, b1, w2, b2, k_gain0, cos, sin):
    b, s, _ = cv.shape
    n_str = s // CMP_STRIDE
    hid = w1.shape[-1]
    w1bd = _block_diag2(w1.reshape(2, CMP_BLOCK, HEAD_DIM, hid)).astype(BF16)
    w2bd = _block_diag2(w2).astype(BF16)
    pe2 = jnp.broadcast_to(jnp.tile(pe, (1, 1, 2))[:, :, None, :], (2, CMP_BLOCK, 8, LANES))
    b1t = jnp.tile(b1, (1, 2)).reshape(2, 1, 2 * hid)
    b2t = jnp.tile(b2, (1, 2)).reshape(2, 1, LANES)
    kg = jnp.tile(k_gain0, 2).reshape(1, LANES)
    sel3 = lambda i, j: (i, 0, 0)
    sel4 = lambda i, j: (i, 0, 0, 0)
    const = lambda i, j: (0, 0)
    return pl.pallas_call(
        functools.partial(_compress_kernel, n_str=n_str),
        out_shape=jax.ShapeDtypeStruct((2, b, n_str, LANES), BF16),
        grid=(2, b),
        in_specs=[pl.BlockSpec((1, s, LANES), lambda i, j: (j, 0, i)),
                  pl.BlockSpec((1, CMP_BLOCK, 8, LANES), sel4),
                  pl.BlockSpec((1, CMP_BLOCK, LANES, 2 * hid), sel4),
                  pl.BlockSpec((1, 1, 2 * hid), sel3),
                  pl.BlockSpec((1, 2 * hid, LANES), sel3),
                  pl.BlockSpec((1, 1, LANES), sel3),
                  pl.BlockSpec((1, LANES), const),
                  pl.BlockSpec((n_str, LANES), const),
                  pl.BlockSpec((n_str, LANES), const)],
        out_specs=pl.BlockSpec((1, 1, n_str, LANES), lambda i, j: (i, j, 0, 0)),
        compiler_params=_cparams(2),
        name="nsa_compress",
    )(cv, pe2, w1bd, b1t, w2bd, b2t, kg, cos, sin)


def _attend(qb, k_ref, v_ref, k0, spans, g, bias):
    rows = qb.shape[0]
    m = acc = None
    for off, size in spans:
        k = k_ref[0, pl.ds(k0 + off, size), :]
        v = v_ref[0, pl.ds(k0 + off, size), :]
        s = _dot_nt(qb, k).reshape(rows // NSA_Q_BLOCK, NSA_Q_BLOCK, size) + bias[None, :, off:off + size]
        s = s.reshape(rows, size)
        m_span = jnp.max(s, axis=-1, keepdims=True)
        m_new = m_span if m is None else jnp.maximum(m, m_span)
        p = jnp.exp2(s - m_new).astype(BF16)
        v_lane = lax.broadcasted_iota(jnp.int32, v.shape, 1)
        pv = _dot(p, jnp.where((v_lane // HEAD_DIM) == g, v, jnp.ones_like(v)))
        acc = pv if m is None else acc * jnp.exp2(m - m_new) + pv
        m = m_new
    return acc


def _normalise(acc, g):
    c = (1 - g) * HEAD_DIM
    return acc / acc[:, c:c + 1]


def _nsa_attn_kernel(q_ref, g_ref, qg_ref, cos_ref, sin_ref, kc_ref, vc_ref, ks_ref, vs_ref, kw_ref, vw_ref,
                     c2s_ref, exp_ref, o_ref, *, seq, n_cmp, n_top, win_keys, sel_span, n_spans, q_block0):
    R, QB = NSA_HEADS_PER_GROUP, NSA_Q_BLOCK
    rows = R * QB
    n_sel = seq // SEL_BLOCK
    qi = pl.program_id(1) + q_block0
    s0 = qi * QB
    qt = q_ref[0]
    gt = jax.nn.sigmoid(g_ref[0])
    lane = lax.broadcasted_iota(jnp.int32, (QB, LANES), 1)
    tq = s0 + lax.broadcasted_iota(jnp.int32, (rows, 1), 0) % QB
    tq1 = s0 + lax.broadcasted_iota(jnp.int32, (QB, 1), 0)
    cos = jnp.concatenate([cos_ref[...]] * R, axis=0)
    sin = jnp.concatenate([sin_ref[...]] * R, axis=0)

    w0 = pl.multiple_of(jnp.maximum(s0 + QB - win_keys, 0), SEL_BLOCK)
    wpos = w0 + lax.broadcasted_iota(jnp.int32, (1, win_keys), 1)
    bias_w = jnp.where((wpos <= tq1) & (wpos > tq1 - WINDOW), 0.0, NEG)
    win_spans = [(off, min(2 * LANES, win_keys - off)) for off in range(0, win_keys, 2 * LANES)]

    qbs, psums, o_cs, o_ws = [], [], [], []
    for g in range(NSA_KV_GROUPS):
        in_g = (lane // HEAD_DIM) == g
        slabs = []
        for r in range(R):
            pair = (g * R + r) // 2
            slab = qt[:, pair * LANES:(pair + 1) * LANES]
            if r % 2 != g:
                slab = pltpu.roll(slab, HEAD_DIM, 1)
            slabs.append(jnp.where(in_g, slab, 0.0))
        q2 = jnp.concatenate(slabs, axis=0)
        q2 = _half_norm_rope(q2, qg_ref[...], cos, sin, other_half_zero=True) * (HEAD_DIM ** -0.5 * LOG2E)
        qb = q2.astype(BF16)

        sc = _dot_nt(qb, kc_ref[0])
        cpos = lax.broadcasted_iota(jnp.int32, (1, LANES), 1) * CMP_STRIDE + (CMP_BLOCK - 1)
        valid_c = (cpos <= tq) & (lax.broadcasted_iota(jnp.int32, (1, LANES), 1) < n_cmp)
        sc = jnp.where(valid_c, sc, NEG)
        e_c = jnp.exp2(sc - jnp.max(sc, axis=-1, keepdims=True))
        p_c = jnp.where(valid_c, e_c / jnp.sum(e_c, axis=-1, keepdims=True), 0.0)
        o_c = _dot(p_c.astype(BF16), vc_ref[0])

        psums.append(jnp.sum(p_c.reshape(R, QB, LANES), axis=0))

        o_ws.append(_normalise(_attend(qb, kw_ref, vw_ref, w0, win_spans, g, bias_w), g))
        qbs.append(qb)
        o_cs.append(o_c)

    imp = lax.dot_general(c2s_ref[...], jnp.concatenate(psums, axis=0), (((1,), (1,)), ((), ())),
                          preferred_element_type=F32, precision=HIGHEST)
    blk = lax.broadcasted_iota(jnp.int32, (n_sel, NSA_KV_GROUPS * QB), 0)
    forced = (blk == 0) | (blk == qi) | (blk == qi - 1)
    imp = jnp.where(blk <= qi, jnp.where(forced, BIG, imp), -BIG)
    beaten = jnp.zeros(imp.shape, F32)
    for k in range(1, n_sel):
        other = pltpu.roll(imp, k, 0)
        beats = (other > imp) | ((blk >= k) & (other == imp))
        beaten = beaten + jnp.where(beats, 1.0, 0.0)
    chosen = jnp.where(beaten < n_top, 1.0, 0.0).astype(BF16)
    n_keys = n_spans * sel_span
    picked = _dot_tn(chosen, exp_ref[:, :n_keys])
    kpos = lax.broadcasted_iota(jnp.int32, (1, n_keys), 1)
    sel_spans = [(c * sel_span, sel_span) for c in range(n_spans)]

    for g in range(NSA_KV_GROUPS):
        bias_s = jnp.where((picked[g * QB:(g + 1) * QB] > 0.5) & (kpos <= tq1), 0.0, NEG)
        o_s = _normalise(_attend(qbs[g], ks_ref, vs_ref, 0, sel_spans, g, bias_s), g)
        o_c, o_w = o_cs[g], o_ws[g]
        heads = []
        for r in range(R):
            h = g * R + r
            rs = slice(r * QB, (r + 1) * QB)
            o_h = (gt[:, h:h + 1] * o_c[rs]
                   + gt[:, NSA_HEADS + h:NSA_HEADS + h + 1] * o_s[rs]
                   + gt[:, 2 * NSA_HEADS + h:2 * NSA_HEADS + h + 1] * o_w[rs])
            if r % 2 != g:
                o_h = pltpu.roll(o_h, HEAD_DIM, 1)
            heads.append(o_h)
        for k in range(R // 2):
            pair = g * (R // 2) + k
            slab = jnp.where(lane < HEAD_DIM, heads[2 * k], heads[2 * k + 1])
            o_ref[0, :, pair * LANES:(pair + 1) * LANES] = slab.astype(o_ref.dtype)


def _nsa_attn_call(q, gates, q_gain, cos, sin, kc, vc, ks, vs, kw, vw, cmp_to_sel, n_cmp):
    b, s, nq = q.shape
    qb = NSA_Q_BLOCK
    n_top = min(SEL_TOPN, s // SEL_BLOCK)
    win_keys = min(WINDOW + 2 * qb, s)
    sel_span = min(512, s)
    n_sel = s // SEL_BLOCK
    qg = jnp.tile(q_gain, 2).reshape(1, LANES)
    expand = jnp.asarray(np.arange(n_sel)[:, None] == (np.arange(s)[None, :] // SEL_BLOCK), BF16)
    per_b = lambda i, j: (i, 0, 0)
    const = lambda i, j: (0, 0)
    per_call = sel_span // qb
    outs = []
    for n in range(1, s // sel_span + 1):
        q0 = (n - 1) * per_call
        row = lambda i, j, q0=q0: (i, j + q0, 0)
        tab = lambda i, j, q0=q0: (j + q0, 0)
        kern = functools.partial(_nsa_attn_kernel, seq=s, n_cmp=n_cmp, n_top=n_top, win_keys=win_keys,
                                 sel_span=sel_span, n_spans=n, q_block0=q0)
        outs.append(pl.pallas_call(
            kern,
            out_shape=jax.ShapeDtypeStruct((b, sel_span, nq), BF16),
            grid=(b, per_call),
            in_specs=[pl.BlockSpec((1, qb, nq), row),
                      pl.BlockSpec((1, qb, LANES), row),
                      pl.BlockSpec((1, LANES), const),
                      pl.BlockSpec((qb, LANES), tab),
                      pl.BlockSpec((qb, LANES), tab),
                      pl.BlockSpec((1, LANES, LANES), per_b),
                      pl.BlockSpec((1, LANES, LANES), per_b),
                      pl.BlockSpec((1, n * sel_span, LANES), per_b),
                      pl.BlockSpec((1, n * sel_span, LANES), per_b),
                      pl.BlockSpec((1, s, LANES), per_b),
                      pl.BlockSpec((1, s, LANES), per_b),
                      pl.BlockSpec((n_sel, LANES), const),
                      pl.BlockSpec((n_sel, s), const)],
            out_specs=pl.BlockSpec((1, qb, nq), lambda i, j: (i, j, 0)),
            compiler_params=_cparams(2),
            name="nsa_attention",
        )(q, gates, qg, cos, sin, kc, vc, ks, vs, kw, vw, cmp_to_sel, expand))
    return jnp.concatenate(outs, axis=1)


def _nsa_out_kernel(a_ref, w_ref, x_ref, g_ref, o_ref):
    o_ref[0] = x_ref[0] + g_ref[0] * _dot(a_ref[0], w_ref[...])


def _mlstm_out_kernel(a_ref, og_ref, w_ref, x_ref, g_ref, o_ref):
    lhs = (jax.nn.sigmoid(og_ref[0].astype(F32)) * a_ref[0].astype(F32)).astype(BF16)
    o_ref[0] = x_ref[0] + g_ref[0] * _dot(lhs, w_ref[...])


def _out_proj_call(a, og, w_out, x, g, tm):
    b, s, d = x.shape
    k = a.shape[-1]
    row = lambda i, j: (i, j, 0)
    per_b = lambda i, j: (i, 0, 0)
    a_spec = pl.BlockSpec((1, tm, k), row)
    tail = [pl.BlockSpec((k, d), lambda i, j: (0, 0)), pl.BlockSpec((1, tm, d), row),
            pl.BlockSpec((1, 1, d), per_b)]
    if og is None:
        kern, ins, args = _nsa_out_kernel, [a_spec] + tail, (a, w_out.astype(BF16), x, g)
    else:
        kern, ins, args = _mlstm_out_kernel, [a_spec, a_spec] + tail, (a, og, w_out.astype(BF16), x, g)
    return pl.pallas_call(
        kern,
        out_shape=jax.ShapeDtypeStruct((b, s, d), F32),
        grid=(b, s // tm),
        in_specs=ins,
        out_specs=pl.BlockSpec((1, tm, d), row),
        compiler_params=_cparams(2),
        name="mixer_out_proj",
    )(*args)


def _mlstm_in_kernel(x_ref, gain_ref, sc_ref, sh_ref, wq_ref, wk_ref, wv_ref, wo_ref, wg_ref,
                     q_ref, k_ref, v_ref, o_ref, g_ref):
    h = _norm_mod(x_ref[0], gain_ref[...], sc_ref[0], sh_ref[0]).astype(BF16)
    q_ref[0] = _dot(h, wq_ref[...]).astype(BF16)
    k_ref[0] = _dot(h, wk_ref[...]).astype(BF16)
    v_ref[0] = _dot(h, wv_ref[...]).astype(BF16)
    o_ref[0] = _dot(h, wo_ref[...]).astype(BF16)
    g_ref[0] = _dot(h, wg_ref[...])


def _mlstm_in_call(x, gain, sc, sh, w_in, dqk, dv, tm):
    b, s, d = x.shape
    nh = MLSTM_HEADS
    sizes = [nh * dqk, nh * dqk, nh * dv, nh * dv]
    offs = np.cumsum([0] + sizes)
    ws = [w_in[:, offs[i]:offs[i + 1]].astype(BF16) for i in range(4)]
    wg = jnp.pad(w_in[:, offs[4]:], ((0, 0), (0, LANES - 2 * nh))).astype(BF16)
    row = lambda i, j: (i, j, 0)
    per_b = lambda i, j: (i, 0, 0)
    const = lambda i, j: (0, 0)
    widths = sizes + [LANES]
    return pl.pallas_call(
        _mlstm_in_kernel,
        out_shape=tuple(jax.ShapeDtypeStruct((b, s, n), BF16) for n in sizes)
        + (jax.ShapeDtypeStruct((b, s, LANES), F32),),
        grid=(b, s // tm),
        in_specs=[pl.BlockSpec((1, tm, d), row),
                  pl.BlockSpec((1, d), const),
                  pl.BlockSpec((1, 1, d), per_b),
                  pl.BlockSpec((1, 1, d), per_b)] + [pl.BlockSpec((d, n), const) for n in widths],
        out_specs=tuple(pl.BlockSpec((1, tm, n), row) for n in widths),
        compiler_params=_cparams(2),
        name="mlstm_in_proj",
    )(x, gain.reshape(1, d), sc, sh, *ws, wg)


def _softcap(a):
    return GATE_SOFTCAP * jnp.tanh(a / GATE_SOFTCAP)


def _mlstm_kernel(bi_ref, bf_ref, q_ref, k_ref, v_ref, ig_ref, fg_ref, gain_ref, o_ref,
                  li_s, b_s, *, n_chunks, dqk, dv):
    L = MLSTM_CHUNK
    h = pl.program_id(1)
    li_s[...] = _softcap(ig_ref[0, 0] + bi_ref[h])
    fa = _softcap(fg_ref[0, 0] + bf_ref[h])
    lf = jnp.minimum(fa, 0.0) - jnp.log1p(jnp.exp(-jnp.abs(fa)))
    r_i = lax.broadcasted_iota(jnp.int32, (L, L), 0)
    c_i = lax.broadcasted_iota(jnp.int32, (L, L), 1)
    upper = jnp.where(r_i <= c_i, 1.0, 0.0)
    b_s[...] = jnp.dot(lf, upper, preferred_element_type=F32, precision=HIGHEST)
    eye = r_i == c_i
    causal = r_i >= c_i
    gain = gain_ref[0]
    k_scale = dqk ** -0.5

    def to_col(row):
        return jnp.sum(jnp.where(eye, jnp.broadcast_to(row, (L, L)), 0.0), axis=1, keepdims=True)

    def local_part(r0, c):
        qcb = q_ref[0, pl.ds(r0, L), :]
        kc = k_ref[0, pl.ds(r0, L), :].astype(F32) * k_scale
        vc = v_ref[0, pl.ds(r0, L), :]
        b_row = b_s[pl.ds(c, 1), :]
        li_row = li_s[pl.ds(c, 1), :]
        b_col, li_col = to_col(b_row), to_col(li_row)
        b_last = b_row[:, L - 1:L]
        dmat = jnp.where(causal, b_col - b_row + li_row, NEG)
        m_loc = jnp.max(dmat, axis=-1, keepdims=True)
        a_loc = jnp.exp(dmat - m_loc) * _dot_nt(qcb, kc.astype(BF16))
        num_loc = _dot(a_loc.astype(BF16), vc)
        den_loc = jnp.sum(a_loc, axis=-1, keepdims=True)
        g_max = m_loc[L - 1:L, :]
        kw = kc * jnp.exp(b_last - b_col + li_col - g_max)
        kv = _dot_tn(kw.astype(BF16), vc)
        kn = jnp.sum(kw, axis=0, keepdims=True)
        return qcb, b_col, b_last, m_loc, num_loc, den_loc, g_max, kv, kn

    def body(grp, carry):
        state, n_row, m_prev = carry
        r0 = pl.multiple_of(grp * (MLSTM_GROUP * L), MLSTM_GROUP * L)
        parts = [local_part(r0 + j * L, grp * MLSTM_GROUP + j) for j in range(MLSTM_GROUP)]
        outs = []
        for qcb, b_col, b_last, m_loc, num_loc, den_loc, g_max, kv, kn in parts:
            m_inter = b_col + m_prev
            m_t = jnp.maximum(m_inter, m_loc)
            intra = jnp.exp(m_loc - m_t)
            inter = jnp.exp(m_inter - m_t)
            num = intra * num_loc + inter * _dot(qcb, state.astype(BF16))
            den = intra * den_loc + inter * jnp.sum(qcb.astype(F32) * n_row, axis=-1, keepdims=True)
            h_out = num / jnp.maximum(jnp.abs(den), jnp.exp(-m_t))
            hs = h_out * lax.rsqrt(jnp.mean(h_out * h_out, axis=-1, keepdims=True) + EPS) * gain
            outs.append(hs.astype(o_ref.dtype))
            m_new = jnp.maximum(b_last + m_prev, g_max)
            decay = jnp.exp(b_last + m_prev - m_new)
            grow = jnp.exp(g_max - m_new)
            state, n_row, m_prev = decay * state + grow * kv, decay * n_row + grow * kn, m_new
        o_ref[0, pl.ds(r0, MLSTM_GROUP * L), :] = jnp.concatenate(outs, axis=0)
        return state, n_row, m_prev

    init = (jnp.zeros((dqk, dv), F32), jnp.zeros((1, dqk), F32), jnp.zeros((1, 1), F32))
    lax.fori_loop(0, n_chunks // MLSTM_GROUP, body, init)


def _mlstm_call(q, k, v, gates, b_igate, b_fgate, norm_gain):
    b, s, _ = q.shape
    nh = MLSTM_HEADS
    dqk, dv = q.shape[-1] // nh, v.shape[-1] // nh
    L = MLSTM_CHUNK
    nch = s // L
    g = jnp.transpose(gates[..., :2 * nh], (0, 2, 1)).reshape(b, 2 * nh, nch, L)
    smem = pl.BlockSpec(memory_space=pltpu.SMEM)
    head_cols = lambda n: pl.BlockSpec((1, s, n), lambda i, j: (i, 0, j))
    kern = functools.partial(_mlstm_kernel, n_chunks=nch, dqk=dqk, dv=dv)
    return pl.pallas_call(
        kern,
        out_shape=jax.ShapeDtypeStruct((b, s, nh * dv), BF16),
        grid=(b, nh),
        in_specs=[smem, smem, head_cols(dqk), head_cols(dqk), head_cols(dv),
                  pl.BlockSpec((1, 1, nch, L), lambda i, j: (i, j, 0, 0)),
                  pl.BlockSpec((1, 1, nch, L), lambda i, j: (i, j + nh, 0, 0)),
                  pl.BlockSpec((1, 1, dv), lambda i, j: (j, 0, 0))],
        out_specs=head_cols(dv),
        scratch_shapes=[pltpu.VMEM((nch, L), F32), pltpu.VMEM((nch, L), F32)],
        compiler_params=_cparams(2),
        name="mlstm_chunk_scan",
    )(b_igate, b_fgate, q, k, v, g, g, norm_gain.reshape(nh, 1, dv))


def _router_kernel(x_ref, gain_ref, sc_ref, sh_ref, wh_ref, wl_ref, rb_ref, h_ref, ri_ref, rw_ref, cnt_ref):
    h = _norm_mod(x_ref[0], gain_ref[...], sc_ref[0], sh_ref[0])
    hi = h.astype(BF16)
    lo = (h - hi.astype(F32)).astype(BF16)
    logits = _dot(hi, wh_ref[...]) + (_dot(lo, wh_ref[...]) + _dot(hi, wl_ref[...]))
    h_ref[0] = hi
    aff = jax.nn.sigmoid(logits)
    tm = aff.shape[0]
    lane = lax.broadcasted_iota(jnp.int32, (tm, LANES), 1)
    choice = jnp.where(lane < N_EXPERTS, aff + rb_ref[...], -jnp.inf)

    def first_max(v):
        m = jnp.max(v, axis=-1, keepdims=True)
        return m, jnp.min(jnp.where(v == m, lane, LANES), axis=-1, keepdims=True)

    best = e0 = e1 = None
    for g in range(N_GROUPS):
        cg = jnp.where((lane // EXPERTS_PER_GROUP) == g, choice, -jnp.inf)
        m1, i1 = first_max(cg)
        m2, i2 = first_max(jnp.where(lane == i1, -jnp.inf, cg))
        score = m1 + m2
        if g == 0:
            best, e0, e1 = score, i1, i2
        else:
            better = score > best
            best = jnp.where(better, score, best)
            e0 = jnp.where(better, i1, e0)
            e1 = jnp.where(better, i2, e1)
    is0, is1 = lane == e0, lane == e1
    a0 = jnp.sum(jnp.where(is0, aff, 0.0), axis=-1, keepdims=True)
    a1 = jnp.sum(jnp.where(is1, aff, 0.0), axis=-1, keepdims=True)
    tot = a0 + a1
    rw_ref[0] = jnp.where(lane == 0, a0 / tot, jnp.where(lane == 1, a1 / tot, 0.0))
    onehot = jnp.where(is0 | is1, 1.0, 0.0).astype(BF16)
    r_i = lax.broadcasted_iota(jnp.int32, (tm, tm), 0)
    c_i = lax.broadcasted_iota(jnp.int32, (tm, tm), 1)
    running = _dot(jnp.where(r_i >= c_i, 1.0, 0.0).astype(BF16), onehot)
    r0 = jnp.sum(jnp.where(is0, running, 0.0), axis=-1, keepdims=True) - 1.0
    r1 = jnp.sum(jnp.where(is1, running, 0.0), axis=-1, keepdims=True) - 1.0
    ri_ref[0] = jnp.where(lane == 0, e0, jnp.where(lane == 1, e1, jnp.where(
        lane == 2, r0.astype(jnp.int32), jnp.where(lane == 3, r1.astype(jnp.int32), 0))))
    cnt_ref[0] = jnp.broadcast_to(running[tm - 1:tm, :], (8, LANES))


def _router_call(x, gain, sc, sh, router_w, router_b, tm):
    b, s, d = x.shape
    nt = s // tm
    wp = jnp.pad(router_w, ((0, 0), (0, LANES - N_EXPERTS)))
    wh = wp.astype(BF16)
    wl = (wp - wh.astype(F32)).astype(BF16)
    rb = jnp.pad(router_b.astype(F32), (0, LANES - N_EXPERTS)).reshape(1, LANES)
    row = lambda i, j: (i, j, 0)
    per_b = lambda i, j: (i, 0, 0)
    const = lambda i, j: (0, 0)
    return pl.pallas_call(
        _router_kernel,
        out_shape=(jax.ShapeDtypeStruct((b, s, d), BF16), jax.ShapeDtypeStruct((b, s, LANES), jnp.int32),
                   jax.ShapeDtypeStruct((b, s, LANES), F32), jax.ShapeDtypeStruct((b * nt, 8, LANES), F32)),
        grid=(b, nt),
        in_specs=[pl.BlockSpec((1, tm, d), row),
                  pl.BlockSpec((1, d), const),
                  pl.BlockSpec((1, 1, d), per_b),
                  pl.BlockSpec((1, 1, d), per_b),
                  pl.BlockSpec((d, LANES), const),
                  pl.BlockSpec((d, LANES), const),
                  pl.BlockSpec((1, LANES), const)],
        out_specs=(pl.BlockSpec((1, tm, d), row), pl.BlockSpec((1, tm, LANES), row),
                   pl.BlockSpec((1, tm, LANES), row), pl.BlockSpec((1, 8, LANES), lambda i, j: (i * nt + j, 0, 0))),
        compiler_params=_cparams(2),
        name="moe_router",
    )(x, gain.reshape(1, d), sc, sh, wh, wl, rb)


def _expert_kernel(be_ref, live_ref, x_ref, wg_ref, wu_ref, wd_ref, o_ref, wg_s, wu_s, wd_s):
    i = pl.program_id(0)
    fresh = (i == 0) | (be_ref[i] != be_ref[jnp.maximum(i - 1, 0)])

    @pl.when(fresh)
    def _():
        wg_s[...] = wg_ref[0, 0].astype(BF16)
        wu_s[...] = wu_ref[0, 0].astype(BF16)
        wd_s[...] = wd_ref[0, 0].astype(BF16)

    @pl.when(live_ref[i] == 1)
    def _():
        xb = x_ref[...]
        gate = _dot(xb, wg_s[...])
        hid = gate * jax.nn.sigmoid(gate) * _dot(xb, wu_s[...])
        o_ref[...] = _dot(hid.astype(BF16), wd_s[...]).astype(o_ref.dtype)

    @pl.when(live_ref[i] == 0)
    def _():
        o_ref[...] = jnp.zeros_like(o_ref)


def _expert_call(block_expert, block_live, xs, w_gate, w_up, w_down, layer):
    p, d = xs.shape
    de = w_gate.shape[-1]
    nb = p // MOE_ROWS
    grid_spec = pltpu.PrefetchScalarGridSpec(
        num_scalar_prefetch=2,
        grid=(nb,),
        in_specs=[pl.BlockSpec((MOE_ROWS, d), lambda i, be, lv: (i, 0)),
                  pl.BlockSpec((1, 1, d, de), lambda i, be, lv: (layer, be[i], 0, 0)),
                  pl.BlockSpec((1, 1, d, de), lambda i, be, lv: (layer, be[i], 0, 0)),
                  pl.BlockSpec((1, 1, de, d), lambda i, be, lv: (layer, be[i], 0, 0))],
        out_specs=pl.BlockSpec((MOE_ROWS, d), lambda i, be, lv: (i, 0)),
        scratch_shapes=[pltpu.VMEM((d, de), BF16), pltpu.VMEM((d, de), BF16), pltpu.VMEM((de, d), BF16)],
    )
    return pl.pallas_call(
        _expert_kernel,
        out_shape=jax.ShapeDtypeStruct((p, d), F32),
        grid_spec=grid_spec,
        compiler_params=_cparams(1),
        name="moe_experts",
    )(block_expert, block_live, xs, w_gate, w_up, w_down)


def _combine_kernel(x_ref, g_ref, rw_ref, ya_ref, yb_ref, o_ref):
    rw = rw_ref[0]
    y = rw[:, 0:1] * ya_ref[0].astype(F32) + rw[:, 1:2] * yb_ref[0].astype(F32)
    o_ref[0] = x_ref[0] + g_ref[0] * y


def _combine_call(x, g, rw, ya, yb, tm):
    b, s, d = x.shape
    row = lambda i, j: (i, j, 0)
    spec = pl.BlockSpec((1, tm, d), row)
    return pl.pallas_call(
        _combine_kernel,
        out_shape=jax.ShapeDtypeStruct((b, s, d), F32),
        grid=(b, s // tm),
        in_specs=[spec, pl.BlockSpec((1, 1, d), lambda i, j: (i, 0, 0)),
                  pl.BlockSpec((1, tm, LANES), row), spec, spec],
        out_specs=spec,
        compiler_params=_cparams(2),
        name="moe_combine",
    )(x, g, rw, ya, yb)


def _row_layout(ri, cnt, tm):
    t = ri.shape[0]
    nt = t // tm
    lanes = jnp.arange(N_EXPERTS, dtype=jnp.int32)
    tile_cnt = cnt[:, 0, :N_EXPERTS].astype(jnp.int32)
    tile_off = jnp.cumsum(tile_cnt, axis=0) - tile_cnt
    counts = jnp.sum(tile_cnt, axis=0)
    padded = (counts + MOE_ROWS - 1) // MOE_ROWS * MOE_ROWS
    p_ends = jnp.cumsum(padded)
    base = (p_ends - padded)[None, :] + tile_off
    e = ri[:, 0:TOP_K].reshape(nt, tm, TOP_K)
    rank = ri[:, TOP_K:2 * TOP_K].reshape(nt, tm, TOP_K)
    dest = jnp.sum(jnp.where(e[..., None] == lanes, base[:, None, None, :], 0), axis=-1) + rank
    dest = dest.reshape(t, TOP_K)
    nb = (t * TOP_K) // MOE_ROWS + N_EXPERTS
    tok = jnp.repeat(jnp.arange(t, dtype=jnp.int32), TOP_K)
    buf_tok = (jnp.arange(nb * MOE_ROWS, dtype=jnp.int32) % t).at[dest.reshape(-1)].set(
        tok, mode="promise_in_bounds", unique_indices=True)
    block_start = jnp.arange(nb, dtype=jnp.int32) * MOE_ROWS
    block_expert = jnp.minimum(jnp.sum((p_ends[None, :] <= block_start[:, None]).astype(jnp.int32), axis=-1),
                               N_EXPERTS - 1)
    block_live = (block_start < p_ends[-1]).astype(jnp.int32)
    return dest, buf_tok, block_expert, block_live


def _moe_layer(x, gain, sc, sh, g, router_w, router_b, w_gate, w_up, w_down, layer, tm):
    b, s, d = x.shape
    t = b * s
    hf, ri, rw, cnt = _router_call(x, gain, sc, sh, router_w, router_b, tm)
    dest, buf_tok, block_expert, block_live = _row_layout(ri.reshape(t, LANES), cnt, tm)
    take = lambda a, idx: a.at[idx].get(mode="promise_in_bounds")
    xs = take(hf.reshape(t, d), buf_tok)
    out = _expert_call(block_expert, block_live, xs, w_gate, w_up, w_down, layer)
    ya = take(out, dest[:, 0]).reshape(b, s, d)
    yb = take(out, dest[:, 1]).reshape(b, s, d)
    return _combine_call(x, g, rw, ya, yb, tm)


def _nsa_layer(x, gain, sc, sh, g, w_in, w_out, q_gain, k_gain, cmp_pe, cmp_w1, cmp_b1, cmp_w2, cmp_b2, tm):
    b, s, d = x.shape
    G, dh = NSA_KV_GROUPS, HEAD_DIM
    cos, sin = _rope_tables(jnp.arange(s, dtype=jnp.int32))
    q, cv, ks, vs, kw, vw, gates = _nsa_in_call(x, gain, sc, sh, w_in, k_gain, cos, sin, tm)

    n_cmp = (s - CMP_BLOCK) // CMP_STRIDE + 1
    n_str = s // CMP_STRIDE
    cmp_pos = jnp.arange(n_str, dtype=jnp.int32) * CMP_STRIDE + (CMP_BLOCK - 1)
    ccos, csin = _rope_tables(cmp_pos)
    cmp = _compress_call(cv, cmp_pe, cmp_w1, cmp_b1, cmp_w2, cmp_b2, k_gain[0], ccos, csin)
    cmp = jnp.pad(cmp, ((0, 0), (0, 0), (0, LANES - n_str), (0, 0)))
    kc, vc = cmp[0], cmp[1]

    ns = s // SEL_BLOCK
    r_, u_ = SEL_BLOCK // CMP_STRIDE, CMP_BLOCK // CMP_STRIDE
    c_idx = (r_ * np.arange(ns)[:, None, None] + np.arange(r_)[None, :, None]
             + np.arange(u_)[None, None, :]).reshape(ns, -1)
    c2s = (c_idx[:, :, None] == np.arange(n_cmp)[None, None, :]).sum(1).astype(np.float32)
    c2s = jnp.asarray(np.pad(c2s, ((0, 0), (0, LANES - n_cmp))))

    o = _nsa_attn_call(q, gates, q_gain, cos, sin, kc, vc, ks, vs, kw, vw, c2s, n_cmp)
    return _out_proj_call(o, None, w_out, x, g, tm)


def _mlstm_layer(x, gain, sc, sh, g, w_in, w_out, b_igate, b_fgate, norm_gain, tm):
    nh = MLSTM_HEADS
    dv = norm_gain.shape[-1]
    dqk = (w_in.shape[-1] - 2 * nh - 2 * nh * dv) // (2 * nh)
    q, k, v, og, gates = _mlstm_in_call(x, gain, sc, sh, w_in, dqk, dv, tm)
    hs = _mlstm_call(q, k, v, gates, b_igate, b_fgate, norm_gain)
    return _out_proj_call(hs, og, w_out, x, g, tm)


def kernel(x, c, ada_w, ada_b, norm_mix_gain, norm_ffn_gain, nsa_w_in, nsa_w_out, nsa_q_gain, nsa_k_gain, nsa_cmp_pe, nsa_cmp_w1, nsa_cmp_b1, nsa_cmp_w2, nsa_cmp_b2, mlstm_w_in, mlstm_b_igate, mlstm_b_fgate, mlstm_norm_gain, mlstm_w_out, router_w, router_b, moe_w_gate, moe_w_up, moe_w_down):
    b, s, d = x.shape
    depth = ada_w.shape[0]
    tm = min(512, s)
    mod = _mod_call(c, ada_w, ada_b)
    for i in range(depth):
        sh_m, sc_m, g_m, sh_f, sc_f, g_f = [mod[i, :, None, k * d:(k + 1) * d] for k in range(6)]
        j = i // 2
        if i % 2 == 0:
            x = _nsa_layer(x, norm_mix_gain[i], sc_m, sh_m, g_m, nsa_w_in[j], nsa_w_out[j], nsa_q_gain[j],
                           nsa_k_gain[j], nsa_cmp_pe[j], nsa_cmp_w1[j], nsa_cmp_b1[j], nsa_cmp_w2[j],
                           nsa_cmp_b2[j], tm)
        else:
            x = _mlstm_layer(x, norm_mix_gain[i], sc_m, sh_m, g_m, mlstm_w_in[j], mlstm_w_out[j],
                             mlstm_b_igate[j], mlstm_b_fgate[j], mlstm_norm_gain[j], tm)
        x = _moe_layer(x, norm_ffn_gain[i], sc_f, sh_f, g_f, router_w, router_b,
                       moe_w_gate, moe_w_up, moe_w_down, i, tm)
    return x
```

```python
import functools

import numpy as np
import jax
import jax.numpy as jnp
from jax import lax
from jax.experimental import pallas as pl
from jax.experimental.pallas import tpu as pltpu

F32 = jnp.float32
BF16 = jnp.bfloat16
HIGHEST = lax.Precision.HIGHEST

EPS = 1e-6
NEG = -1e30
BIG = 1e9
ROPE_THETA = 500000.0
LOG2E = 1.4426950408889634

NSA_HEADS = 16
NSA_KV_GROUPS = 2
NSA_HEADS_PER_GROUP = NSA_HEADS // NSA_KV_GROUPS
HEAD_DIM = 64
ROT_DIM = HEAD_DIM // 4
CMP_BLOCK = 32
CMP_STRIDE = 16
SEL_BLOCK = 64
SEL_TOPN = 8
WINDOW = 512
NSA_Q_BLOCK = 64
NSA_BRANCHES = 3
NSA_BLOCKS_PER_STEP = 2

MLSTM_HEADS = 4
MLSTM_CHUNK = 64
MLSTM_GROUP = 4
GATE_SOFTCAP = 15.0

N_EXPERTS = 32
N_GROUPS = 4
EXPERTS_PER_GROUP = N_EXPERTS // N_GROUPS
TOP_K = 2
MOE_ROWS = 256

LANES = 128
VMEM_LIMIT = 48 * 1024 * 1024


def _cparams(n_axes):
    return pltpu.CompilerParams(dimension_semantics=("arbitrary",) * n_axes,
                                vmem_limit_bytes=VMEM_LIMIT)


def _dot(a, b):
    return jnp.dot(a, b, preferred_element_type=F32)


def _dot_nt(a, b):
    return lax.dot_general(a, b, (((1,), (1,)), ((), ())), preferred_element_type=F32)


def _dot_tn(a, b):
    return lax.dot_general(a, b, (((0,), (0,)), ((), ())), preferred_element_type=F32)


def _norm_mod(x, gain, sc, sh):
    y = x * lax.rsqrt(jnp.mean(x * x, axis=-1, keepdims=True) + EPS) * gain
    return y * (1.0 + sc) + sh


def _half_norm_rope(x, gain, cos, sin, other_half_zero=False):
    lane = lax.broadcasted_iota(jnp.int32, x.shape, x.ndim - 1)
    x2 = x * x
    if other_half_zero:
        ms = jnp.sum(x2, axis=-1, keepdims=True) * (1.0 / HEAD_DIM)
    else:
        left = lane < HEAD_DIM
        ss_l = jnp.sum(jnp.where(left, x2, 0.0), axis=-1, keepdims=True)
        ss_r = jnp.sum(jnp.where(left, 0.0, x2), axis=-1, keepdims=True)
        ms = jnp.where(left, ss_l, ss_r) * (1.0 / HEAD_DIM)
    y = x * lax.rsqrt(ms + EPS) * gain
    half = ROT_DIM // 2
    nd = x.ndim - 1
    partner = jnp.where((lane % HEAD_DIM) < half,
                        pltpu.roll(y, LANES - half, nd), pltpu.roll(y, half, nd))
    return y * cos + partner * sin


def _rope_tables(pos):
    half = ROT_DIM // 2
    inv_freq = ROPE_THETA ** (-jnp.arange(half, dtype=F32) / half)
    ang = pos.astype(F32)[:, None] * inv_freq[None, :]
    cos, sin = jnp.cos(ang), jnp.sin(ang)
    n = pos.shape[0]
    one = jnp.ones((n, HEAD_DIM - ROT_DIM), F32)
    cos_h = jnp.concatenate([cos, cos, one], axis=-1)
    sin_h = jnp.concatenate([-sin, sin, 0.0 * one], axis=-1)
    return jnp.tile(cos_h, (1, 2)), jnp.tile(sin_h, (1, 2))


def _mod_kernel(c_ref, w_ref, b_ref, o_ref):
    c = c_ref[...]
    cond = c * jax.nn.sigmoid(c)
    o_ref[0] = jnp.dot(cond, w_ref[0], preferred_element_type=F32, precision=HIGHEST) + b_ref[0]


def _mod_call(c, ada_w, ada_b):
    depth, d, n = ada_w.shape
    b = c.shape[0]
    tn = n // 4
    return pl.pallas_call(
        _mod_kernel,
        out_shape=jax.ShapeDtypeStruct((depth, b, n), F32),
        grid=(depth, n // tn),
        in_specs=[pl.BlockSpec((b, d), lambda i, j: (0, 0)),
                  pl.BlockSpec((1, d, tn), lambda i, j: (i, 0, j)),
                  pl.BlockSpec((1, 1, tn), lambda i, j: (i, 0, j))],
        out_specs=pl.BlockSpec((1, b, tn), lambda i, j: (i, 0, j)),
        compiler_params=_cparams(2),
        name="adaln_mod",
    )(c, ada_w, ada_b.reshape(depth, 1, n))


def _nsa_in_kernel(x_ref, gain_ref, sc_ref, sh_ref, wq_ref, wkv_ref, wg_ref, qg_ref, kg_ref, cos_ref, sin_ref,
                   q_ref, cv_ref, ks_ref, vs_ref, kw_ref, vw_ref, g_ref):
    h = _norm_mod(x_ref[0], gain_ref[...], sc_ref[0], sh_ref[0]).astype(BF16)
    g_ref[0] = _dot(h, wg_ref[...])
    kv = _dot(h, wkv_ref[...])
    cos, sin = cos_ref[...], sin_ref[...]
    q = _dot(h, wq_ref[...])
    for r in range(NSA_HEADS_PER_GROUP):
        slab = _half_norm_rope(q[:, r * LANES:(r + 1) * LANES], qg_ref[...], cos, sin)
        q_ref[0, :, r * LANES:(r + 1) * LANES] = (slab * (HEAD_DIM ** -0.5 * LOG2E)).astype(BF16)
    cv_ref[0] = kv[:, 0:2 * LANES]
    ks_ref[0] = _half_norm_rope(kv[:, 2 * LANES:3 * LANES], kg_ref[1:2, :], cos, sin).astype(BF16)
    vs_ref[0] = kv[:, 3 * LANES:4 * LANES].astype(BF16)
    kw_ref[0] = _half_norm_rope(kv[:, 4 * LANES:5 * LANES], kg_ref[2:3, :], cos, sin).astype(BF16)
    vw_ref[0] = kv[:, 5 * LANES:6 * LANES].astype(BF16)


def _head_pair_order():
    r, g, dd = np.meshgrid(np.arange(NSA_HEADS_PER_GROUP), np.arange(NSA_KV_GROUPS), np.arange(HEAD_DIM),
                           indexing="ij")
    return ((g * NSA_HEADS_PER_GROUP + r) * HEAD_DIM + dd).reshape(-1)


def _nsa_in_call(x, gain, sc, sh, w_in, q_gain, k_gain, cos, sin, tm):
    b, s, d = x.shape
    nq = NSA_HEADS * HEAD_DIM
    nkv = 6 * LANES
    wq = w_in[:, :nq][:, _head_pair_order()].astype(BF16)
    qg = jnp.tile(q_gain, 2).reshape(1, LANES)
    wkv = w_in[:, nq:nq + nkv].astype(BF16)
    ng = NSA_BRANCHES * NSA_HEADS
    wg = jnp.pad(w_in[:, nq + nkv:], ((0, 0), (0, LANES - ng))).astype(BF16)
    kg = jnp.tile(k_gain, (1, 2))
    row = lambda i, j: (i, j, 0)
    per_b = lambda i, j: (i, 0, 0)
    const = lambda i, j: (0, 0)
    kv_out = lambda dt: jax.ShapeDtypeStruct((b, s, LANES), dt)
    return pl.pallas_call(
        _nsa_in_kernel,
        out_shape=(jax.ShapeDtypeStruct((b, s, nq), BF16), jax.ShapeDtypeStruct((b, s, 2 * LANES), F32),
                   kv_out(BF16), kv_out(BF16), kv_out(BF16), kv_out(BF16), kv_out(F32)),
        grid=(b, s // tm),
        in_specs=[pl.BlockSpec((1, tm, d), row),
                  pl.BlockSpec((1, d), const),
                  pl.BlockSpec((1, 1, d), per_b),
                  pl.BlockSpec((1, 1, d), per_b),
                  pl.BlockSpec((d, nq), const),
                  pl.BlockSpec((d, nkv), const),
                  pl.BlockSpec((d, LANES), const),
                  pl.BlockSpec((1, LANES), const),
                  pl.BlockSpec((3, LANES), const),
                  pl.BlockSpec((tm, LANES), lambda i, j: (j, 0)),
                  pl.BlockSpec((tm, LANES), lambda i, j: (j, 0))],
        out_specs=(pl.BlockSpec((1, tm, nq), row), pl.BlockSpec((1, tm, 2 * LANES), row))
        + (pl.BlockSpec((1, tm, LANES), row),) * 5,
        compiler_params=_cparams(2),
        name="nsa_in_proj",
    )(x, gain.reshape(1, d), sc, sh, wq, wkv, wg, qg, kg, cos, sin)


def _compress_kernel(a_ref, pe_ref, w1_ref, b1_ref, w2_ref, b2_ref, kg_ref, cos_ref, sin_ref, o_ref, *, n_str):
    is_key = pl.program_id(0) == 0
    hid2 = w1_ref.shape[-1]
    first = jnp.zeros((n_str, hid2), F32)
    second = jnp.zeros((n_str, hid2), F32)
    pe_term = jnp.zeros((8, hid2), F32)
    for l in range(CMP_STRIDE):
        rows = a_ref[0, pl.ds(l, n_str, stride=CMP_STRIDE), :].astype(BF16)
        first = first + _dot(rows, w1_ref[0, l])
        second = second + _dot(rows, w1_ref[0, CMP_STRIDE + l])
    for l in range(CMP_BLOCK):
        pe_term = pe_term + _dot(pe_ref[0, l].astype(BF16), w1_ref[0, l])
    hid = first + pltpu.roll(second, n_str - 1, 0) + pe_term[0:1] + b1_ref[0]
    hid = 0.5 * hid * (1.0 + jnp.tanh(np.sqrt(2.0 / np.pi) * (hid + 0.044715 * hid * hid * hid)))
    out = _dot(hid.astype(BF16), w2_ref[0]) + b2_ref[0]
    normed = _half_norm_rope(out, kg_ref[...], cos_ref[...], sin_ref[...])
    o_ref[0, 0] = jnp.where(is_key, normed, out).astype(o_ref.dtype)


def _block_diag2(w):
    z = jnp.zeros_like(w)
    return jnp.concatenate([jnp.concatenate([w, z], axis=-1), jnp.concatenate([z, w], axis=-1)], axis=-2)


def _compress_call(cv, pe, w1, b1, w2, b2, k_gain0, cos, sin):
    b, s, _ = cv.shape
    n_str = s // CMP_STRIDE
    hid = w1.shape[-1]
    w1bd = _block_diag2(w1.reshape(2, CMP_BLOCK, HEAD_DIM, hid)).astype(BF16)
    w2bd = _block_diag2(w2).astype(BF16)
    pe2 = jnp.broadcast_to(jnp.tile(pe, (1, 1, 2))[:, :, None, :], (2, CMP_BLOCK, 8, LANES))
    b1t = jnp.tile(b1, (1, 2)).reshape(2, 1, 2 * hid)
    b2t = jnp.tile(b2, (1, 2)).reshape(2, 1, LANES)
    kg = jnp.tile(k_gain0, 2).reshape(1, LANES)
    sel3 = lambda i, j: (i, 0, 0)
    sel4 = lambda i, j: (i, 0, 0, 0)
    const = lambda i, j: (0, 0)
    return pl.pallas_call(
        functools.partial(_compress_kernel, n_str=n_str),
        out_shape=jax.ShapeDtypeStruct((2, b, n_str, LANES), BF16),
        grid=(2, b),
        in_specs=[pl.BlockSpec((1, s, LANES), lambda i, j: (j, 0, i)),
                  pl.BlockSpec((1, CMP_BLOCK, 8, LANES), sel4),
                  pl.BlockSpec((1, CMP_BLOCK, LANES, 2 * hid), sel4),
                  pl.BlockSpec((1, 1, 2 * hid), sel3),
                  pl.BlockSpec((1, 2 * hid, LANES), sel3),
                  pl.BlockSpec((1, 1, LANES), sel3),
                  pl.BlockSpec((1, LANES), const),
                  pl.BlockSpec((n_str, LANES), const),
                  pl.BlockSpec((n_str, LANES), const)],
        out_specs=pl.BlockSpec((1, 1, n_str, LANES), lambda i, j: (i, j, 0, 0)),
        compiler_params=_cparams(2),
        name="nsa_compress",
    )(cv, pe2, w1bd, b1t, w2bd, b2t, kg, cos, sin)


def _attend(qb, k_ref, v_ref, k0, spans, g, bias):
    rows = qb.shape[0]
    m = acc = None
    for off, size in spans:
        k = k_ref[0, pl.ds(k0 + off, size), :]
        v = v_ref[0, pl.ds(k0 + off, size), :]
        s = _dot_nt(qb, k).reshape(rows // NSA_Q_BLOCK, NSA_Q_BLOCK, size) + bias[None, :, off:off + size]
        s = s.reshape(rows, size)
        m_span = jnp.max(s, axis=-1, keepdims=True)
        m_new = m_span if m is None else jnp.maximum(m, m_span)
        p = jnp.exp2(s - m_new).astype(BF16)
        v_lane = lax.broadcasted_iota(jnp.int32, v.shape, 1)
        pv = _dot(p, jnp.where((v_lane // HEAD_DIM) == g, v, jnp.ones_like(v)))
        acc = pv if m is None else acc * jnp.exp2(m - m_new) + pv
        m = m_new
    return acc


def _nsa_attn_kernel(*refs, blocks_per_step, **static):
    for sub in range(blocks_per_step):
        _nsa_attn_block(sub, pl.program_id(1) * blocks_per_step + sub, *refs, **static)


def _nsa_attn_block(sub, q_block, q_ref, g_ref, kc_ref, vc_ref, ks_ref, vs_ref, kw_ref, vw_ref,
                    c2s_ref, exp_ref, o_ref, *, seq, n_cmp, n_top, win_keys, sel_span, n_spans, q_block0):
    R, QB = NSA_HEADS_PER_GROUP, NSA_Q_BLOCK
    rows = R * QB
    n_sel = seq // SEL_BLOCK
    qi = q_block + q_block0
    s0 = qi * QB
    tok = slice(sub * QB, (sub + 1) * QB)
    gt = jax.nn.sigmoid(g_ref[0, tok, :])
    lane = lax.broadcasted_iota(jnp.int32, (QB, LANES), 1)
    tq = s0 + lax.broadcasted_iota(jnp.int32, (rows, 1), 0) % QB
    tq1 = s0 + lax.broadcasted_iota(jnp.int32, (QB, 1), 0)
    ones_sq = jnp.ones((LANES, LANES), BF16)

    w0 = pl.multiple_of(jnp.maximum(s0 + QB - win_keys, 0), SEL_BLOCK)
    wpos = w0 + lax.broadcasted_iota(jnp.int32, (1, win_keys), 1)
    bias_w = jnp.where((wpos <= tq1) & (wpos > tq1 - WINDOW), 0.0, NEG)
    win_spans = [(off, min(3 * LANES, win_keys - off)) for off in range(0, win_keys, 3 * LANES)]

    qbs, psums, o_cs, acc_ws = [], [], [], []
    for g in range(NSA_KV_GROUPS):
        in_g = (lane // HEAD_DIM) == g
        zero = jnp.zeros((QB, LANES), BF16)
        qb = jnp.concatenate([jnp.where(in_g, q_ref[0, tok, r * LANES:(r + 1) * LANES], zero) for r in range(R)],
                             axis=0)

        sc = _dot_nt(qb, kc_ref[0])
        cpos = lax.broadcasted_iota(jnp.int32, (1, LANES), 1) * CMP_STRIDE + (CMP_BLOCK - 1)
        valid_c = (cpos <= tq) & (lax.broadcasted_iota(jnp.int32, (1, LANES), 1) < n_cmp)
        sc = jnp.where(valid_c, sc, NEG)
        e_c = jnp.exp2(sc - jnp.max(sc, axis=-1, keepdims=True)).astype(BF16)
        p_c = jnp.where(valid_c, e_c.astype(F32) / _dot(e_c, ones_sq), 0.0)
        o_cs.append(_dot(p_c.astype(BF16), vc_ref[0]))

        psums.append(jnp.sum(p_c.reshape(R, QB, LANES), axis=0))

        acc_ws.append(_attend(qb, kw_ref, vw_ref, w0, win_spans, g, bias_w))
        qbs.append(qb)

    imp = lax.dot_general(c2s_ref[...], jnp.concatenate(psums, axis=0), (((1,), (1,)), ((), ())),
                          preferred_element_type=F32, precision=HIGHEST)
    blk = lax.broadcasted_iota(jnp.int32, (n_sel, NSA_KV_GROUPS * QB), 0)
    forced = (blk == 0) | (blk == qi) | (blk == qi - 1)
    imp = jnp.where(blk <= qi, jnp.where(forced, BIG, imp), -BIG)
    beaten = jnp.zeros(imp.shape, F32)
    for k in range(1, n_sel):
        other = pltpu.roll(imp, k, 0)
        beats = (other > imp) | ((blk >= k) & (other == imp))
        beaten = beaten + jnp.where(beats, 1.0, 0.0)
    chosen = jnp.where(beaten < n_top, 1.0, 0.0).astype(BF16)
    n_keys = n_spans * sel_span
    picked = _dot_tn(chosen, exp_ref[:, :n_keys])
    kpos = lax.broadcasted_iota(jnp.int32, (1, n_keys), 1)
    sel_spans = [(c * sel_span, sel_span) for c in range(n_spans)]

    heads = []
    for g in range(NSA_KV_GROUPS):
        bias_s = jnp.where((picked[g * QB:(g + 1) * QB] > 0.5) & (kpos <= tq1), 0.0, NEG)
        acc_s = _attend(qbs[g], ks_ref, vs_ref, 0, sel_spans, g, bias_s)
        o_c, acc_w = o_cs[g], acc_ws[g]
        c_sum = (1 - g) * HEAD_DIM
        per_head = []
        for r in range(R):
            h = g * R + r
            rs = slice(r * QB, (r + 1) * QB)
            a_s, a_w = acc_s[rs], acc_w[rs]
            g_s = gt[:, NSA_HEADS + h:NSA_HEADS + h + 1] / a_s[:, c_sum:c_sum + 1]
            g_w = gt[:, 2 * NSA_HEADS + h:2 * NSA_HEADS + h + 1] / a_w[:, c_sum:c_sum + 1]
            per_head.append(gt[:, h:h + 1] * o_c[rs] + g_s * a_s + g_w * a_w)
        heads.append(per_head)
    for r in range(R):
        slab = jnp.where(lane < HEAD_DIM, heads[0][r], heads[1][r])
        o_ref[0, tok, r * LANES:(r + 1) * LANES] = slab.astype(o_ref.dtype)


def _nsa_attn_call(q, gates, kc, vc, ks, vs, kw, vw, cmp_to_sel, n_cmp):
    b, s, nq = q.shape
    qb = NSA_Q_BLOCK
    n_top = min(SEL_TOPN, s // SEL_BLOCK)
    win_keys = min(WINDOW + 2 * qb, s)
    sel_span = min(512, s)
    n_sel = s // SEL_BLOCK
    expand = jnp.asarray(np.arange(n_sel)[:, None] == (np.arange(s)[None, :] // SEL_BLOCK), BF16)
    per_b = lambda i, j: (i, 0, 0)
    const = lambda i, j: (0, 0)
    per_call = sel_span // qb
    per_step = min(NSA_BLOCKS_PER_STEP, per_call)
    qrows = per_step * qb
    outs = []
    for n in range(1, s // sel_span + 1):
        q0 = (n - 1) * per_call
        row = lambda i, j, t0=q0 // per_step: (i, j + t0, 0)
        kern = functools.partial(_nsa_attn_kernel, blocks_per_step=per_step, seq=s, n_cmp=n_cmp, n_top=n_top,
                                 win_keys=win_keys, sel_span=sel_span, n_spans=n, q_block0=q0)
        outs.append(pl.pallas_call(
            kern,
            out_shape=jax.ShapeDtypeStruct((b, sel_span, nq), BF16),
            grid=(b, per_call // per_step),
            in_specs=[pl.BlockSpec((1, qrows, nq), row),
                      pl.BlockSpec((1, qrows, LANES), row),
                      pl.BlockSpec((1, LANES, LANES), per_b),
                      pl.BlockSpec((1, LANES, LANES), per_b),
                      pl.BlockSpec((1, n * sel_span, LANES), per_b),
                      pl.BlockSpec((1, n * sel_span, LANES), per_b),
                      pl.BlockSpec((1, s, LANES), per_b),
                      pl.BlockSpec((1, s, LANES), per_b),
                      pl.BlockSpec((n_sel, LANES), const),
                      pl.BlockSpec((n_sel, s), const)],
            out_specs=pl.BlockSpec((1, qrows, nq), lambda i, j: (i, j, 0)),
            compiler_params=_cparams(2),
            name="nsa_attention",
        )(q, gates, kc, vc, ks, vs, kw, vw, cmp_to_sel, expand))
    return jnp.concatenate(outs, axis=1)


def _nsa_out_kernel(a_ref, w_ref, x_ref, g_ref, o_ref):
    o_ref[0] = x_ref[0] + g_ref[0] * _dot(a_ref[0], w_ref[...])


def _mlstm_out_kernel(a_ref, og_ref, w_ref, x_ref, g_ref, o_ref):
    lhs = (jax.nn.sigmoid(og_ref[0].astype(F32)) * a_ref[0].astype(F32)).astype(BF16)
    o_ref[0] = x_ref[0] + g_ref[0] * _dot(lhs, w_ref[...])


def _out_proj_call(a, og, w_out, x, g, tm):
    b, s, d = x.shape
    k = a.shape[-1]
    row = lambda i, j: (i, j, 0)
    per_b = lambda i, j: (i, 0, 0)
    a_spec = pl.BlockSpec((1, tm, k), row)
    tail = [pl.BlockSpec((k, d), lambda i, j: (0, 0)), pl.BlockSpec((1, tm, d), row),
            pl.BlockSpec((1, 1, d), per_b)]
    if og is None:
        kern, ins, args = _nsa_out_kernel, [a_spec] + tail, (a, w_out.astype(BF16), x, g)
    else:
        kern, ins, args = _mlstm_out_kernel, [a_spec, a_spec] + tail, (a, og, w_out.astype(BF16), x, g)
    return pl.pallas_call(
        kern,
        out_shape=jax.ShapeDtypeStruct((b, s, d), F32),
        grid=(b, s // tm),
        in_specs=ins,
        out_specs=pl.BlockSpec((1, tm, d), row),
        compiler_params=_cparams(2),
        name="mixer_out_proj",
    )(*args)


def _mlstm_in_kernel(x_ref, gain_ref, sc_ref, sh_ref, wq_ref, wk_ref, wv_ref, wo_ref, wg_ref,
                     q_ref, k_ref, v_ref, o_ref, g_ref):
    h = _norm_mod(x_ref[0], gain_ref[...], sc_ref[0], sh_ref[0]).astype(BF16)
    q_ref[0] = _dot(h, wq_ref[...]).astype(BF16)
    k_ref[0] = _dot(h, wk_ref[...]).astype(BF16)
    v_ref[0] = _dot(h, wv_ref[...]).astype(BF16)
    o_ref[0] = _dot(h, wo_ref[...]).astype(BF16)
    g_ref[0] = _dot(h, wg_ref[...])


def _mlstm_in_call(x, gain, sc, sh, w_in, dqk, dv, tm):
    b, s, d = x.shape
    nh = MLSTM_HEADS
    sizes = [nh * dqk, nh * dqk, nh * dv, nh * dv]
    offs = np.cumsum([0] + sizes)
    ws = [w_in[:, offs[i]:offs[i + 1]].astype(BF16) for i in range(4)]
    wg = jnp.pad(w_in[:, offs[4]:], ((0, 0), (0, LANES - 2 * nh))).astype(BF16)
    row = lambda i, j: (i, j, 0)
    per_b = lambda i, j: (i, 0, 0)
    const = lambda i, j: (0, 0)
    widths = sizes + [LANES]
    return pl.pallas_call(
        _mlstm_in_kernel,
        out_shape=tuple(jax.ShapeDtypeStruct((b, s, n), BF16) for n in sizes)
        + (jax.ShapeDtypeStruct((b, s, LANES), F32),),
        grid=(b, s // tm),
        in_specs=[pl.BlockSpec((1, tm, d), row),
                  pl.BlockSpec((1, d), const),
                  pl.BlockSpec((1, 1, d), per_b),
                  pl.BlockSpec((1, 1, d), per_b)] + [pl.BlockSpec((d, n), const) for n in widths],
        out_specs=tuple(pl.BlockSpec((1, tm, n), row) for n in widths),
        compiler_params=_cparams(2),
        name="mlstm_in_proj",
    )(x, gain.reshape(1, d), sc, sh, *ws, wg)


def _softcap(a):
    return GATE_SOFTCAP * jnp.tanh(a / GATE_SOFTCAP)


def _mlstm_kernel(bi_ref, bf_ref, q_ref, k_ref, v_ref, ig_ref, fg_ref, gain_ref, o_ref,
                  li_s, b_s, *, n_chunks, dqk, dv):
    L = MLSTM_CHUNK
    h = pl.program_id(1)
    li_s[...] = _softcap(ig_ref[0, 0] + bi_ref[h])
    fa = _softcap(fg_ref[0, 0] + bf_ref[h])
    lf = jnp.minimum(fa, 0.0) - jnp.log1p(jnp.exp(-jnp.abs(fa)))
    r_i = lax.broadcasted_iota(jnp.int32, (L, L), 0)
    c_i = lax.broadcasted_iota(jnp.int32, (L, L), 1)
    upper = jnp.where(r_i <= c_i, 1.0, 0.0)
    b_s[...] = jnp.dot(lf, upper, preferred_element_type=F32, precision=HIGHEST)
    eye = r_i == c_i
    causal = r_i >= c_i
    gain = gain_ref[0]
    k_scale = dqk ** -0.5

    def to_col(row):
        return jnp.sum(jnp.where(eye, jnp.broadcast_to(row, (L, L)), 0.0), axis=1, keepdims=True)

    def local_part(r0, c):
        qcb = q_ref[0, pl.ds(r0, L), :]
        kc = k_ref[0, pl.ds(r0, L), :].astype(F32) * k_scale
        vc = v_ref[0, pl.ds(r0, L), :]
        b_row = b_s[pl.ds(c, 1), :]
        li_row = li_s[pl.ds(c, 1), :]
        b_col, li_col = to_col(b_row), to_col(li_row)
        b_last = b_row[:, L - 1:L]
        dmat = jnp.where(causal, b_col - b_row + li_row, NEG)
        m_loc = jnp.max(dmat, axis=-1, keepdims=True)
        a_loc = jnp.exp(dmat - m_loc) * _dot_nt(qcb, kc.astype(BF16))
        num_loc = _dot(a_loc.astype(BF16), vc)
        den_loc = jnp.sum(a_loc, axis=-1, keepdims=True)
        g_max = m_loc[L - 1:L, :]
        kw = kc * jnp.exp(b_last - b_col + li_col - g_max)
        kv = _dot_tn(kw.astype(BF16), vc)
        kn = jnp.sum(kw, axis=0, keepdims=True)
        return qcb, b_col, b_last, m_loc, num_loc, den_loc, g_max, kv, kn

    def body(grp, carry):
        state, n_row, m_prev = carry
        r0 = pl.multiple_of(grp * (MLSTM_GROUP * L), MLSTM_GROUP * L)
        parts = [local_part(r0 + j * L, grp * MLSTM_GROUP + j) for j in range(MLSTM_GROUP)]
        outs = []
        for qcb, b_col, b_last, m_loc, num_loc, den_loc, g_max, kv, kn in parts:
            m_inter = b_col + m_prev
            m_t = jnp.maximum(m_inter, m_loc)
            intra = jnp.exp(m_loc - m_t)
            inter = jnp.exp(m_inter - m_t)
            num = intra * num_loc + inter * _dot(qcb, state.astype(BF16))
            den = intra * den_loc + inter * jnp.sum(qcb.astype(F32) * n_row, axis=-1, keepdims=True)
            h_out = num / jnp.maximum(jnp.abs(den), jnp.exp(-m_t))
            hs = h_out * lax.rsqrt(jnp.mean(h_out * h_out, axis=-1, keepdims=True) + EPS) * gain
            outs.append(hs.astype(o_ref.dtype))
            m_new = jnp.maximum(b_last + m_prev, g_max)
            decay = jnp.exp(b_last + m_prev - m_new)
            grow = jnp.exp(g_max - m_new)
            state, n_row, m_prev = decay * state + grow * kv, decay * n_row + grow * kn, m_new
        o_ref[0, pl.ds(r0, MLSTM_GROUP * L), :] = jnp.concatenate(outs, axis=0)
        return state, n_row, m_prev

    init = (jnp.zeros((dqk, dv), F32), jnp.zeros((1, dqk), F32), jnp.zeros((1, 1), F32))
    lax.fori_loop(0, n_chunks // MLSTM_GROUP, body, init)


def _mlstm_call(q, k, v, gates, b_igate, b_fgate, norm_gain):
    b, s, _ = q.shape
    nh = MLSTM_HEADS
    dqk, dv = q.shape[-1] // nh, v.shape[-1] // nh
    L = MLSTM_CHUNK
    nch = s // L
    g = jnp.transpose(gates[..., :2 * nh], (0, 2, 1)).reshape(b, 2 * nh, nch, L)
    smem = pl.BlockSpec(memory_space=pltpu.SMEM)
    head_cols = lambda n: pl.BlockSpec((1, s, n), lambda i, j: (i, 0, j))
    kern = functools.partial(_mlstm_kernel, n_chunks=nch, dqk=dqk, dv=dv)
    return pl.pallas_call(
        kern,
        out_shape=jax.ShapeDtypeStruct((b, s, nh * dv), BF16),
        grid=(b, nh),
        in_specs=[smem, smem, head_cols(dqk), head_cols(dqk), head_cols(dv),
                  pl.BlockSpec((1, 1, nch, L), lambda i, j: (i, j, 0, 0)),
                  pl.BlockSpec((1, 1, nch, L), lambda i, j: (i, j + nh, 0, 0)),
                  pl.BlockSpec((1, 1, dv), lambda i, j: (j, 0, 0))],
        out_specs=head_cols(dv),
        scratch_shapes=[pltpu.VMEM((nch, L), F32), pltpu.VMEM((nch, L), F32)],
        compiler_params=_cparams(2),
        name="mlstm_chunk_scan",
    )(b_igate, b_fgate, q, k, v, g, g, norm_gain.reshape(nh, 1, dv))


def _router_kernel(x_ref, gain_ref, sc_ref, sh_ref, wh_ref, wl_ref, rb_ref, h_ref, ri_ref, rw_ref, cnt_ref):
    h = _norm_mod(x_ref[0], gain_ref[...], sc_ref[0], sh_ref[0])
    hi = h.astype(BF16)
    lo = (h - hi.astype(F32)).astype(BF16)
    logits = _dot(hi, wh_ref[...]) + (_dot(lo, wh_ref[...]) + _dot(hi, wl_ref[...]))
    h_ref[0] = hi
    aff = jax.nn.sigmoid(logits)
    tm = aff.shape[0]
    lane = lax.broadcasted_iota(jnp.int32, (tm, LANES), 1)
    choice = jnp.where(lane < N_EXPERTS, aff + rb_ref[...], -jnp.inf)

    def first_max(v):
        m = jnp.max(v, axis=-1, keepdims=True)
        return m, jnp.min(jnp.where(v == m, lane, LANES), axis=-1, keepdims=True)

    best = e0 = e1 = None
    for g in range(N_GROUPS):
        cg = jnp.where((lane // EXPERTS_PER_GROUP) == g, choice, -jnp.inf)
        m1, i1 = first_max(cg)
        m2, i2 = first_max(jnp.where(lane == i1, -jnp.inf, cg))
        score = m1 + m2
        if g == 0:
            best, e0, e1 = score, i1, i2
        else:
            better = score > best
            best = jnp.where(better, score, best)
            e0 = jnp.where(better, i1, e0)
            e1 = jnp.where(better, i2, e1)
    is0, is1 = lane == e0, lane == e1
    a0 = jnp.sum(jnp.where(is0, aff, 0.0), axis=-1, keepdims=True)
    a1 = jnp.sum(jnp.where(is1, aff, 0.0), axis=-1, keepdims=True)
    tot = a0 + a1
    rw_ref[0] = jnp.where(lane == 0, a0 / tot, jnp.where(lane == 1, a1 / tot, 0.0))
    onehot = jnp.where(is0 | is1, 1.0, 0.0).astype(BF16)
    r_i = lax.broadcasted_iota(jnp.int32, (tm, tm), 0)
    c_i = lax.broadcasted_iota(jnp.int32, (tm, tm), 1)
    running = _dot(jnp.where(r_i >= c_i, 1.0, 0.0).astype(BF16), onehot)
    r0 = jnp.sum(jnp.where(is0, running, 0.0), axis=-1, keepdims=True) - 1.0
    r1 = jnp.sum(jnp.where(is1, running, 0.0), axis=-1, keepdims=True) - 1.0
    ri_ref[0] = jnp.where(lane == 0, e0, jnp.where(lane == 1, e1, jnp.where(
        lane == 2, r0.astype(jnp.int32), jnp.where(lane == 3, r1.astype(jnp.int32), 0))))
    cnt_ref[0] = jnp.broadcast_to(running[tm - 1:tm, :], (8, LANES))


def _router_call(x, gain, sc, sh, router_w, router_b, tm):
    b, s, d = x.shape
    nt = s // tm
    wp = jnp.pad(router_w, ((0, 0), (0, LANES - N_EXPERTS)))
    wh = wp.astype(BF16)
    wl = (wp - wh.astype(F32)).astype(BF16)
    rb = jnp.pad(router_b.astype(F32), (0, LANES - N_EXPERTS)).reshape(1, LANES)
    row = lambda i, j: (i, j, 0)
    per_b = lambda i, j: (i, 0, 0)
    const = lambda i, j: (0, 0)
    return pl.pallas_call(
        _router_kernel,
        out_shape=(jax.ShapeDtypeStruct((b, s, d), BF16), jax.ShapeDtypeStruct((b, s, LANES), jnp.int32),
                   jax.ShapeDtypeStruct((b, s, LANES), F32), jax.ShapeDtypeStruct((b * nt, 8, LANES), F32)),
        grid=(b, nt),
        in_specs=[pl.BlockSpec((1, tm, d), row),
                  pl.BlockSpec((1, d), const),
                  pl.BlockSpec((1, 1, d), per_b),
                  pl.BlockSpec((1, 1, d), per_b),
                  pl.BlockSpec((d, LANES), const),
                  pl.BlockSpec((d, LANES), const),
                  pl.BlockSpec((1, LANES), const)],
        out_specs=(pl.BlockSpec((1, tm, d), row), pl.BlockSpec((1, tm, LANES), row),
                   pl.BlockSpec((1, tm, LANES), row), pl.BlockSpec((1, 8, LANES), lambda i, j: (i * nt + j, 0, 0))),
        compiler_params=_cparams(2),
        name="moe_router",
    )(x, gain.reshape(1, d), sc, sh, wh, wl, rb)


def _expert_kernel(be_ref, live_ref, x_ref, wg_ref, wu_ref, wd_ref, o_ref, wg_s, wu_s, wd_s):
    i = pl.program_id(0)
    fresh = (i == 0) | (be_ref[i] != be_ref[jnp.maximum(i - 1, 0)])

    @pl.when(fresh)
    def _():
        wg_s[...] = wg_ref[0, 0].astype(BF16)
        wu_s[...] = wu_ref[0, 0].astype(BF16)
        wd_s[...] = wd_ref[0, 0].astype(BF16)

    @pl.when(live_ref[i] == 1)
    def _():
        xb = x_ref[...]
        gate = _dot(xb, wg_s[...])
        hid = gate * jax.nn.sigmoid(gate) * _dot(xb, wu_s[...])
        o_ref[...] = _dot(hid.astype(BF16), wd_s[...]).astype(o_ref.dtype)

    @pl.when(live_ref[i] == 0)
    def _():
        o_ref[...] = jnp.zeros_like(o_ref)


def _expert_call(block_expert, block_live, xs, w_gate, w_up, w_down, layer):
    p, d = xs.shape
    de = w_gate.shape[-1]
    nb = p // MOE_ROWS
    grid_spec = pltpu.PrefetchScalarGridSpec(
        num_scalar_prefetch=2,
        grid=(nb,),
        in_specs=[pl.BlockSpec((MOE_ROWS, d), lambda i, be, lv: (i, 0)),
                  pl.BlockSpec((1, 1, d, de), lambda i, be, lv: (layer, be[i], 0, 0)),
                  pl.BlockSpec((1, 1, d, de), lambda i, be, lv: (layer, be[i], 0, 0)),
                  pl.BlockSpec((1, 1, de, d), lambda i, be, lv: (layer, be[i], 0, 0))],
        out_specs=pl.BlockSpec((MOE_ROWS, d), lambda i, be, lv: (i, 0)),
        scratch_shapes=[pltpu.VMEM((d, de), BF16), pltpu.VMEM((d, de), BF16), pltpu.VMEM((de, d), BF16)],
    )
    return pl.pallas_call(
        _expert_kernel,
        out_shape=jax.ShapeDtypeStruct((p, d), F32),
        grid_spec=grid_spec,
        compiler_params=_cparams(1),
        name="moe_experts",
    )(block_expert, block_live, xs, w_gate, w_up, w_down)


def _combine_kernel(x_ref, g_ref, rw_ref, ya_ref, yb_ref, o_ref):
    rw = rw_ref[0]
    y = rw[:, 0:1] * ya_ref[0].astype(F32) + rw[:, 1:2] * yb_ref[0].astype(F32)
    o_ref[0] = x_ref[0] + g_ref[0] * y


def _combine_call(x, g, rw, ya, yb, tm):
    b, s, d = x.shape
    row = lambda i, j: (i, j, 0)
    spec = pl.BlockSpec((1, tm, d), row)
    return pl.pallas_call(
        _combine_kernel,
        out_shape=jax.ShapeDtypeStruct((b, s, d), F32),
        grid=(b, s // tm),
        in_specs=[spec, pl.BlockSpec((1, 1, d), lambda i, j: (i, 0, 0)),
                  pl.BlockSpec((1, tm, LANES), row), spec, spec],
        out_specs=spec,
        compiler_params=_cparams(2),
        name="moe_combine",
    )(x, g, rw, ya, yb)


def _row_layout(ri, cnt, tm):
    t = ri.shape[0]
    nt = t // tm
    lanes = jnp.arange(N_EXPERTS, dtype=jnp.int32)
    tile_cnt = cnt[:, 0, :N_EXPERTS].astype(jnp.int32)
    tile_off = jnp.cumsum(tile_cnt, axis=0) - tile_cnt
    counts = jnp.sum(tile_cnt, axis=0)
    padded = (counts + MOE_ROWS - 1) // MOE_ROWS * MOE_ROWS
    p_ends = jnp.cumsum(padded)
    base = (p_ends - padded)[None, :] + tile_off
    e = ri[:, 0:TOP_K].reshape(nt, tm, TOP_K)
    rank = ri[:, TOP_K:2 * TOP_K].reshape(nt, tm, TOP_K)
    dest = jnp.sum(jnp.where(e[..., None] == lanes, base[:, None, None, :], 0), axis=-1) + rank
    dest = dest.reshape(t, TOP_K)
    nb = (t * TOP_K) // MOE_ROWS + N_EXPERTS
    tok = jnp.repeat(jnp.arange(t, dtype=jnp.int32), TOP_K)
    buf_tok = (jnp.arange(nb * MOE_ROWS, dtype=jnp.int32) % t).at[dest.reshape(-1)].set(
        tok, mode="promise_in_bounds", unique_indices=True)
    block_start = jnp.arange(nb, dtype=jnp.int32) * MOE_ROWS
    block_expert = jnp.minimum(jnp.sum((p_ends[None, :] <= block_start[:, None]).astype(jnp.int32), axis=-1),
                               N_EXPERTS - 1)
    block_live = (block_start < p_ends[-1]).astype(jnp.int32)
    return dest, buf_tok, block_expert, block_live


def _moe_layer(x, gain, sc, sh, g, router_w, router_b, w_gate, w_up, w_down, layer, tm):
    b, s, d = x.shape
    t = b * s
    hf, ri, rw, cnt = _router_call(x, gain, sc, sh, router_w, router_b, tm)
    dest, buf_tok, block_expert, block_live = _row_layout(ri.reshape(t, LANES), cnt, tm)
    take = lambda a, idx: a.at[idx].get(mode="promise_in_bounds")
    xs = take(hf.reshape(t, d), buf_tok)
    out = _expert_call(block_expert, block_live, xs, w_gate, w_up, w_down, layer)
    ya = take(out, dest[:, 0]).reshape(b, s, d)
    yb = take(out, dest[:, 1]).reshape(b, s, d)
    return _combine_call(x, g, rw, ya, yb, tm)


def _nsa_layer(x, gain, sc, sh, g, w_in, w_out, q_gain, k_gain, cmp_pe, cmp_w1, cmp_b1, cmp_w2, cmp_b2, tm):
    b, s, d = x.shape
    G, dh = NSA_KV_GROUPS, HEAD_DIM
    cos, sin = _rope_tables(jnp.arange(s, dtype=jnp.int32))
    q, cv, ks, vs, kw, vw, gates = _nsa_in_call(x, gain, sc, sh, w_in, q_gain, k_gain, cos, sin, tm)

    n_cmp = (s - CMP_BLOCK) // CMP_STRIDE + 1
    n_str = s // CMP_STRIDE
    cmp_pos = jnp.arange(n_str, dtype=jnp.int32) * CMP_STRIDE + (CMP_BLOCK - 1)
    ccos, csin = _rope_tables(cmp_pos)
    cmp = _compress_call(cv, cmp_pe, cmp_w1, cmp_b1, cmp_w2, cmp_b2, k_gain[0], ccos, csin)
    cmp = jnp.pad(cmp, ((0, 0), (0, 0), (0, LANES - n_str), (0, 0)))
    kc, vc = cmp[0], cmp[1]

    ns = s // SEL_BLOCK
    r_, u_ = SEL_BLOCK // CMP_STRIDE, CMP_BLOCK // CMP_STRIDE
    c_idx = (r_ * np.arange(ns)[:, None, None] + np.arange(r_)[None, :, None]
             + np.arange(u_)[None, None, :]).reshape(ns, -1)
    c2s = (c_idx[:, :, None] == np.arange(n_cmp)[None, None, :]).sum(1).astype(np.float32)
    c2s = jnp.asarray(np.pad(c2s, ((0, 0), (0, LANES - n_cmp))))

    o = _nsa_attn_call(q, gates, kc, vc, ks, vs, kw, vw, c2s, n_cmp)
    return _out_proj_call(o, None, w_out[_head_pair_order(), :], x, g, tm)


def _mlstm_layer(x, gain, sc, sh, g, w_in, w_out, b_igate, b_fgate, norm_gain, tm):
    nh = MLSTM_HEADS
    dv = norm_gain.shape[-1]
    dqk = (w_in.shape[-1] - 2 * nh - 2 * nh * dv) // (2 * nh)
    q, k, v, og, gates = _mlstm_in_call(x, gain, sc, sh, w_in, dqk, dv, tm)
    hs = _mlstm_call(q, k, v, gates, b_igate, b_fgate, norm_gain)
    return _out_proj_call(hs, og, w_out, x, g, tm)


def kernel(x, c, ada_w, ada_b, norm_mix_gain, norm_ffn_gain, nsa_w_in, nsa_w_out, nsa_q_gain, nsa_k_gain, nsa_cmp_pe, nsa_cmp_w1, nsa_cmp_b1, nsa_cmp_w2, nsa_cmp_b2, mlstm_w_in, mlstm_b_igate, mlstm_b_fgate, mlstm_norm_gain, mlstm_w_out, router_w, router_b, moe_w_gate, moe_w_up, moe_w_down):
    b, s, d = x.shape
    depth = ada_w.shape[0]
    tm = min(512, s)
    mod = _mod_call(c, ada_w, ada_b)
    for i in range(depth):
        sh_m, sc_m, g_m, sh_f, sc_f, g_f = [mod[i, :, None, k * d:(k + 1) * d] for k in range(6)]
        j = i // 2
        if i % 2 == 0:
            x = _nsa_layer(x, norm_mix_gain[i], sc_m, sh_m, g_m, nsa_w_in[j], nsa_w_out[j], nsa_q_gain[j],
                           nsa_k_gain[j], nsa_cmp_pe[j], nsa_cmp_w1[j], nsa_cmp_b1[j], nsa_cmp_w2[j],
                           nsa_cmp_b2[j], tm)
        else:
            x = _mlstm_layer(x, norm_mix_gain[i], sc_m, sh_m, g_m, mlstm_w_in[j], mlstm_w_out[j],
                             mlstm_b_igate[j], mlstm_b_fgate[j], mlstm_norm_gain[j], tm)
        x = _moe_layer(x, norm_ffn_gain[i], sc_f, sh_f, g_f, router_w, router_b,
                       moe_w_gate, moe_w_up, moe_w_down, i, tm)
    return x
```

```python
import functools

import numpy as np
import jax
import jax.numpy as jnp
from jax import lax
from jax.experimental import pallas as pl
from jax.experimental.pallas import tpu as pltpu

F32 = jnp.float32
BF16 = jnp.bfloat16
HIGHEST = lax.Precision.HIGHEST

EPS = 1e-6
NEG = -1e30
BIG = 1e9
ROPE_THETA = 500000.0
LOG2E = 1.4426950408889634

NSA_HEADS = 16
NSA_KV_GROUPS = 2
NSA_HEADS_PER_GROUP = NSA_HEADS // NSA_KV_GROUPS
HEAD_DIM = 64
ROT_DIM = HEAD_DIM // 4
CMP_BLOCK = 32
CMP_STRIDE = 16
SEL_BLOCK = 64
SEL_TOPN = 8
WINDOW = 512
NSA_Q_BLOCK = 64
NSA_BRANCHES = 3
NSA_BLOCKS_PER_STEP = 2

MLSTM_HEADS = 4
MLSTM_CHUNK = 256
MLSTM_GROUP = 1
GATE_SOFTCAP = 15.0

N_EXPERTS = 32
N_GROUPS = 4
EXPERTS_PER_GROUP = N_EXPERTS // N_GROUPS
TOP_K = 2
MOE_ROWS = 512

LANES = 128
VMEM_LIMIT = 48 * 1024 * 1024


def _cparams(n_axes):
    return pltpu.CompilerParams(dimension_semantics=("arbitrary",) * n_axes,
                                vmem_limit_bytes=VMEM_LIMIT)


def _dot(a, b):
    return jnp.dot(a, b, preferred_element_type=F32)


def _dot_nt(a, b):
    return lax.dot_general(a, b, (((1,), (1,)), ((), ())), preferred_element_type=F32)


def _dot_tn(a, b):
    return lax.dot_general(a, b, (((0,), (0,)), ((), ())), preferred_element_type=F32)


def _norm_mod(x, gain, sc, sh):
    y = x * lax.rsqrt(jnp.mean(x * x, axis=-1, keepdims=True) + EPS) * gain
    return y * (1.0 + sc) + sh


def _half_norm_rope(x, gain, cos, sin, other_half_zero=False):
    lane = lax.broadcasted_iota(jnp.int32, x.shape, x.ndim - 1)
    x2 = x * x
    if other_half_zero:
        ms = jnp.sum(x2, axis=-1, keepdims=True) * (1.0 / HEAD_DIM)
    else:
        left = lane < HEAD_DIM
        ss_l = jnp.sum(jnp.where(left, x2, 0.0), axis=-1, keepdims=True)
        ss_r = jnp.sum(jnp.where(left, 0.0, x2), axis=-1, keepdims=True)
        ms = jnp.where(left, ss_l, ss_r) * (1.0 / HEAD_DIM)
    y = x * lax.rsqrt(ms + EPS) * gain
    half = ROT_DIM // 2
    nd = x.ndim - 1
    partner = jnp.where((lane % HEAD_DIM) < half,
                        pltpu.roll(y, LANES - half, nd), pltpu.roll(y, half, nd))
    return y * cos + partner * sin


def _rope_tables(pos):
    half = ROT_DIM // 2
    inv_freq = ROPE_THETA ** (-jnp.arange(half, dtype=F32) / half)
    ang = pos.astype(F32)[:, None] * inv_freq[None, :]
    cos, sin = jnp.cos(ang), jnp.sin(ang)
    n = pos.shape[0]
    one = jnp.ones((n, HEAD_DIM - ROT_DIM), F32)
    cos_h = jnp.concatenate([cos, cos, one], axis=-1)
    sin_h = jnp.concatenate([-sin, sin, 0.0 * one], axis=-1)
    return jnp.tile(cos_h, (1, 2)), jnp.tile(sin_h, (1, 2))


def _mod_kernel(c_ref, w_ref, b_ref, o_ref):
    c = c_ref[...]
    cond = c * jax.nn.sigmoid(c)
    o_ref[0] = jnp.dot(cond, w_ref[0], preferred_element_type=F32, precision=HIGHEST) + b_ref[0]


def _mod_call(c, ada_w, ada_b):
    depth, d, n = ada_w.shape
    b = c.shape[0]
    tn = n // 4
    return pl.pallas_call(
        _mod_kernel,
        out_shape=jax.ShapeDtypeStruct((depth, b, n), F32),
        grid=(depth, n // tn),
        in_specs=[pl.BlockSpec((b, d), lambda i, j: (0, 0)),
                  pl.BlockSpec((1, d, tn), lambda i, j: (i, 0, j)),
                  pl.BlockSpec((1, 1, tn), lambda i, j: (i, 0, j))],
        out_specs=pl.BlockSpec((1, b, tn), lambda i, j: (i, 0, j)),
        compiler_params=_cparams(2),
        name="adaln_mod",
    )(c, ada_w, ada_b.reshape(depth, 1, n))


def _nsa_in_kernel(x_ref, gain_ref, sc_ref, sh_ref, wq_ref, wkv_ref, wg_ref, qg_ref, kg_ref, cos_ref, sin_ref,
                   q_ref, cv_ref, ks_ref, vs_ref, kw_ref, vw_ref, g_ref):
    h = _norm_mod(x_ref[0], gain_ref[...], sc_ref[0], sh_ref[0]).astype(BF16)
    g_ref[0] = _dot(h, wg_ref[...])
    kv = _dot(h, wkv_ref[...])
    cos, sin = cos_ref[...], sin_ref[...]
    q = _dot(h, wq_ref[...])
    for r in range(NSA_HEADS_PER_GROUP):
        slab = _half_norm_rope(q[:, r * LANES:(r + 1) * LANES], qg_ref[...], cos, sin)
        q_ref[0, :, r * LANES:(r + 1) * LANES] = (slab * (HEAD_DIM ** -0.5 * LOG2E)).astype(BF16)
    cv_ref[0] = kv[:, 0:2 * LANES]
    ks_ref[0] = _half_norm_rope(kv[:, 2 * LANES:3 * LANES], kg_ref[1:2, :], cos, sin).astype(BF16)
    vs_ref[0] = kv[:, 3 * LANES:4 * LANES].astype(BF16)
    kw_ref[0] = _half_norm_rope(kv[:, 4 * LANES:5 * LANES], kg_ref[2:3, :], cos, sin).astype(BF16)
    vw_ref[0] = kv[:, 5 * LANES:6 * LANES].astype(BF16)


def _head_pair_order():
    r, g, dd = np.meshgrid(np.arange(NSA_HEADS_PER_GROUP), np.arange(NSA_KV_GROUPS), np.arange(HEAD_DIM),
                           indexing="ij")
    return ((g * NSA_HEADS_PER_GROUP + r) * HEAD_DIM + dd).reshape(-1)


def _nsa_in_call(x, gain, sc, sh, w_in, q_gain, k_gain, cos, sin, tm):
    b, s, d = x.shape
    nq = NSA_HEADS * HEAD_DIM
    nkv = 6 * LANES
    wq = w_in[:, :nq][:, _head_pair_order()].astype(BF16)
    qg = jnp.tile(q_gain, 2).reshape(1, LANES)
    wkv = w_in[:, nq:nq + nkv].astype(BF16)
    ng = NSA_BRANCHES * NSA_HEADS
    wg = jnp.pad(w_in[:, nq + nkv:], ((0, 0), (0, LANES - ng))).astype(BF16)
    kg = jnp.tile(k_gain, (1, 2))
    row = lambda i, j: (i, j, 0)
    per_b = lambda i, j: (i, 0, 0)
    const = lambda i, j: (0, 0)
    kv_out = lambda dt: jax.ShapeDtypeStruct((b, s, LANES), dt)
    return pl.pallas_call(
        _nsa_in_kernel,
        out_shape=(jax.ShapeDtypeStruct((b, s, nq), BF16), jax.ShapeDtypeStruct((b, s, 2 * LANES), F32),
                   kv_out(BF16), kv_out(BF16), kv_out(BF16), kv_out(BF16), kv_out(F32)),
        grid=(b, s // tm),
        in_specs=[pl.BlockSpec((1, tm, d), row),
                  pl.BlockSpec((1, d), const),
                  pl.BlockSpec((1, 1, d), per_b),
                  pl.BlockSpec((1, 1, d), per_b),
                  pl.BlockSpec((d, nq), const),
                  pl.BlockSpec((d, nkv), const),
                  pl.BlockSpec((d, LANES), const),
                  pl.BlockSpec((1, LANES), const),
                  pl.BlockSpec((3, LANES), const),
                  pl.BlockSpec((tm, LANES), lambda i, j: (j, 0)),
                  pl.BlockSpec((tm, LANES), lambda i, j: (j, 0))],
        out_specs=(pl.BlockSpec((1, tm, nq), row), pl.BlockSpec((1, tm, 2 * LANES), row))
        + (pl.BlockSpec((1, tm, LANES), row),) * 5,
        compiler_params=_cparams(2),
        name="nsa_in_proj",
    )(x, gain.reshape(1, d), sc, sh, wq, wkv, wg, qg, kg, cos, sin)


def _compress_kernel(a_ref, pe_ref, w1_ref, b1_ref, w2_ref, b2_ref, kg_ref, cos_ref, sin_ref, o_ref, *, n_str):
    is_key = pl.program_id(0) == 0
    hid2 = w1_ref.shape[-1]
    first = jnp.zeros((n_str, hid2), F32)
    second = jnp.zeros((n_str, hid2), F32)
    pe_term = jnp.zeros((8, hid2), F32)
    for l in range(CMP_STRIDE):
        rows = a_ref[0, pl.ds(l, n_str, stride=CMP_STRIDE), :].astype(BF16)
        first = first + _dot(rows, w1_ref[0, l])
        second = second + _dot(rows, w1_ref[0, CMP_STRIDE + l])
    for l in range(CMP_BLOCK):
        pe_term = pe_term + _dot(pe_ref[0, l].astype(BF16), w1_ref[0, l])
    hid = first + pltpu.roll(second, n_str - 1, 0) + pe_term[0:1] + b1_ref[0]
    hid = 0.5 * hid * (1.0 + jnp.tanh(np.sqrt(2.0 / np.pi) * (hid + 0.044715 * hid * hid * hid)))
    out = _dot(hid.astype(BF16), w2_ref[0]) + b2_ref[0]
    normed = _half_norm_rope(out, kg_ref[...], cos_ref[...], sin_ref[...])
    o_ref[0, 0] = jnp.where(is_key, normed, out).astype(o_ref.dtype)


def _block_diag2(w):
    z = jnp.zeros_like(w)
    return jnp.concatenate([jnp.concatenate([w, z], axis=-1), jnp.concatenate([z, w], axis=-1)], axis=-2)


def _compress_call(cv, pe, w1, b1, w2, b2, k_gain0, cos, sin):
    b, s, _ = cv.shape
    n_str = s // CMP_STRIDE
    hid = w1.shape[-1]
    w1bd = _block_diag2(w1.reshape(2, CMP_BLOCK, HEAD_DIM, hid)).astype(BF16)
    w2bd = _block_diag2(w2).astype(BF16)
    pe2 = jnp.broadcast_to(jnp.tile(pe, (1, 1, 2))[:, :, None, :], (2, CMP_BLOCK, 8, LANES))
    b1t = jnp.tile(b1, (1, 2)).reshape(2, 1, 2 * hid)
    b2t = jnp.tile(b2, (1, 2)).reshape(2, 1, LANES)
    kg = jnp.tile(k_gain0, 2).reshape(1, LANES)
    sel3 = lambda i, j: (i, 0, 0)
    sel4 = lambda i, j: (i, 0, 0, 0)
    const = lambda i, j: (0, 0)
    return pl.pallas_call(
        functools.partial(_compress_kernel, n_str=n_str),
        out_shape=jax.ShapeDtypeStruct((2, b, n_str, LANES), BF16),
        grid=(2, b),
        in_specs=[pl.BlockSpec((1, s, LANES), lambda i, j: (j, 0, i)),
                  pl.BlockSpec((1, CMP_BLOCK, 8, LANES), sel4),
                  pl.BlockSpec((1, CMP_BLOCK, LANES, 2 * hid), sel4),
                  pl.BlockSpec((1, 1, 2 * hid), sel3),
                  pl.BlockSpec((1, 2 * hid, LANES), sel3),
                  pl.BlockSpec((1, 1, LANES), sel3),
                  pl.BlockSpec((1, LANES), const),
                  pl.BlockSpec((n_str, LANES), const),
                  pl.BlockSpec((n_str, LANES), const)],
        out_specs=pl.BlockSpec((1, 1, n_str, LANES), lambda i, j: (i, j, 0, 0)),
        compiler_params=_cparams(2),
        name="nsa_compress",
    )(cv, pe2, w1bd, b1t, w2bd, b2t, kg, cos, sin)


def _attend(qb, k_ref, v_ref, k0, spans, g, bias):
    rows = qb.shape[0]
    m = acc = None
    for off, size in spans:
        k = k_ref[0, pl.ds(k0 + off, size), :]
        v = v_ref[0, pl.ds(k0 + off, size), :]
        s = _dot_nt(qb, k).reshape(rows // NSA_Q_BLOCK, NSA_Q_BLOCK, size) + bias[None, :, off:off + size]
        s = s.reshape(rows, size)
        m_span = jnp.max(s, axis=-1, keepdims=True)
        m_new = m_span if m is None else jnp.maximum(m, m_span)
        p = jnp.exp2(s - m_new).astype(BF16)
        v_lane = lax.broadcasted_iota(jnp.int32, v.shape, 1)
        pv = _dot(p, jnp.where((v_lane // HEAD_DIM) == g, v, jnp.ones_like(v)))
        acc = pv if m is None else acc * jnp.exp2(m - m_new) + pv
        m = m_new
    return acc


def _nsa_attn_kernel(*refs, blocks_per_step, **static):
    for sub in range(blocks_per_step):
        _nsa_attn_block(sub, pl.program_id(1) * blocks_per_step + sub, *refs, **static)


def _nsa_attn_block(sub, q_block, q_ref, g_ref, kc_ref, vc_ref, ks_ref, vs_ref, kw_ref, vw_ref,
                    c2s_ref, exp_ref, o_ref, *, seq, n_cmp, n_top, win_keys, sel_span, n_spans, q_block0):
    R, QB = NSA_HEADS_PER_GROUP, NSA_Q_BLOCK
    rows = R * QB
    n_sel = seq // SEL_BLOCK
    qi = q_block + q_block0
    s0 = qi * QB
    tok = slice(sub * QB, (sub + 1) * QB)
    gt = jax.nn.sigmoid(g_ref[0, tok, :])
    lane = lax.broadcasted_iota(jnp.int32, (QB, LANES), 1)
    tq = s0 + lax.broadcasted_iota(jnp.int32, (rows, 1), 0) % QB
    tq1 = s0 + lax.broadcasted_iota(jnp.int32, (QB, 1), 0)
    ones_sq = jnp.ones((LANES, LANES), BF16)

    w0 = pl.multiple_of(jnp.maximum(s0 + QB - win_keys, 0), SEL_BLOCK)
    wpos = w0 + lax.broadcasted_iota(jnp.int32, (1, win_keys), 1)
    bias_w = jnp.where((wpos <= tq1) & (wpos > tq1 - WINDOW), 0.0, NEG)
    win_spans = [(off, min(3 * LANES, win_keys - off)) for off in range(0, win_keys, 3 * LANES)]

    qbs, psums, o_cs, acc_ws = [], [], [], []
    for g in range(NSA_KV_GROUPS):
        in_g = (lane // HEAD_DIM) == g
        zero = jnp.zeros((QB, LANES), BF16)
        qb = jnp.concatenate([jnp.where(in_g, q_ref[0, tok, r * LANES:(r + 1) * LANES], zero) for r in range(R)],
                             axis=0)

        sc = _dot_nt(qb, kc_ref[0])
        cpos = lax.broadcasted_iota(jnp.int32, (1, LANES), 1) * CMP_STRIDE + (CMP_BLOCK - 1)
        valid_c = (cpos <= tq) & (lax.broadcasted_iota(jnp.int32, (1, LANES), 1) < n_cmp)
        sc = jnp.where(valid_c, sc, NEG)
        e_c = jnp.exp2(sc - jnp.max(sc, axis=-1, keepdims=True)).astype(BF16)
        p_c = jnp.where(valid_c, e_c.astype(F32) / _dot(e_c, ones_sq), 0.0)
        o_cs.append(_dot(p_c.astype(BF16), vc_ref[0]))

        psums.append(jnp.sum(p_c.reshape(R, QB, LANES), axis=0))

        acc_ws.append(_attend(qb, kw_ref, vw_ref, w0, win_spans, g, bias_w))
        qbs.append(qb)

    imp = lax.dot_general(c2s_ref[...], jnp.concatenate(psums, axis=0), (((1,), (1,)), ((), ())),
                          preferred_element_type=F32, precision=HIGHEST)
    blk = lax.broadcasted_iota(jnp.int32, (n_sel, NSA_KV_GROUPS * QB), 0)
    forced = (blk == 0) | (blk == qi) | (blk == qi - 1)
    imp = jnp.where(blk <= qi, jnp.where(forced, BIG, imp), -BIG)
    beaten = jnp.zeros(imp.shape, F32)
    for k in range(1, n_sel):
        other = pltpu.roll(imp, k, 0)
        beats = (other > imp) | ((blk >= k) & (other == imp))
        beaten = beaten + jnp.where(beats, 1.0, 0.0)
    chosen = jnp.where(beaten < n_top, 1.0, 0.0).astype(BF16)
    n_keys = n_spans * sel_span
    picked = _dot_tn(chosen, exp_ref[:, :n_keys])
    kpos = lax.broadcasted_iota(jnp.int32, (1, n_keys), 1)
    sel_spans = [(c * sel_span, sel_span) for c in range(n_spans)]

    heads = []
    for g in range(NSA_KV_GROUPS):
        bias_s = jnp.where((picked[g * QB:(g + 1) * QB] > 0.5) & (kpos <= tq1), 0.0, NEG)
        acc_s = _attend(qbs[g], ks_ref, vs_ref, 0, sel_spans, g, bias_s)
        o_c, acc_w = o_cs[g], acc_ws[g]
        c_sum = (1 - g) * HEAD_DIM
        per_head = []
        for r in range(R):
            h = g * R + r
            rs = slice(r * QB, (r + 1) * QB)
            a_s, a_w = acc_s[rs], acc_w[rs]
            g_s = gt[:, NSA_HEADS + h:NSA_HEADS + h + 1] / a_s[:, c_sum:c_sum + 1]
            g_w = gt[:, 2 * NSA_HEADS + h:2 * NSA_HEADS + h + 1] / a_w[:, c_sum:c_sum + 1]
            per_head.append(gt[:, h:h + 1] * o_c[rs] + g_s * a_s + g_w * a_w)
        heads.append(per_head)
    for r in range(R):
        slab = jnp.where(lane < HEAD_DIM, heads[0][r], heads[1][r])
        o_ref[0, tok, r * LANES:(r + 1) * LANES] = slab.astype(o_ref.dtype)


def _nsa_attn_call(q, gates, kc, vc, ks, vs, kw, vw, cmp_to_sel, n_cmp):
    b, s, nq = q.shape
    qb = NSA_Q_BLOCK
    n_top = min(SEL_TOPN, s // SEL_BLOCK)
    win_keys = min(WINDOW + 2 * qb, s)
    sel_span = min(512, s)
    n_sel = s // SEL_BLOCK
    expand = jnp.asarray(np.arange(n_sel)[:, None] == (np.arange(s)[None, :] // SEL_BLOCK), BF16)
    per_b = lambda i, j: (i, 0, 0)
    const = lambda i, j: (0, 0)
    per_call = sel_span // qb
    per_step = min(NSA_BLOCKS_PER_STEP, per_call)
    qrows = per_step * qb
    outs = []
    for n in range(1, s // sel_span + 1):
        q0 = (n - 1) * per_call
        row = lambda i, j, t0=q0 // per_step: (i, j + t0, 0)
        kern = functools.partial(_nsa_attn_kernel, blocks_per_step=per_step, seq=s, n_cmp=n_cmp, n_top=n_top,
                                 win_keys=win_keys, sel_span=sel_span, n_spans=n, q_block0=q0)
        outs.append(pl.pallas_call(
            kern,
            out_shape=jax.ShapeDtypeStruct((b, sel_span, nq), BF16),
            grid=(b, per_call // per_step),
            in_specs=[pl.BlockSpec((1, qrows, nq), row),
                      pl.BlockSpec((1, qrows, LANES), row),
                      pl.BlockSpec((1, LANES, LANES), per_b),
                      pl.BlockSpec((1, LANES, LANES), per_b),
                      pl.BlockSpec((1, n * sel_span, LANES), per_b),
                      pl.BlockSpec((1, n * sel_span, LANES), per_b),
                      pl.BlockSpec((1, s, LANES), per_b),
                      pl.BlockSpec((1, s, LANES), per_b),
                      pl.BlockSpec((n_sel, LANES), const),
                      pl.BlockSpec((n_sel, s), const)],
            out_specs=pl.BlockSpec((1, qrows, nq), lambda i, j: (i, j, 0)),
            compiler_params=_cparams(2),
            name="nsa_attention",
        )(q, gates, kc, vc, ks, vs, kw, vw, cmp_to_sel, expand))
    return jnp.concatenate(outs, axis=1)


def _nsa_out_kernel(a_ref, w_ref, x_ref, g_ref, o_ref):
    o_ref[0] = x_ref[0] + g_ref[0] * _dot(a_ref[0], w_ref[...])


def _mlstm_out_kernel(a_ref, og_ref, w_ref, x_ref, g_ref, o_ref):
    lhs = (jax.nn.sigmoid(og_ref[0].astype(F32)) * a_ref[0].astype(F32)).astype(BF16)
    o_ref[0] = x_ref[0] + g_ref[0] * _dot(lhs, w_ref[...])


def _out_proj_call(a, og, w_out, x, g, tm):
    b, s, d = x.shape
    k = a.shape[-1]
    row = lambda i, j: (i, j, 0)
    per_b = lambda i, j: (i, 0, 0)
    a_spec = pl.BlockSpec((1, tm, k), row)
    tail = [pl.BlockSpec((k, d), lambda i, j: (0, 0)), pl.BlockSpec((1, tm, d), row),
            pl.BlockSpec((1, 1, d), per_b)]
    if og is None:
        kern, ins, args = _nsa_out_kernel, [a_spec] + tail, (a, w_out.astype(BF16), x, g)
    else:
        kern, ins, args = _mlstm_out_kernel, [a_spec, a_spec] + tail, (a, og, w_out.astype(BF16), x, g)
    return pl.pallas_call(
        kern,
        out_shape=jax.ShapeDtypeStruct((b, s, d), F32),
        grid=(b, s // tm),
        in_specs=ins,
        out_specs=pl.BlockSpec((1, tm, d), row),
        compiler_params=_cparams(2),
        name="mixer_out_proj",
    )(*args)


def _mlstm_in_kernel(x_ref, gain_ref, sc_ref, sh_ref, wq_ref, wk_ref, wv_ref, wo_ref, wg_ref,
                     q_ref, k_ref, v_ref, o_ref, g_ref):
    h = _norm_mod(x_ref[0], gain_ref[...], sc_ref[0], sh_ref[0]).astype(BF16)
    q_ref[0] = _dot(h, wq_ref[...]).astype(BF16)
    k_ref[0] = _dot(h, wk_ref[...]).astype(BF16)
    v_ref[0] = _dot(h, wv_ref[...]).astype(BF16)
    o_ref[0] = _dot(h, wo_ref[...]).astype(BF16)
    g_ref[0] = _dot(h, wg_ref[...])


def _mlstm_in_call(x, gain, sc, sh, w_in, dqk, dv, tm):
    b, s, d = x.shape
    nh = MLSTM_HEADS
    sizes = [nh * dqk, nh * dqk, nh * dv, nh * dv]
    offs = np.cumsum([0] + sizes)
    ws = [w_in[:, offs[i]:offs[i + 1]].astype(BF16) for i in range(4)]
    wg = jnp.pad(w_in[:, offs[4]:], ((0, 0), (0, LANES - 2 * nh))).astype(BF16)
    row = lambda i, j: (i, j, 0)
    per_b = lambda i, j: (i, 0, 0)
    const = lambda i, j: (0, 0)
    widths = sizes + [LANES]
    return pl.pallas_call(
        _mlstm_in_kernel,
        out_shape=tuple(jax.ShapeDtypeStruct((b, s, n), BF16) for n in sizes)
        + (jax.ShapeDtypeStruct((b, s, LANES), F32),),
        grid=(b, s // tm),
        in_specs=[pl.BlockSpec((1, tm, d), row),
                  pl.BlockSpec((1, d), const),
                  pl.BlockSpec((1, 1, d), per_b),
                  pl.BlockSpec((1, 1, d), per_b)] + [pl.BlockSpec((d, n), const) for n in widths],
        out_specs=tuple(pl.BlockSpec((1, tm, n), row) for n in widths),
        compiler_params=_cparams(2),
        name="mlstm_in_proj",
    )(x, gain.reshape(1, d), sc, sh, *ws, wg)


def _softcap(a):
    return GATE_SOFTCAP * jnp.tanh(a / GATE_SOFTCAP)


def _mlstm_kernel(bi_ref, bf_ref, q_ref, k_ref, v_ref, g_ref, gain_ref, o_ref,
                  li_s, b_s, *, n_chunks, n_heads, dqk, dv):
    L = MLSTM_CHUNK
    r_i = lax.broadcasted_iota(jnp.int32, (L, L), 0)
    c_i = lax.broadcasted_iota(jnp.int32, (L, L), 1)
    upper = jnp.where(r_i <= c_i, 1.0, 0.0)
    for h in range(n_heads):
        li_s[h] = _softcap(g_ref[0, h] + bi_ref[h])
        fa = _softcap(g_ref[0, n_heads + h] + bf_ref[h])
        lf = jnp.minimum(fa, 0.0) - jnp.log1p(jnp.exp(-jnp.abs(fa)))
        b_s[h] = jnp.dot(lf, upper, preferred_element_type=F32, precision=HIGHEST)
    eye = r_i == c_i
    causal = r_i >= c_i
    k_scale = dqk ** -0.5

    def to_col(row):
        return jnp.sum(jnp.where(eye, jnp.broadcast_to(row, (L, L)), 0.0), axis=1, keepdims=True)

    def local_part(h, r0, c):
        qcb = q_ref[0, pl.ds(r0, L), h * dqk:(h + 1) * dqk]
        kc = k_ref[0, pl.ds(r0, L), h * dqk:(h + 1) * dqk].astype(F32) * k_scale
        vc = v_ref[0, pl.ds(r0, L), h * dv:(h + 1) * dv]
        b_row = b_s[h, pl.ds(c, 1), :]
        li_row = li_s[h, pl.ds(c, 1), :]
        b_col, li_col = to_col(b_row), to_col(li_row)
        b_last = b_row[:, L - 1:L]
        dmat = jnp.where(causal, b_col - b_row + li_row, NEG)
        m_loc = jnp.max(dmat, axis=-1, keepdims=True)
        a_loc = jnp.exp(dmat - m_loc) * _dot_nt(qcb, kc.astype(BF16))
        num_loc = _dot(a_loc.astype(BF16), vc)
        den_loc = jnp.sum(a_loc, axis=-1, keepdims=True)
        g_max = m_loc[L - 1:L, :]
        kw = kc * jnp.exp(b_last - b_col + li_col - g_max)
        kv = _dot_tn(kw.astype(BF16), vc)
        kn = jnp.sum(kw, axis=0, keepdims=True)
        return qcb, b_col, b_last, m_loc, num_loc, den_loc, g_max, kv, kn

    def body(grp, carry):
        r0 = pl.multiple_of(grp * (MLSTM_GROUP * L), MLSTM_GROUP * L)
        parts = [[local_part(h, r0 + j * L, grp * MLSTM_GROUP + j) for j in range(MLSTM_GROUP)]
                 for h in range(n_heads)]
        new_carry, outs = [], []
        for h in range(n_heads):
            state, n_row, m_prev = carry[h]
            gain = gain_ref[h]
            head_out = []
            for qcb, b_col, b_last, m_loc, num_loc, den_loc, g_max, kv, kn in parts[h]:
                m_inter = b_col + m_prev
                m_t = jnp.maximum(m_inter, m_loc)
                intra = jnp.exp(m_loc - m_t)
                inter = jnp.exp(m_inter - m_t)
                num = intra * num_loc + inter * _dot(qcb, state.astype(BF16))
                den = intra * den_loc + inter * jnp.sum(qcb.astype(F32) * n_row, axis=-1, keepdims=True)
                h_out = num / jnp.maximum(jnp.abs(den), jnp.exp(-m_t))
                hs = h_out * lax.rsqrt(jnp.mean(h_out * h_out, axis=-1, keepdims=True) + EPS) * gain
                head_out.append(hs.astype(o_ref.dtype))
                m_new = jnp.maximum(b_last + m_prev, g_max)
                decay = jnp.exp(b_last + m_prev - m_new)
                grow = jnp.exp(g_max - m_new)
                state, n_row, m_prev = decay * state + grow * kv, decay * n_row + grow * kn, m_new
            new_carry.append((state, n_row, m_prev))
            outs.append(jnp.concatenate(head_out, axis=0))
        for h in range(n_heads):
            o_ref[0, pl.ds(r0, MLSTM_GROUP * L), h * dv:(h + 1) * dv] = outs[h]
        return tuple(new_carry)

    init = tuple((jnp.zeros((dqk, dv), F32), jnp.zeros((1, dqk), F32), jnp.zeros((1, 1), F32))
                 for _ in range(n_heads))
    lax.fori_loop(0, n_chunks // MLSTM_GROUP, body, init)


def _mlstm_call(q, k, v, gates, b_igate, b_fgate, norm_gain):
    b, s, _ = q.shape
    nh = MLSTM_HEADS
    dqk, dv = q.shape[-1] // nh, v.shape[-1] // nh
    L = MLSTM_CHUNK
    nch = s // L
    g = jnp.transpose(gates[..., :2 * nh], (0, 2, 1)).reshape(b, 2 * nh, nch, L)
    smem = pl.BlockSpec(memory_space=pltpu.SMEM)
    per_b = lambda n: pl.BlockSpec((1, s, n), lambda i: (i, 0, 0))
    kern = functools.partial(_mlstm_kernel, n_chunks=nch, n_heads=nh, dqk=dqk, dv=dv)
    return pl.pallas_call(
        kern,
        out_shape=jax.ShapeDtypeStruct((b, s, nh * dv), BF16),
        grid=(b,),
        in_specs=[smem, smem, per_b(nh * dqk), per_b(nh * dqk), per_b(nh * dv),
                  pl.BlockSpec((1, 2 * nh, nch, L), lambda i: (i, 0, 0, 0)),
                  pl.BlockSpec((nh, 1, dv), lambda i: (0, 0, 0))],
        out_specs=per_b(nh * dv),
        scratch_shapes=[pltpu.VMEM((nh, nch, L), F32), pltpu.VMEM((nh, nch, L), F32)],
        compiler_params=_cparams(1),
        name="mlstm_chunk_scan",
    )(b_igate, b_fgate, q, k, v, g, norm_gain.reshape(nh, 1, dv))


def _router_kernel(x_ref, gain_ref, sc_ref, sh_ref, wh_ref, wl_ref, rb_ref, h_ref, ri_ref, rw_ref, cnt_ref):
    h = _norm_mod(x_ref[0], gain_ref[...], sc_ref[0], sh_ref[0])
    hi = h.astype(BF16)
    lo = (h - hi.astype(F32)).astype(BF16)
    logits = _dot(hi, wh_ref[...]) + (_dot(lo, wh_ref[...]) + _dot(hi, wl_ref[...]))
    h_ref[0] = hi
    aff = jax.nn.sigmoid(logits)
    tm = aff.shape[0]
    lane = lax.broadcasted_iota(jnp.int32, (tm, LANES), 1)
    choice = jnp.where(lane < N_EXPERTS, aff + rb_ref[...], -jnp.inf)

    def first_max(v):
        m = jnp.max(v, axis=-1, keepdims=True)
        return m, jnp.min(jnp.where(v == m, lane, LANES), axis=-1, keepdims=True)

    best = e0 = e1 = None
    for g in range(N_GROUPS):
        cg = jnp.where((lane // EXPERTS_PER_GROUP) == g, choice, -jnp.inf)
        m1, i1 = first_max(cg)
        m2, i2 = first_max(jnp.where(lane == i1, -jnp.inf, cg))
        score = m1 + m2
        if g == 0:
            best, e0, e1 = score, i1, i2
        else:
            better = score > best
            best = jnp.where(better, score, best)
            e0 = jnp.where(better, i1, e0)
            e1 = jnp.where(better, i2, e1)
    is0, is1 = lane == e0, lane == e1
    a0 = jnp.sum(jnp.where(is0, aff, 0.0), axis=-1, keepdims=True)
    a1 = jnp.sum(jnp.where(is1, aff, 0.0), axis=-1, keepdims=True)
    tot = a0 + a1
    rw_ref[0] = jnp.where(lane == 0, a0 / tot, jnp.where(lane == 1, a1 / tot, 0.0))
    onehot = jnp.where(is0 | is1, 1.0, 0.0).astype(BF16)
    r_i = lax.broadcasted_iota(jnp.int32, (tm, tm), 0)
    c_i = lax.broadcasted_iota(jnp.int32, (tm, tm), 1)
    running = _dot(jnp.where(r_i >= c_i, 1.0, 0.0).astype(BF16), onehot)
    r0 = jnp.sum(jnp.where(is0, running, 0.0), axis=-1, keepdims=True) - 1.0
    r1 = jnp.sum(jnp.where(is1, running, 0.0), axis=-1, keepdims=True) - 1.0
    ri_ref[0] = jnp.where(lane == 0, e0, jnp.where(lane == 1, e1, jnp.where(
        lane == 2, r0.astype(jnp.int32), jnp.where(lane == 3, r1.astype(jnp.int32), 0))))
    cnt_ref[0] = jnp.broadcast_to(running[tm - 1:tm, :], (8, LANES))


def _router_call(x, gain, sc, sh, router_w, router_b, tm):
    b, s, d = x.shape
    nt = s // tm
    wp = jnp.pad(router_w, ((0, 0), (0, LANES - N_EXPERTS)))
    wh = wp.astype(BF16)
    wl = (wp - wh.astype(F32)).astype(BF16)
    rb = jnp.pad(router_b.astype(F32), (0, LANES - N_EXPERTS)).reshape(1, LANES)
    row = lambda i, j: (i, j, 0)
    per_b = lambda i, j: (i, 0, 0)
    const = lambda i, j: (0, 0)
    return pl.pallas_call(
        _router_kernel,
        out_shape=(jax.ShapeDtypeStruct((b, s, d), BF16), jax.ShapeDtypeStruct((b, s, LANES), jnp.int32),
                   jax.ShapeDtypeStruct((b, s, LANES), F32), jax.ShapeDtypeStruct((b * nt, 8, LANES), F32)),
        grid=(b, nt),
        in_specs=[pl.BlockSpec((1, tm, d), row),
                  pl.BlockSpec((1, d), const),
                  pl.BlockSpec((1, 1, d), per_b),
                  pl.BlockSpec((1, 1, d), per_b),
                  pl.BlockSpec((d, LANES), const),
                  pl.BlockSpec((d, LANES), const),
                  pl.BlockSpec((1, LANES), const)],
        out_specs=(pl.BlockSpec((1, tm, d), row), pl.BlockSpec((1, tm, LANES), row),
                   pl.BlockSpec((1, tm, LANES), row), pl.BlockSpec((1, 8, LANES), lambda i, j: (i * nt + j, 0, 0))),
        compiler_params=_cparams(2),
        name="moe_router",
    )(x, gain.reshape(1, d), sc, sh, wh, wl, rb)


def _expert_kernel(be_ref, live_ref, x_ref, wg_ref, wu_ref, wd_ref, o_ref, wg_s, wu_s, wd_s):
    i = pl.program_id(0)
    fresh = (i == 0) | (be_ref[i] != be_ref[jnp.maximum(i - 1, 0)])

    @pl.when(fresh)
    def _():
        wg_s[...] = wg_ref[0, 0].astype(BF16)
        wu_s[...] = wu_ref[0, 0].astype(BF16)
        wd_s[...] = wd_ref[0, 0].astype(BF16)

    @pl.when(live_ref[i] == 1)
    def _():
        xb = x_ref[...]
        gate = _dot(xb, wg_s[...])
        hid = gate * jax.nn.sigmoid(gate) * _dot(xb, wu_s[...])
        o_ref[...] = _dot(hid.astype(BF16), wd_s[...]).astype(o_ref.dtype)

    @pl.when(live_ref[i] == 0)
    def _():
        o_ref[...] = jnp.zeros_like(o_ref)


def _expert_call(block_expert, block_live, xs, w_gate, w_up, w_down, layer):
    p, d = xs.shape
    de = w_gate.shape[-1]
    nb = p // MOE_ROWS
    grid_spec = pltpu.PrefetchScalarGridSpec(
        num_scalar_prefetch=2,
        grid=(nb,),
        in_specs=[pl.BlockSpec((MOE_ROWS, d), lambda i, be, lv: (i, 0)),
                  pl.BlockSpec((1, 1, d, de), lambda i, be, lv: (layer, be[i], 0, 0)),
                  pl.BlockSpec((1, 1, d, de), lambda i, be, lv: (layer, be[i], 0, 0)),
                  pl.BlockSpec((1, 1, de, d), lambda i, be, lv: (layer, be[i], 0, 0))],
        out_specs=pl.BlockSpec((MOE_ROWS, d), lambda i, be, lv: (i, 0)),
        scratch_shapes=[pltpu.VMEM((d, de), BF16), pltpu.VMEM((d, de), BF16), pltpu.VMEM((de, d), BF16)],
    )
    return pl.pallas_call(
        _expert_kernel,
        out_shape=jax.ShapeDtypeStruct((p, d), F32),
        grid_spec=grid_spec,
        compiler_params=_cparams(1),
        name="moe_experts",
    )(block_expert, block_live, xs, w_gate, w_up, w_down)


def _combine_kernel(x_ref, g_ref, rw_ref, ya_ref, yb_ref, o_ref):
    rw = rw_ref[0]
    y = rw[:, 0:1] * ya_ref[0].astype(F32) + rw[:, 1:2] * yb_ref[0].astype(F32)
    o_ref[0] = x_ref[0] + g_ref[0] * y


def _combine_call(x, g, rw, ya, yb, tm):
    b, s, d = x.shape
    row = lambda i, j: (i, j, 0)
    spec = pl.BlockSpec((1, tm, d), row)
    return pl.pallas_call(
        _combine_kernel,
        out_shape=jax.ShapeDtypeStruct((b, s, d), F32),
        grid=(b, s // tm),
        in_specs=[spec, pl.BlockSpec((1, 1, d), lambda i, j: (i, 0, 0)),
                  pl.BlockSpec((1, tm, LANES), row), spec, spec],
        out_specs=spec,
        compiler_params=_cparams(2),
        name="moe_combine",
    )(x, g, rw, ya, yb)


def _row_layout(ri, cnt, tm):
    t = ri.shape[0]
    nt = t // tm
    lanes = jnp.arange(N_EXPERTS, dtype=jnp.int32)
    tile_cnt = cnt[:, 0, :N_EXPERTS].astype(jnp.int32)
    tile_off = jnp.cumsum(tile_cnt, axis=0) - tile_cnt
    counts = jnp.sum(tile_cnt, axis=0)
    padded = (counts + MOE_ROWS - 1) // MOE_ROWS * MOE_ROWS
    p_ends = jnp.cumsum(padded)
    base = (p_ends - padded)[None, :] + tile_off
    e = ri[:, 0:TOP_K].reshape(nt, tm, TOP_K)
    rank = ri[:, TOP_K:2 * TOP_K].reshape(nt, tm, TOP_K)
    dest = jnp.sum(jnp.where(e[..., None] == lanes, base[:, None, None, :], 0), axis=-1) + rank
    dest = dest.reshape(t, TOP_K)
    nb = (t * TOP_K) // MOE_ROWS + N_EXPERTS
    block_start = jnp.arange(nb, dtype=jnp.int32) * MOE_ROWS
    block_expert = jnp.minimum(jnp.sum((p_ends[None, :] <= block_start[:, None]).astype(jnp.int32), axis=-1),
                               N_EXPERTS - 1)
    block_live = (block_start < p_ends[-1]).astype(jnp.int32)
    return dest, nb, block_expert, block_live


def _moe_layer(x, gain, sc, sh, g, router_w, router_b, w_gate, w_up, w_down, layer, tm):
    b, s, d = x.shape
    t = b * s
    hf, ri, rw, cnt = _router_call(x, gain, sc, sh, router_w, router_b, tm)
    dest, nb, block_expert, block_live = _row_layout(ri.reshape(t, LANES), cnt, tm)
    take = lambda a, idx: a.at[idx].get(mode="promise_in_bounds")
    put = lambda a, idx, rows: a.at[idx].set(rows, mode="promise_in_bounds", unique_indices=True)
    hf2 = hf.reshape(t, d)
    xs = put(put(jnp.zeros((nb * MOE_ROWS, d), BF16), dest[:, 0], hf2), dest[:, 1], hf2)
    out = _expert_call(block_expert, block_live, xs, w_gate, w_up, w_down, layer)
    ya = take(out, dest[:, 0]).reshape(b, s, d)
    yb = take(out, dest[:, 1]).reshape(b, s, d)
    return _combine_call(x, g, rw, ya, yb, tm)


def _nsa_layer(x, gain, sc, sh, g, w_in, w_out, q_gain, k_gain, cmp_pe, cmp_w1, cmp_b1, cmp_w2, cmp_b2, tm):
    b, s, d = x.shape
    G, dh = NSA_KV_GROUPS, HEAD_DIM
    cos, sin = _rope_tables(jnp.arange(s, dtype=jnp.int32))
    q, cv, ks, vs, kw, vw, gates = _nsa_in_call(x, gain, sc, sh, w_in, q_gain, k_gain, cos, sin, tm)

    n_cmp = (s - CMP_BLOCK) // CMP_STRIDE + 1
    n_str = s // CMP_STRIDE
    cmp_pos = jnp.arange(n_str, dtype=jnp.int32) * CMP_STRIDE + (CMP_BLOCK - 1)
    ccos, csin = _rope_tables(cmp_pos)
    cmp = _compress_call(cv, cmp_pe, cmp_w1, cmp_b1, cmp_w2, cmp_b2, k_gain[0], ccos, csin)
    cmp = jnp.pad(cmp, ((0, 0), (0, 0), (0, LANES - n_str), (0, 0)))
    kc, vc = cmp[0], cmp[1]

    ns = s // SEL_BLOCK
    r_, u_ = SEL_BLOCK // CMP_STRIDE, CMP_BLOCK // CMP_STRIDE
    c_idx = (r_ * np.arange(ns)[:, None, None] + np.arange(r_)[None, :, None]
             + np.arange(u_)[None, None, :]).reshape(ns, -1)
    c2s = (c_idx[:, :, None] == np.arange(n_cmp)[None, None, :]).sum(1).astype(np.float32)
    c2s = jnp.asarray(np.pad(c2s, ((0, 0), (0, LANES - n_cmp))))

    o = _nsa_attn_call(q, gates, kc, vc, ks, vs, kw, vw, c2s, n_cmp)
    return _out_proj_call(o, None, w_out[_head_pair_order(), :], x, g, tm)


def _mlstm_layer(x, gain, sc, sh, g, w_in, w_out, b_igate, b_fgate, norm_gain, tm):
    nh = MLSTM_HEADS
    dv = norm_gain.shape[-1]
    dqk = (w_in.shape[-1] - 2 * nh - 2 * nh * dv) // (2 * nh)
    q, k, v, og, gates = _mlstm_in_call(x, gain, sc, sh, w_in, dqk, dv, tm)
    hs = _mlstm_call(q, k, v, gates, b_igate, b_fgate, norm_gain)
    return _out_proj_call(hs, og, w_out, x, g, tm)


def kernel(x, c, ada_w, ada_b, norm_mix_gain, norm_ffn_gain, nsa_w_in, nsa_w_out, nsa_q_gain, nsa_k_gain, nsa_cmp_pe, nsa_cmp_w1, nsa_cmp_b1, nsa_cmp_w2, nsa_cmp_b2, mlstm_w_in, mlstm_b_igate, mlstm_b_fgate, mlstm_norm_gain, mlstm_w_out, router_w, router_b, moe_w_gate, moe_w_up, moe_w_down):
    b, s, d = x.shape
    depth = ada_w.shape[0]
    tm = min(512, s)
    mod = _mod_call(c, ada_w, ada_b)
    for i in range(depth):
        sh_m, sc_m, g_m, sh_f, sc_f, g_f = [mod[i, :, None, k * d:(k + 1) * d] for k in range(6)]
        j = i // 2
        if i % 2 == 0:
            x = _nsa_layer(x, norm_mix_gain[i], sc_m, sh_m, g_m, nsa_w_in[j], nsa_w_out[j], nsa_q_gain[j],
                           nsa_k_gain[j], nsa_cmp_pe[j], nsa_cmp_w1[j], nsa_cmp_b1[j], nsa_cmp_w2[j],
                           nsa_cmp_b2[j], tm)
        else:
            x = _mlstm_layer(x, norm_mix_gain[i], sc_m, sh_m, g_m, mlstm_w_in[j], mlstm_w_out[j],
                             mlstm_b_igate[j], mlstm_b_fgate[j], mlstm_norm_gain[j], tm)
        x = _moe_layer(x, norm_ffn_gain[i], sc_f, sh_f, g_f, router_w, router_b,
                       moe_w_gate, moe_w_up, moe_w_down, i, tm)
    return x
```

```python
import functools

import numpy as np
import jax
import jax.numpy as jnp
from jax import lax
from jax.experimental import pallas as pl
from jax.experimental.pallas import tpu as pltpu

F32 = jnp.float32
BF16 = jnp.bfloat16
HIGHEST = lax.Precision.HIGHEST

EPS = 1e-6
NEG = -1e30
BIG = 1e9
ROPE_THETA = 500000.0
LOG2E = 1.4426950408889634

NSA_HEADS = 16
NSA_KV_GROUPS = 2
NSA_HEADS_PER_GROUP = NSA_HEADS // NSA_KV_GROUPS
HEAD_DIM = 64
ROT_DIM = HEAD_DIM // 4
CMP_BLOCK = 32
CMP_STRIDE = 16
SEL_BLOCK = 64
SEL_TOPN = 8
WINDOW = 512
NSA_Q_BLOCK = 64
NSA_BRANCHES = 3
NSA_BLOCKS_PER_STEP = 2

MLSTM_HEADS = 4
MLSTM_CHUNK = 256
MLSTM_GROUP = 1
GATE_SOFTCAP = 15.0

N_EXPERTS = 32
N_GROUPS = 4
EXPERTS_PER_GROUP = N_EXPERTS // N_GROUPS
TOP_K = 2
MOE_ROWS = 512

LANES = 128
VMEM_LIMIT = 48 * 1024 * 1024


def _cparams(n_axes):
    return pltpu.CompilerParams(dimension_semantics=("arbitrary",) * n_axes,
                                vmem_limit_bytes=VMEM_LIMIT)


def _dot(a, b):
    return jnp.dot(a, b, preferred_element_type=F32)


def _dot_nt(a, b):
    return lax.dot_general(a, b, (((1,), (1,)), ((), ())), preferred_element_type=F32)


def _dot_tn(a, b):
    return lax.dot_general(a, b, (((0,), (0,)), ((), ())), preferred_element_type=F32)


def _norm_mod(x, gain, sc, sh):
    y = x * lax.rsqrt(jnp.mean(x * x, axis=-1, keepdims=True) + EPS) * gain
    return y * (1.0 + sc) + sh


def _half_norm_rope(x, gain, cos, sin, other_half_zero=False):
    lane = lax.broadcasted_iota(jnp.int32, x.shape, x.ndim - 1)
    x2 = x * x
    if other_half_zero:
        ms = jnp.sum(x2, axis=-1, keepdims=True) * (1.0 / HEAD_DIM)
    else:
        left = lane < HEAD_DIM
        ss_l = jnp.sum(jnp.where(left, x2, 0.0), axis=-1, keepdims=True)
        ss_r = jnp.sum(jnp.where(left, 0.0, x2), axis=-1, keepdims=True)
        ms = jnp.where(left, ss_l, ss_r) * (1.0 / HEAD_DIM)
    y = x * lax.rsqrt(ms + EPS) * gain
    half = ROT_DIM // 2
    nd = x.ndim - 1
    partner = jnp.where((lane % HEAD_DIM) < half,
                        pltpu.roll(y, LANES - half, nd), pltpu.roll(y, half, nd))
    return y * cos + partner * sin


def _rope_tables(pos):
    half = ROT_DIM // 2
    inv_freq = ROPE_THETA ** (-jnp.arange(half, dtype=F32) / half)
    ang = pos.astype(F32)[:, None] * inv_freq[None, :]
    cos, sin = jnp.cos(ang), jnp.sin(ang)
    n = pos.shape[0]
    one = jnp.ones((n, HEAD_DIM - ROT_DIM), F32)
    cos_h = jnp.concatenate([cos, cos, one], axis=-1)
    sin_h = jnp.concatenate([-sin, sin, 0.0 * one], axis=-1)
    return jnp.tile(cos_h, (1, 2)), jnp.tile(sin_h, (1, 2))


def _mod_kernel(c_ref, w_ref, b_ref, o_ref):
    c = c_ref[...]
    cond = c * jax.nn.sigmoid(c)
    o_ref[0] = jnp.dot(cond, w_ref[0], preferred_element_type=F32, precision=HIGHEST) + b_ref[0]


def _mod_call(c, ada_w, ada_b):
    depth, d, n = ada_w.shape
    b = c.shape[0]
    tn = n // 4
    return pl.pallas_call(
        _mod_kernel,
        out_shape=jax.ShapeDtypeStruct((depth, b, n), F32),
        grid=(depth, n // tn),
        in_specs=[pl.BlockSpec((b, d), lambda i, j: (0, 0)),
                  pl.BlockSpec((1, d, tn), lambda i, j: (i, 0, j)),
                  pl.BlockSpec((1, 1, tn), lambda i, j: (i, 0, j))],
        out_specs=pl.BlockSpec((1, b, tn), lambda i, j: (i, 0, j)),
        compiler_params=_cparams(2),
        name="adaln_mod",
    )(c, ada_w, ada_b.reshape(depth, 1, n))


def _nsa_in_kernel(x_ref, gain_ref, sc_ref, sh_ref, wq_ref, wkv_ref, wg_ref, qg_ref, kg_ref, cos_ref, sin_ref,
                   q_ref, cv_ref, ks_ref, vs_ref, kw_ref, vw_ref, g_ref):
    h = _norm_mod(x_ref[0], gain_ref[...], sc_ref[0], sh_ref[0]).astype(BF16)
    g_ref[0] = _dot(h, wg_ref[...])
    kv = _dot(h, wkv_ref[...])
    cos, sin = cos_ref[...], sin_ref[...]
    q = _dot(h, wq_ref[...])
    for r in range(NSA_HEADS_PER_GROUP):
        slab = _half_norm_rope(q[:, r * LANES:(r + 1) * LANES], qg_ref[...], cos, sin)
        q_ref[0, :, r * LANES:(r + 1) * LANES] = (slab * (HEAD_DIM ** -0.5 * LOG2E)).astype(BF16)
    cv_ref[0] = kv[:, 0:2 * LANES]
    ks_ref[0] = _half_norm_rope(kv[:, 2 * LANES:3 * LANES], kg_ref[1:2, :], cos, sin).astype(BF16)
    vs_ref[0] = kv[:, 3 * LANES:4 * LANES].astype(BF16)
    kw_ref[0] = _half_norm_rope(kv[:, 4 * LANES:5 * LANES], kg_ref[2:3, :], cos, sin).astype(BF16)
    vw_ref[0] = kv[:, 5 * LANES:6 * LANES].astype(BF16)


def _head_pair_order():
    r, g, dd = np.meshgrid(np.arange(NSA_HEADS_PER_GROUP), np.arange(NSA_KV_GROUPS), np.arange(HEAD_DIM),
                           indexing="ij")
    return ((g * NSA_HEADS_PER_GROUP + r) * HEAD_DIM + dd).reshape(-1)


def _nsa_in_call(x, gain, sc, sh, w_in, q_gain, k_gain, cos, sin, tm):
    b, s, d = x.shape
    nq = NSA_HEADS * HEAD_DIM
    nkv = 6 * LANES
    wq = w_in[:, :nq][:, _head_pair_order()].astype(BF16)
    qg = jnp.tile(q_gain, 2).reshape(1, LANES)
    wkv = w_in[:, nq:nq + nkv].astype(BF16)
    ng = NSA_BRANCHES * NSA_HEADS
    wg = jnp.pad(w_in[:, nq + nkv:], ((0, 0), (0, LANES - ng))).astype(BF16)
    kg = jnp.tile(k_gain, (1, 2))
    row = lambda i, j: (i, j, 0)
    per_b = lambda i, j: (i, 0, 0)
    const = lambda i, j: (0, 0)
    kv_out = lambda dt: jax.ShapeDtypeStruct((b, s, LANES), dt)
    return pl.pallas_call(
        _nsa_in_kernel,
        out_shape=(jax.ShapeDtypeStruct((b, s, nq), BF16), jax.ShapeDtypeStruct((b, s, 2 * LANES), F32),
                   kv_out(BF16), kv_out(BF16), kv_out(BF16), kv_out(BF16), kv_out(F32)),
        grid=(b, s // tm),
        in_specs=[pl.BlockSpec((1, tm, d), row),
                  pl.BlockSpec((1, d), const),
                  pl.BlockSpec((1, 1, d), per_b),
                  pl.BlockSpec((1, 1, d), per_b),
                  pl.BlockSpec((d, nq), const),
                  pl.BlockSpec((d, nkv), const),
                  pl.BlockSpec((d, LANES), const),
                  pl.BlockSpec((1, LANES), const),
                  pl.BlockSpec((3, LANES), const),
                  pl.BlockSpec((tm, LANES), lambda i, j: (j, 0)),
                  pl.BlockSpec((tm, LANES), lambda i, j: (j, 0))],
        out_specs=(pl.BlockSpec((1, tm, nq), row), pl.BlockSpec((1, tm, 2 * LANES), row))
        + (pl.BlockSpec((1, tm, LANES), row),) * 5,
        compiler_params=_cparams(2),
        name="nsa_in_proj",
    )(x, gain.reshape(1, d), sc, sh, wq, wkv, wg, qg, kg, cos, sin)


def _compress_kernel(a_ref, pe_ref, w1_ref, b1_ref, w2_ref, b2_ref, kg_ref, cos_ref, sin_ref, o_ref, *, n_str):
    is_key = pl.program_id(0) == 0
    hid2 = w1_ref.shape[-1]
    first = jnp.zeros((n_str, hid2), F32)
    second = jnp.zeros((n_str, hid2), F32)
    pe_term = jnp.zeros((8, hid2), F32)
    for l in range(CMP_STRIDE):
        rows = a_ref[0, pl.ds(l, n_str, stride=CMP_STRIDE), :].astype(BF16)
        first = first + _dot(rows, w1_ref[0, l])
        second = second + _dot(rows, w1_ref[0, CMP_STRIDE + l])
    for l in range(CMP_BLOCK):
        pe_term = pe_term + _dot(pe_ref[0, l].astype(BF16), w1_ref[0, l])
    hid = first + pltpu.roll(second, n_str - 1, 0) + pe_term[0:1] + b1_ref[0]
    hid = 0.5 * hid * (1.0 + jnp.tanh(np.sqrt(2.0 / np.pi) * (hid + 0.044715 * hid * hid * hid)))
    out = _dot(hid.astype(BF16), w2_ref[0]) + b2_ref[0]
    normed = _half_norm_rope(out, kg_ref[...], cos_ref[...], sin_ref[...])
    o_ref[0, 0] = jnp.where(is_key, normed, out).astype(o_ref.dtype)


def _block_diag2(w):
    z = jnp.zeros_like(w)
    return jnp.concatenate([jnp.concatenate([w, z], axis=-1), jnp.concatenate([z, w], axis=-1)], axis=-2)


def _compress_call(cv, pe, w1, b1, w2, b2, k_gain0, cos, sin):
    b, s, _ = cv.shape
    n_str = s // CMP_STRIDE
    hid = w1.shape[-1]
    w1bd = _block_diag2(w1.reshape(2, CMP_BLOCK, HEAD_DIM, hid)).astype(BF16)
    w2bd = _block_diag2(w2).astype(BF16)
    pe2 = jnp.broadcast_to(jnp.tile(pe, (1, 1, 2))[:, :, None, :], (2, CMP_BLOCK, 8, LANES))
    b1t = jnp.tile(b1, (1, 2)).reshape(2, 1, 2 * hid)
    b2t = jnp.tile(b2, (1, 2)).reshape(2, 1, LANES)
    kg = jnp.tile(k_gain0, 2).reshape(1, LANES)
    sel3 = lambda i, j: (i, 0, 0)
    sel4 = lambda i, j: (i, 0, 0, 0)
    const = lambda i, j: (0, 0)
    return pl.pallas_call(
        functools.partial(_compress_kernel, n_str=n_str),
        out_shape=jax.ShapeDtypeStruct((2, b, n_str, LANES), BF16),
        grid=(2, b),
        in_specs=[pl.BlockSpec((1, s, LANES), lambda i, j: (j, 0, i)),
                  pl.BlockSpec((1, CMP_BLOCK, 8, LANES), sel4),
                  pl.BlockSpec((1, CMP_BLOCK, LANES, 2 * hid), sel4),
                  pl.BlockSpec((1, 1, 2 * hid), sel3),
                  pl.BlockSpec((1, 2 * hid, LANES), sel3),
                  pl.BlockSpec((1, 1, LANES), sel3),
                  pl.BlockSpec((1, LANES), const),
                  pl.BlockSpec((n_str, LANES), const),
                  pl.BlockSpec((n_str, LANES), const)],
        out_specs=pl.BlockSpec((1, 1, n_str, LANES), lambda i, j: (i, j, 0, 0)),
        compiler_params=_cparams(2),
        name="nsa_compress",
    )(cv, pe2, w1bd, b1t, w2bd, b2t, kg, cos, sin)


def _attend(qb, k_ref, v_ref, k0, spans, g, bias):
    rows = qb.shape[0]
    m = acc = None
    for off, size in spans:
        k = k_ref[0, pl.ds(k0 + off, size), :]
        v = v_ref[0, pl.ds(k0 + off, size), :]
        s = _dot_nt(qb, k).reshape(rows // NSA_Q_BLOCK, NSA_Q_BLOCK, size) + bias[None, :, off:off + size]
        s = s.reshape(rows, size)
        m_span = jnp.max(s, axis=-1, keepdims=True)
        m_new = m_span if m is None else jnp.maximum(m, m_span)
        p = jnp.exp2(s - m_new).astype(BF16)
        v_lane = lax.broadcasted_iota(jnp.int32, v.shape, 1)
        pv = _dot(p, jnp.where((v_lane // HEAD_DIM) == g, v, jnp.ones_like(v)))
        acc = pv if m is None else acc * jnp.exp2(m - m_new) + pv
        m = m_new
    return acc


def _nsa_attn_kernel(*refs, blocks_per_step, **static):
    for sub in range(blocks_per_step):
        _nsa_attn_block(sub, pl.program_id(1) * blocks_per_step + sub, *refs, **static)


def _nsa_attn_block(sub, q_block, q_ref, g_ref, kc_ref, vc_ref, ks_ref, vs_ref, kw_ref, vw_ref,
                    c2s_ref, exp_ref, o_ref, *, seq, n_cmp, n_top, win_keys, sel_span, n_spans, q_block0):
    R, QB = NSA_HEADS_PER_GROUP, NSA_Q_BLOCK
    rows = R * QB
    n_sel = seq // SEL_BLOCK
    qi = q_block + q_block0
    s0 = qi * QB
    tok = slice(sub * QB, (sub + 1) * QB)
    gt = jax.nn.sigmoid(g_ref[0, tok, :])
    lane = lax.broadcasted_iota(jnp.int32, (QB, LANES), 1)
    tq = s0 + lax.broadcasted_iota(jnp.int32, (rows, 1), 0) % QB
    tq1 = s0 + lax.broadcasted_iota(jnp.int32, (QB, 1), 0)
    ones_sq = jnp.ones((LANES, LANES), BF16)

    w0 = pl.multiple_of(jnp.maximum(s0 + QB - win_keys, 0), SEL_BLOCK)
    wpos = w0 + lax.broadcasted_iota(jnp.int32, (1, win_keys), 1)
    bias_w = jnp.where((wpos <= tq1) & (wpos > tq1 - WINDOW), 0.0, NEG)
    win_spans = [(off, min(3 * LANES, win_keys - off)) for off in range(0, win_keys, 3 * LANES)]

    qbs, psums, o_cs, acc_ws = [], [], [], []
    for g in range(NSA_KV_GROUPS):
        in_g = (lane // HEAD_DIM) == g
        zero = jnp.zeros((QB, LANES), BF16)
        qb = jnp.concatenate([jnp.where(in_g, q_ref[0, tok, r * LANES:(r + 1) * LANES], zero) for r in range(R)],
                             axis=0)

        sc = _dot_nt(qb, kc_ref[0])
        cpos = lax.broadcasted_iota(jnp.int32, (1, LANES), 1) * CMP_STRIDE + (CMP_BLOCK - 1)
        valid_c = (cpos <= tq) & (lax.broadcasted_iota(jnp.int32, (1, LANES), 1) < n_cmp)
        sc = jnp.where(valid_c, sc, NEG)
        e_c = jnp.exp2(sc - jnp.max(sc, axis=-1, keepdims=True)).astype(BF16)
        p_c = jnp.where(valid_c, e_c.astype(F32) / _dot(e_c, ones_sq), 0.0)
        o_cs.append(_dot(p_c.astype(BF16), vc_ref[0]))

        psums.append(jnp.sum(p_c.reshape(R, QB, LANES), axis=0))

        acc_ws.append(_attend(qb, kw_ref, vw_ref, w0, win_spans, g, bias_w))
        qbs.append(qb)

    imp = lax.dot_general(c2s_ref[...], jnp.concatenate(psums, axis=0), (((1,), (1,)), ((), ())),
                          preferred_element_type=F32, precision=HIGHEST)
    blk = lax.broadcasted_iota(jnp.int32, (n_sel, NSA_KV_GROUPS * QB), 0)
    forced = (blk == 0) | (blk == qi) | (blk == qi - 1)
    imp = jnp.where(blk <= qi, jnp.where(forced, BIG, imp), -BIG)
    beaten = jnp.zeros(imp.shape, F32)
    for k in range(1, n_sel):
        other = pltpu.roll(imp, k, 0)
        beats = (other > imp) | ((blk >= k) & (other == imp))
        beaten = beaten + jnp.where(beats, 1.0, 0.0)
    chosen = jnp.where(beaten < n_top, 1.0, 0.0).astype(BF16)
    n_keys = n_spans * sel_span
    picked = _dot_tn(chosen, exp_ref[:, :n_keys])
    kpos = lax.broadcasted_iota(jnp.int32, (1, n_keys), 1)
    sel_spans = [(c * sel_span, sel_span) for c in range(n_spans)]

    heads = []
    for g in range(NSA_KV_GROUPS):
        bias_s = jnp.where((picked[g * QB:(g + 1) * QB] > 0.5) & (kpos <= tq1), 0.0, NEG)
        acc_s = _attend(qbs[g], ks_ref, vs_ref, 0, sel_spans, g, bias_s)
        o_c, acc_w = o_cs[g], acc_ws[g]
        c_sum = (1 - g) * HEAD_DIM
        per_head = []
        for r in range(R):
            h = g * R + r
            rs = slice(r * QB, (r + 1) * QB)
            a_s, a_w = acc_s[rs], acc_w[rs]
            g_s = gt[:, NSA_HEADS + h:NSA_HEADS + h + 1] / a_s[:, c_sum:c_sum + 1]
            g_w = gt[:, 2 * NSA_HEADS + h:2 * NSA_HEADS + h + 1] / a_w[:, c_sum:c_sum + 1]
            per_head.append(gt[:, h:h + 1] * o_c[rs] + g_s * a_s + g_w * a_w)
        heads.append(per_head)
    for r in range(R):
        slab = jnp.where(lane < HEAD_DIM, heads[0][r], heads[1][r])
        o_ref[0, tok, r * LANES:(r + 1) * LANES] = slab.astype(o_ref.dtype)


def _nsa_attn_call(q, gates, kc, vc, ks, vs, kw, vw, cmp_to_sel, n_cmp):
    b, s, nq = q.shape
    qb = NSA_Q_BLOCK
    n_top = min(SEL_TOPN, s // SEL_BLOCK)
    win_keys = min(WINDOW + 2 * qb, s)
    sel_span = min(512, s)
    n_sel = s // SEL_BLOCK
    expand = jnp.asarray(np.arange(n_sel)[:, None] == (np.arange(s)[None, :] // SEL_BLOCK), BF16)
    per_b = lambda i, j: (i, 0, 0)
    const = lambda i, j: (0, 0)
    per_call = sel_span // qb
    per_step = min(NSA_BLOCKS_PER_STEP, per_call)
    qrows = per_step * qb
    outs = []
    for n in range(1, s // sel_span + 1):
        q0 = (n - 1) * per_call
        row = lambda i, j, t0=q0 // per_step: (i, j + t0, 0)
        kern = functools.partial(_nsa_attn_kernel, blocks_per_step=per_step, seq=s, n_cmp=n_cmp, n_top=n_top,
                                 win_keys=win_keys, sel_span=sel_span, n_spans=n, q_block0=q0)
        outs.append(pl.pallas_call(
            kern,
            out_shape=jax.ShapeDtypeStruct((b, sel_span, nq), BF16),
            grid=(b, per_call // per_step),
            in_specs=[pl.BlockSpec((1, qrows, nq), row),
                      pl.BlockSpec((1, qrows, LANES), row),
                      pl.BlockSpec((1, LANES, LANES), per_b),
                      pl.BlockSpec((1, LANES, LANES), per_b),
                      pl.BlockSpec((1, n * sel_span, LANES), per_b),
                      pl.BlockSpec((1, n * sel_span, LANES), per_b),
                      pl.BlockSpec((1, s, LANES), per_b),
                      pl.BlockSpec((1, s, LANES), per_b),
                      pl.BlockSpec((n_sel, LANES), const),
                      pl.BlockSpec((n_sel, s), const)],
            out_specs=pl.BlockSpec((1, qrows, nq), lambda i, j: (i, j, 0)),
            compiler_params=_cparams(2),
            name="nsa_attention",
        )(q, gates, kc, vc, ks, vs, kw, vw, cmp_to_sel, expand))
    return jnp.concatenate(outs, axis=1)


def _nsa_out_kernel(a_ref, w_ref, x_ref, g_ref, o_ref):
    o_ref[0] = x_ref[0] + g_ref[0] * _dot(a_ref[0], w_ref[...])


def _mlstm_out_kernel(a_ref, og_ref, w_ref, x_ref, g_ref, o_ref):
    lhs = (jax.nn.sigmoid(og_ref[0].astype(F32)) * a_ref[0].astype(F32)).astype(BF16)
    o_ref[0] = x_ref[0] + g_ref[0] * _dot(lhs, w_ref[...])


def _out_proj_call(a, og, w_out, x, g, tm):
    b, s, d = x.shape
    k = a.shape[-1]
    row = lambda i, j: (i, j, 0)
    per_b = lambda i, j: (i, 0, 0)
    a_spec = pl.BlockSpec((1, tm, k), row)
    tail = [pl.BlockSpec((k, d), lambda i, j: (0, 0)), pl.BlockSpec((1, tm, d), row),
            pl.BlockSpec((1, 1, d), per_b)]
    if og is None:
        kern, ins, args = _nsa_out_kernel, [a_spec] + tail, (a, w_out.astype(BF16), x, g)
    else:
        kern, ins, args = _mlstm_out_kernel, [a_spec, a_spec] + tail, (a, og, w_out.astype(BF16), x, g)
    return pl.pallas_call(
        kern,
        out_shape=jax.ShapeDtypeStruct((b, s, d), F32),
        grid=(b, s // tm),
        in_specs=ins,
        out_specs=pl.BlockSpec((1, tm, d), row),
        compiler_params=_cparams(2),
        name="mixer_out_proj",
    )(*args)


def _mlstm_in_kernel(x_ref, gain_ref, sc_ref, sh_ref, wq_ref, wk_ref, wv_ref, wo_ref, wg_ref,
                     q_ref, k_ref, v_ref, o_ref, g_ref):
    h = _norm_mod(x_ref[0], gain_ref[...], sc_ref[0], sh_ref[0]).astype(BF16)
    q_ref[0] = _dot(h, wq_ref[...]).astype(BF16)
    k_ref[0] = _dot(h, wk_ref[...]).astype(BF16)
    v_ref[0] = _dot(h, wv_ref[...]).astype(BF16)
    o_ref[0] = _dot(h, wo_ref[...]).astype(BF16)
    g_ref[0] = _dot(h, wg_ref[...])


def _mlstm_in_call(x, gain, sc, sh, w_in, dqk, dv, tm):
    b, s, d = x.shape
    nh = MLSTM_HEADS
    sizes = [nh * dqk, nh * dqk, nh * dv, nh * dv]
    offs = np.cumsum([0] + sizes)
    ws = [w_in[:, offs[i]:offs[i + 1]].astype(BF16) for i in range(4)]
    wg = jnp.pad(w_in[:, offs[4]:], ((0, 0), (0, LANES - 2 * nh))).astype(BF16)
    row = lambda i, j: (i, j, 0)
    per_b = lambda i, j: (i, 0, 0)
    const = lambda i, j: (0, 0)
    widths = sizes + [LANES]
    return pl.pallas_call(
        _mlstm_in_kernel,
        out_shape=tuple(jax.ShapeDtypeStruct((b, s, n), BF16) for n in sizes)
        + (jax.ShapeDtypeStruct((b, s, LANES), F32),),
        grid=(b, s // tm),
        in_specs=[pl.BlockSpec((1, tm, d), row),
                  pl.BlockSpec((1, d), const),
                  pl.BlockSpec((1, 1, d), per_b),
                  pl.BlockSpec((1, 1, d), per_b)] + [pl.BlockSpec((d, n), const) for n in widths],
        out_specs=tuple(pl.BlockSpec((1, tm, n), row) for n in widths),
        compiler_params=_cparams(2),
        name="mlstm_in_proj",
    )(x, gain.reshape(1, d), sc, sh, *ws, wg)


def _softcap(a):
    return GATE_SOFTCAP * jnp.tanh(a / GATE_SOFTCAP)


def _mlstm_kernel(bi_ref, bf_ref, q_ref, k_ref, v_ref, g_ref, gain_ref, o_ref,
                  li_s, b_s, *, n_chunks, n_heads, dqk, dv):
    L = MLSTM_CHUNK
    r_i = lax.broadcasted_iota(jnp.int32, (L, L), 0)
    c_i = lax.broadcasted_iota(jnp.int32, (L, L), 1)
    upper = jnp.where(r_i <= c_i, 1.0, 0.0)
    for h in range(n_heads):
        li_s[h] = _softcap(g_ref[0, h] + bi_ref[h])
        fa = _softcap(g_ref[0, n_heads + h] + bf_ref[h])
        lf = jnp.minimum(fa, 0.0) - jnp.log1p(jnp.exp(-jnp.abs(fa)))
        b_s[h] = jnp.dot(lf, upper, preferred_element_type=F32, precision=HIGHEST)
    eye = r_i == c_i
    causal = r_i >= c_i
    k_scale = dqk ** -0.5

    def to_col(row):
        return jnp.sum(jnp.where(eye, jnp.broadcast_to(row, (L, L)), 0.0), axis=1, keepdims=True)

    def local_part(h, r0, c):
        qcb = q_ref[0, pl.ds(r0, L), h * dqk:(h + 1) * dqk]
        kc = k_ref[0, pl.ds(r0, L), h * dqk:(h + 1) * dqk].astype(F32) * k_scale
        vc = v_ref[0, pl.ds(r0, L), h * dv:(h + 1) * dv]
        b_row = b_s[h, pl.ds(c, 1), :]
        li_row = li_s[h, pl.ds(c, 1), :]
        b_col, li_col = to_col(b_row), to_col(li_row)
        b_last = b_row[:, L - 1:L]
        dmat = jnp.where(causal, b_col - b_row + li_row, NEG)
        m_loc = jnp.max(dmat, axis=-1, keepdims=True)
        a_loc = jnp.exp(dmat - m_loc) * _dot_nt(qcb, kc.astype(BF16))
        num_loc = _dot(a_loc.astype(BF16), vc)
        den_loc = jnp.sum(a_loc, axis=-1, keepdims=True)
        g_max = m_loc[L - 1:L, :]
        kw = kc * jnp.exp(b_last - b_col + li_col - g_max)
        kv = _dot_tn(kw.astype(BF16), vc)
        kn = jnp.sum(kw, axis=0, keepdims=True)
        return qcb, b_col, b_last, m_loc, num_loc, den_loc, g_max, kv, kn

    def body(grp, carry):
        r0 = pl.multiple_of(grp * (MLSTM_GROUP * L), MLSTM_GROUP * L)
        parts = [[local_part(h, r0 + j * L, grp * MLSTM_GROUP + j) for j in range(MLSTM_GROUP)]
                 for h in range(n_heads)]
        new_carry, outs = [], []
        for h in range(n_heads):
            state, n_row, m_prev = carry[h]
            gain = gain_ref[h]
            head_out = []
            for qcb, b_col, b_last, m_loc, num_loc, den_loc, g_max, kv, kn in parts[h]:
                m_inter = b_col + m_prev
                m_t = jnp.maximum(m_inter, m_loc)
                intra = jnp.exp(m_loc - m_t)
                inter = jnp.exp(m_inter - m_t)
                num = intra * num_loc + inter * _dot(qcb, state.astype(BF16))
                den = intra * den_loc + inter * jnp.sum(qcb.astype(F32) * n_row, axis=-1, keepdims=True)
                h_out = num / jnp.maximum(jnp.abs(den), jnp.exp(-m_t))
                hs = h_out * lax.rsqrt(jnp.mean(h_out * h_out, axis=-1, keepdims=True) + EPS) * gain
                head_out.append(hs.astype(o_ref.dtype))
                m_new = jnp.maximum(b_last + m_prev, g_max)
                decay = jnp.exp(b_last + m_prev - m_new)
                grow = jnp.exp(g_max - m_new)
                state, n_row, m_prev = decay * state + grow * kv, decay * n_row + grow * kn, m_new
            new_carry.append((state, n_row, m_prev))
            outs.append(jnp.concatenate(head_out, axis=0))
        for h in range(n_heads):
            o_ref[0, pl.ds(r0, MLSTM_GROUP * L), h * dv:(h + 1) * dv] = outs[h]
        return tuple(new_carry)

    init = tuple((jnp.zeros((dqk, dv), F32), jnp.zeros((1, dqk), F32), jnp.zeros((1, 1), F32))
                 for _ in range(n_heads))
    lax.fori_loop(0, n_chunks // MLSTM_GROUP, body, init)


def _mlstm_call(q, k, v, gates, b_igate, b_fgate, norm_gain):
    b, s, _ = q.shape
    nh = MLSTM_HEADS
    dqk, dv = q.shape[-1] // nh, v.shape[-1] // nh
    L = MLSTM_CHUNK
    nch = s // L
    g = jnp.transpose(gates[..., :2 * nh], (0, 2, 1)).reshape(b, 2 * nh, nch, L)
    smem = pl.BlockSpec(memory_space=pltpu.SMEM)
    per_b = lambda n: pl.BlockSpec((1, s, n), lambda i: (i, 0, 0))
    kern = functools.partial(_mlstm_kernel, n_chunks=nch, n_heads=nh, dqk=dqk, dv=dv)
    return pl.pallas_call(
        kern,
        out_shape=jax.ShapeDtypeStruct((b, s, nh * dv), BF16),
        grid=(b,),
        in_specs=[smem, smem, per_b(nh * dqk), per_b(nh * dqk), per_b(nh * dv),
                  pl.BlockSpec((1, 2 * nh, nch, L), lambda i: (i, 0, 0, 0)),
                  pl.BlockSpec((nh, 1, dv), lambda i: (0, 0, 0))],
        out_specs=per_b(nh * dv),
        scratch_shapes=[pltpu.VMEM((nh, nch, L), F32), pltpu.VMEM((nh, nch, L), F32)],
        compiler_params=_cparams(1),
        name="mlstm_chunk_scan",
    )(b_igate, b_fgate, q, k, v, g, norm_gain.reshape(nh, 1, dv))


def _router_kernel(x_ref, gain_ref, sc_ref, sh_ref, wh_ref, wl_ref, rb_ref, h_ref, ri_ref, rw_ref, cnt_ref):
    h = _norm_mod(x_ref[0], gain_ref[...], sc_ref[0], sh_ref[0])
    hi = h.astype(BF16)
    lo = (h - hi.astype(F32)).astype(BF16)
    logits = _dot(hi, wh_ref[...]) + (_dot(lo, wh_ref[...]) + _dot(hi, wl_ref[...]))
    h_ref[0] = hi
    aff = jax.nn.sigmoid(logits)
    tm = aff.shape[0]
    lane = lax.broadcasted_iota(jnp.int32, (tm, LANES), 1)
    choice = jnp.where(lane < N_EXPERTS, aff + rb_ref[...], -jnp.inf)

    def first_max(v):
        m = jnp.max(v, axis=-1, keepdims=True)
        return m, jnp.min(jnp.where(v == m, lane, LANES), axis=-1, keepdims=True)

    best = e0 = e1 = None
    for g in range(N_GROUPS):
        cg = jnp.where((lane // EXPERTS_PER_GROUP) == g, choice, -jnp.inf)
        m1, i1 = first_max(cg)
        m2, i2 = first_max(jnp.where(lane == i1, -jnp.inf, cg))
        score = m1 + m2
        if g == 0:
            best, e0, e1 = score, i1, i2
        else:
            better = score > best
            best = jnp.where(better, score, best)
            e0 = jnp.where(better, i1, e0)
            e1 = jnp.where(better, i2, e1)
    is0, is1 = lane == e0, lane == e1
    a0 = jnp.sum(jnp.where(is0, aff, 0.0), axis=-1, keepdims=True)
    a1 = jnp.sum(jnp.where(is1, aff, 0.0), axis=-1, keepdims=True)
    tot = a0 + a1
    rw_ref[0] = jnp.where(lane == 0, a0 / tot, jnp.where(lane == 1, a1 / tot, 0.0))
    onehot = jnp.where(is0 | is1, 1.0, 0.0).astype(BF16)
    r_i = lax.broadcasted_iota(jnp.int32, (tm, tm), 0)
    c_i = lax.broadcasted_iota(jnp.int32, (tm, tm), 1)
    running = _dot(jnp.where(r_i >= c_i, 1.0, 0.0).astype(BF16), onehot)
    r0 = jnp.sum(jnp.where(is0, running, 0.0), axis=-1, keepdims=True) - 1.0
    r1 = jnp.sum(jnp.where(is1, running, 0.0), axis=-1, keepdims=True) - 1.0
    ri_ref[0] = jnp.where(lane == 0, e0, jnp.where(lane == 1, e1, jnp.where(
        lane == 2, r0.astype(jnp.int32), jnp.where(lane == 3, r1.astype(jnp.int32), 0))))
    cnt_ref[0] = jnp.broadcast_to(running[tm - 1:tm, :], (8, LANES))


def _router_call(x, gain, sc, sh, router_w, router_b, tm):
    b, s, d = x.shape
    nt = s // tm
    wp = jnp.pad(router_w, ((0, 0), (0, LANES - N_EXPERTS)))
    wh = wp.astype(BF16)
    wl = (wp - wh.astype(F32)).astype(BF16)
    rb = jnp.pad(router_b.astype(F32), (0, LANES - N_EXPERTS)).reshape(1, LANES)
    row = lambda i, j: (i, j, 0)
    per_b = lambda i, j: (i, 0, 0)
    const = lambda i, j: (0, 0)
    return pl.pallas_call(
        _router_kernel,
        out_shape=(jax.ShapeDtypeStruct((b, s, d), BF16), jax.ShapeDtypeStruct((b, s, LANES), jnp.int32),
                   jax.ShapeDtypeStruct((b, s, LANES), F32), jax.ShapeDtypeStruct((b * nt, 8, LANES), F32)),
        grid=(b, nt),
        in_specs=[pl.BlockSpec((1, tm, d), row),
                  pl.BlockSpec((1, d), const),
                  pl.BlockSpec((1, 1, d), per_b),
                  pl.BlockSpec((1, 1, d), per_b),
                  pl.BlockSpec((d, LANES), const),
                  pl.BlockSpec((d, LANES), const),
                  pl.BlockSpec((1, LANES), const)],
        out_specs=(pl.BlockSpec((1, tm, d), row), pl.BlockSpec((1, tm, LANES), row),
                   pl.BlockSpec((1, tm, LANES), row), pl.BlockSpec((1, 8, LANES), lambda i, j: (i * nt + j, 0, 0))),
        compiler_params=_cparams(2),
        name="moe_router",
    )(x, gain.reshape(1, d), sc, sh, wh, wl, rb)


def _expert_kernel(be_ref, live_ref, x_ref, wg_ref, wu_ref, wd_ref, o_ref, wg_s, wu_s, wd_s):
    i = pl.program_id(0)
    fresh = (i == 0) | (be_ref[i] != be_ref[jnp.maximum(i - 1, 0)])

    @pl.when(fresh)
    def _():
        wg_s[...] = wg_ref[0, 0].astype(BF16)
        wu_s[...] = wu_ref[0, 0].astype(BF16)
        wd_s[...] = wd_ref[0, 0].astype(BF16)

    @pl.when(live_ref[i] == 1)
    def _():
        xb = x_ref[...]
        gate = _dot(xb, wg_s[...])
        hid = gate * jax.nn.sigmoid(gate) * _dot(xb, wu_s[...])
        o_ref[...] = _dot(hid.astype(BF16), wd_s[...]).astype(o_ref.dtype)

    @pl.when(live_ref[i] == 0)
    def _():
        o_ref[...] = jnp.zeros_like(o_ref)


def _expert_call(block_expert, block_live, xs, w_gate, w_up, w_down, layer):
    p, d = xs.shape
    de = w_gate.shape[-1]
    nb = p // MOE_ROWS
    grid_spec = pltpu.PrefetchScalarGridSpec(
        num_scalar_prefetch=2,
        grid=(nb,),
        in_specs=[pl.BlockSpec((MOE_ROWS, d), lambda i, be, lv: (i, 0)),
                  pl.BlockSpec((1, 1, d, de), lambda i, be, lv: (layer, be[i], 0, 0)),
                  pl.BlockSpec((1, 1, d, de), lambda i, be, lv: (layer, be[i], 0, 0)),
                  pl.BlockSpec((1, 1, de, d), lambda i, be, lv: (layer, be[i], 0, 0))],
        out_specs=pl.BlockSpec((MOE_ROWS, d), lambda i, be, lv: (i, 0)),
        scratch_shapes=[pltpu.VMEM((d, de), BF16), pltpu.VMEM((d, de), BF16), pltpu.VMEM((de, d), BF16)],
    )
    return pl.pallas_call(
        _expert_kernel,
        out_shape=jax.ShapeDtypeStruct((p, d), BF16),
        grid_spec=grid_spec,
        compiler_params=_cparams(1),
        name="moe_experts",
    )(block_expert, block_live, xs, w_gate, w_up, w_down)


def _combine_kernel(x_ref, g_ref, rw_ref, ya_ref, yb_ref, o_ref):
    rw = rw_ref[0]
    y = rw[:, 0:1] * ya_ref[0].astype(F32) + rw[:, 1:2] * yb_ref[0].astype(F32)
    o_ref[0] = x_ref[0] + g_ref[0] * y


def _combine_call(x, g, rw, ya, yb, tm):
    b, s, d = x.shape
    row = lambda i, j: (i, j, 0)
    spec = pl.BlockSpec((1, tm, d), row)
    return pl.pallas_call(
        _combine_kernel,
        out_shape=jax.ShapeDtypeStruct((b, s, d), F32),
        grid=(b, s // tm),
        in_specs=[spec, pl.BlockSpec((1, 1, d), lambda i, j: (i, 0, 0)),
                  pl.BlockSpec((1, tm, LANES), row), spec, spec],
        out_specs=spec,
        compiler_params=_cparams(2),
        name="moe_combine",
    )(x, g, rw, ya, yb)


def _row_layout(ri, cnt, tm):
    t = ri.shape[0]
    nt = t // tm
    lanes = jnp.arange(N_EXPERTS, dtype=jnp.int32)
    tile_cnt = cnt[:, 0, :N_EXPERTS].astype(jnp.int32)
    tile_off = jnp.cumsum(tile_cnt, axis=0) - tile_cnt
    counts = jnp.sum(tile_cnt, axis=0)
    padded = (counts + MOE_ROWS - 1) // MOE_ROWS * MOE_ROWS
    p_ends = jnp.cumsum(padded)
    base = (p_ends - padded)[None, :] + tile_off
    e = ri[:, 0:TOP_K].reshape(nt, tm, TOP_K)
    rank = ri[:, TOP_K:2 * TOP_K].reshape(nt, tm, TOP_K)
    dest = jnp.sum(jnp.where(e[..., None] == lanes, base[:, None, None, :], 0), axis=-1) + rank
    dest = dest.reshape(t, TOP_K)
    nb = (t * TOP_K) // MOE_ROWS + N_EXPERTS
    tok = jnp.repeat(jnp.arange(t, dtype=jnp.int32), TOP_K)
    buf_tok = (jnp.arange(nb * MOE_ROWS, dtype=jnp.int32) % t).at[dest.reshape(-1)].set(
        tok, mode="promise_in_bounds", unique_indices=True)
    block_start = jnp.arange(nb, dtype=jnp.int32) * MOE_ROWS
    block_expert = jnp.minimum(jnp.sum((p_ends[None, :] <= block_start[:, None]).astype(jnp.int32), axis=-1),
                               N_EXPERTS - 1)
    block_live = (block_start < p_ends[-1]).astype(jnp.int32)
    return dest, buf_tok, block_expert, block_live


def _moe_layer(x, gain, sc, sh, g, router_w, router_b, w_gate, w_up, w_down, layer, tm):
    b, s, d = x.shape
    t = b * s
    hf, ri, rw, cnt = _router_call(x, gain, sc, sh, router_w, router_b, tm)
    dest, buf_tok, block_expert, block_live = _row_layout(ri.reshape(t, LANES), cnt, tm)
    take = lambda a, idx: a.at[idx].get(mode="promise_in_bounds")
    xs = take(hf.reshape(t, d), buf_tok)
    out = _expert_call(block_expert, block_live, xs, w_gate, w_up, w_down, layer)
    ya = take(out, dest[:, 0]).reshape(b, s, d)
    yb = take(out, dest[:, 1]).reshape(b, s, d)
    return _combine_call(x, g, rw, ya, yb, tm)


def _nsa_layer(x, gain, sc, sh, g, w_in, w_out, q_gain, k_gain, cmp_pe, cmp_w1, cmp_b1, cmp_w2, cmp_b2, tm):
    b, s, d = x.shape
    G, dh = NSA_KV_GROUPS, HEAD_DIM
    cos, sin = _rope_tables(jnp.arange(s, dtype=jnp.int32))
    q, cv, ks, vs, kw, vw, gates = _nsa_in_call(x, gain, sc, sh, w_in, q_gain, k_gain, cos, sin, tm)

    n_cmp = (s - CMP_BLOCK) // CMP_STRIDE + 1
    n_str = s // CMP_STRIDE
    cmp_pos = jnp.arange(n_str, dtype=jnp.int32) * CMP_STRIDE + (CMP_BLOCK - 1)
    ccos, csin = _rope_tables(cmp_pos)
    cmp = _compress_call(cv, cmp_pe, cmp_w1, cmp_b1, cmp_w2, cmp_b2, k_gain[0], ccos, csin)
    cmp = jnp.pad(cmp, ((0, 0), (0, 0), (0, LANES - n_str), (0, 0)))
    kc, vc = cmp[0], cmp[1]

    ns = s // SEL_BLOCK
    r_, u_ = SEL_BLOCK // CMP_STRIDE, CMP_BLOCK // CMP_STRIDE
    c_idx = (r_ * np.arange(ns)[:, None, None] + np.arange(r_)[None, :, None]
             + np.arange(u_)[None, None, :]).reshape(ns, -1)
    c2s = (c_idx[:, :, None] == np.arange(n_cmp)[None, None, :]).sum(1).astype(np.float32)
    c2s = jnp.asarray(np.pad(c2s, ((0, 0), (0, LANES - n_cmp))))

    o = _nsa_attn_call(q, gates, kc, vc, ks, vs, kw, vw, c2s, n_cmp)
    return _out_proj_call(o, None, w_out[_head_pair_order(), :], x, g, tm)


def _mlstm_layer(x, gain, sc, sh, g, w_in, w_out, b_igate, b_fgate, norm_gain, tm):
    nh = MLSTM_HEADS
    dv = norm_gain.shape[-1]
    dqk = (w_in.shape[-1] - 2 * nh - 2 * nh * dv) // (2 * nh)
    q, k, v, og, gates = _mlstm_in_call(x, gain, sc, sh, w_in, dqk, dv, tm)
    hs = _mlstm_call(q, k, v, gates, b_igate, b_fgate, norm_gain)
    return _out_proj_call(hs, og, w_out, x, g, tm)


def kernel(x, c, ada_w, ada_b, norm_mix_gain, norm_ffn_gain, nsa_w_in, nsa_w_out, nsa_q_gain, nsa_k_gain, nsa_cmp_pe, nsa_cmp_w1, nsa_cmp_b1, nsa_cmp_w2, nsa_cmp_b2, mlstm_w_in, mlstm_b_igate, mlstm_b_fgate, mlstm_norm_gain, mlstm_w_out, router_w, router_b, moe_w_gate, moe_w_up, moe_w_down):
    b, s, d = x.shape
    depth = ada_w.shape[0]
    tm = min(512, s)
    mod = _mod_call(c, ada_w, ada_b)
    for i in range(depth):
        sh_m, sc_m, g_m, sh_f, sc_f, g_f = [mod[i, :, None, k * d:(k + 1) * d] for k in range(6)]
        j = i // 2
        if i % 2 == 0:
            x = _nsa_layer(x, norm_mix_gain[i], sc_m, sh_m, g_m, nsa_w_in[j], nsa_w_out[j], nsa_q_gain[j],
                           nsa_k_gain[j], nsa_cmp_pe[j], nsa_cmp_w1[j], nsa_cmp_b1[j], nsa_cmp_w2[j],
                           nsa_cmp_b2[j], tm)
        else:
            x = _mlstm_layer(x, norm_mix_gain[i], sc_m, sh_m, g_m, mlstm_w_in[j], mlstm_w_out[j],
                             mlstm_b_igate[j], mlstm_b_fgate[j], mlstm_norm_gain[j], tm)
        x = _moe_layer(x, norm_ffn_gain[i], sc_f, sh_f, g_f, router_w, router_b,
                       moe_w_gate, moe_w_up, moe_w_down, i, tm)
    return x
```

```python
import functools

import numpy as np
import jax
import jax.numpy as jnp
from jax import lax
from jax.experimental import pallas as pl
from jax.experimental.pallas import tpu as pltpu

F32 = jnp.float32
BF16 = jnp.bfloat16
HIGHEST = lax.Precision.HIGHEST

EPS = 1e-6
NEG = -1e30
BIG = 1e9
ROPE_THETA = 500000.0
LOG2E = 1.4426950408889634

NSA_HEADS = 16
NSA_KV_GROUPS = 2
NSA_HEADS_PER_GROUP = NSA_HEADS // NSA_KV_GROUPS
HEAD_DIM = 64
ROT_DIM = HEAD_DIM // 4
CMP_BLOCK = 32
CMP_STRIDE = 16
SEL_BLOCK = 64
SEL_TOPN = 8
WINDOW = 512
NSA_Q_BLOCK = 64
NSA_BRANCHES = 3
NSA_BLOCKS_PER_STEP = 2

MLSTM_HEADS = 4
MLSTM_CHUNK = 256
MLSTM_GROUP = 1
GATE_SOFTCAP = 15.0

N_EXPERTS = 32
N_GROUPS = 4
EXPERTS_PER_GROUP = N_EXPERTS // N_GROUPS
TOP_K = 2
MOE_ROWS = 512

LANES = 128
VMEM_LIMIT = 48 * 1024 * 1024


def _cparams(n_axes):
    return pltpu.CompilerParams(dimension_semantics=("arbitrary",) * n_axes,
                                vmem_limit_bytes=VMEM_LIMIT)


def _dot(a, b):
    return jnp.dot(a, b, preferred_element_type=F32)


def _dot_nt(a, b):
    return lax.dot_general(a, b, (((1,), (1,)), ((), ())), preferred_element_type=F32)


def _dot_tn(a, b):
    return lax.dot_general(a, b, (((0,), (0,)), ((), ())), preferred_element_type=F32)


def _norm_mod(x, gain, sc, sh):
    y = x * lax.rsqrt(jnp.mean(x * x, axis=-1, keepdims=True) + EPS) * gain
    return y * (1.0 + sc) + sh


def _half_norm_rope(x, gain, cos, sin):
    lane = lax.broadcasted_iota(jnp.int32, x.shape, x.ndim - 1)
    x2 = x * x
    left = lane < HEAD_DIM
    ss_l = jnp.sum(jnp.where(left, x2, 0.0), axis=-1, keepdims=True)
    ss_r = jnp.sum(jnp.where(left, 0.0, x2), axis=-1, keepdims=True)
    ms = jnp.where(left, ss_l, ss_r) * (1.0 / HEAD_DIM)
    y = x * lax.rsqrt(ms + EPS) * gain
    half = ROT_DIM // 2
    nd = x.ndim - 1
    partner = jnp.where((lane % HEAD_DIM) < half,
                        pltpu.roll(y, LANES - half, nd), pltpu.roll(y, half, nd))
    return y * cos + partner * sin


def _rope_tables(pos):
    half = ROT_DIM // 2
    inv_freq = ROPE_THETA ** (-jnp.arange(half, dtype=F32) / half)
    ang = pos.astype(F32)[:, None] * inv_freq[None, :]
    cos, sin = jnp.cos(ang), jnp.sin(ang)
    n = pos.shape[0]
    one = jnp.ones((n, HEAD_DIM - ROT_DIM), F32)
    cos_h = jnp.concatenate([cos, cos, one], axis=-1)
    sin_h = jnp.concatenate([-sin, sin, 0.0 * one], axis=-1)
    return jnp.tile(cos_h, (1, 2)), jnp.tile(sin_h, (1, 2))


def _mod_kernel(c_ref, w_ref, b_ref, o_ref):
    c = c_ref[...]
    cond = c * jax.nn.sigmoid(c)
    o_ref[0] = jnp.dot(cond, w_ref[0], preferred_element_type=F32, precision=HIGHEST) + b_ref[0]


def _mod_call(c, ada_w, ada_b):
    depth, d, n = ada_w.shape
    b = c.shape[0]
    tn = n // 4
    return pl.pallas_call(
        _mod_kernel,
        out_shape=jax.ShapeDtypeStruct((depth, b, n), F32),
        grid=(depth, n // tn),
        in_specs=[pl.BlockSpec((b, d), lambda i, j: (0, 0)),
                  pl.BlockSpec((1, d, tn), lambda i, j: (i, 0, j)),
                  pl.BlockSpec((1, 1, tn), lambda i, j: (i, 0, j))],
        out_specs=pl.BlockSpec((1, b, tn), lambda i, j: (i, 0, j)),
        compiler_params=_cparams(2),
        name="adaln_mod",
    )(c, ada_w, ada_b.reshape(depth, 1, n))


def _nsa_in_kernel(x_ref, gain_ref, sc_ref, sh_ref, wq_ref, wkv_ref, wg_ref, qg_ref, kg_ref, cos_ref, sin_ref,
                   q_ref, cv_ref, ks_ref, vs_ref, kw_ref, vw_ref, g_ref):
    h = _norm_mod(x_ref[0], gain_ref[...], sc_ref[0], sh_ref[0]).astype(BF16)
    g_ref[0] = _dot(h, wg_ref[...])
    kv = _dot(h, wkv_ref[...])
    cos, sin = cos_ref[...], sin_ref[...]
    q = _dot(h, wq_ref[...])
    for r in range(NSA_HEADS_PER_GROUP):
        slab = _half_norm_rope(q[:, r * LANES:(r + 1) * LANES], qg_ref[...], cos, sin)
        q_ref[0, :, r * LANES:(r + 1) * LANES] = (slab * (HEAD_DIM ** -0.5 * LOG2E)).astype(BF16)
    cv_ref[0] = kv[:, 0:2 * LANES]
    ks_ref[0] = _half_norm_rope(kv[:, 2 * LANES:3 * LANES], kg_ref[1:2, :], cos, sin).astype(BF16)
    vs_ref[0] = kv[:, 3 * LANES:4 * LANES].astype(BF16)
    kw_ref[0] = _half_norm_rope(kv[:, 4 * LANES:5 * LANES], kg_ref[2:3, :], cos, sin).astype(BF16)
    vw_ref[0] = kv[:, 5 * LANES:6 * LANES].astype(BF16)


def _head_pair_order():
    r, g, dd = np.meshgrid(np.arange(NSA_HEADS_PER_GROUP), np.arange(NSA_KV_GROUPS), np.arange(HEAD_DIM),
                           indexing="ij")
    return ((g * NSA_HEADS_PER_GROUP + r) * HEAD_DIM + dd).reshape(-1)


def _nsa_in_call(x, gain, sc, sh, w_in, q_gain, k_gain, cos, sin, tm):
    b, s, d = x.shape
    nq = NSA_HEADS * HEAD_DIM
    nkv = 6 * LANES
    wq = w_in[:, :nq][:, _head_pair_order()].astype(BF16)
    qg = jnp.tile(q_gain, 2).reshape(1, LANES)
    wkv = w_in[:, nq:nq + nkv].astype(BF16)
    ng = NSA_BRANCHES * NSA_HEADS
    wg = jnp.pad(w_in[:, nq + nkv:], ((0, 0), (0, LANES - ng))).astype(BF16)
    kg = jnp.tile(k_gain, (1, 2))
    row = lambda i, j: (i, j, 0)
    per_b = lambda i, j: (i, 0, 0)
    const = lambda i, j: (0, 0)
    kv_out = lambda dt: jax.ShapeDtypeStruct((b, s, LANES), dt)
    return pl.pallas_call(
        _nsa_in_kernel,
        out_shape=(jax.ShapeDtypeStruct((b, s, nq), BF16), jax.ShapeDtypeStruct((b, s, 2 * LANES), F32),
                   kv_out(BF16), kv_out(BF16), kv_out(BF16), kv_out(BF16), kv_out(F32)),
        grid=(b, s // tm),
        in_specs=[pl.BlockSpec((1, tm, d), row),
                  pl.BlockSpec((1, d), const),
                  pl.BlockSpec((1, 1, d), per_b),
                  pl.BlockSpec((1, 1, d), per_b),
                  pl.BlockSpec((d, nq), const),
                  pl.BlockSpec((d, nkv), const),
                  pl.BlockSpec((d, LANES), const),
                  pl.BlockSpec((1, LANES), const),
                  pl.BlockSpec((3, LANES), const),
                  pl.BlockSpec((tm, LANES), lambda i, j: (j, 0)),
                  pl.BlockSpec((tm, LANES), lambda i, j: (j, 0))],
        out_specs=(pl.BlockSpec((1, tm, nq), row), pl.BlockSpec((1, tm, 2 * LANES), row))
        + (pl.BlockSpec((1, tm, LANES), row),) * 5,
        compiler_params=_cparams(2),
        name="nsa_in_proj",
    )(x, gain.reshape(1, d), sc, sh, wq, wkv, wg, qg, kg, cos, sin)


def _compress_kernel(a_ref, pe_ref, w1_ref, b1_ref, w2_ref, b2_ref, kg_ref, cos_ref, sin_ref, o_ref, *, n_str):
    is_key = pl.program_id(0) == 0
    hid2 = w1_ref.shape[-1]
    first = jnp.zeros((n_str, hid2), F32)
    second = jnp.zeros((n_str, hid2), F32)
    pe_term = jnp.zeros((8, hid2), F32)
    for l in range(CMP_STRIDE):
        rows = a_ref[0, pl.ds(l, n_str, stride=CMP_STRIDE), :].astype(BF16)
        first = first + _dot(rows, w1_ref[0, l])
        second = second + _dot(rows, w1_ref[0, CMP_STRIDE + l])
    for l in range(CMP_BLOCK):
        pe_term = pe_term + _dot(pe_ref[0, l].astype(BF16), w1_ref[0, l])
    hid = first + pltpu.roll(second, n_str - 1, 0) + pe_term[0:1] + b1_ref[0]
    hid = 0.5 * hid * (1.0 + jnp.tanh(np.sqrt(2.0 / np.pi) * (hid + 0.044715 * hid * hid * hid)))
    out = _dot(hid.astype(BF16), w2_ref[0]) + b2_ref[0]
    normed = _half_norm_rope(out, kg_ref[...], cos_ref[...], sin_ref[...])
    o_ref[0, 0] = jnp.where(is_key, normed, out).astype(o_ref.dtype)


def _block_diag2(w):
    z = jnp.zeros_like(w)
    return jnp.concatenate([jnp.concatenate([w, z], axis=-1), jnp.concatenate([z, w], axis=-1)], axis=-2)


def _compress_call(cv, pe, w1, b1, w2, b2, k_gain0, cos, sin):
    b, s, _ = cv.shape
    n_str = s // CMP_STRIDE
    hid = w1.shape[-1]
    w1bd = _block_diag2(w1.reshape(2, CMP_BLOCK, HEAD_DIM, hid)).astype(BF16)
    w2bd = _block_diag2(w2).astype(BF16)
    pe2 = jnp.broadcast_to(jnp.tile(pe, (1, 1, 2))[:, :, None, :], (2, CMP_BLOCK, 8, LANES))
    b1t = jnp.tile(b1, (1, 2)).reshape(2, 1, 2 * hid)
    b2t = jnp.tile(b2, (1, 2)).reshape(2, 1, LANES)
    kg = jnp.tile(k_gain0, 2).reshape(1, LANES)
    sel3 = lambda i, j: (i, 0, 0)
    sel4 = lambda i, j: (i, 0, 0, 0)
    const = lambda i, j: (0, 0)
    return pl.pallas_call(
        functools.partial(_compress_kernel, n_str=n_str),
        out_shape=jax.ShapeDtypeStruct((2, b, n_str, LANES), BF16),
        grid=(2, b),
        in_specs=[pl.BlockSpec((1, s, LANES), lambda i, j: (j, 0, i)),
                  pl.BlockSpec((1, CMP_BLOCK, 8, LANES), sel4),
                  pl.BlockSpec((1, CMP_BLOCK, LANES, 2 * hid), sel4),
                  pl.BlockSpec((1, 1, 2 * hid), sel3),
                  pl.BlockSpec((1, 2 * hid, LANES), sel3),
                  pl.BlockSpec((1, 1, LANES), sel3),
                  pl.BlockSpec((1, LANES), const),
                  pl.BlockSpec((n_str, LANES), const),
                  pl.BlockSpec((n_str, LANES), const)],
        out_specs=pl.BlockSpec((1, 1, n_str, LANES), lambda i, j: (i, j, 0, 0)),
        compiler_params=_cparams(2),
        name="nsa_compress",
    )(cv, pe2, w1bd, b1t, w2bd, b2t, kg, cos, sin)


def _attend(qb, k_ref, v_ref, k0, spans, g, bias):
    rows = qb.shape[0]
    m = acc = None
    for off, size in spans:
        k = k_ref[0, pl.ds(k0 + off, size), :]
        v = v_ref[0, pl.ds(k0 + off, size), :]
        s = _dot_nt(qb, k).reshape(rows // NSA_Q_BLOCK, NSA_Q_BLOCK, size) + bias[None, :, off:off + size]
        s = s.reshape(rows, size)
        m_span = jnp.max(s, axis=-1, keepdims=True)
        m_new = m_span if m is None else jnp.maximum(m, m_span)
        p = jnp.exp2(s - m_new).astype(BF16)
        v_lane = lax.broadcasted_iota(jnp.int32, v.shape, 1)
        pv = _dot(p, jnp.where((v_lane // HEAD_DIM) == g, v, jnp.ones_like(v)))
        acc = pv if m is None else acc * jnp.exp2(m - m_new) + pv
        m = m_new
    return acc


def _nsa_attn_kernel(*refs, blocks_per_step, **static):
    for sub in range(blocks_per_step):
        _nsa_attn_block(sub, pl.program_id(1) * blocks_per_step + sub, *refs, **static)


def _nsa_attn_block(sub, q_block, q_ref, g_ref, kc_ref, vc_ref, ks_ref, vs_ref, kw_ref, vw_ref,
                    c2s_ref, exp_ref, o_ref, *, seq, n_cmp, n_top, win_keys, sel_span, n_spans, q_block0):
    R, QB = NSA_HEADS_PER_GROUP, NSA_Q_BLOCK
    rows = R * QB
    n_sel = seq // SEL_BLOCK
    qi = q_block + q_block0
    s0 = qi * QB
    tok = slice(sub * QB, (sub + 1) * QB)
    gt = jax.nn.sigmoid(g_ref[0, tok, :])
    lane = lax.broadcasted_iota(jnp.int32, (QB, LANES), 1)
    tq = s0 + lax.broadcasted_iota(jnp.int32, (rows, 1), 0) % QB
    tq1 = s0 + lax.broadcasted_iota(jnp.int32, (QB, 1), 0)
    ones_sq = jnp.ones((LANES, LANES), BF16)

    w0 = pl.multiple_of(jnp.maximum(s0 + QB - win_keys, 0), SEL_BLOCK)
    wpos = w0 + lax.broadcasted_iota(jnp.int32, (1, win_keys), 1)
    bias_w = jnp.where((wpos <= tq1) & (wpos > tq1 - WINDOW), 0.0, NEG)
    win_spans = [(off, min(3 * LANES, win_keys - off)) for off in range(0, win_keys, 3 * LANES)]

    qbs, psums, o_cs, acc_ws = [], [], [], []
    for g in range(NSA_KV_GROUPS):
        in_g = (lane // HEAD_DIM) == g
        zero = jnp.zeros((QB, LANES), BF16)
        qb = jnp.concatenate([jnp.where(in_g, q_ref[0, tok, r * LANES:(r + 1) * LANES], zero) for r in range(R)],
                             axis=0)

        sc = _dot_nt(qb, kc_ref[0])
        cpos = lax.broadcasted_iota(jnp.int32, (1, LANES), 1) * CMP_STRIDE + (CMP_BLOCK - 1)
        valid_c = (cpos <= tq) & (lax.broadcasted_iota(jnp.int32, (1, LANES), 1) < n_cmp)
        sc = jnp.where(valid_c, sc, NEG)
        e_c = jnp.exp2(sc - jnp.max(sc, axis=-1, keepdims=True)).astype(BF16)
        p_c = jnp.where(valid_c, e_c.astype(F32) / _dot(e_c, ones_sq), 0.0)
        o_cs.append(_dot(p_c.astype(BF16), vc_ref[0]))

        psums.append(jnp.sum(p_c.reshape(R, QB, LANES), axis=0))

        acc_ws.append(_attend(qb, kw_ref, vw_ref, w0, win_spans, g, bias_w))
        qbs.append(qb)

    imp = lax.dot_general(c2s_ref[...], jnp.concatenate(psums, axis=0), (((1,), (1,)), ((), ())),
                          preferred_element_type=F32, precision=HIGHEST)
    blk = lax.broadcasted_iota(jnp.int32, (n_sel, NSA_KV_GROUPS * QB), 0)
    forced = (blk == 0) | (blk == qi) | (blk == qi - 1)
    imp = jnp.where(blk <= qi, jnp.where(forced, BIG, imp), -BIG)
    beaten = jnp.zeros(imp.shape, F32)
    for k in range(1, n_sel):
        other = pltpu.roll(imp, k, 0)
        beats = (other > imp) | ((blk >= k) & (other == imp))
        beaten = beaten + jnp.where(beats, 1.0, 0.0)
    chosen = jnp.where(beaten < n_top, 1.0, 0.0).astype(BF16)
    n_keys = n_spans * sel_span
    picked = _dot_tn(chosen, exp_ref[:, :n_keys])
    kpos = lax.broadcasted_iota(jnp.int32, (1, n_keys), 1)
    sel_spans = [(c * sel_span, sel_span) for c in range(n_spans)]

    heads = []
    for g in range(NSA_KV_GROUPS):
        bias_s = jnp.where((picked[g * QB:(g + 1) * QB] > 0.5) & (kpos <= tq1), 0.0, NEG)
        acc_s = _attend(qbs[g], ks_ref, vs_ref, 0, sel_spans, g, bias_s)
        o_c, acc_w = o_cs[g], acc_ws[g]
        c_sum = (1 - g) * HEAD_DIM
        per_head = []
        for r in range(R):
            h = g * R + r
            rs = slice(r * QB, (r + 1) * QB)
            a_s, a_w = acc_s[rs], acc_w[rs]
            g_s = gt[:, NSA_HEADS + h:NSA_HEADS + h + 1] / a_s[:, c_sum:c_sum + 1]
            g_w = gt[:, 2 * NSA_HEADS + h:2 * NSA_HEADS + h + 1] / a_w[:, c_sum:c_sum + 1]
            per_head.append(gt[:, h:h + 1] * o_c[rs] + g_s * a_s + g_w * a_w)
        heads.append(per_head)
    for r in range(R):
        slab = jnp.where(lane < HEAD_DIM, heads[0][r], heads[1][r])
        o_ref[0, tok, r * LANES:(r + 1) * LANES] = slab.astype(o_ref.dtype)


def _nsa_attn_call(q, gates, kc, vc, ks, vs, kw, vw, cmp_to_sel, n_cmp):
    b, s, nq = q.shape
    qb = NSA_Q_BLOCK
    n_top = min(SEL_TOPN, s // SEL_BLOCK)
    win_keys = min(WINDOW + 2 * qb, s)
    sel_span = min(512, s)
    n_sel = s // SEL_BLOCK
    expand = jnp.asarray(np.arange(n_sel)[:, None] == (np.arange(s)[None, :] // SEL_BLOCK), BF16)
    per_b = lambda i, j: (i, 0, 0)
    const = lambda i, j: (0, 0)
    per_call = sel_span // qb
    per_step = min(NSA_BLOCKS_PER_STEP, per_call)
    qrows = per_step * qb
    outs = []
    for n in range(1, s // sel_span + 1):
        q0 = (n - 1) * per_call
        row = lambda i, j, t0=q0 // per_step: (i, j + t0, 0)
        kern = functools.partial(_nsa_attn_kernel, blocks_per_step=per_step, seq=s, n_cmp=n_cmp, n_top=n_top,
                                 win_keys=win_keys, sel_span=sel_span, n_spans=n, q_block0=q0)
        outs.append(pl.pallas_call(
            kern,
            out_shape=jax.ShapeDtypeStruct((b, sel_span, nq), BF16),
            grid=(b, per_call // per_step),
            in_specs=[pl.BlockSpec((1, qrows, nq), row),
                      pl.BlockSpec((1, qrows, LANES), row),
                      pl.BlockSpec((1, LANES, LANES), per_b),
                      pl.BlockSpec((1, LANES, LANES), per_b),
                      pl.BlockSpec((1, n * sel_span, LANES), per_b),
                      pl.BlockSpec((1, n * sel_span, LANES), per_b),
                      pl.BlockSpec((1, s, LANES), per_b),
                      pl.BlockSpec((1, s, LANES), per_b),
                      pl.BlockSpec((n_sel, LANES), const),
                      pl.BlockSpec((n_sel, s), const)],
            out_specs=pl.BlockSpec((1, qrows, nq), lambda i, j: (i, j, 0)),
            compiler_params=_cparams(2),
            name="nsa_attention",
        )(q, gates, kc, vc, ks, vs, kw, vw, cmp_to_sel, expand))
    return jnp.concatenate(outs, axis=1)


def _nsa_out_kernel(a_ref, w_ref, x_ref, g_ref, o_ref):
    o_ref[0] = x_ref[0] + g_ref[0] * _dot(a_ref[0], w_ref[...])


def _mlstm_out_kernel(a_ref, og_ref, w_ref, x_ref, g_ref, o_ref):
    lhs = (jax.nn.sigmoid(og_ref[0].astype(F32)) * a_ref[0].astype(F32)).astype(BF16)
    o_ref[0] = x_ref[0] + g_ref[0] * _dot(lhs, w_ref[...])


def _out_proj_call(a, og, w_out, x, g, tm):
    b, s, d = x.shape
    k = a.shape[-1]
    row = lambda i, j: (i, j, 0)
    per_b = lambda i, j: (i, 0, 0)
    a_spec = pl.BlockSpec((1, tm, k), row)
    tail = [pl.BlockSpec((k, d), lambda i, j: (0, 0)), pl.BlockSpec((1, tm, d), row),
            pl.BlockSpec((1, 1, d), per_b)]
    if og is None:
        kern, ins, args = _nsa_out_kernel, [a_spec] + tail, (a, w_out.astype(BF16), x, g)
    else:
        kern, ins, args = _mlstm_out_kernel, [a_spec, a_spec] + tail, (a, og, w_out.astype(BF16), x, g)
    return pl.pallas_call(
        kern,
        out_shape=jax.ShapeDtypeStruct((b, s, d), F32),
        grid=(b, s // tm),
        in_specs=ins,
        out_specs=pl.BlockSpec((1, tm, d), row),
        compiler_params=_cparams(2),
        name="mixer_out_proj",
    )(*args)


def _mlstm_in_kernel(x_ref, gain_ref, sc_ref, sh_ref, wq_ref, wk_ref, wv_ref, wo_ref, wg_ref,
                     q_ref, k_ref, v_ref, o_ref, g_ref):
    h = _norm_mod(x_ref[0], gain_ref[...], sc_ref[0], sh_ref[0]).astype(BF16)
    q_ref[0] = _dot(h, wq_ref[...]).astype(BF16)
    k_ref[0] = _dot(h, wk_ref[...]).astype(BF16)
    v_ref[0] = _dot(h, wv_ref[...]).astype(BF16)
    o_ref[0] = _dot(h, wo_ref[...]).astype(BF16)
    g_ref[0] = _dot(h, wg_ref[...])


def _mlstm_in_call(x, gain, sc, sh, w_in, dqk, dv, tm):
    b, s, d = x.shape
    nh = MLSTM_HEADS
    sizes = [nh * dqk, nh * dqk, nh * dv, nh * dv]
    offs = np.cumsum([0] + sizes)
    ws = [w_in[:, offs[i]:offs[i + 1]].astype(BF16) for i in range(4)]
    wg = jnp.pad(w_in[:, offs[4]:], ((0, 0), (0, LANES - 2 * nh))).astype(BF16)
    row = lambda i, j: (i, j, 0)
    per_b = lambda i, j: (i, 0, 0)
    const = lambda i, j: (0, 0)
    widths = sizes + [LANES]
    return pl.pallas_call(
        _mlstm_in_kernel,
        out_shape=tuple(jax.ShapeDtypeStruct((b, s, n), BF16) for n in sizes)
        + (jax.ShapeDtypeStruct((b, s, LANES), F32),),
        grid=(b, s // tm),
        in_specs=[pl.BlockSpec((1, tm, d), row),
                  pl.BlockSpec((1, d), const),
                  pl.BlockSpec((1, 1, d), per_b),
                  pl.BlockSpec((1, 1, d), per_b)] + [pl.BlockSpec((d, n), const) for n in widths],
        out_specs=tuple(pl.BlockSpec((1, tm, n), row) for n in widths),
        compiler_params=_cparams(2),
        name="mlstm_in_proj",
    )(x, gain.reshape(1, d), sc, sh, *ws, wg)


def _softcap(a):
    return GATE_SOFTCAP * jnp.tanh(a / GATE_SOFTCAP)


def _mlstm_kernel(bi_ref, bf_ref, q_ref, k_ref, v_ref, g_ref, gain_ref, o_ref,
                  li_s, b_s, *, n_chunks, n_heads, dqk, dv):
    L = MLSTM_CHUNK
    r_i = lax.broadcasted_iota(jnp.int32, (L, L), 0)
    c_i = lax.broadcasted_iota(jnp.int32, (L, L), 1)
    upper = jnp.where(r_i <= c_i, 1.0, 0.0)
    for h in range(n_heads):
        li_s[h] = _softcap(g_ref[0, h] + bi_ref[h])
        fa = _softcap(g_ref[0, n_heads + h] + bf_ref[h])
        lf = jnp.minimum(fa, 0.0) - jnp.log1p(jnp.exp(-jnp.abs(fa)))
        b_s[h] = jnp.dot(lf, upper, preferred_element_type=F32, precision=HIGHEST)
    eye = r_i == c_i
    causal = r_i >= c_i
    k_scale = dqk ** -0.5

    def to_col(row):
        return jnp.sum(jnp.where(eye, jnp.broadcast_to(row, (L, L)), 0.0), axis=1, keepdims=True)

    def local_part(h, r0, c):
        qcb = q_ref[0, pl.ds(r0, L), h * dqk:(h + 1) * dqk]
        kc = k_ref[0, pl.ds(r0, L), h * dqk:(h + 1) * dqk].astype(F32) * k_scale
        vc = v_ref[0, pl.ds(r0, L), h * dv:(h + 1) * dv]
        b_row = b_s[h, pl.ds(c, 1), :]
        li_row = li_s[h, pl.ds(c, 1), :]
        b_col, li_col = to_col(b_row), to_col(li_row)
        b_last = b_row[:, L - 1:L]
        dmat = jnp.where(causal, b_col - b_row + li_row, NEG)
        m_loc = jnp.max(dmat, axis=-1, keepdims=True)
        a_loc = jnp.exp(dmat - m_loc) * _dot_nt(qcb, kc.astype(BF16))
        num_loc = _dot(a_loc.astype(BF16), vc)
        den_loc = jnp.sum(a_loc, axis=-1, keepdims=True)
        g_max = m_loc[L - 1:L, :]
        kw = kc * jnp.exp(b_last - b_col + li_col - g_max)
        kv = _dot_tn(kw.astype(BF16), vc)
        kn = jnp.sum(kw, axis=0, keepdims=True)
        return qcb, b_col, b_last, m_loc, num_loc, den_loc, g_max, kv, kn

    def body(grp, carry):
        r0 = pl.multiple_of(grp * (MLSTM_GROUP * L), MLSTM_GROUP * L)
        parts = [[local_part(h, r0 + j * L, grp * MLSTM_GROUP + j) for j in range(MLSTM_GROUP)]
                 for h in range(n_heads)]
        new_carry, outs = [], []
        for h in range(n_heads):
            state, n_row, m_prev = carry[h]
            gain = gain_ref[h]
            head_out = []
            for qcb, b_col, b_last, m_loc, num_loc, den_loc, g_max, kv, kn in parts[h]:
                m_inter = b_col + m_prev
                m_t = jnp.maximum(m_inter, m_loc)
                intra = jnp.exp(m_loc - m_t)
                inter = jnp.exp(m_inter - m_t)
                num = intra * num_loc + inter * _dot(qcb, state.astype(BF16))
                den = intra * den_loc + inter * jnp.sum(qcb.astype(F32) * n_row, axis=-1, keepdims=True)
                h_out = num / jnp.maximum(jnp.abs(den), jnp.exp(-m_t))
                hs = h_out * lax.rsqrt(jnp.mean(h_out * h_out, axis=-1, keepdims=True) + EPS) * gain
                head_out.append(hs.astype(o_ref.dtype))
                m_new = jnp.maximum(b_last + m_prev, g_max)
                decay = jnp.exp(b_last + m_prev - m_new)
                grow = jnp.exp(g_max - m_new)
                state, n_row, m_prev = decay * state + grow * kv, decay * n_row + grow * kn, m_new
            new_carry.append((state, n_row, m_prev))
            outs.append(jnp.concatenate(head_out, axis=0))
        for h in range(n_heads):
            o_ref[0, pl.ds(r0, MLSTM_GROUP * L), h * dv:(h + 1) * dv] = outs[h]
        return tuple(new_carry)

    init = tuple((jnp.zeros((dqk, dv), F32), jnp.zeros((1, dqk), F32), jnp.zeros((1, 1), F32))
                 for _ in range(n_heads))
    lax.fori_loop(0, n_chunks // MLSTM_GROUP, body, init)


def _mlstm_call(q, k, v, gates, b_igate, b_fgate, norm_gain):
    b, s, _ = q.shape
    nh = MLSTM_HEADS
    dqk, dv = q.shape[-1] // nh, v.shape[-1] // nh
    L = MLSTM_CHUNK
    nch = s // L
    g = jnp.transpose(gates[..., :2 * nh], (0, 2, 1)).reshape(b, 2 * nh, nch, L)
    smem = pl.BlockSpec(memory_space=pltpu.SMEM)
    per_b = lambda n: pl.BlockSpec((1, s, n), lambda i: (i, 0, 0))
    kern = functools.partial(_mlstm_kernel, n_chunks=nch, n_heads=nh, dqk=dqk, dv=dv)
    return pl.pallas_call(
        kern,
        out_shape=jax.ShapeDtypeStruct((b, s, nh * dv), BF16),
        grid=(b,),
        in_specs=[smem, smem, per_b(nh * dqk), per_b(nh * dqk), per_b(nh * dv),
                  pl.BlockSpec((1, 2 * nh, nch, L), lambda i: (i, 0, 0, 0)),
                  pl.BlockSpec((nh, 1, dv), lambda i: (0, 0, 0))],
        out_specs=per_b(nh * dv),
        scratch_shapes=[pltpu.VMEM((nh, nch, L), F32), pltpu.VMEM((nh, nch, L), F32)],
        compiler_params=_cparams(1),
        name="mlstm_chunk_scan",
    )(b_igate, b_fgate, q, k, v, g, norm_gain.reshape(nh, 1, dv))


def _router_kernel(x_ref, gain_ref, sc_ref, sh_ref, wh_ref, wl_ref, rb_ref, h_ref, route_ref, cnt_ref):
    h = _norm_mod(x_ref[0], gain_ref[...], sc_ref[0], sh_ref[0])
    hi = h.astype(BF16)
    lo = (h - hi.astype(F32)).astype(BF16)
    logits = _dot(hi, wh_ref[...]) + (_dot(lo, wh_ref[...]) + _dot(hi, wl_ref[...]))
    h_ref[0] = hi
    tm = logits.shape[0]
    aff = jax.nn.sigmoid(logits.T[:N_EXPERTS])
    choice = (aff + rb_ref[...]).reshape(N_GROUPS, EXPERTS_PER_GROUP, tm)
    local = lax.broadcasted_iota(jnp.int32, choice.shape, 1)

    def first_max(v):
        m = jnp.max(v, axis=1, keepdims=True)
        return m, jnp.min(jnp.where(v == m, local, EXPERTS_PER_GROUP), axis=1, keepdims=True)

    m1, i1 = first_max(choice)
    m2, i2 = first_max(jnp.where(local == i1, -jnp.inf, choice))
    score = m1 + m2
    best, e0, e1 = score[0], i1[0], i2[0]
    for g in range(1, N_GROUPS):
        better = score[g] > best
        best = jnp.where(better, score[g], best)
        e0 = jnp.where(better, i1[g] + g * EXPERTS_PER_GROUP, e0)
        e1 = jnp.where(better, i2[g] + g * EXPERTS_PER_GROUP, e1)
    expert = lax.broadcasted_iota(jnp.int32, (N_EXPERTS, tm), 0)
    is0, is1 = expert == e0, expert == e1
    a0 = jnp.sum(jnp.where(is0, aff, 0.0), axis=0, keepdims=True)
    a1 = jnp.sum(jnp.where(is1, aff, 0.0), axis=0, keepdims=True)
    tot = a0 + a1
    onehot = jnp.where(is0 | is1, 1.0, 0.0).astype(BF16)
    r_i = lax.broadcasted_iota(jnp.int32, (tm, tm), 0)
    c_i = lax.broadcasted_iota(jnp.int32, (tm, tm), 1)
    running = _dot(onehot, jnp.where(r_i <= c_i, 1.0, 0.0).astype(BF16))
    r0 = jnp.sum(jnp.where(is0, running, 0.0), axis=0, keepdims=True) - 1.0
    r1 = jnp.sum(jnp.where(is1, running, 0.0), axis=0, keepdims=True) - 1.0
    rows = [e0.astype(F32), e1.astype(F32), r0, r1, a0 / tot, a1 / tot]
    packed = jnp.concatenate(rows + [jnp.zeros((LANES - len(rows), tm), F32)], axis=0)
    route_ref[0] = packed.T
    cnt_ref[0] = jnp.broadcast_to(running[:, tm - 1:tm], (N_EXPERTS, LANES))


def _router_call(x, gain, sc, sh, router_w, router_b, tm):
    b, s, d = x.shape
    nt = s // tm
    wp = jnp.pad(router_w, ((0, 0), (0, LANES - N_EXPERTS)))
    wh = wp.astype(BF16)
    wl = (wp - wh.astype(F32)).astype(BF16)
    rb = router_b.astype(F32).reshape(N_EXPERTS, 1)
    row = lambda i, j: (i, j, 0)
    per_b = lambda i, j: (i, 0, 0)
    const = lambda i, j: (0, 0)
    return pl.pallas_call(
        _router_kernel,
        out_shape=(jax.ShapeDtypeStruct((b, s, d), BF16), jax.ShapeDtypeStruct((b, s, LANES), F32),
                   jax.ShapeDtypeStruct((b * nt, N_EXPERTS, LANES), F32)),
        grid=(b, nt),
        in_specs=[pl.BlockSpec((1, tm, d), row),
                  pl.BlockSpec((1, d), const),
                  pl.BlockSpec((1, 1, d), per_b),
                  pl.BlockSpec((1, 1, d), per_b),
                  pl.BlockSpec((d, LANES), const),
                  pl.BlockSpec((d, LANES), const),
                  pl.BlockSpec((N_EXPERTS, 1), const)],
        out_specs=(pl.BlockSpec((1, tm, d), row), pl.BlockSpec((1, tm, LANES), row),
                   pl.BlockSpec((1, N_EXPERTS, LANES), lambda i, j: (i * nt + j, 0, 0))),
        compiler_params=_cparams(2),
        name="moe_router",
    )(x, gain.reshape(1, d), sc, sh, wh, wl, rb)


def _expert_kernel(be_ref, live_ref, x_ref, wg_ref, wu_ref, wd_ref, o_ref, wg_s, wu_s, wd_s):
    i = pl.program_id(0)
    fresh = (i == 0) | (be_ref[i] != be_ref[jnp.maximum(i - 1, 0)])

    @pl.when(fresh)
    def _():
        wg_s[...] = wg_ref[0, 0].astype(BF16)
        wu_s[...] = wu_ref[0, 0].astype(BF16)
        wd_s[...] = wd_ref[0, 0].astype(BF16)

    @pl.when(live_ref[i] == 1)
    def _():
        xb = x_ref[...]
        gate = _dot(xb, wg_s[...])
        hid = gate * jax.nn.sigmoid(gate) * _dot(xb, wu_s[...])
        o_ref[...] = _dot(hid.astype(BF16), wd_s[...]).astype(o_ref.dtype)

    @pl.when(live_ref[i] == 0)
    def _():
        o_ref[...] = jnp.zeros_like(o_ref)


def _expert_call(block_expert, block_live, xs, w_gate, w_up, w_down, layer):
    p, d = xs.shape
    de = w_gate.shape[-1]
    nb = p // MOE_ROWS
    grid_spec = pltpu.PrefetchScalarGridSpec(
        num_scalar_prefetch=2,
        grid=(nb,),
        in_specs=[pl.BlockSpec((MOE_ROWS, d), lambda i, be, lv: (i, 0)),
                  pl.BlockSpec((1, 1, d, de), lambda i, be, lv: (layer, be[i], 0, 0)),
                  pl.BlockSpec((1, 1, d, de), lambda i, be, lv: (layer, be[i], 0, 0)),
                  pl.BlockSpec((1, 1, de, d), lambda i, be, lv: (layer, be[i], 0, 0))],
        out_specs=pl.BlockSpec((MOE_ROWS, d), lambda i, be, lv: (i, 0)),
        scratch_shapes=[pltpu.VMEM((d, de), BF16), pltpu.VMEM((d, de), BF16), pltpu.VMEM((de, d), BF16)],
    )
    return pl.pallas_call(
        _expert_kernel,
        out_shape=jax.ShapeDtypeStruct((p, d), BF16),
        grid_spec=grid_spec,
        compiler_params=_cparams(1),
        name="moe_experts",
    )(block_expert, block_live, xs, w_gate, w_up, w_down)


def _combine_kernel(x_ref, g_ref, route_ref, y_ref, o_ref):
    d = x_ref.shape[-1]
    w = route_ref[0][:, 2 * TOP_K:3 * TOP_K]
    y = w[:, 0:1] * y_ref[0, :, :d].astype(F32) + w[:, 1:2] * y_ref[0, :, d:].astype(F32)
    o_ref[0] = x_ref[0] + g_ref[0] * y


def _combine_call(x, g, route, y2, tm):
    b, s, d = x.shape
    row = lambda i, j: (i, j, 0)
    spec = pl.BlockSpec((1, tm, d), row)
    return pl.pallas_call(
        _combine_kernel,
        out_shape=jax.ShapeDtypeStruct((b, s, d), F32),
        grid=(b, s // tm),
        in_specs=[spec, pl.BlockSpec((1, 1, d), lambda i, j: (i, 0, 0)),
                  pl.BlockSpec((1, tm, LANES), row), pl.BlockSpec((1, tm, TOP_K * d), row)],
        out_specs=spec,
        compiler_params=_cparams(2),
        name="moe_combine",
    )(x, g, route, y2)


def _row_layout(ri, cnt, tm):
    t = ri.shape[0]
    nt = t // tm
    lanes = jnp.arange(N_EXPERTS, dtype=jnp.int32)
    tile_cnt = cnt[:, :, 0].astype(jnp.int32)
    tile_off = jnp.cumsum(tile_cnt, axis=0) - tile_cnt
    counts = jnp.sum(tile_cnt, axis=0)
    padded = (counts + MOE_ROWS - 1) // MOE_ROWS * MOE_ROWS
    p_ends = jnp.cumsum(padded)
    base = (p_ends - padded)[None, :] + tile_off
    e = ri[:, 0:TOP_K].reshape(nt, tm, TOP_K)
    rank = ri[:, TOP_K:2 * TOP_K].reshape(nt, tm, TOP_K)
    dest = jnp.sum(jnp.where(e[..., None] == lanes, base[:, None, None, :], 0), axis=-1) + rank
    dest = dest.reshape(t, TOP_K)
    nb = (t * TOP_K) // MOE_ROWS + N_EXPERTS
    tok = jnp.repeat(jnp.arange(t, dtype=jnp.int32), TOP_K)
    buf_tok = (jnp.arange(nb * MOE_ROWS, dtype=jnp.int32) % t).at[dest.reshape(-1)].set(
        tok, mode="promise_in_bounds", unique_indices=True)
    block_start = jnp.arange(nb, dtype=jnp.int32) * MOE_ROWS
    block_expert = jnp.minimum(jnp.sum((p_ends[None, :] <= block_start[:, None]).astype(jnp.int32), axis=-1),
                               N_EXPERTS - 1)
    block_live = (block_start < p_ends[-1]).astype(jnp.int32)
    return dest, buf_tok, block_expert, block_live


def _moe_layer(x, gain, sc, sh, g, router_w, router_b, w_gate, w_up, w_down, layer, tm):
    b, s, d = x.shape
    t = b * s
    hf, route, cnt = _router_call(x, gain, sc, sh, router_w, router_b, tm)
    ri = route.reshape(t, LANES)[:, :2 * TOP_K].astype(jnp.int32)
    dest, buf_tok, block_expert, block_live = _row_layout(ri, cnt, tm)
    take = lambda a, idx: a.at[idx].get(mode="promise_in_bounds")
    xs = take(hf.reshape(t, d), buf_tok)
    out = _expert_call(block_expert, block_live, xs, w_gate, w_up, w_down, layer)
    y2 = take(out, dest.reshape(-1)).reshape(b, s, TOP_K * d)
    return _combine_call(x, g, route, y2, tm)


def _nsa_layer(x, gain, sc, sh, g, w_in, w_out, q_gain, k_gain, cmp_pe, cmp_w1, cmp_b1, cmp_w2, cmp_b2, tm):
    b, s, d = x.shape
    G, dh = NSA_KV_GROUPS, HEAD_DIM
    cos, sin = _rope_tables(jnp.arange(s, dtype=jnp.int32))
    q, cv, ks, vs, kw, vw, gates = _nsa_in_call(x, gain, sc, sh, w_in, q_gain, k_gain, cos, sin, tm)

    n_cmp = (s - CMP_BLOCK) // CMP_STRIDE + 1
    n_str = s // CMP_STRIDE
    cmp_pos = jnp.arange(n_str, dtype=jnp.int32) * CMP_STRIDE + (CMP_BLOCK - 1)
    ccos, csin = _rope_tables(cmp_pos)
    cmp = _compress_call(cv, cmp_pe, cmp_w1, cmp_b1, cmp_w2, cmp_b2, k_gain[0], ccos, csin)
    cmp = jnp.pad(cmp, ((0, 0), (0, 0), (0, LANES - n_str), (0, 0)))
    kc, vc = cmp[0], cmp[1]

    ns = s // SEL_BLOCK
    r_, u_ = SEL_BLOCK // CMP_STRIDE, CMP_BLOCK // CMP_STRIDE
    c_idx = (r_ * np.arange(ns)[:, None, None] + np.arange(r_)[None, :, None]
             + np.arange(u_)[None, None, :]).reshape(ns, -1)
    c2s = (c_idx[:, :, None] == np.arange(n_cmp)[None, None, :]).sum(1).astype(np.float32)
    c2s = jnp.asarray(np.pad(c2s, ((0, 0), (0, LANES - n_cmp))))

    o = _nsa_attn_call(q, gates, kc, vc, ks, vs, kw, vw, c2s, n_cmp)
    return _out_proj_call(o, None, w_out[_head_pair_order(), :], x, g, tm)


def _mlstm_layer(x, gain, sc, sh, g, w_in, w_out, b_igate, b_fgate, norm_gain, tm):
    nh = MLSTM_HEADS
    dv = norm_gain.shape[-1]
    dqk = (w_in.shape[-1] - 2 * nh - 2 * nh * dv) // (2 * nh)
    q, k, v, og, gates = _mlstm_in_call(x, gain, sc, sh, w_in, dqk, dv, tm)
    hs = _mlstm_call(q, k, v, gates, b_igate, b_fgate, norm_gain)
    return _out_proj_call(hs, og, w_out, x, g, tm)


def kernel(x, c, ada_w, ada_b, norm_mix_gain, norm_ffn_gain, nsa_w_in, nsa_w_out, nsa_q_gain, nsa_k_gain, nsa_cmp_pe, nsa_cmp_w1, nsa_cmp_b1, nsa_cmp_w2, nsa_cmp_b2, mlstm_w_in, mlstm_b_igate, mlstm_b_fgate, mlstm_norm_gain, mlstm_w_out, router_w, router_b, moe_w_gate, moe_w_up, moe_w_down):
    b, s, d = x.shape
    depth = ada_w.shape[0]
    tm = min(512, s)
    mod = _mod_call(c, ada_w, ada_b)
    for i in range(depth):
        sh_m, sc_m, g_m, sh_f, sc_f, g_f = [mod[i, :, None, k * d:(k + 1) * d] for k in range(6)]
        j = i // 2
        if i % 2 == 0:
            x = _nsa_layer(x, norm_mix_gain[i], sc_m, sh_m, g_m, nsa_w_in[j], nsa_w_out[j], nsa_q_gain[j],
                           nsa_k_gain[j], nsa_cmp_pe[j], nsa_cmp_w1[j], nsa_cmp_b1[j], nsa_cmp_w2[j],
                           nsa_cmp_b2[j], tm)
        else:
            x = _mlstm_layer(x, norm_mix_gain[i], sc_m, sh_m, g_m, mlstm_w_in[j], mlstm_w_out[j],
                             mlstm_b_igate[j], mlstm_b_fgate[j], mlstm_norm_gain[j], tm)
        x = _moe_layer(x, norm_ffn_gain[i], sc_f, sh_f, g_f, router_w, router_b,
                       moe_w_gate, moe_w_up, moe_w_down, i, tm)
    return x
```

```python
import functools

import numpy as np
import jax
import jax.numpy as jnp
from jax import lax
from jax.experimental import pallas as pl
from jax.experimental.pallas import tpu as pltpu

F32 = jnp.float32
BF16 = jnp.bfloat16
HIGHEST = lax.Precision.HIGHEST

EPS = 1e-6
NEG = -1e30
BIG = 1e9
ROPE_THETA = 500000.0
LOG2E = 1.4426950408889634

NSA_HEADS = 16
NSA_KV_GROUPS = 2
NSA_HEADS_PER_GROUP = NSA_HEADS // NSA_KV_GROUPS
HEAD_DIM = 64
ROT_DIM = HEAD_DIM // 4
CMP_BLOCK = 32
CMP_STRIDE = 16
SEL_BLOCK = 64
SEL_TOPN = 8
WINDOW = 512
NSA_Q_BLOCK = 64
NSA_BRANCHES = 3
NSA_BLOCKS_PER_STEP = 2

MLSTM_HEADS = 4
MLSTM_CHUNK = 256
MLSTM_GROUP = 1
GATE_SOFTCAP = 15.0

N_EXPERTS = 32
N_GROUPS = 4
EXPERTS_PER_GROUP = N_EXPERTS // N_GROUPS
TOP_K = 2
MOE_ROWS = 512

LANES = 128
VMEM_LIMIT = 48 * 1024 * 1024


def _cparams(n_axes):
    return pltpu.CompilerParams(dimension_semantics=("arbitrary",) * n_axes,
                                vmem_limit_bytes=VMEM_LIMIT)


def _dot(a, b):
    return jnp.dot(a, b, preferred_element_type=F32)


def _dot_nt(a, b):
    return lax.dot_general(a, b, (((1,), (1,)), ((), ())), preferred_element_type=F32)


def _dot_tn(a, b):
    return lax.dot_general(a, b, (((0,), (0,)), ((), ())), preferred_element_type=F32)


def _norm_mod(x, gain, sc, sh):
    y = x * lax.rsqrt(jnp.mean(x * x, axis=-1, keepdims=True) + EPS) * gain
    return y * (1.0 + sc) + sh


def _half_norm_rope(x, gain, cos, sin):
    lane = lax.broadcasted_iota(jnp.int32, x.shape, x.ndim - 1)
    x2 = x * x
    left = lane < HEAD_DIM
    ss_l = jnp.sum(jnp.where(left, x2, 0.0), axis=-1, keepdims=True)
    ss_r = jnp.sum(jnp.where(left, 0.0, x2), axis=-1, keepdims=True)
    ms = jnp.where(left, ss_l, ss_r) * (1.0 / HEAD_DIM)
    y = x * lax.rsqrt(ms + EPS) * gain
    half = ROT_DIM // 2
    nd = x.ndim - 1
    partner = jnp.where((lane % HEAD_DIM) < half,
                        pltpu.roll(y, LANES - half, nd), pltpu.roll(y, half, nd))
    return y * cos + partner * sin


def _rope_tables(pos):
    half = ROT_DIM // 2
    inv_freq = ROPE_THETA ** (-jnp.arange(half, dtype=F32) / half)
    ang = pos.astype(F32)[:, None] * inv_freq[None, :]
    cos, sin = jnp.cos(ang), jnp.sin(ang)
    n = pos.shape[0]
    one = jnp.ones((n, HEAD_DIM - ROT_DIM), F32)
    cos_h = jnp.concatenate([cos, cos, one], axis=-1)
    sin_h = jnp.concatenate([-sin, sin, 0.0 * one], axis=-1)
    return jnp.tile(cos_h, (1, 2)), jnp.tile(sin_h, (1, 2))


def _mod_kernel(c_ref, w_ref, b_ref, o_ref):
    c = c_ref[...]
    cond = c * jax.nn.sigmoid(c)
    o_ref[0] = jnp.dot(cond, w_ref[0], preferred_element_type=F32, precision=HIGHEST) + b_ref[0]


def _mod_call(c, ada_w, ada_b):
    depth, d, n = ada_w.shape
    b = c.shape[0]
    tn = n // 4
    return pl.pallas_call(
        _mod_kernel,
        out_shape=jax.ShapeDtypeStruct((depth, b, n), F32),
        grid=(depth, n // tn),
        in_specs=[pl.BlockSpec((b, d), lambda i, j: (0, 0)),
                  pl.BlockSpec((1, d, tn), lambda i, j: (i, 0, j)),
                  pl.BlockSpec((1, 1, tn), lambda i, j: (i, 0, j))],
        out_specs=pl.BlockSpec((1, b, tn), lambda i, j: (i, 0, j)),
        compiler_params=_cparams(2),
        name="adaln_mod",
    )(c, ada_w, ada_b.reshape(depth, 1, n))


def _nsa_in_kernel(x_ref, gain_ref, sc_ref, sh_ref, wq_ref, wkv_ref, wg_ref, qg_ref, kg_ref, cos_ref, sin_ref,
                   q_ref, cv_ref, ks_ref, vs_ref, kw_ref, vw_ref, g_ref):
    h = _norm_mod(x_ref[0], gain_ref[...], sc_ref[0], sh_ref[0]).astype(BF16)
    g_ref[0] = _dot(h, wg_ref[...])
    kv = _dot(h, wkv_ref[...])
    cos, sin = cos_ref[...], sin_ref[...]
    q = _dot(h, wq_ref[...])
    for r in range(NSA_HEADS_PER_GROUP):
        slab = _half_norm_rope(q[:, r * LANES:(r + 1) * LANES], qg_ref[...], cos, sin)
        q_ref[0, :, r * LANES:(r + 1) * LANES] = (slab * (HEAD_DIM ** -0.5 * LOG2E)).astype(BF16)
    cv_ref[0] = kv[:, 0:2 * LANES]
    ks_ref[0] = _half_norm_rope(kv[:, 2 * LANES:3 * LANES], kg_ref[1:2, :], cos, sin).astype(BF16)
    vs_ref[0] = kv[:, 3 * LANES:4 * LANES].astype(BF16)
    kw_ref[0] = _half_norm_rope(kv[:, 4 * LANES:5 * LANES], kg_ref[2:3, :], cos, sin).astype(BF16)
    vw_ref[0] = kv[:, 5 * LANES:6 * LANES].astype(BF16)


def _head_pair_order():
    r, g, dd = np.meshgrid(np.arange(NSA_HEADS_PER_GROUP), np.arange(NSA_KV_GROUPS), np.arange(HEAD_DIM),
                           indexing="ij")
    return ((g * NSA_HEADS_PER_GROUP + r) * HEAD_DIM + dd).reshape(-1)


def _nsa_in_call(x, gain, sc, sh, w_in, q_gain, k_gain, cos, sin, tm):
    b, s, d = x.shape
    nq = NSA_HEADS * HEAD_DIM
    nkv = 6 * LANES
    wq = w_in[:, :nq][:, _head_pair_order()].astype(BF16)
    qg = jnp.tile(q_gain, 2).reshape(1, LANES)
    wkv = w_in[:, nq:nq + nkv].astype(BF16)
    ng = NSA_BRANCHES * NSA_HEADS
    wg = jnp.pad(w_in[:, nq + nkv:], ((0, 0), (0, LANES - ng))).astype(BF16)
    kg = jnp.tile(k_gain, (1, 2))
    row = lambda i, j: (i, j, 0)
    per_b = lambda i, j: (i, 0, 0)
    const = lambda i, j: (0, 0)
    kv_out = lambda dt: jax.ShapeDtypeStruct((b, s, LANES), dt)
    return pl.pallas_call(
        _nsa_in_kernel,
        out_shape=(jax.ShapeDtypeStruct((b, s, nq), BF16), jax.ShapeDtypeStruct((b, s, 2 * LANES), F32),
                   kv_out(BF16), kv_out(BF16), kv_out(BF16), kv_out(BF16), kv_out(F32)),
        grid=(b, s // tm),
        in_specs=[pl.BlockSpec((1, tm, d), row),
                  pl.BlockSpec((1, d), const),
                  pl.BlockSpec((1, 1, d), per_b),
                  pl.BlockSpec((1, 1, d), per_b),
                  pl.BlockSpec((d, nq), const),
                  pl.BlockSpec((d, nkv), const),
                  pl.BlockSpec((d, LANES), const),
                  pl.BlockSpec((1, LANES), const),
                  pl.BlockSpec((3, LANES), const),
                  pl.BlockSpec((tm, LANES), lambda i, j: (j, 0)),
                  pl.BlockSpec((tm, LANES), lambda i, j: (j, 0))],
        out_specs=(pl.BlockSpec((1, tm, nq), row), pl.BlockSpec((1, tm, 2 * LANES), row))
        + (pl.BlockSpec((1, tm, LANES), row),) * 5,
        compiler_params=_cparams(2),
        name="nsa_in_proj",
    )(x, gain.reshape(1, d), sc, sh, wq, wkv, wg, qg, kg, cos, sin)


def _compress_kernel(a_ref, pe_ref, w1_ref, b1_ref, w2_ref, b2_ref, kg_ref, cos_ref, sin_ref, o_ref, *, n_str):
    is_key = pl.program_id(0) == 0
    hid2 = w1_ref.shape[-1]
    first = jnp.zeros((n_str, hid2), F32)
    second = jnp.zeros((n_str, hid2), F32)
    pe_term = jnp.zeros((8, hid2), F32)
    for l in range(CMP_STRIDE):
        rows = a_ref[0, pl.ds(l, n_str, stride=CMP_STRIDE), :].astype(BF16)
        first = first + _dot(rows, w1_ref[0, l])
        second = second + _dot(rows, w1_ref[0, CMP_STRIDE + l])
    for l in range(CMP_BLOCK):
        pe_term = pe_term + _dot(pe_ref[0, l].astype(BF16), w1_ref[0, l])
    hid = first + pltpu.roll(second, n_str - 1, 0) + pe_term[0:1] + b1_ref[0]
    hid = 0.5 * hid * (1.0 + jnp.tanh(np.sqrt(2.0 / np.pi) * (hid + 0.044715 * hid * hid * hid)))
    out = _dot(hid.astype(BF16), w2_ref[0]) + b2_ref[0]
    normed = _half_norm_rope(out, kg_ref[...], cos_ref[...], sin_ref[...])
    o_ref[0, 0] = jnp.where(is_key, normed, out).astype(o_ref.dtype)


def _block_diag2(w):
    z = jnp.zeros_like(w)
    return jnp.concatenate([jnp.concatenate([w, z], axis=-1), jnp.concatenate([z, w], axis=-1)], axis=-2)


def _compress_call(cv, pe, w1, b1, w2, b2, k_gain0, cos, sin):
    b, s, _ = cv.shape
    n_str = s // CMP_STRIDE
    hid = w1.shape[-1]
    w1bd = _block_diag2(w1.reshape(2, CMP_BLOCK, HEAD_DIM, hid)).astype(BF16)
    w2bd = _block_diag2(w2).astype(BF16)
    pe2 = jnp.broadcast_to(jnp.tile(pe, (1, 1, 2))[:, :, None, :], (2, CMP_BLOCK, 8, LANES))
    b1t = jnp.tile(b1, (1, 2)).reshape(2, 1, 2 * hid)
    b2t = jnp.tile(b2, (1, 2)).reshape(2, 1, LANES)
    kg = jnp.tile(k_gain0, 2).reshape(1, LANES)
    sel3 = lambda i, j: (i, 0, 0)
    sel4 = lambda i, j: (i, 0, 0, 0)
    const = lambda i, j: (0, 0)
    return pl.pallas_call(
        functools.partial(_compress_kernel, n_str=n_str),
        out_shape=jax.ShapeDtypeStruct((2, b, n_str, LANES), BF16),
        grid=(2, b),
        in_specs=[pl.BlockSpec((1, s, LANES), lambda i, j: (j, 0, i)),
                  pl.BlockSpec((1, CMP_BLOCK, 8, LANES), sel4),
                  pl.BlockSpec((1, CMP_BLOCK, LANES, 2 * hid), sel4),
                  pl.BlockSpec((1, 1, 2 * hid), sel3),
                  pl.BlockSpec((1, 2 * hid, LANES), sel3),
                  pl.BlockSpec((1, 1, LANES), sel3),
                  pl.BlockSpec((1, LANES), const),
                  pl.BlockSpec((n_str, LANES), const),
                  pl.BlockSpec((n_str, LANES), const)],
        out_specs=pl.BlockSpec((1, 1, n_str, LANES), lambda i, j: (i, j, 0, 0)),
        compiler_params=_cparams(2),
        name="nsa_compress",
    )(cv, pe2, w1bd, b1t, w2bd, b2t, kg, cos, sin)


def _attend(qb, k_ref, v_ref, k0, spans, g, bias):
    rows = qb.shape[0]
    m = acc = None
    for off, size in spans:
        k = k_ref[0, pl.ds(k0 + off, size), :]
        v = v_ref[0, pl.ds(k0 + off, size), :]
        s = _dot_nt(qb, k).reshape(rows // NSA_Q_BLOCK, NSA_Q_BLOCK, size) + bias[None, :, off:off + size]
        s = s.reshape(rows, size)
        m_span = jnp.max(s, axis=-1, keepdims=True)
        m_new = m_span if m is None else jnp.maximum(m, m_span)
        p = jnp.exp2(s - m_new).astype(BF16)
        v_lane = lax.broadcasted_iota(jnp.int32, v.shape, 1)
        pv = _dot(p, jnp.where((v_lane // HEAD_DIM) == g, v, jnp.ones_like(v)))
        acc = pv if m is None else acc * jnp.exp2(m - m_new) + pv
        m = m_new
    return acc


def _nsa_attn_kernel(*refs, blocks_per_step, **static):
    for sub in range(blocks_per_step):
        _nsa_attn_block(sub, pl.program_id(1) * blocks_per_step + sub, *refs, **static)


def _nsa_attn_block(sub, q_block, q_ref, g_ref, kc_ref, vc_ref, ks_ref, vs_ref, kw_ref, vw_ref,
                    c2s_ref, exp_ref, o_ref, *, seq, n_cmp, n_top, win_keys, sel_span, n_spans, q_block0):
    R, QB = NSA_HEADS_PER_GROUP, NSA_Q_BLOCK
    rows = R * QB
    n_sel = seq // SEL_BLOCK
    qi = q_block + q_block0
    s0 = qi * QB
    tok = slice(sub * QB, (sub + 1) * QB)
    gt = jax.nn.sigmoid(g_ref[0, tok, :])
    lane = lax.broadcasted_iota(jnp.int32, (QB, LANES), 1)
    tq = s0 + lax.broadcasted_iota(jnp.int32, (rows, 1), 0) % QB
    tq1 = s0 + lax.broadcasted_iota(jnp.int32, (QB, 1), 0)
    ones_sq = jnp.ones((LANES, LANES), BF16)

    w0 = pl.multiple_of(jnp.maximum(s0 + QB - win_keys, 0), SEL_BLOCK)
    wpos = w0 + lax.broadcasted_iota(jnp.int32, (1, win_keys), 1)
    bias_w = jnp.where((wpos <= tq1) & (wpos > tq1 - WINDOW), 0.0, NEG)
    win_spans = [(off, min(3 * LANES, win_keys - off)) for off in range(0, win_keys, 3 * LANES)]

    qbs, psums, o_cs, acc_ws = [], [], [], []
    for g in range(NSA_KV_GROUPS):
        in_g = (lane // HEAD_DIM) == g
        zero = jnp.zeros((QB, LANES), BF16)
        qb = jnp.concatenate([jnp.where(in_g, q_ref[0, tok, r * LANES:(r + 1) * LANES], zero) for r in range(R)],
                             axis=0)

        sc = _dot_nt(qb, kc_ref[0])
        cpos = lax.broadcasted_iota(jnp.int32, (1, LANES), 1) * CMP_STRIDE + (CMP_BLOCK - 1)
        valid_c = (cpos <= tq) & (lax.broadcasted_iota(jnp.int32, (1, LANES), 1) < n_cmp)
        sc = jnp.where(valid_c, sc, NEG)
        e_c = jnp.exp2(sc - jnp.max(sc, axis=-1, keepdims=True)).astype(BF16)
        p_c = jnp.where(valid_c, e_c.astype(F32) / _dot(e_c, ones_sq), 0.0)
        o_cs.append(_dot(p_c.astype(BF16), vc_ref[0]))

        psums.append(jnp.sum(p_c.reshape(R, QB, LANES), axis=0))

        acc_ws.append(_attend(qb, kw_ref, vw_ref, w0, win_spans, g, bias_w))
        qbs.append(qb)

    imp = lax.dot_general(c2s_ref[...], jnp.concatenate(psums, axis=0), (((1,), (1,)), ((), ())),
                          preferred_element_type=F32, precision=HIGHEST)
    blk = lax.broadcasted_iota(jnp.int32, (n_sel, NSA_KV_GROUPS * QB), 0)
    forced = (blk == 0) | (blk == qi) | (blk == qi - 1)
    imp = jnp.where(blk <= qi, jnp.where(forced, BIG, imp), -BIG)
    beaten = jnp.zeros(imp.shape, F32)
    for k in range(1, n_sel):
        other = pltpu.roll(imp, k, 0)
        beats = (other > imp) | ((blk >= k) & (other == imp))
        beaten = beaten + jnp.where(beats, 1.0, 0.0)
    chosen = jnp.where(beaten < n_top, 1.0, 0.0).astype(BF16)
    n_keys = n_spans * sel_span
    picked = _dot_tn(chosen, exp_ref[:, :n_keys])
    kpos = lax.broadcasted_iota(jnp.int32, (1, n_keys), 1)
    sel_spans = [(c * sel_span, sel_span) for c in range(n_spans)]

    heads = []
    for g in range(NSA_KV_GROUPS):
        bias_s = jnp.where((picked[g * QB:(g + 1) * QB] > 0.5) & (kpos <= tq1), 0.0, NEG)
        acc_s = _attend(qbs[g], ks_ref, vs_ref, 0, sel_spans, g, bias_s)
        o_c, acc_w = o_cs[g], acc_ws[g]
        c_sum = (1 - g) * HEAD_DIM
        per_head = []
        for r in range(R):
            h = g * R + r
            rs = slice(r * QB, (r + 1) * QB)
            a_s, a_w = acc_s[rs], acc_w[rs]
            g_s = gt[:, NSA_HEADS + h:NSA_HEADS + h + 1] / a_s[:, c_sum:c_sum + 1]
            g_w = gt[:, 2 * NSA_HEADS + h:2 * NSA_HEADS + h + 1] / a_w[:, c_sum:c_sum + 1]
            per_head.append(gt[:, h:h + 1] * o_c[rs] + g_s * a_s + g_w * a_w)
        heads.append(per_head)
    for r in range(R):
        slab = jnp.where(lane < HEAD_DIM, heads[0][r], heads[1][r])
        o_ref[0, tok, r * LANES:(r + 1) * LANES] = slab.astype(o_ref.dtype)


def _nsa_attn_call(q, gates, kc, vc, ks, vs, kw, vw, cmp_to_sel, n_cmp):
    b, s, nq = q.shape
    qb = NSA_Q_BLOCK
    n_top = min(SEL_TOPN, s // SEL_BLOCK)
    win_keys = min(WINDOW + 2 * qb, s)
    sel_span = min(512, s)
    n_sel = s // SEL_BLOCK
    expand = jnp.asarray(np.arange(n_sel)[:, None] == (np.arange(s)[None, :] // SEL_BLOCK), BF16)
    per_b = lambda i, j: (i, 0, 0)
    const = lambda i, j: (0, 0)
    per_call = sel_span // qb
    per_step = min(NSA_BLOCKS_PER_STEP, per_call)
    qrows = per_step * qb
    outs = []
    for n in range(1, s // sel_span + 1):
        q0 = (n - 1) * per_call
        row = lambda i, j, t0=q0 // per_step: (i, j + t0, 0)
        kern = functools.partial(_nsa_attn_kernel, blocks_per_step=per_step, seq=s, n_cmp=n_cmp, n_top=n_top,
                                 win_keys=win_keys, sel_span=sel_span, n_spans=n, q_block0=q0)
        outs.append(pl.pallas_call(
            kern,
            out_shape=jax.ShapeDtypeStruct((b, sel_span, nq), BF16),
            grid=(b, per_call // per_step),
            in_specs=[pl.BlockSpec((1, qrows, nq), row),
                      pl.BlockSpec((1, qrows, LANES), row),
                      pl.BlockSpec((1, LANES, LANES), per_b),
                      pl.BlockSpec((1, LANES, LANES), per_b),
                      pl.BlockSpec((1, n * sel_span, LANES), per_b),
                      pl.BlockSpec((1, n * sel_span, LANES), per_b),
                      pl.BlockSpec((1, s, LANES), per_b),
                      pl.BlockSpec((1, s, LANES), per_b),
                      pl.BlockSpec((n_sel, LANES), const),
                      pl.BlockSpec((n_sel, s), const)],
            out_specs=pl.BlockSpec((1, qrows, nq), lambda i, j: (i, j, 0)),
            compiler_params=_cparams(2),
            name="nsa_attention",
        )(q, gates, kc, vc, ks, vs, kw, vw, cmp_to_sel, expand))
    return jnp.concatenate(outs, axis=1)


def _nsa_out_kernel(a_ref, w_ref, x_ref, g_ref, o_ref):
    o_ref[0] = x_ref[0] + g_ref[0] * _dot(a_ref[0], w_ref[...])


def _mlstm_out_kernel(a_ref, og_ref, w_ref, x_ref, g_ref, o_ref):
    lhs = (jax.nn.sigmoid(og_ref[0].astype(F32)) * a_ref[0].astype(F32)).astype(BF16)
    o_ref[0] = x_ref[0] + g_ref[0] * _dot(lhs, w_ref[...])


def _out_proj_call(a, og, w_out, x, g, tm):
    b, s, d = x.shape
    k = a.shape[-1]
    row = lambda i, j: (i, j, 0)
    per_b = lambda i, j: (i, 0, 0)
    a_spec = pl.BlockSpec((1, tm, k), row)
    tail = [pl.BlockSpec((k, d), lambda i, j: (0, 0)), pl.BlockSpec((1, tm, d), row),
            pl.BlockSpec((1, 1, d), per_b)]
    if og is None:
        kern, ins, args = _nsa_out_kernel, [a_spec] + tail, (a, w_out.astype(BF16), x, g)
    else:
        kern, ins, args = _mlstm_out_kernel, [a_spec, a_spec] + tail, (a, og, w_out.astype(BF16), x, g)
    return pl.pallas_call(
        kern,
        out_shape=jax.ShapeDtypeStruct((b, s, d), F32),
        grid=(b, s // tm),
        in_specs=ins,
        out_specs=pl.BlockSpec((1, tm, d), row),
        compiler_params=_cparams(2),
        name="mixer_out_proj",
    )(*args)


def _mlstm_in_kernel(x_ref, gain_ref, sc_ref, sh_ref, wq_ref, wk_ref, wv_ref, wo_ref, wg_ref,
                     q_ref, k_ref, v_ref, o_ref, g_ref):
    h = _norm_mod(x_ref[0], gain_ref[...], sc_ref[0], sh_ref[0]).astype(BF16)
    q_ref[0] = _dot(h, wq_ref[...]).astype(BF16)
    k_ref[0] = _dot(h, wk_ref[...]).astype(BF16)
    v_ref[0] = _dot(h, wv_ref[...]).astype(BF16)
    o_ref[0] = _dot(h, wo_ref[...]).astype(BF16)
    g_ref[0] = _dot(h, wg_ref[...])


def _mlstm_in_call(x, gain, sc, sh, w_in, dqk, dv, tm):
    b, s, d = x.shape
    nh = MLSTM_HEADS
    sizes = [nh * dqk, nh * dqk, nh * dv, nh * dv]
    offs = np.cumsum([0] + sizes)
    ws = [w_in[:, offs[i]:offs[i + 1]].astype(BF16) for i in range(4)]
    wg = jnp.pad(w_in[:, offs[4]:], ((0, 0), (0, LANES - 2 * nh))).astype(BF16)
    row = lambda i, j: (i, j, 0)
    per_b = lambda i, j: (i, 0, 0)
    const = lambda i, j: (0, 0)
    widths = sizes + [LANES]
    return pl.pallas_call(
        _mlstm_in_kernel,
        out_shape=tuple(jax.ShapeDtypeStruct((b, s, n), BF16) for n in sizes)
        + (jax.ShapeDtypeStruct((b, s, LANES), F32),),
        grid=(b, s // tm),
        in_specs=[pl.BlockSpec((1, tm, d), row),
                  pl.BlockSpec((1, d), const),
                  pl.BlockSpec((1, 1, d), per_b),
                  pl.BlockSpec((1, 1, d), per_b)] + [pl.BlockSpec((d, n), const) for n in widths],
        out_specs=tuple(pl.BlockSpec((1, tm, n), row) for n in widths),
        compiler_params=_cparams(2),
        name="mlstm_in_proj",
    )(x, gain.reshape(1, d), sc, sh, *ws, wg)


def _softcap(a):
    return GATE_SOFTCAP * jnp.tanh(a / GATE_SOFTCAP)


def _mlstm_kernel(bi_ref, bf_ref, q_ref, k_ref, v_ref, g_ref, gain_ref, o_ref,
                  li_s, b_s, *, n_chunks, n_heads, dqk, dv):
    L = MLSTM_CHUNK
    r_i = lax.broadcasted_iota(jnp.int32, (L, L), 0)
    c_i = lax.broadcasted_iota(jnp.int32, (L, L), 1)
    upper = jnp.where(r_i <= c_i, 1.0, 0.0)
    for h in range(n_heads):
        li_s[h] = _softcap(g_ref[0, h] + bi_ref[h])
        fa = _softcap(g_ref[0, n_heads + h] + bf_ref[h])
        lf = jnp.minimum(fa, 0.0) - jnp.log1p(jnp.exp(-jnp.abs(fa)))
        b_s[h] = jnp.dot(lf, upper, preferred_element_type=F32, precision=HIGHEST)
    eye = r_i == c_i
    causal = r_i >= c_i
    k_scale = dqk ** -0.5

    def to_col(row):
        return jnp.sum(jnp.where(eye, jnp.broadcast_to(row, (L, L)), 0.0), axis=1, keepdims=True)

    def local_part(h, r0, c):
        qcb = q_ref[0, pl.ds(r0, L), h * dqk:(h + 1) * dqk]
        kc = k_ref[0, pl.ds(r0, L), h * dqk:(h + 1) * dqk].astype(F32) * k_scale
        vc = v_ref[0, pl.ds(r0, L), h * dv:(h + 1) * dv]
        b_row = b_s[h, pl.ds(c, 1), :]
        li_row = li_s[h, pl.ds(c, 1), :]
        b_col, li_col = to_col(b_row), to_col(li_row)
        b_last = b_row[:, L - 1:L]
        dmat = jnp.where(causal, b_col - b_row + li_row, NEG)
        m_loc = jnp.max(dmat, axis=-1, keepdims=True)
        a_loc = jnp.exp(dmat - m_loc) * _dot_nt(qcb, kc.astype(BF16))
        num_loc = _dot(a_loc.astype(BF16), vc)
        den_loc = jnp.sum(a_loc, axis=-1, keepdims=True)
        g_max = m_loc[L - 1:L, :]
        kw = kc * jnp.exp(b_last - b_col + li_col - g_max)
        kv = _dot_tn(kw.astype(BF16), vc)
        kn = jnp.sum(kw, axis=0, keepdims=True)
        return qcb, b_col, b_last, m_loc, num_loc, den_loc, g_max, kv, kn

    def body(grp, carry):
        r0 = pl.multiple_of(grp * (MLSTM_GROUP * L), MLSTM_GROUP * L)
        parts = [[local_part(h, r0 + j * L, grp * MLSTM_GROUP + j) for j in range(MLSTM_GROUP)]
                 for h in range(n_heads)]
        new_carry, outs = [], []
        for h in range(n_heads):
            state, n_row, m_prev = carry[h]
            gain = gain_ref[h]
            head_out = []
            for qcb, b_col, b_last, m_loc, num_loc, den_loc, g_max, kv, kn in parts[h]:
                m_inter = b_col + m_prev
                m_t = jnp.maximum(m_inter, m_loc)
                intra = jnp.exp(m_loc - m_t)
                inter = jnp.exp(m_inter - m_t)
                num = intra * num_loc + inter * _dot(qcb, state.astype(BF16))
                den = intra * den_loc + inter * jnp.sum(qcb.astype(F32) * n_row, axis=-1, keepdims=True)
                h_out = num / jnp.maximum(jnp.abs(den), jnp.exp(-m_t))
                hs = h_out * lax.rsqrt(jnp.mean(h_out * h_out, axis=-1, keepdims=True) + EPS) * gain
                head_out.append(hs.astype(o_ref.dtype))
                m_new = jnp.maximum(b_last + m_prev, g_max)
                decay = jnp.exp(b_last + m_prev - m_new)
                grow = jnp.exp(g_max - m_new)
                state, n_row, m_prev = decay * state + grow * kv, decay * n_row + grow * kn, m_new
            new_carry.append((state, n_row, m_prev))
            outs.append(jnp.concatenate(head_out, axis=0))
        for h in range(n_heads):
            o_ref[0, pl.ds(r0, MLSTM_GROUP * L), h * dv:(h + 1) * dv] = outs[h]
        return tuple(new_carry)

    init = tuple((jnp.zeros((dqk, dv), F32), jnp.zeros((1, dqk), F32), jnp.zeros((1, 1), F32))
                 for _ in range(n_heads))
    lax.fori_loop(0, n_chunks // MLSTM_GROUP, body, init)


def _mlstm_call(q, k, v, gates, b_igate, b_fgate, norm_gain):
    b, s, _ = q.shape
    nh = MLSTM_HEADS
    dqk, dv = q.shape[-1] // nh, v.shape[-1] // nh
    L = MLSTM_CHUNK
    nch = s // L
    g = jnp.transpose(gates[..., :2 * nh], (0, 2, 1)).reshape(b, 2 * nh, nch, L)
    smem = pl.BlockSpec(memory_space=pltpu.SMEM)
    per_b = lambda n: pl.BlockSpec((1, s, n), lambda i: (i, 0, 0))
    kern = functools.partial(_mlstm_kernel, n_chunks=nch, n_heads=nh, dqk=dqk, dv=dv)
    return pl.pallas_call(
        kern,
        out_shape=jax.ShapeDtypeStruct((b, s, nh * dv), BF16),
        grid=(b,),
        in_specs=[smem, smem, per_b(nh * dqk), per_b(nh * dqk), per_b(nh * dv),
                  pl.BlockSpec((1, 2 * nh, nch, L), lambda i: (i, 0, 0, 0)),
                  pl.BlockSpec((nh, 1, dv), lambda i: (0, 0, 0))],
        out_specs=per_b(nh * dv),
        scratch_shapes=[pltpu.VMEM((nh, nch, L), F32), pltpu.VMEM((nh, nch, L), F32)],
        compiler_params=_cparams(1),
        name="mlstm_chunk_scan",
    )(b_igate, b_fgate, q, k, v, g, norm_gain.reshape(nh, 1, dv))


def _router_kernel(x_ref, gain_ref, sc_ref, sh_ref, wh_ref, wl_ref, rb_ref, h_ref, route_ref, cnt_ref):
    h = _norm_mod(x_ref[0], gain_ref[...], sc_ref[0], sh_ref[0])
    hi = h.astype(BF16)
    lo = (h - hi.astype(F32)).astype(BF16)
    logits = _dot(hi, wh_ref[...]) + (_dot(lo, wh_ref[...]) + _dot(hi, wl_ref[...]))
    h_ref[0] = hi
    tm = logits.shape[0]
    aff = jax.nn.sigmoid(logits.T[:N_EXPERTS])
    choice = (aff + rb_ref[...]).reshape(N_GROUPS, EXPERTS_PER_GROUP, tm)
    local = lax.broadcasted_iota(jnp.int32, choice.shape, 1)

    def first_max(v):
        m = jnp.max(v, axis=1, keepdims=True)
        return m, jnp.min(jnp.where(v == m, local, EXPERTS_PER_GROUP), axis=1, keepdims=True)

    m1, i1 = first_max(choice)
    m2, i2 = first_max(jnp.where(local == i1, -jnp.inf, choice))
    score = m1 + m2
    best, e0, e1 = score[0], i1[0], i2[0]
    for g in range(1, N_GROUPS):
        better = score[g] > best
        best = jnp.where(better, score[g], best)
        e0 = jnp.where(better, i1[g] + g * EXPERTS_PER_GROUP, e0)
        e1 = jnp.where(better, i2[g] + g * EXPERTS_PER_GROUP, e1)
    expert = lax.broadcasted_iota(jnp.int32, (N_EXPERTS, tm), 0)
    is0, is1 = expert == e0, expert == e1
    a0 = jnp.sum(jnp.where(is0, aff, 0.0), axis=0, keepdims=True)
    a1 = jnp.sum(jnp.where(is1, aff, 0.0), axis=0, keepdims=True)
    tot = a0 + a1
    onehot = jnp.where(is0 | is1, 1.0, 0.0).astype(BF16)
    r_i = lax.broadcasted_iota(jnp.int32, (tm, tm), 0)
    c_i = lax.broadcasted_iota(jnp.int32, (tm, tm), 1)
    running = _dot(onehot, jnp.where(r_i <= c_i, 1.0, 0.0).astype(BF16))
    r0 = jnp.sum(jnp.where(is0, running, 0.0), axis=0, keepdims=True) - 1.0
    r1 = jnp.sum(jnp.where(is1, running, 0.0), axis=0, keepdims=True) - 1.0
    rows = [e0.astype(F32), e1.astype(F32), r0, r1, a0 / tot, a1 / tot]
    packed = jnp.concatenate(rows + [jnp.zeros((LANES - len(rows), tm), F32)], axis=0)
    route_ref[0] = packed.T
    cnt_ref[0] = jnp.broadcast_to(running[:, tm - 1:tm], (N_EXPERTS, LANES))


def _router_call(x, gain, sc, sh, router_w, router_b, tm):
    b, s, d = x.shape
    nt = s // tm
    wp = jnp.pad(router_w, ((0, 0), (0, LANES - N_EXPERTS)))
    wh = wp.astype(BF16)
    wl = (wp - wh.astype(F32)).astype(BF16)
    rb = router_b.astype(F32).reshape(N_EXPERTS, 1)
    row = lambda i, j: (i, j, 0)
    per_b = lambda i, j: (i, 0, 0)
    const = lambda i, j: (0, 0)
    return pl.pallas_call(
        _router_kernel,
        out_shape=(jax.ShapeDtypeStruct((b, s, d), BF16), jax.ShapeDtypeStruct((b, s, LANES), F32),
                   jax.ShapeDtypeStruct((b * nt, N_EXPERTS, LANES), F32)),
        grid=(b, nt),
        in_specs=[pl.BlockSpec((1, tm, d), row),
                  pl.BlockSpec((1, d), const),
                  pl.BlockSpec((1, 1, d), per_b),
                  pl.BlockSpec((1, 1, d), per_b),
                  pl.BlockSpec((d, LANES), const),
                  pl.BlockSpec((d, LANES), const),
                  pl.BlockSpec((N_EXPERTS, 1), const)],
        out_specs=(pl.BlockSpec((1, tm, d), row), pl.BlockSpec((1, tm, LANES), row),
                   pl.BlockSpec((1, N_EXPERTS, LANES), lambda i, j: (i * nt + j, 0, 0))),
        compiler_params=_cparams(2),
        name="moe_router",
    )(x, gain.reshape(1, d), sc, sh, wh, wl, rb)


def _expert_kernel(be_ref, live_ref, x_ref, wg_ref, wu_ref, wd_ref, o_ref, wg_s, wu_s, wd_s):
    i = pl.program_id(0)
    fresh = (i == 0) | (be_ref[i] != be_ref[jnp.maximum(i - 1, 0)])

    @pl.when(fresh)
    def _():
        wg_s[...] = wg_ref[0, 0].astype(BF16)
        wu_s[...] = wu_ref[0, 0].astype(BF16)
        wd_s[...] = wd_ref[0, 0].astype(BF16)

    @pl.when(live_ref[i] == 1)
    def _():
        xb = x_ref[...]
        gate = _dot(xb, wg_s[...])
        hid = gate * jax.nn.sigmoid(gate) * _dot(xb, wu_s[...])
        o_ref[...] = _dot(hid.astype(BF16), wd_s[...]).astype(o_ref.dtype)

    @pl.when(live_ref[i] == 0)
    def _():
        o_ref[...] = jnp.zeros_like(o_ref)


def _expert_call(block_expert, block_live, xs, w_gate, w_up, w_down, layer):
    p, d = xs.shape
    de = w_gate.shape[-1]
    nb = p // MOE_ROWS
    grid_spec = pltpu.PrefetchScalarGridSpec(
        num_scalar_prefetch=2,
        grid=(nb,),
        in_specs=[pl.BlockSpec((MOE_ROWS, d), lambda i, be, lv: (i, 0)),
                  pl.BlockSpec((1, 1, d, de), lambda i, be, lv: (layer, be[i], 0, 0)),
                  pl.BlockSpec((1, 1, d, de), lambda i, be, lv: (layer, be[i], 0, 0)),
                  pl.BlockSpec((1, 1, de, d), lambda i, be, lv: (layer, be[i], 0, 0))],
        out_specs=pl.BlockSpec((MOE_ROWS, d), lambda i, be, lv: (i, 0)),
        scratch_shapes=[pltpu.VMEM((d, de), BF16), pltpu.VMEM((d, de), BF16), pltpu.VMEM((de, d), BF16)],
    )
    return pl.pallas_call(
        _expert_kernel,
        out_shape=jax.ShapeDtypeStruct((p, d), BF16),
        grid_spec=grid_spec,
        compiler_params=_cparams(1),
        name="moe_experts",
    )(block_expert, block_live, xs, w_gate, w_up, w_down)


def _combine_kernel(x_ref, g_ref, route_ref, ya_ref, yb_ref, o_ref):
    w = route_ref[0][:, 2 * TOP_K:3 * TOP_K]
    y = w[:, 0:1] * ya_ref[0].astype(F32) + w[:, 1:2] * yb_ref[0].astype(F32)
    o_ref[0] = x_ref[0] + g_ref[0] * y


def _combine_call(x, g, route, ya, yb, tm):
    b, s, d = x.shape
    row = lambda i, j: (i, j, 0)
    spec = pl.BlockSpec((1, tm, d), row)
    return pl.pallas_call(
        _combine_kernel,
        out_shape=jax.ShapeDtypeStruct((b, s, d), F32),
        grid=(b, s // tm),
        in_specs=[spec, pl.BlockSpec((1, 1, d), lambda i, j: (i, 0, 0)),
                  pl.BlockSpec((1, tm, LANES), row), spec, spec],
        out_specs=spec,
        compiler_params=_cparams(2),
        name="moe_combine",
    )(x, g, route, ya, yb)


def _row_layout(ri, cnt, tm):
    t = ri.shape[0]
    nt = t // tm
    lanes = jnp.arange(N_EXPERTS, dtype=jnp.int32)
    tile_cnt = cnt[:, :, 0].astype(jnp.int32)
    tile_off = jnp.cumsum(tile_cnt, axis=0) - tile_cnt
    counts = jnp.sum(tile_cnt, axis=0)
    padded = (counts + MOE_ROWS - 1) // MOE_ROWS * MOE_ROWS
    p_ends = jnp.cumsum(padded)
    base = (p_ends - padded)[None, :] + tile_off
    e = ri[:, 0:TOP_K].reshape(nt, tm, TOP_K)
    rank = ri[:, TOP_K:2 * TOP_K].reshape(nt, tm, TOP_K)
    dest = jnp.sum(jnp.where(e[..., None] == lanes, base[:, None, None, :], 0), axis=-1) + rank
    dest = dest.reshape(t, TOP_K)
    nb = (t * TOP_K) // MOE_ROWS + N_EXPERTS
    tok = jnp.repeat(jnp.arange(t, dtype=jnp.int32), TOP_K)
    buf_tok = (jnp.arange(nb * MOE_ROWS, dtype=jnp.int32) % t).at[dest.reshape(-1)].set(
        tok, mode="promise_in_bounds", unique_indices=True)
    block_start = jnp.arange(nb, dtype=jnp.int32) * MOE_ROWS
    block_expert = jnp.minimum(jnp.sum((p_ends[None, :] <= block_start[:, None]).astype(jnp.int32), axis=-1),
                               N_EXPERTS - 1)
    block_live = (block_start < p_ends[-1]).astype(jnp.int32)
    return dest, buf_tok, block_expert, block_live


def _moe_layer(x, gain, sc, sh, g, router_w, router_b, w_gate, w_up, w_down, layer, tm):
    b, s, d = x.shape
    t = b * s
    hf, route, cnt = _router_call(x, gain, sc, sh, router_w, router_b, tm)
    ri = route.reshape(t, LANES)[:, :2 * TOP_K].astype(jnp.int32)
    dest, buf_tok, block_expert, block_live = _row_layout(ri, cnt, tm)
    take = lambda a, idx: a.at[idx].get(mode="promise_in_bounds")
    xs = take(hf.reshape(t, d), buf_tok)
    out = _expert_call(block_expert, block_live, xs, w_gate, w_up, w_down, layer)
    ya = take(out, dest[:, 0]).reshape(b, s, d)
    yb = take(out, dest[:, 1]).reshape(b, s, d)
    return _combine_call(x, g, route, ya, yb, tm)


def _nsa_layer(x, gain, sc, sh, g, w_in, w_out, q_gain, k_gain, cmp_pe, cmp_w1, cmp_b1, cmp_w2, cmp_b2, tm):
    b, s, d = x.shape
    G, dh = NSA_KV_GROUPS, HEAD_DIM
    cos, sin = _rope_tables(jnp.arange(s, dtype=jnp.int32))
    q, cv, ks, vs, kw, vw, gates = _nsa_in_call(x, gain, sc, sh, w_in, q_gain, k_gain, cos, sin, tm)

    n_cmp = (s - CMP_BLOCK) // CMP_STRIDE + 1
    n_str = s // CMP_STRIDE
    cmp_pos = jnp.arange(n_str, dtype=jnp.int32) * CMP_STRIDE + (CMP_BLOCK - 1)
    ccos, csin = _rope_tables(cmp_pos)
    cmp = _compress_call(cv, cmp_pe, cmp_w1, cmp_b1, cmp_w2, cmp_b2, k_gain[0], ccos, csin)
    cmp = jnp.pad(cmp, ((0, 0), (0, 0), (0, LANES - n_str), (0, 0)))
    kc, vc = cmp[0], cmp[1]

    ns = s // SEL_BLOCK
    r_, u_ = SEL_BLOCK // CMP_STRIDE, CMP_BLOCK // CMP_STRIDE
    c_idx = (r_ * np.arange(ns)[:, None, None] + np.arange(r_)[None, :, None]
             + np.arange(u_)[None, None, :]).reshape(ns, -1)
    c2s = (c_idx[:, :, None] == np.arange(n_cmp)[None, None, :]).sum(1).astype(np.float32)
    c2s = jnp.asarray(np.pad(c2s, ((0, 0), (0, LANES - n_cmp))))

    o = _nsa_attn_call(q, gates, kc, vc, ks, vs, kw, vw, c2s, n_cmp)
    return _out_proj_call(o, None, w_out[_head_pair_order(), :], x, g, tm)


def _mlstm_layer(x, gain, sc, sh, g, w_in, w_out, b_igate, b_fgate, norm_gain, tm):
    nh = MLSTM_HEADS
    dv = norm_gain.shape[-1]
    dqk = (w_in.shape[-1] - 2 * nh - 2 * nh * dv) // (2 * nh)
    q, k, v, og, gates = _mlstm_in_call(x, gain, sc, sh, w_in, dqk, dv, tm)
    hs = _mlstm_call(q, k, v, gates, b_igate, b_fgate, norm_gain)
    return _out_proj_call(hs, og, w_out, x, g, tm)


def kernel(x, c, ada_w, ada_b, norm_mix_gain, norm_ffn_gain, nsa_w_in, nsa_w_out, nsa_q_gain, nsa_k_gain, nsa_cmp_pe, nsa_cmp_w1, nsa_cmp_b1, nsa_cmp_w2, nsa_cmp_b2, mlstm_w_in, mlstm_b_igate, mlstm_b_fgate, mlstm_norm_gain, mlstm_w_out, router_w, router_b, moe_w_gate, moe_w_up, moe_w_down):
    b, s, d = x.shape
    depth = ada_w.shape[0]
    tm = min(512, s)
    mod = _mod_call(c, ada_w, ada_b)
    for i in range(depth):
        sh_m, sc_m, g_m, sh_f, sc_f, g_f = [mod[i, :, None, k * d:(k + 1) * d] for k in range(6)]
        j = i // 2
        if i % 2 == 0:
            x = _nsa_layer(x, norm_mix_gain[i], sc_m, sh_m, g_m, nsa_w_in[j], nsa_w_out[j], nsa_q_gain[j],
                           nsa_k_gain[j], nsa_cmp_pe[j], nsa_cmp_w1[j], nsa_cmp_b1[j], nsa_cmp_w2[j],
                           nsa_cmp_b2[j], tm)
        else:
            x = _mlstm_layer(x, norm_mix_gain[i], sc_m, sh_m, g_m, mlstm_w_in[j], mlstm_w_out[j],
                             mlstm_b_igate[j], mlstm_b_fgate[j], mlstm_norm_gain[j], tm)
        x = _moe_layer(x, norm_ffn_gain[i], sc_f, sh_f, g_f, router_w, router_b,
                       moe_w_gate, moe_w_up, moe_w_down, i, tm)
    return x
```

```python
import functools

import numpy as np
import jax
import jax.numpy as jnp
from jax import lax
from jax.experimental import pallas as pl
from jax.experimental.pallas import tpu as pltpu

F32 = jnp.float32
BF16 = jnp.bfloat16
HIGHEST = lax.Precision.HIGHEST

EPS = 1e-6
NEG = -1e30
BIG = 1e9
ROPE_THETA = 500000.0
LOG2E = 1.4426950408889634

NSA_HEADS = 16
NSA_KV_GROUPS = 2
NSA_HEADS_PER_GROUP = NSA_HEADS // NSA_KV_GROUPS
HEAD_DIM = 64
ROT_DIM = HEAD_DIM // 4
CMP_BLOCK = 32
CMP_STRIDE = 16
SEL_BLOCK = 64
SEL_TOPN = 8
WINDOW = 512
NSA_Q_BLOCK = 64
NSA_BRANCHES = 3
NSA_BLOCKS_PER_STEP = 2

MLSTM_HEADS = 4
MLSTM_CHUNK = 256
MLSTM_GROUP = 1
GATE_SOFTCAP = 15.0

N_EXPERTS = 32
N_GROUPS = 4
EXPERTS_PER_GROUP = N_EXPERTS // N_GROUPS
TOP_K = 2
MOE_ROWS = 512

LANES = 128
VMEM_LIMIT = 48 * 1024 * 1024


def _cparams(n_axes):
    return pltpu.CompilerParams(dimension_semantics=("arbitrary",) * n_axes,
                                vmem_limit_bytes=VMEM_LIMIT)


def _dot(a, b):
    return jnp.dot(a, b, preferred_element_type=F32)


def _dot_nt(a, b):
    return lax.dot_general(a, b, (((1,), (1,)), ((), ())), preferred_element_type=F32)


def _dot_tn(a, b):
    return lax.dot_general(a, b, (((0,), (0,)), ((), ())), preferred_element_type=F32)


def _norm_mod(x, gain, sc, sh):
    y = x * lax.rsqrt(jnp.mean(x * x, axis=-1, keepdims=True) + EPS) * gain
    return y * (1.0 + sc) + sh


def _half_norm_rope(x, gain, cos, sin):
    lane = lax.broadcasted_iota(jnp.int32, x.shape, x.ndim - 1)
    x2 = x * x
    left = lane < HEAD_DIM
    ss_l = jnp.sum(jnp.where(left, x2, 0.0), axis=-1, keepdims=True)
    ss_r = jnp.sum(jnp.where(left, 0.0, x2), axis=-1, keepdims=True)
    ms = jnp.where(left, ss_l, ss_r) * (1.0 / HEAD_DIM)
    y = x * lax.rsqrt(ms + EPS) * gain
    half = ROT_DIM // 2
    src = lax.broadcasted_iota(jnp.int32, (LANES, LANES), 0)
    dst = lax.broadcasted_iota(jnp.int32, (LANES, LANES), 1)
    dst_in_head = dst % HEAD_DIM
    pair = jnp.where(dst_in_head < half, dst + half, jnp.where(dst_in_head < ROT_DIM, dst - half, -1))
    partner = _dot(y.astype(BF16), jnp.where(src == pair, 1.0, 0.0).astype(BF16))
    return y * cos + partner * sin


def _rope_tables(pos):
    half = ROT_DIM // 2
    inv_freq = ROPE_THETA ** (-jnp.arange(half, dtype=F32) / half)
    ang = pos.astype(F32)[:, None] * inv_freq[None, :]
    cos, sin = jnp.cos(ang), jnp.sin(ang)
    n = pos.shape[0]
    one = jnp.ones((n, HEAD_DIM - ROT_DIM), F32)
    cos_h = jnp.concatenate([cos, cos, one], axis=-1)
    sin_h = jnp.concatenate([-sin, sin, 0.0 * one], axis=-1)
    return jnp.tile(cos_h, (1, 2)), jnp.tile(sin_h, (1, 2))


def _mod_kernel(c_ref, w_ref, b_ref, o_ref):
    c = c_ref[...]
    cond = c * jax.nn.sigmoid(c)
    o_ref[0] = jnp.dot(cond, w_ref[0], preferred_element_type=F32, precision=HIGHEST) + b_ref[0]


def _mod_call(c, ada_w, ada_b):
    depth, d, n = ada_w.shape
    b = c.shape[0]
    tn = n // 4
    return pl.pallas_call(
        _mod_kernel,
        out_shape=jax.ShapeDtypeStruct((depth, b, n), F32),
        grid=(depth, n // tn),
        in_specs=[pl.BlockSpec((b, d), lambda i, j: (0, 0)),
                  pl.BlockSpec((1, d, tn), lambda i, j: (i, 0, j)),
                  pl.BlockSpec((1, 1, tn), lambda i, j: (i, 0, j))],
        out_specs=pl.BlockSpec((1, b, tn), lambda i, j: (i, 0, j)),
        compiler_params=_cparams(2),
        name="adaln_mod",
    )(c, ada_w, ada_b.reshape(depth, 1, n))


def _nsa_in_kernel(x_ref, gain_ref, sc_ref, sh_ref, wq_ref, wkv_ref, wg_ref, qg_ref, kg_ref, cos_ref, sin_ref,
                   q_ref, cv_ref, ks_ref, vs_ref, kw_ref, vw_ref, g_ref):
    h = _norm_mod(x_ref[0], gain_ref[...], sc_ref[0], sh_ref[0]).astype(BF16)
    g_ref[0] = _dot(h, wg_ref[...])
    kv = _dot(h, wkv_ref[...])
    cos, sin = cos_ref[...], sin_ref[...]
    q = _dot(h, wq_ref[...])
    for r in range(NSA_HEADS_PER_GROUP):
        slab = _half_norm_rope(q[:, r * LANES:(r + 1) * LANES], qg_ref[...], cos, sin)
        q_ref[0, :, r * LANES:(r + 1) * LANES] = (slab * (HEAD_DIM ** -0.5 * LOG2E)).astype(BF16)
    cv_ref[0] = kv[:, 0:2 * LANES]
    ks_ref[0] = _half_norm_rope(kv[:, 2 * LANES:3 * LANES], kg_ref[1:2, :], cos, sin).astype(BF16)
    vs_ref[0] = kv[:, 3 * LANES:4 * LANES].astype(BF16)
    kw_ref[0] = _half_norm_rope(kv[:, 4 * LANES:5 * LANES], kg_ref[2:3, :], cos, sin).astype(BF16)
    vw_ref[0] = kv[:, 5 * LANES:6 * LANES].astype(BF16)


def _head_pair_order():
    r, g, dd = np.meshgrid(np.arange(NSA_HEADS_PER_GROUP), np.arange(NSA_KV_GROUPS), np.arange(HEAD_DIM),
                           indexing="ij")
    return ((g * NSA_HEADS_PER_GROUP + r) * HEAD_DIM + dd).reshape(-1)


def _nsa_in_call(x, gain, sc, sh, w_in, q_gain, k_gain, cos, sin, tm):
    b, s, d = x.shape
    nq = NSA_HEADS * HEAD_DIM
    nkv = 6 * LANES
    wq = w_in[:, :nq][:, _head_pair_order()].astype(BF16)
    qg = jnp.tile(q_gain, 2).reshape(1, LANES)
    wkv = w_in[:, nq:nq + nkv].astype(BF16)
    ng = NSA_BRANCHES * NSA_HEADS
    wg = jnp.pad(w_in[:, nq + nkv:], ((0, 0), (0, LANES - ng))).astype(BF16)
    kg = jnp.tile(k_gain, (1, 2))
    row = lambda i, j: (i, j, 0)
    per_b = lambda i, j: (i, 0, 0)
    const = lambda i, j: (0, 0)
    kv_out = lambda dt: jax.ShapeDtypeStruct((b, s, LANES), dt)
    return pl.pallas_call(
        _nsa_in_kernel,
        out_shape=(jax.ShapeDtypeStruct((b, s, nq), BF16), jax.ShapeDtypeStruct((b, s, 2 * LANES), F32),
                   kv_out(BF16), kv_out(BF16), kv_out(BF16), kv_out(BF16), kv_out(F32)),
        grid=(b, s // tm),
        in_specs=[pl.BlockSpec((1, tm, d), row),
                  pl.BlockSpec((1, d), const),
                  pl.BlockSpec((1, 1, d), per_b),
                  pl.BlockSpec((1, 1, d), per_b),
                  pl.BlockSpec((d, nq), const),
                  pl.BlockSpec((d, nkv), const),
                  pl.BlockSpec((d, LANES), const),
                  pl.BlockSpec((1, LANES), const),
                  pl.BlockSpec((3, LANES), const),
                  pl.BlockSpec((tm, LANES), lambda i, j: (j, 0)),
                  pl.BlockSpec((tm, LANES), lambda i, j: (j, 0))],
        out_specs=(pl.BlockSpec((1, tm, nq), row), pl.BlockSpec((1, tm, 2 * LANES), row))
        + (pl.BlockSpec((1, tm, LANES), row),) * 5,
        compiler_params=_cparams(2),
        name="nsa_in_proj",
    )(x, gain.reshape(1, d), sc, sh, wq, wkv, wg, qg, kg, cos, sin)


def _compress_kernel(a_ref, pe_ref, w1_ref, b1_ref, w2_ref, b2_ref, kg_ref, cos_ref, sin_ref, o_ref, *, n_str):
    is_key = pl.program_id(0) == 0
    hid2 = w1_ref.shape[-1]
    first = jnp.zeros((n_str, hid2), F32)
    second = jnp.zeros((n_str, hid2), F32)
    pe_term = jnp.zeros((8, hid2), F32)
    for l in range(CMP_STRIDE):
        rows = a_ref[0, pl.ds(l, n_str, stride=CMP_STRIDE), :].astype(BF16)
        first = first + _dot(rows, w1_ref[0, l])
        second = second + _dot(rows, w1_ref[0, CMP_STRIDE + l])
    for l in range(CMP_BLOCK):
        pe_term = pe_term + _dot(pe_ref[0, l].astype(BF16), w1_ref[0, l])
    hid = first + pltpu.roll(second, n_str - 1, 0) + pe_term[0:1] + b1_ref[0]
    hid = 0.5 * hid * (1.0 + jnp.tanh(np.sqrt(2.0 / np.pi) * (hid + 0.044715 * hid * hid * hid)))
    out = _dot(hid.astype(BF16), w2_ref[0]) + b2_ref[0]
    normed = _half_norm_rope(out, kg_ref[...], cos_ref[...], sin_ref[...])
    o_ref[0, 0] = jnp.where(is_key, normed, out).astype(o_ref.dtype)


def _block_diag2(w):
    z = jnp.zeros_like(w)
    return jnp.concatenate([jnp.concatenate([w, z], axis=-1), jnp.concatenate([z, w], axis=-1)], axis=-2)


def _compress_call(cv, pe, w1, b1, w2, b2, k_gain0, cos, sin):
    b, s, _ = cv.shape
    n_str = s // CMP_STRIDE
    hid = w1.shape[-1]
    w1bd = _block_diag2(w1.reshape(2, CMP_BLOCK, HEAD_DIM, hid)).astype(BF16)
    w2bd = _block_diag2(w2).astype(BF16)
    pe2 = jnp.broadcast_to(jnp.tile(pe, (1, 1, 2))[:, :, None, :], (2, CMP_BLOCK, 8, LANES))
    b1t = jnp.tile(b1, (1, 2)).reshape(2, 1, 2 * hid)
    b2t = jnp.tile(b2, (1, 2)).reshape(2, 1, LANES)
    kg = jnp.tile(k_gain0, 2).reshape(1, LANES)
    sel3 = lambda i, j: (i, 0, 0)
    sel4 = lambda i, j: (i, 0, 0, 0)
    const = lambda i, j: (0, 0)
    return pl.pallas_call(
        functools.partial(_compress_kernel, n_str=n_str),
        out_shape=jax.ShapeDtypeStruct((2, b, n_str, LANES), BF16),
        grid=(2, b),
        in_specs=[pl.BlockSpec((1, s, LANES), lambda i, j: (j, 0, i)),
                  pl.BlockSpec((1, CMP_BLOCK, 8, LANES), sel4),
                  pl.BlockSpec((1, CMP_BLOCK, LANES, 2 * hid), sel4),
                  pl.BlockSpec((1, 1, 2 * hid), sel3),
                  pl.BlockSpec((1, 2 * hid, LANES), sel3),
                  pl.BlockSpec((1, 1, LANES), sel3),
                  pl.BlockSpec((1, LANES), const),
                  pl.BlockSpec((n_str, LANES), const),
                  pl.BlockSpec((n_str, LANES), const)],
        out_specs=pl.BlockSpec((1, 1, n_str, LANES), lambda i, j: (i, j, 0, 0)),
        compiler_params=_cparams(2),
        name="nsa_compress",
    )(cv, pe2, w1bd, b1t, w2bd, b2t, kg, cos, sin)


def _attend(qb, k_ref, v_ref, k0, spans, g, bias):
    rows = qb.shape[0]
    m = acc = None
    for off, size in spans:
        k = k_ref[0, pl.ds(k0 + off, size), :]
        v = v_ref[0, pl.ds(k0 + off, size), :]
        s = _dot_nt(qb, k).reshape(rows // NSA_Q_BLOCK, NSA_Q_BLOCK, size) + bias[None, :, off:off + size]
        s = s.reshape(rows, size)
        m_span = jnp.max(s, axis=-1, keepdims=True)
        m_new = m_span if m is None else jnp.maximum(m, m_span)
        p = jnp.exp2(s - m_new).astype(BF16)
        v_lane = lax.broadcasted_iota(jnp.int32, v.shape, 1)
        pv = _dot(p, jnp.where((v_lane // HEAD_DIM) == g, v, jnp.ones_like(v)))
        acc = pv if m is None else acc * jnp.exp2(m - m_new) + pv
        m = m_new
    return acc


def _nsa_attn_kernel(*refs, blocks_per_step, n_inputs, **static):
    for sub in range(blocks_per_step):
        _nsa_attn_block(sub, pl.program_id(1) * blocks_per_step + sub, *refs[:n_inputs], refs[-1], **static)


def _nsa_attn_block(sub, q_block, q_ref, g_ref, kc_ref, vc_ref, ks_ref, vs_ref, kw_ref, vw_ref,
                    c2s_ref, exp_ref, o_ref, *, seq, n_cmp, n_top, win_keys, sel_span, n_spans, q_block0):
    R, QB = NSA_HEADS_PER_GROUP, NSA_Q_BLOCK
    rows = R * QB
    n_sel = seq // SEL_BLOCK
    qi = q_block + q_block0
    s0 = qi * QB
    tok = slice(sub * QB, (sub + 1) * QB)
    gt = jax.nn.sigmoid(g_ref[0, tok, :])
    lane = lax.broadcasted_iota(jnp.int32, (QB, LANES), 1)
    tq = s0 + lax.broadcasted_iota(jnp.int32, (rows, 1), 0) % QB
    tq1 = s0 + lax.broadcasted_iota(jnp.int32, (QB, 1), 0)
    ones_sq = jnp.ones((LANES, LANES), BF16)

    w0 = pl.multiple_of(jnp.maximum(s0 + QB - win_keys, 0), SEL_BLOCK)
    wpos = w0 + lax.broadcasted_iota(jnp.int32, (1, win_keys), 1)
    bias_w = jnp.where((wpos <= tq1) & (wpos > tq1 - WINDOW), 0.0, NEG)
    win_spans = [(off, min(3 * LANES, win_keys - off)) for off in range(0, win_keys, 3 * LANES)]

    qbs, psums, o_cs, acc_ws = [], [], [], []
    for g in range(NSA_KV_GROUPS):
        in_g = (lane // HEAD_DIM) == g
        zero = jnp.zeros((QB, LANES), BF16)
        qb = jnp.concatenate([jnp.where(in_g, q_ref[0, tok, r * LANES:(r + 1) * LANES], zero) for r in range(R)],
                             axis=0)

        sc = _dot_nt(qb, kc_ref[0])
        cpos = lax.broadcasted_iota(jnp.int32, (1, LANES), 1) * CMP_STRIDE + (CMP_BLOCK - 1)
        valid_c = (cpos <= tq) & (lax.broadcasted_iota(jnp.int32, (1, LANES), 1) < n_cmp)
        sc = jnp.where(valid_c, sc, NEG)
        e_c = jnp.exp2(sc - jnp.max(sc, axis=-1, keepdims=True)).astype(BF16)
        p_c = jnp.where(valid_c, e_c.astype(F32) / _dot(e_c, ones_sq), 0.0)
        o_cs.append(_dot(p_c.astype(BF16), vc_ref[0]))

        psums.append(jnp.sum(p_c.reshape(R, QB, LANES), axis=0))

        acc_ws.append(_attend(qb, kw_ref, vw_ref, w0, win_spans, g, bias_w))
        qbs.append(qb)

    imp = lax.dot_general(c2s_ref[...], jnp.concatenate(psums, axis=0), (((1,), (1,)), ((), ())),
                          preferred_element_type=F32, precision=HIGHEST)
    blk = lax.broadcasted_iota(jnp.int32, (n_sel, NSA_KV_GROUPS * QB), 0)
    forced = (blk == 0) | (blk == qi) | (blk == qi - 1)
    imp = jnp.where(blk <= qi, jnp.where(forced, BIG, imp), -BIG)
    beaten = jnp.zeros(imp.shape, F32)
    for k in range(1, n_sel):
        other = pltpu.roll(imp, k, 0)
        beats = (other > imp) | ((blk >= k) & (other == imp))
        beaten = beaten + jnp.where(beats, 1.0, 0.0)
    chosen = jnp.where(beaten < n_top, 1.0, 0.0).astype(BF16)
    n_keys = n_spans * sel_span
    picked = _dot_tn(chosen, exp_ref[:, :n_keys])
    kpos = lax.broadcasted_iota(jnp.int32, (1, n_keys), 1)
    sel_spans = [(c * sel_span, sel_span) for c in range(n_spans)]

    heads = []
    for g in range(NSA_KV_GROUPS):
        bias_s = jnp.where((picked[g * QB:(g + 1) * QB] > 0.5) & (kpos <= tq1), 0.0, NEG)
        acc_s = _attend(qbs[g], ks_ref, vs_ref, 0, sel_spans, g, bias_s)
        o_c, acc_w = o_cs[g], acc_ws[g]
        c_sum = (1 - g) * HEAD_DIM
        per_head = []
        for r in range(R):
            h = g * R + r
            rs = slice(r * QB, (r + 1) * QB)
            a_s, a_w = acc_s[rs], acc_w[rs]
            g_s = gt[:, NSA_HEADS + h:NSA_HEADS + h + 1] / a_s[:, c_sum:c_sum + 1]
            g_w = gt[:, 2 * NSA_HEADS + h:2 * NSA_HEADS + h + 1] / a_w[:, c_sum:c_sum + 1]
            per_head.append(gt[:, h:h + 1] * o_c[rs] + g_s * a_s + g_w * a_w)
        heads.append(per_head)
    for r in range(R):
        slab = jnp.where(lane < HEAD_DIM, heads[0][r], heads[1][r])
        o_ref[0, tok, r * LANES:(r + 1) * LANES] = slab.astype(o_ref.dtype)


def _nsa_attn_call(q, gates, kc, vc, ks, vs, kw, vw, cmp_to_sel, n_cmp):
    b, s, nq = q.shape
    qb = NSA_Q_BLOCK
    n_top = min(SEL_TOPN, s // SEL_BLOCK)
    win_keys = min(WINDOW + 2 * qb, s)
    sel_span = min(512, s)
    n_sel = s // SEL_BLOCK
    expand = jnp.asarray(np.arange(n_sel)[:, None] == (np.arange(s)[None, :] // SEL_BLOCK), BF16)
    per_b = lambda i, j: (i, 0, 0)
    const = lambda i, j: (0, 0)
    per_call = sel_span // qb
    per_step = min(NSA_BLOCKS_PER_STEP, per_call)
    qrows = per_step * qb
    out = None
    for n in range(1, s // sel_span + 1):
        q0 = (n - 1) * per_call
        row = lambda i, j, t0=q0 // per_step: (i, j + t0, 0)
        kern = functools.partial(_nsa_attn_kernel, blocks_per_step=per_step, n_inputs=10, seq=s, n_cmp=n_cmp,
                                 n_top=n_top, win_keys=win_keys, sel_span=sel_span, n_spans=n, q_block0=q0)
        carried = [] if out is None else [out]
        out = pl.pallas_call(
            kern,
            out_shape=jax.ShapeDtypeStruct((b, s, nq), BF16),
            grid=(b, per_call // per_step),
            in_specs=[pl.BlockSpec((1, qrows, nq), row),
                      pl.BlockSpec((1, qrows, LANES), row),
                      pl.BlockSpec((1, LANES, LANES), per_b),
                      pl.BlockSpec((1, LANES, LANES), per_b),
                      pl.BlockSpec((1, n * sel_span, LANES), per_b),
                      pl.BlockSpec((1, n * sel_span, LANES), per_b),
                      pl.BlockSpec((1, s, LANES), per_b),
                      pl.BlockSpec((1, s, LANES), per_b),
                      pl.BlockSpec((n_sel, LANES), const),
                      pl.BlockSpec((n_sel, s), const)] + [pl.BlockSpec(memory_space=pl.ANY)] * len(carried),
            out_specs=pl.BlockSpec((1, qrows, nq), row),
            input_output_aliases={10: 0} if carried else {},
            compiler_params=_cparams(2),
            name="nsa_attention",
        )(q, gates, kc, vc, ks, vs, kw, vw, cmp_to_sel, expand, *carried)
    return out


def _nsa_out_kernel(a_ref, w_ref, x_ref, g_ref, o_ref):
    o_ref[0] = x_ref[0] + g_ref[0] * _dot(a_ref[0], w_ref[...])


def _mlstm_out_kernel(a_ref, og_ref, w_ref, x_ref, g_ref, o_ref):
    lhs = (jax.nn.sigmoid(og_ref[0].astype(F32)) * a_ref[0].astype(F32)).astype(BF16)
    o_ref[0] = x_ref[0] + g_ref[0] * _dot(lhs, w_ref[...])


def _out_proj_call(a, og, w_out, x, g, tm):
    b, s, d = x.shape
    k = a.shape[-1]
    row = lambda i, j: (i, j, 0)
    per_b = lambda i, j: (i, 0, 0)
    a_spec = pl.BlockSpec((1, tm, k), row)
    tail = [pl.BlockSpec((k, d), lambda i, j: (0, 0)), pl.BlockSpec((1, tm, d), row),
            pl.BlockSpec((1, 1, d), per_b)]
    if og is None:
        kern, ins, args = _nsa_out_kernel, [a_spec] + tail, (a, w_out.astype(BF16), x, g)
    else:
        kern, ins, args = _mlstm_out_kernel, [a_spec, a_spec] + tail, (a, og, w_out.astype(BF16), x, g)
    return pl.pallas_call(
        kern,
        out_shape=jax.ShapeDtypeStruct((b, s, d), F32),
        grid=(b, s // tm),
        in_specs=ins,
        out_specs=pl.BlockSpec((1, tm, d), row),
        compiler_params=_cparams(2),
        name="mixer_out_proj",
    )(*args)


def _mlstm_in_kernel(x_ref, gain_ref, sc_ref, sh_ref, wq_ref, wk_ref, wv_ref, wo_ref, wg_ref,
                     q_ref, k_ref, v_ref, o_ref, g_ref):
    h = _norm_mod(x_ref[0], gain_ref[...], sc_ref[0], sh_ref[0]).astype(BF16)
    q_ref[0] = _dot(h, wq_ref[...]).astype(BF16)
    k_ref[0] = _dot(h, wk_ref[...]).astype(BF16)
    v_ref[0] = _dot(h, wv_ref[...]).astype(BF16)
    o_ref[0] = _dot(h, wo_ref[...]).astype(BF16)
    g_ref[0] = _dot(h, wg_ref[...])


def _mlstm_in_call(x, gain, sc, sh, w_in, dqk, dv, tm):
    b, s, d = x.shape
    nh = MLSTM_HEADS
    sizes = [nh * dqk, nh * dqk, nh * dv, nh * dv]
    offs = np.cumsum([0] + sizes)
    ws = [w_in[:, offs[i]:offs[i + 1]].astype(BF16) for i in range(4)]
    wg = jnp.pad(w_in[:, offs[4]:], ((0, 0), (0, LANES - 2 * nh))).astype(BF16)
    row = lambda i, j: (i, j, 0)
    per_b = lambda i, j: (i, 0, 0)
    const = lambda i, j: (0, 0)
    widths = sizes + [LANES]
    return pl.pallas_call(
        _mlstm_in_kernel,
        out_shape=tuple(jax.ShapeDtypeStruct((b, s, n), BF16) for n in sizes)
        + (jax.ShapeDtypeStruct((b, s, LANES), F32),),
        grid=(b, s // tm),
        in_specs=[pl.BlockSpec((1, tm, d), row),
                  pl.BlockSpec((1, d), const),
                  pl.BlockSpec((1, 1, d), per_b),
                  pl.BlockSpec((1, 1, d), per_b)] + [pl.BlockSpec((d, n), const) for n in widths],
        out_specs=tuple(pl.BlockSpec((1, tm, n), row) for n in widths),
        compiler_params=_cparams(2),
        name="mlstm_in_proj",
    )(x, gain.reshape(1, d), sc, sh, *ws, wg)


def _softcap(a):
    return GATE_SOFTCAP * jnp.tanh(a / GATE_SOFTCAP)


def _mlstm_kernel(bi_ref, bf_ref, q_ref, k_ref, v_ref, g_ref, gain_ref, o_ref,
                  li_s, b_s, *, n_chunks, n_heads, dqk, dv):
    L = MLSTM_CHUNK
    r_i = lax.broadcasted_iota(jnp.int32, (L, L), 0)
    c_i = lax.broadcasted_iota(jnp.int32, (L, L), 1)
    upper = jnp.where(r_i <= c_i, 1.0, 0.0)
    for h in range(n_heads):
        li_s[h] = _softcap(g_ref[0, h] + bi_ref[h])
        fa = _softcap(g_ref[0, n_heads + h] + bf_ref[h])
        lf = jnp.minimum(fa, 0.0) - jnp.log1p(jnp.exp(-jnp.abs(fa)))
        b_s[h] = jnp.dot(lf, upper, preferred_element_type=F32, precision=HIGHEST)
    eye = r_i == c_i
    causal = r_i >= c_i
    k_scale = dqk ** -0.5

    def to_col(row):
        return jnp.sum(jnp.where(eye, jnp.broadcast_to(row, (L, L)), 0.0), axis=1, keepdims=True)

    def local_part(h, r0, c):
        qcb = q_ref[0, pl.ds(r0, L), h * dqk:(h + 1) * dqk]
        kc = k_ref[0, pl.ds(r0, L), h * dqk:(h + 1) * dqk].astype(F32) * k_scale
        vc = v_ref[0, pl.ds(r0, L), h * dv:(h + 1) * dv]
        b_row = b_s[h, pl.ds(c, 1), :]
        li_row = li_s[h, pl.ds(c, 1), :]
        b_col, li_col = to_col(b_row), to_col(li_row)
        b_last = b_row[:, L - 1:L]
        dmat = jnp.where(causal, b_col - b_row + li_row, NEG)
        m_loc = jnp.max(dmat, axis=-1, keepdims=True)
        a_loc = jnp.exp(dmat - m_loc) * _dot_nt(qcb, kc.astype(BF16))
        num_loc = _dot(a_loc.astype(BF16), vc)
        den_loc = jnp.sum(a_loc, axis=-1, keepdims=True)
        g_max = m_loc[L - 1:L, :]
        kw = kc * jnp.exp(b_last - b_col + li_col - g_max)
        kv = _dot_tn(kw.astype(BF16), vc)
        kn = jnp.sum(kw, axis=0, keepdims=True)
        return qcb, b_col, b_last, m_loc, num_loc, den_loc, g_max, kv, kn

    def body(grp, carry):
        r0 = pl.multiple_of(grp * (MLSTM_GROUP * L), MLSTM_GROUP * L)
        parts = [[local_part(h, r0 + j * L, grp * MLSTM_GROUP + j) for j in range(MLSTM_GROUP)]
                 for h in range(n_heads)]
        new_carry, outs = [], []
        for h in range(n_heads):
            state, n_row, m_prev = carry[h]
            gain = gain_ref[h]
            head_out = []
            for qcb, b_col, b_last, m_loc, num_loc, den_loc, g_max, kv, kn in parts[h]:
                m_inter = b_col + m_prev
                m_t = jnp.maximum(m_inter, m_loc)
                intra = jnp.exp(m_loc - m_t)
                inter = jnp.exp(m_inter - m_t)
                num = intra * num_loc + inter * _dot(qcb, state.astype(BF16))
                den = intra * den_loc + inter * jnp.sum(qcb.astype(F32) * n_row, axis=-1, keepdims=True)
                h_out = num / jnp.maximum(jnp.abs(den), jnp.exp(-m_t))
                hs = h_out * lax.rsqrt(jnp.mean(h_out * h_out, axis=-1, keepdims=True) + EPS) * gain
                head_out.append(hs.astype(o_ref.dtype))
                m_new = jnp.maximum(b_last + m_prev, g_max)
                decay = jnp.exp(b_last + m_prev - m_new)
                grow = jnp.exp(g_max - m_new)
                state, n_row, m_prev = decay * state + grow * kv, decay * n_row + grow * kn, m_new
            new_carry.append((state, n_row, m_prev))
            outs.append(jnp.concatenate(head_out, axis=0))
        for h in range(n_heads):
            o_ref[0, pl.ds(r0, MLSTM_GROUP * L), h * dv:(h + 1) * dv] = outs[h]
        return tuple(new_carry)

    init = tuple((jnp.zeros((dqk, dv), F32), jnp.zeros((1, dqk), F32), jnp.zeros((1, 1), F32))
                 for _ in range(n_heads))
    lax.fori_loop(0, n_chunks // MLSTM_GROUP, body, init)


def _mlstm_call(q, k, v, gates, b_igate, b_fgate, norm_gain):
    b, s, _ = q.shape
    nh = MLSTM_HEADS
    dqk, dv = q.shape[-1] // nh, v.shape[-1] // nh
    L = MLSTM_CHUNK
    nch = s // L
    g = jnp.transpose(gates[..., :2 * nh], (0, 2, 1)).reshape(b, 2 * nh, nch, L)
    smem = pl.BlockSpec(memory_space=pltpu.SMEM)
    per_b = lambda n: pl.BlockSpec((1, s, n), lambda i: (i, 0, 0))
    kern = functools.partial(_mlstm_kernel, n_chunks=nch, n_heads=nh, dqk=dqk, dv=dv)
    return pl.pallas_call(
        kern,
        out_shape=jax.ShapeDtypeStruct((b, s, nh * dv), BF16),
        grid=(b,),
        in_specs=[smem, smem, per_b(nh * dqk), per_b(nh * dqk), per_b(nh * dv),
                  pl.BlockSpec((1, 2 * nh, nch, L), lambda i: (i, 0, 0, 0)),
                  pl.BlockSpec((nh, 1, dv), lambda i: (0, 0, 0))],
        out_specs=per_b(nh * dv),
        scratch_shapes=[pltpu.VMEM((nh, nch, L), F32), pltpu.VMEM((nh, nch, L), F32)],
        compiler_params=_cparams(1),
        name="mlstm_chunk_scan",
    )(b_igate, b_fgate, q, k, v, g, norm_gain.reshape(nh, 1, dv))


def _router_kernel(x_ref, gain_ref, sc_ref, sh_ref, wh_ref, wl_ref, rb_ref, h_ref, route_ref, cnt_ref):
    h = _norm_mod(x_ref[0], gain_ref[...], sc_ref[0], sh_ref[0])
    hi = h.astype(BF16)
    lo = (h - hi.astype(F32)).astype(BF16)
    logits = _dot(hi, wh_ref[...]) + (_dot(lo, wh_ref[...]) + _dot(hi, wl_ref[...]))
    h_ref[0] = hi
    tm = logits.shape[0]
    aff = jax.nn.sigmoid(logits.T[:N_EXPERTS])
    choice = (aff + rb_ref[...]).reshape(N_GROUPS, EXPERTS_PER_GROUP, tm)
    local = lax.broadcasted_iota(jnp.int32, choice.shape, 1)

    def first_max(v):
        m = jnp.max(v, axis=1, keepdims=True)
        return m, jnp.min(jnp.where(v == m, local, EXPERTS_PER_GROUP), axis=1, keepdims=True)

    m1, i1 = first_max(choice)
    m2, i2 = first_max(jnp.where(local == i1, -jnp.inf, choice))
    score = m1 + m2
    best, e0, e1 = score[0], i1[0], i2[0]
    for g in range(1, N_GROUPS):
        better = score[g] > best
        best = jnp.where(better, score[g], best)
        e0 = jnp.where(better, i1[g] + g * EXPERTS_PER_GROUP, e0)
        e1 = jnp.where(better, i2[g] + g * EXPERTS_PER_GROUP, e1)
    expert = lax.broadcasted_iota(jnp.int32, (N_EXPERTS, tm), 0)
    is0, is1 = expert == e0, expert == e1
    a0 = jnp.sum(jnp.where(is0, aff, 0.0), axis=0, keepdims=True)
    a1 = jnp.sum(jnp.where(is1, aff, 0.0), axis=0, keepdims=True)
    tot = a0 + a1
    onehot = jnp.where(is0 | is1, 1.0, 0.0).astype(BF16)
    r_i = lax.broadcasted_iota(jnp.int32, (tm, tm), 0)
    c_i = lax.broadcasted_iota(jnp.int32, (tm, tm), 1)
    running = _dot(onehot, jnp.where(r_i <= c_i, 1.0, 0.0).astype(BF16))
    r0 = jnp.sum(jnp.where(is0, running, 0.0), axis=0, keepdims=True) - 1.0
    r1 = jnp.sum(jnp.where(is1, running, 0.0), axis=0, keepdims=True) - 1.0
    rows = [e0.astype(F32), e1.astype(F32), r0, r1, a0 / tot, a1 / tot]
    packed = jnp.concatenate(rows + [jnp.zeros((LANES - len(rows), tm), F32)], axis=0)
    route_ref[0] = packed.T
    cnt_ref[0] = jnp.broadcast_to(running[:, tm - 1:tm], (N_EXPERTS, LANES))


def _router_call(x, gain, sc, sh, router_w, router_b, tm):
    b, s, d = x.shape
    nt = s // tm
    wp = jnp.pad(router_w, ((0, 0), (0, LANES - N_EXPERTS)))
    wh = wp.astype(BF16)
    wl = (wp - wh.astype(F32)).astype(BF16)
    rb = router_b.astype(F32).reshape(N_EXPERTS, 1)
    row = lambda i, j: (i, j, 0)
    per_b = lambda i, j: (i, 0, 0)
    const = lambda i, j: (0, 0)
    return pl.pallas_call(
        _router_kernel,
        out_shape=(jax.ShapeDtypeStruct((b, s, d), BF16), jax.ShapeDtypeStruct((b, s, LANES), F32),
                   jax.ShapeDtypeStruct((b * nt, N_EXPERTS, LANES), F32)),
        grid=(b, nt),
        in_specs=[pl.BlockSpec((1, tm, d), row),
                  pl.BlockSpec((1, d), const),
                  pl.BlockSpec((1, 1, d), per_b),
                  pl.BlockSpec((1, 1, d), per_b),
                  pl.BlockSpec((d, LANES), const),
                  pl.BlockSpec((d, LANES), const),
                  pl.BlockSpec((N_EXPERTS, 1), const)],
        out_specs=(pl.BlockSpec((1, tm, d), row), pl.BlockSpec((1, tm, LANES), row),
                   pl.BlockSpec((1, N_EXPERTS, LANES), lambda i, j: (i * nt + j, 0, 0))),
        compiler_params=_cparams(2),
        name="moe_router",
    )(x, gain.reshape(1, d), sc, sh, wh, wl, rb)


def _expert_kernel(be_ref, live_ref, x_ref, wg_ref, wu_ref, wd_ref, o_ref, wg_s, wu_s, wd_s):
    i = pl.program_id(0)
    fresh = (i == 0) | (be_ref[i] != be_ref[jnp.maximum(i - 1, 0)])

    @pl.when(fresh)
    def _():
        wg_s[...] = wg_ref[0, 0].astype(BF16)
        wu_s[...] = wu_ref[0, 0].astype(BF16)
        wd_s[...] = wd_ref[0, 0].astype(BF16)

    @pl.when(live_ref[i] == 1)
    def _():
        xb = x_ref[...]
        gate = _dot(xb, wg_s[...])
        hid = gate * jax.nn.sigmoid(gate) * _dot(xb, wu_s[...])
        o_ref[...] = _dot(hid.astype(BF16), wd_s[...]).astype(o_ref.dtype)

    @pl.when(live_ref[i] == 0)
    def _():
        o_ref[...] = jnp.zeros_like(o_ref)


def _expert_call(block_expert, block_live, xs, w_gate, w_up, w_down, layer):
    p, d = xs.shape
    de = w_gate.shape[-1]
    nb = p // MOE_ROWS
    grid_spec = pltpu.PrefetchScalarGridSpec(
        num_scalar_prefetch=2,
        grid=(nb,),
        in_specs=[pl.BlockSpec((MOE_ROWS, d), lambda i, be, lv: (i, 0)),
                  pl.BlockSpec((1, 1, d, de), lambda i, be, lv: (layer, be[i], 0, 0)),
                  pl.BlockSpec((1, 1, d, de), lambda i, be, lv: (layer, be[i], 0, 0)),
                  pl.BlockSpec((1, 1, de, d), lambda i, be, lv: (layer, be[i], 0, 0))],
        out_specs=pl.BlockSpec((MOE_ROWS, d), lambda i, be, lv: (i, 0)),
        scratch_shapes=[pltpu.VMEM((d, de), BF16), pltpu.VMEM((d, de), BF16), pltpu.VMEM((de, d), BF16)],
    )
    return pl.pallas_call(
        _expert_kernel,
        out_shape=jax.ShapeDtypeStruct((p, d), BF16),
        grid_spec=grid_spec,
        compiler_params=_cparams(1),
        name="moe_experts",
    )(block_expert, block_live, xs, w_gate, w_up, w_down)


def _combine_kernel(x_ref, g_ref, route_ref, ya_ref, yb_ref, o_ref):
    w = route_ref[0][:, 2 * TOP_K:3 * TOP_K]
    y = w[:, 0:1] * ya_ref[0].astype(F32) + w[:, 1:2] * yb_ref[0].astype(F32)
    o_ref[0] = x_ref[0] + g_ref[0] * y


def _combine_call(x, g, route, ya, yb, tm):
    b, s, d = x.shape
    row = lambda i, j: (i, j, 0)
    spec = pl.BlockSpec((1, tm, d), row)
    return pl.pallas_call(
        _combine_kernel,
        out_shape=jax.ShapeDtypeStruct((b, s, d), F32),
        grid=(b, s // tm),
        in_specs=[spec, pl.BlockSpec((1, 1, d), lambda i, j: (i, 0, 0)),
                  pl.BlockSpec((1, tm, LANES), row), spec, spec],
        out_specs=spec,
        compiler_params=_cparams(2),
        name="moe_combine",
    )(x, g, route, ya, yb)


def _row_layout(ri, cnt, tm):
    t = ri.shape[0]
    nt = t // tm
    lanes = jnp.arange(N_EXPERTS, dtype=jnp.int32)
    tile_cnt = cnt[:, :, 0].astype(jnp.int32)
    tile_off = jnp.cumsum(tile_cnt, axis=0) - tile_cnt
    counts = jnp.sum(tile_cnt, axis=0)
    padded = (counts + MOE_ROWS - 1) // MOE_ROWS * MOE_ROWS
    p_ends = jnp.cumsum(padded)
    base = (p_ends - padded)[None, :] + tile_off
    e = ri[:, 0:TOP_K].reshape(nt, tm, TOP_K)
    rank = ri[:, TOP_K:2 * TOP_K].reshape(nt, tm, TOP_K)
    dest = jnp.sum(jnp.where(e[..., None] == lanes, base[:, None, None, :], 0), axis=-1) + rank
    dest = dest.reshape(t, TOP_K)
    nb = (t * TOP_K) // MOE_ROWS + N_EXPERTS
    tok = jnp.repeat(jnp.arange(t, dtype=jnp.int32), TOP_K)
    buf_tok = (jnp.arange(nb * MOE_ROWS, dtype=jnp.int32) % t).at[dest.reshape(-1)].set(
        tok, mode="promise_in_bounds", unique_indices=True)
    block_start = jnp.arange(nb, dtype=jnp.int32) * MOE_ROWS
    block_expert = jnp.minimum(jnp.sum((p_ends[None, :] <= block_start[:, None]).astype(jnp.int32), axis=-1),
                               N_EXPERTS - 1)
    block_live = (block_start < p_ends[-1]).astype(jnp.int32)
    return dest, buf_tok, block_expert, block_live


def _moe_layer(x, gain, sc, sh, g, router_w, router_b, w_gate, w_up, w_down, layer, tm):
    b, s, d = x.shape
    t = b * s
    hf, route, cnt = _router_call(x, gain, sc, sh, router_w, router_b, tm)
    ri = route.reshape(t, LANES)[:, :2 * TOP_K].astype(jnp.int32)
    dest, buf_tok, block_expert, block_live = _row_layout(ri, cnt, tm)
    take = lambda a, idx: a.at[idx].get(mode="promise_in_bounds")
    xs = take(hf.reshape(t, d), buf_tok)
    out = _expert_call(block_expert, block_live, xs, w_gate, w_up, w_down, layer)
    ya = take(out, dest[:, 0]).reshape(b, s, d)
    yb = take(out, dest[:, 1]).reshape(b, s, d)
    return _combine_call(x, g, route, ya, yb, tm)


def _nsa_layer(x, gain, sc, sh, g, w_in, w_out, q_gain, k_gain, cmp_pe, cmp_w1, cmp_b1, cmp_w2, cmp_b2, tm):
    b, s, d = x.shape
    G, dh = NSA_KV_GROUPS, HEAD_DIM
    cos, sin = _rope_tables(jnp.arange(s, dtype=jnp.int32))
    q, cv, ks, vs, kw, vw, gates = _nsa_in_call(x, gain, sc, sh, w_in, q_gain, k_gain, cos, sin, tm)

    n_cmp = (s - CMP_BLOCK) // CMP_STRIDE + 1
    n_str = s // CMP_STRIDE
    cmp_pos = jnp.arange(n_str, dtype=jnp.int32) * CMP_STRIDE + (CMP_BLOCK - 1)
    ccos, csin = _rope_tables(cmp_pos)
    cmp = _compress_call(cv, cmp_pe, cmp_w1, cmp_b1, cmp_w2, cmp_b2, k_gain[0], ccos, csin)
    cmp = jnp.pad(cmp, ((0, 0), (0, 0), (0, LANES - n_str), (0, 0)))
    kc, vc = cmp[0], cmp[1]

    ns = s // SEL_BLOCK
    r_, u_ = SEL_BLOCK // CMP_STRIDE, CMP_BLOCK // CMP_STRIDE
    c_idx = (r_ * np.arange(ns)[:, None, None] + np.arange(r_)[None, :, None]
             + np.arange(u_)[None, None, :]).reshape(ns, -1)
    c2s = (c_idx[:, :, None] == np.arange(n_cmp)[None, None, :]).sum(1).astype(np.float32)
    c2s = jnp.asarray(np.pad(c2s, ((0, 0), (0, LANES - n_cmp))))

    o = _nsa_attn_call(q, gates, kc, vc, ks, vs, kw, vw, c2s, n_cmp)
    return _out_proj_call(o, None, w_out[_head_pair_order(), :], x, g, tm)


def _mlstm_layer(x, gain, sc, sh, g, w_in, w_out, b_igate, b_fgate, norm_gain, tm):
    nh = MLSTM_HEADS
    dv = norm_gain.shape[-1]
    dqk = (w_in.shape[-1] - 2 * nh - 2 * nh * dv) // (2 * nh)
    q, k, v, og, gates = _mlstm_in_call(x, gain, sc, sh, w_in, dqk, dv, tm)
    hs = _mlstm_call(q, k, v, gates, b_igate, b_fgate, norm_gain)
    return _out_proj_call(hs, og, w_out, x, g, tm)


def kernel(x, c, ada_w, ada_b, norm_mix_gain, norm_ffn_gain, nsa_w_in, nsa_w_out, nsa_q_gain, nsa_k_gain, nsa_cmp_pe, nsa_cmp_w1, nsa_cmp_b1, nsa_cmp_w2, nsa_cmp_b2, mlstm_w_in, mlstm_b_igate, mlstm_b_fgate, mlstm_norm_gain, mlstm_w_out, router_w, router_b, moe_w_gate, moe_w_up, moe_w_down):
    b, s, d = x.shape
    depth = ada_w.shape[0]
    tm = min(512, s)
    mod = _mod_call(c, ada_w, ada_b)
    for i in range(depth):
        sh_m, sc_m, g_m, sh_f, sc_f, g_f = [mod[i, :, None, k * d:(k + 1) * d] for k in range(6)]
        j = i // 2
        if i % 2 == 0:
            x = _nsa_layer(x, norm_mix_gain[i], sc_m, sh_m, g_m, nsa_w_in[j], nsa_w_out[j], nsa_q_gain[j],
                           nsa_k_gain[j], nsa_cmp_pe[j], nsa_cmp_w1[j], nsa_cmp_b1[j], nsa_cmp_w2[j],
                           nsa_cmp_b2[j], tm)
        else:
            x = _mlstm_layer(x, norm_mix_gain[i], sc_m, sh_m, g_m, mlstm_w_in[j], mlstm_w_out[j],
                             mlstm_b_igate[j], mlstm_b_fgate[j], mlstm_norm_gain[j], tm)
        x = _moe_layer(x, norm_ffn_gain[i], sc_f, sh_f, g_f, router_w, router_b,
                       moe_w_gate, moe_w_up, moe_w_down, i, tm)
    return x
```

```python
import functools

import numpy as np
import jax
import jax.numpy as jnp
from jax import lax
from jax.experimental import pallas as pl
from jax.experimental.pallas import tpu as pltpu

F32 = jnp.float32
BF16 = jnp.bfloat16
HIGHEST = lax.Precision.HIGHEST

EPS = 1e-6
NEG = -1e30
BIG = 1e9
ROPE_THETA = 500000.0
LOG2E = 1.4426950408889634

NSA_HEADS = 16
NSA_KV_GROUPS = 2
NSA_HEADS_PER_GROUP = NSA_HEADS // NSA_KV_GROUPS
HEAD_DIM = 64
ROT_DIM = HEAD_DIM // 4
CMP_BLOCK = 32
CMP_STRIDE = 16
SEL_BLOCK = 64
SEL_TOPN = 8
WINDOW = 512
NSA_Q_BLOCK = 64
NSA_BRANCHES = 3
NSA_CALL_KEYS = 512
NSA_BLOCKS_PER_STEP = 2

MLSTM_HEADS = 4
MLSTM_CHUNK = 256
MLSTM_GROUP = 1
GATE_SOFTCAP = 15.0

N_EXPERTS = 32
N_GROUPS = 4
EXPERTS_PER_GROUP = N_EXPERTS // N_GROUPS
TOP_K = 2
MOE_ROWS = 512

LANES = 128
VMEM_LIMIT = 48 * 1024 * 1024


def _cparams(n_axes):
    return pltpu.CompilerParams(dimension_semantics=("arbitrary",) * n_axes,
                                vmem_limit_bytes=VMEM_LIMIT)


def _dot(a, b):
    return jnp.dot(a, b, preferred_element_type=F32)


def _dot_nt(a, b):
    return lax.dot_general(a, b, (((1,), (1,)), ((), ())), preferred_element_type=F32)


def _dot_tn(a, b):
    return lax.dot_general(a, b, (((0,), (0,)), ((), ())), preferred_element_type=F32)


def _norm_mod(x, gain, sc, sh):
    y = x * lax.rsqrt(jnp.mean(x * x, axis=-1, keepdims=True) + EPS) * gain
    return y * (1.0 + sc) + sh


def _half_norm_rope(x, gain, cos, sin):
    lane = lax.broadcasted_iota(jnp.int32, x.shape, x.ndim - 1)
    x2 = x * x
    left = lane < HEAD_DIM
    ss_l = jnp.sum(jnp.where(left, x2, 0.0), axis=-1, keepdims=True)
    ss_r = jnp.sum(jnp.where(left, 0.0, x2), axis=-1, keepdims=True)
    ms = jnp.where(left, ss_l, ss_r) * (1.0 / HEAD_DIM)
    y = x * lax.rsqrt(ms + EPS) * gain
    half = ROT_DIM // 2
    src = lax.broadcasted_iota(jnp.int32, (LANES, LANES), 0)
    dst = lax.broadcasted_iota(jnp.int32, (LANES, LANES), 1)
    dst_in_head = dst % HEAD_DIM
    pair = jnp.where(dst_in_head < half, dst + half, jnp.where(dst_in_head < ROT_DIM, dst - half, -1))
    partner = _dot(y.astype(BF16), jnp.where(src == pair, 1.0, 0.0).astype(BF16))
    return y * cos + partner * sin


def _rope_tables(pos):
    half = ROT_DIM // 2
    inv_freq = ROPE_THETA ** (-jnp.arange(half, dtype=F32) / half)
    ang = pos.astype(F32)[:, None] * inv_freq[None, :]
    cos, sin = jnp.cos(ang), jnp.sin(ang)
    n = pos.shape[0]
    one = jnp.ones((n, HEAD_DIM - ROT_DIM), F32)
    cos_h = jnp.concatenate([cos, cos, one], axis=-1)
    sin_h = jnp.concatenate([-sin, sin, 0.0 * one], axis=-1)
    return jnp.tile(cos_h, (1, 2)), jnp.tile(sin_h, (1, 2))


def _mod_kernel(c_ref, w_ref, b_ref, o_ref):
    c = c_ref[...]
    cond = c * jax.nn.sigmoid(c)
    o_ref[0] = jnp.dot(cond, w_ref[0], preferred_element_type=F32, precision=HIGHEST) + b_ref[0]


def _mod_call(c, ada_w, ada_b):
    depth, d, n = ada_w.shape
    b = c.shape[0]
    tn = n // 4
    return pl.pallas_call(
        _mod_kernel,
        out_shape=jax.ShapeDtypeStruct((depth, b, n), F32),
        grid=(depth, n // tn),
        in_specs=[pl.BlockSpec((b, d), lambda i, j: (0, 0)),
                  pl.BlockSpec((1, d, tn), lambda i, j: (i, 0, j)),
                  pl.BlockSpec((1, 1, tn), lambda i, j: (i, 0, j))],
        out_specs=pl.BlockSpec((1, b, tn), lambda i, j: (i, 0, j)),
        compiler_params=_cparams(2),
        name="adaln_mod",
    )(c, ada_w, ada_b.reshape(depth, 1, n))


def _nsa_in_kernel(x_ref, gain_ref, sc_ref, sh_ref, wq_ref, wkv_ref, wg_ref, qg_ref, kg_ref, cos_ref, sin_ref,
                   q_ref, cv_ref, ks_ref, vs_ref, kw_ref, vw_ref, g_ref):
    h = _norm_mod(x_ref[0], gain_ref[...], sc_ref[0], sh_ref[0]).astype(BF16)
    g_ref[0] = _dot(h, wg_ref[...])
    kv = _dot(h, wkv_ref[...])
    cos, sin = cos_ref[...], sin_ref[...]
    q = _dot(h, wq_ref[...])
    for r in range(NSA_HEADS_PER_GROUP):
        slab = _half_norm_rope(q[:, r * LANES:(r + 1) * LANES], qg_ref[...], cos, sin)
        q_ref[0, :, r * LANES:(r + 1) * LANES] = (slab * (HEAD_DIM ** -0.5 * LOG2E)).astype(BF16)
    cv_ref[0] = kv[:, 0:2 * LANES]
    ks_ref[0] = _half_norm_rope(kv[:, 2 * LANES:3 * LANES], kg_ref[1:2, :], cos, sin).astype(BF16)
    vs_ref[0] = kv[:, 3 * LANES:4 * LANES].astype(BF16)
    kw_ref[0] = _half_norm_rope(kv[:, 4 * LANES:5 * LANES], kg_ref[2:3, :], cos, sin).astype(BF16)
    vw_ref[0] = kv[:, 5 * LANES:6 * LANES].astype(BF16)


def _head_pair_order():
    r, g, dd = np.meshgrid(np.arange(NSA_HEADS_PER_GROUP), np.arange(NSA_KV_GROUPS), np.arange(HEAD_DIM),
                           indexing="ij")
    return ((g * NSA_HEADS_PER_GROUP + r) * HEAD_DIM + dd).reshape(-1)


def _nsa_in_call(x, gain, sc, sh, w_in, q_gain, k_gain, cos, sin, tm):
    b, s, d = x.shape
    nq = NSA_HEADS * HEAD_DIM
    nkv = 6 * LANES
    wq = w_in[:, :nq][:, _head_pair_order()].astype(BF16)
    qg = jnp.tile(q_gain, 2).reshape(1, LANES)
    wkv = w_in[:, nq:nq + nkv].astype(BF16)
    ng = NSA_BRANCHES * NSA_HEADS
    wg = jnp.pad(w_in[:, nq + nkv:], ((0, 0), (0, LANES - ng))).astype(BF16)
    kg = jnp.tile(k_gain, (1, 2))
    row = lambda i, j: (i, j, 0)
    per_b = lambda i, j: (i, 0, 0)
    const = lambda i, j: (0, 0)
    kv_out = lambda dt: jax.ShapeDtypeStruct((b, s, LANES), dt)
    return pl.pallas_call(
        _nsa_in_kernel,
        out_shape=(jax.ShapeDtypeStruct((b, s, nq), BF16), jax.ShapeDtypeStruct((b, s, 2 * LANES), F32),
                   kv_out(BF16), kv_out(BF16), kv_out(BF16), kv_out(BF16), kv_out(F32)),
        grid=(b, s // tm),
        in_specs=[pl.BlockSpec((1, tm, d), row),
                  pl.BlockSpec((1, d), const),
                  pl.BlockSpec((1, 1, d), per_b),
                  pl.BlockSpec((1, 1, d), per_b),
                  pl.BlockSpec((d, nq), const),
                  pl.BlockSpec((d, nkv), const),
                  pl.BlockSpec((d, LANES), const),
                  pl.BlockSpec((1, LANES), const),
                  pl.BlockSpec((3, LANES), const),
                  pl.BlockSpec((tm, LANES), lambda i, j: (j, 0)),
                  pl.BlockSpec((tm, LANES), lambda i, j: (j, 0))],
        out_specs=(pl.BlockSpec((1, tm, nq), row), pl.BlockSpec((1, tm, 2 * LANES), row))
        + (pl.BlockSpec((1, tm, LANES), row),) * 5,
        compiler_params=_cparams(2),
        name="nsa_in_proj",
    )(x, gain.reshape(1, d), sc, sh, wq, wkv, wg, qg, kg, cos, sin)


def _compress_kernel(a_ref, pe_ref, w1_ref, b1_ref, w2_ref, b2_ref, kg_ref, cos_ref, sin_ref, o_ref, *, n_str):
    is_key = pl.program_id(0) == 0
    hid2 = w1_ref.shape[-1]
    first = jnp.zeros((n_str, hid2), F32)
    second = jnp.zeros((n_str, hid2), F32)
    pe_term = jnp.zeros((8, hid2), F32)
    for l in range(CMP_STRIDE):
        rows = a_ref[0, pl.ds(l, n_str, stride=CMP_STRIDE), :].astype(BF16)
        first = first + _dot(rows, w1_ref[0, l])
        second = second + _dot(rows, w1_ref[0, CMP_STRIDE + l])
    for l in range(CMP_BLOCK):
        pe_term = pe_term + _dot(pe_ref[0, l].astype(BF16), w1_ref[0, l])
    hid = first + pltpu.roll(second, n_str - 1, 0) + pe_term[0:1] + b1_ref[0]
    hid = 0.5 * hid * (1.0 + jnp.tanh(np.sqrt(2.0 / np.pi) * (hid + 0.044715 * hid * hid * hid)))
    out = _dot(hid.astype(BF16), w2_ref[0]) + b2_ref[0]
    normed = _half_norm_rope(out, kg_ref[...], cos_ref[...], sin_ref[...])
    o_ref[0, 0] = jnp.where(is_key, normed, out).astype(o_ref.dtype)


def _block_diag2(w):
    z = jnp.zeros_like(w)
    return jnp.concatenate([jnp.concatenate([w, z], axis=-1), jnp.concatenate([z, w], axis=-1)], axis=-2)


def _compress_call(cv, pe, w1, b1, w2, b2, k_gain0, cos, sin):
    b, s, _ = cv.shape
    n_str = s // CMP_STRIDE
    hid = w1.shape[-1]
    w1bd = _block_diag2(w1.reshape(2, CMP_BLOCK, HEAD_DIM, hid)).astype(BF16)
    w2bd = _block_diag2(w2).astype(BF16)
    pe2 = jnp.broadcast_to(jnp.tile(pe, (1, 1, 2))[:, :, None, :], (2, CMP_BLOCK, 8, LANES))
    b1t = jnp.tile(b1, (1, 2)).reshape(2, 1, 2 * hid)
    b2t = jnp.tile(b2, (1, 2)).reshape(2, 1, LANES)
    kg = jnp.tile(k_gain0, 2).reshape(1, LANES)
    sel3 = lambda i, j: (i, 0, 0)
    sel4 = lambda i, j: (i, 0, 0, 0)
    const = lambda i, j: (0, 0)
    return pl.pallas_call(
        functools.partial(_compress_kernel, n_str=n_str),
        out_shape=jax.ShapeDtypeStruct((2, b, n_str, LANES), BF16),
        grid=(2, b),
        in_specs=[pl.BlockSpec((1, s, LANES), lambda i, j: (j, 0, i)),
                  pl.BlockSpec((1, CMP_BLOCK, 8, LANES), sel4),
                  pl.BlockSpec((1, CMP_BLOCK, LANES, 2 * hid), sel4),
                  pl.BlockSpec((1, 1, 2 * hid), sel3),
                  pl.BlockSpec((1, 2 * hid, LANES), sel3),
                  pl.BlockSpec((1, 1, LANES), sel3),
                  pl.BlockSpec((1, LANES), const),
                  pl.BlockSpec((n_str, LANES), const),
                  pl.BlockSpec((n_str, LANES), const)],
        out_specs=pl.BlockSpec((1, 1, n_str, LANES), lambda i, j: (i, j, 0, 0)),
        compiler_params=_cparams(2),
        name="nsa_compress",
    )(cv, pe2, w1bd, b1t, w2bd, b2t, kg, cos, sin)


def _attend(qb, k_ref, v_ref, k0, spans, g, bias):
    rows = qb.shape[0]
    m = acc = None
    for off, size in spans:
        k = k_ref[0, pl.ds(k0 + off, size), :]
        v = v_ref[0, pl.ds(k0 + off, size), :]
        s = _dot_nt(qb, k).reshape(rows // NSA_Q_BLOCK, NSA_Q_BLOCK, size) + bias[None, :, off:off + size]
        s = s.reshape(rows, size)
        m_span = jnp.max(s, axis=-1, keepdims=True)
        m_new = m_span if m is None else jnp.maximum(m, m_span)
        p = jnp.exp2(s - m_new).astype(BF16)
        v_lane = lax.broadcasted_iota(jnp.int32, v.shape, 1)
        pv = _dot(p, jnp.where((v_lane // HEAD_DIM) == g, v, jnp.ones_like(v)))
        acc = pv if m is None else acc * jnp.exp2(m - m_new) + pv
        m = m_new
    return acc


def _nsa_attn_kernel(*refs, blocks_per_step, n_inputs, **static):
    for sub in range(blocks_per_step):
        _nsa_attn_block(sub, pl.program_id(1) * blocks_per_step + sub, *refs[:n_inputs], refs[-1], **static)


def _nsa_attn_block(sub, q_block, q_ref, g_ref, kc_ref, vc_ref, ks_ref, vs_ref, kw_ref, vw_ref,
                    c2s_ref, exp_ref, o_ref, *, seq, n_cmp, n_top, win_keys, sel_span, n_keys, q_block0):
    R, QB = NSA_HEADS_PER_GROUP, NSA_Q_BLOCK
    rows = R * QB
    n_sel = seq // SEL_BLOCK
    qi = q_block + q_block0
    s0 = qi * QB
    tok = slice(sub * QB, (sub + 1) * QB)
    gt = jax.nn.sigmoid(g_ref[0, tok, :])
    lane = lax.broadcasted_iota(jnp.int32, (QB, LANES), 1)
    tq = s0 + lax.broadcasted_iota(jnp.int32, (rows, 1), 0) % QB
    tq1 = s0 + lax.broadcasted_iota(jnp.int32, (QB, 1), 0)
    ones_sq = jnp.ones((LANES, LANES), BF16)

    w0 = pl.multiple_of(jnp.maximum(s0 + QB - win_keys, 0), SEL_BLOCK)
    wpos = w0 + lax.broadcasted_iota(jnp.int32, (1, win_keys), 1)
    bias_w = jnp.where((wpos <= tq1) & (wpos > tq1 - WINDOW), 0.0, NEG)
    win_spans = [(off, min(3 * LANES, win_keys - off)) for off in range(0, win_keys, 3 * LANES)]

    qbs, psums, o_cs, acc_ws = [], [], [], []
    for g in range(NSA_KV_GROUPS):
        in_g = (lane // HEAD_DIM) == g
        zero = jnp.zeros((QB, LANES), BF16)
        qb = jnp.concatenate([jnp.where(in_g, q_ref[0, tok, r * LANES:(r + 1) * LANES], zero) for r in range(R)],
                             axis=0)

        sc = _dot_nt(qb, kc_ref[0])
        cpos = lax.broadcasted_iota(jnp.int32, (1, LANES), 1) * CMP_STRIDE + (CMP_BLOCK - 1)
        valid_c = (cpos <= tq) & (lax.broadcasted_iota(jnp.int32, (1, LANES), 1) < n_cmp)
        sc = jnp.where(valid_c, sc, NEG)
        e_c = jnp.exp2(sc - jnp.max(sc, axis=-1, keepdims=True)).astype(BF16)
        p_c = jnp.where(valid_c, e_c.astype(F32) / _dot(e_c, ones_sq), 0.0)
        o_cs.append(_dot(p_c.astype(BF16), vc_ref[0]))

        psums.append(jnp.sum(p_c.reshape(R, QB, LANES), axis=0))

        acc_ws.append(_attend(qb, kw_ref, vw_ref, w0, win_spans, g, bias_w))
        qbs.append(qb)

    imp = lax.dot_general(c2s_ref[...], jnp.concatenate(psums, axis=0), (((1,), (1,)), ((), ())),
                          preferred_element_type=F32, precision=HIGHEST)
    blk = lax.broadcasted_iota(jnp.int32, (n_sel, NSA_KV_GROUPS * QB), 0)
    forced = (blk == 0) | (blk == qi) | (blk == qi - 1)
    imp = jnp.where(blk <= qi, jnp.where(forced, BIG, imp), -BIG)
    beaten = jnp.zeros(imp.shape, F32)
    for k in range(1, n_sel):
        other = pltpu.roll(imp, k, 0)
        beats = (other > imp) | ((blk >= k) & (other == imp))
        beaten = beaten + jnp.where(beats, 1.0, 0.0)
    chosen = jnp.where(beaten < n_top, 1.0, 0.0).astype(BF16)
    picked = _dot_tn(chosen, exp_ref[:, :n_keys])
    kpos = lax.broadcasted_iota(jnp.int32, (1, n_keys), 1)
    sel_spans = [(off, min(sel_span, n_keys - off)) for off in range(0, n_keys, sel_span)]

    heads = []
    for g in range(NSA_KV_GROUPS):
        bias_s = jnp.where((picked[g * QB:(g + 1) * QB] > 0.5) & (kpos <= tq1), 0.0, NEG)
        acc_s = _attend(qbs[g], ks_ref, vs_ref, 0, sel_spans, g, bias_s)
        o_c, acc_w = o_cs[g], acc_ws[g]
        c_sum = (1 - g) * HEAD_DIM
        per_head = []
        for r in range(R):
            h = g * R + r
            rs = slice(r * QB, (r + 1) * QB)
            a_s, a_w = acc_s[rs], acc_w[rs]
            g_s = gt[:, NSA_HEADS + h:NSA_HEADS + h + 1] / a_s[:, c_sum:c_sum + 1]
            g_w = gt[:, 2 * NSA_HEADS + h:2 * NSA_HEADS + h + 1] / a_w[:, c_sum:c_sum + 1]
            per_head.append(gt[:, h:h + 1] * o_c[rs] + g_s * a_s + g_w * a_w)
        heads.append(per_head)
    for r in range(R):
        slab = jnp.where(lane < HEAD_DIM, heads[0][r], heads[1][r])
        o_ref[0, tok, r * LANES:(r + 1) * LANES] = slab.astype(o_ref.dtype)


def _nsa_attn_call(q, gates, kc, vc, ks, vs, kw, vw, cmp_to_sel, n_cmp):
    b, s, nq = q.shape
    qb = NSA_Q_BLOCK
    n_top = min(SEL_TOPN, s // SEL_BLOCK)
    win_keys = min(WINDOW + 2 * qb, s)
    sel_span = min(512, s)
    n_sel = s // SEL_BLOCK
    expand = jnp.asarray(np.arange(n_sel)[:, None] == (np.arange(s)[None, :] // SEL_BLOCK), BF16)
    per_b = lambda i, j: (i, 0, 0)
    const = lambda i, j: (0, 0)
    call_keys = min(NSA_CALL_KEYS, s)
    per_call = call_keys // qb
    per_step = min(NSA_BLOCKS_PER_STEP, per_call)
    qrows = per_step * qb
    out = None
    for n in range(1, s // call_keys + 1):
        q0 = (n - 1) * per_call
        n_keys = n * call_keys
        row = lambda i, j, t0=q0 // per_step: (i, j + t0, 0)
        kern = functools.partial(_nsa_attn_kernel, blocks_per_step=per_step, n_inputs=10, seq=s, n_cmp=n_cmp,
                                 n_top=n_top, win_keys=win_keys, sel_span=sel_span, n_keys=n_keys, q_block0=q0)
        carried = [] if out is None else [out]
        out = pl.pallas_call(
            kern,
            out_shape=jax.ShapeDtypeStruct((b, s, nq), BF16),
            grid=(b, per_call // per_step),
            in_specs=[pl.BlockSpec((1, qrows, nq), row),
                      pl.BlockSpec((1, qrows, LANES), row),
                      pl.BlockSpec((1, LANES, LANES), per_b),
                      pl.BlockSpec((1, LANES, LANES), per_b),
                      pl.BlockSpec((1, n_keys, LANES), per_b),
                      pl.BlockSpec((1, n_keys, LANES), per_b),
                      pl.BlockSpec((1, s, LANES), per_b),
                      pl.BlockSpec((1, s, LANES), per_b),
                      pl.BlockSpec((n_sel, LANES), const),
                      pl.BlockSpec((n_sel, s), const)] + [pl.BlockSpec(memory_space=pl.ANY)] * len(carried),
            out_specs=pl.BlockSpec((1, qrows, nq), row),
            input_output_aliases={10: 0} if carried else {},
            compiler_params=_cparams(2),
            name="nsa_attention",
        )(q, gates, kc, vc, ks, vs, kw, vw, cmp_to_sel, expand, *carried)
    return out


def _mixer_out_kernel(*refs, gated):
    a_ref, refs = refs[0], refs[1:]
    lhs = a_ref[0]
    if gated:
        og_ref, refs = refs[0], refs[1:]
        lhs = (jax.nn.sigmoid(og_ref[0].astype(F32)) * lhs.astype(F32)).astype(BF16)
    w_ref, x_ref, g_ref, gain_ref, sc_ref, sh_ref, wh_ref, wl_ref, rb_ref, xo_ref, h_ref, route_ref, cnt_ref = refs
    x_new = x_ref[0] + g_ref[0] * _dot(lhs, w_ref[...])
    xo_ref[0] = x_new
    h_ref[0], route_ref[0], cnt_ref[0] = _route_tile(x_new, gain_ref[...], sc_ref[0], sh_ref[0],
                                                     wh_ref[...], wl_ref[...], rb_ref[...])


def _mixer_out_call(a, og, w_out, x, g, ffn, tm):
    gain, sc, sh, router_w, router_b = ffn
    b, s, d = x.shape
    k = a.shape[-1]
    nt = s // tm
    wp = jnp.pad(router_w, ((0, 0), (0, LANES - N_EXPERTS)))
    wh = wp.astype(BF16)
    wl = (wp - wh.astype(F32)).astype(BF16)
    rb = router_b.astype(F32).reshape(N_EXPERTS, 1)
    row = lambda i, j: (i, j, 0)
    per_b = lambda i, j: (i, 0, 0)
    const = lambda i, j: (0, 0)
    a_spec = pl.BlockSpec((1, tm, k), row)
    acts = [a] if og is None else [a, og]
    return pl.pallas_call(
        functools.partial(_mixer_out_kernel, gated=og is not None),
        out_shape=(jax.ShapeDtypeStruct((b, s, d), F32), jax.ShapeDtypeStruct((b, s, d), BF16),
                   jax.ShapeDtypeStruct((b, s, LANES), F32),
                   jax.ShapeDtypeStruct((b * nt, N_EXPERTS, LANES), F32)),
        grid=(b, nt),
        in_specs=[a_spec] * len(acts) + [pl.BlockSpec((k, d), const),
                                         pl.BlockSpec((1, tm, d), row),
                                         pl.BlockSpec((1, 1, d), per_b),
                                         pl.BlockSpec((1, d), const),
                                         pl.BlockSpec((1, 1, d), per_b),
                                         pl.BlockSpec((1, 1, d), per_b),
                                         pl.BlockSpec((d, LANES), const),
                                         pl.BlockSpec((d, LANES), const),
                                         pl.BlockSpec((N_EXPERTS, 1), const)],
        out_specs=(pl.BlockSpec((1, tm, d), row), pl.BlockSpec((1, tm, d), row), pl.BlockSpec((1, tm, LANES), row),
                   pl.BlockSpec((1, N_EXPERTS, LANES), lambda i, j: (i * nt + j, 0, 0))),
        compiler_params=_cparams(2),
        name="mixer_out_router",
    )(*acts, w_out.astype(BF16), x, g, gain.reshape(1, d), sc, sh, wh, wl, rb)


def _mlstm_in_kernel(x_ref, gain_ref, sc_ref, sh_ref, wq_ref, wk_ref, wv_ref, wo_ref, wg_ref,
                     q_ref, k_ref, v_ref, o_ref, g_ref):
    h = _norm_mod(x_ref[0], gain_ref[...], sc_ref[0], sh_ref[0]).astype(BF16)
    q_ref[0] = _dot(h, wq_ref[...]).astype(BF16)
    k_ref[0] = _dot(h, wk_ref[...]).astype(BF16)
    v_ref[0] = _dot(h, wv_ref[...]).astype(BF16)
    o_ref[0] = _dot(h, wo_ref[...]).astype(BF16)
    g_ref[0] = _dot(h, wg_ref[...])


def _mlstm_in_call(x, gain, sc, sh, w_in, dqk, dv, tm):
    b, s, d = x.shape
    nh = MLSTM_HEADS
    sizes = [nh * dqk, nh * dqk, nh * dv, nh * dv]
    offs = np.cumsum([0] + sizes)
    ws = [w_in[:, offs[i]:offs[i + 1]].astype(BF16) for i in range(4)]
    wg = jnp.pad(w_in[:, offs[4]:], ((0, 0), (0, LANES - 2 * nh))).astype(BF16)
    row = lambda i, j: (i, j, 0)
    per_b = lambda i, j: (i, 0, 0)
    const = lambda i, j: (0, 0)
    widths = sizes + [LANES]
    return pl.pallas_call(
        _mlstm_in_kernel,
        out_shape=tuple(jax.ShapeDtypeStruct((b, s, n), BF16) for n in sizes)
        + (jax.ShapeDtypeStruct((b, s, LANES), F32),),
        grid=(b, s // tm),
        in_specs=[pl.BlockSpec((1, tm, d), row),
                  pl.BlockSpec((1, d), const),
                  pl.BlockSpec((1, 1, d), per_b),
                  pl.BlockSpec((1, 1, d), per_b)] + [pl.BlockSpec((d, n), const) for n in widths],
        out_specs=tuple(pl.BlockSpec((1, tm, n), row) for n in widths),
        compiler_params=_cparams(2),
        name="mlstm_in_proj",
    )(x, gain.reshape(1, d), sc, sh, *ws, wg)


def _softcap(a):
    return GATE_SOFTCAP * jnp.tanh(a / GATE_SOFTCAP)


def _mlstm_kernel(bi_ref, bf_ref, q_ref, k_ref, v_ref, g_ref, gain_ref, o_ref,
                  li_s, b_s, *, n_chunks, n_heads, dqk, dv):
    L = MLSTM_CHUNK
    r_i = lax.broadcasted_iota(jnp.int32, (L, L), 0)
    c_i = lax.broadcasted_iota(jnp.int32, (L, L), 1)
    upper = jnp.where(r_i <= c_i, 1.0, 0.0)
    for h in range(n_heads):
        li_s[h] = _softcap(g_ref[0, h] + bi_ref[h])
        fa = _softcap(g_ref[0, n_heads + h] + bf_ref[h])
        lf = jnp.minimum(fa, 0.0) - jnp.log1p(jnp.exp(-jnp.abs(fa)))
        b_s[h] = jnp.dot(lf, upper, preferred_element_type=F32, precision=HIGHEST)
    eye = r_i == c_i
    causal = r_i >= c_i
    k_scale = dqk ** -0.5

    def to_col(row):
        return jnp.sum(jnp.where(eye, jnp.broadcast_to(row, (L, L)), 0.0), axis=1, keepdims=True)

    def local_part(h, r0, c):
        qcb = q_ref[0, pl.ds(r0, L), h * dqk:(h + 1) * dqk]
        kc = k_ref[0, pl.ds(r0, L), h * dqk:(h + 1) * dqk].astype(F32) * k_scale
        vc = v_ref[0, pl.ds(r0, L), h * dv:(h + 1) * dv]
        b_row = b_s[h, pl.ds(c, 1), :]
        li_row = li_s[h, pl.ds(c, 1), :]
        b_col, li_col = to_col(b_row), to_col(li_row)
        b_last = b_row[:, L - 1:L]
        dmat = jnp.where(causal, b_col - b_row + li_row, NEG)
        m_loc = jnp.max(dmat, axis=-1, keepdims=True)
        a_loc = jnp.exp(dmat - m_loc) * _dot_nt(qcb, kc.astype(BF16))
        num_loc = _dot(a_loc.astype(BF16), vc)
        den_loc = jnp.sum(a_loc, axis=-1, keepdims=True)
        g_max = m_loc[L - 1:L, :]
        kw = kc * jnp.exp(b_last - b_col + li_col - g_max)
        kv = _dot_tn(kw.astype(BF16), vc)
        kn = jnp.sum(kw, axis=0, keepdims=True)
        return qcb, b_col, b_last, m_loc, num_loc, den_loc, g_max, kv, kn

    def body(grp, carry):
        r0 = pl.multiple_of(grp * (MLSTM_GROUP * L), MLSTM_GROUP * L)
        parts = [[local_part(h, r0 + j * L, grp * MLSTM_GROUP + j) for j in range(MLSTM_GROUP)]
                 for h in range(n_heads)]
        new_carry, outs = [], []
        for h in range(n_heads):
            state, n_row, m_prev = carry[h]
            gain = gain_ref[h]
            head_out = []
            for qcb, b_col, b_last, m_loc, num_loc, den_loc, g_max, kv, kn in parts[h]:
                m_inter = b_col + m_prev
                m_t = jnp.maximum(m_inter, m_loc)
                intra = jnp.exp(m_loc - m_t)
                inter = jnp.exp(m_inter - m_t)
                num = intra * num_loc + inter * _dot(qcb, state.astype(BF16))
                den = intra * den_loc + inter * jnp.sum(qcb.astype(F32) * n_row, axis=-1, keepdims=True)
                h_out = num / jnp.maximum(jnp.abs(den), jnp.exp(-m_t))
                hs = h_out * lax.rsqrt(jnp.mean(h_out * h_out, axis=-1, keepdims=True) + EPS) * gain
                head_out.append(hs.astype(o_ref.dtype))
                m_new = jnp.maximum(b_last + m_prev, g_max)
                decay = jnp.exp(b_last + m_prev - m_new)
                grow = jnp.exp(g_max - m_new)
                state, n_row, m_prev = decay * state + grow * kv, decay * n_row + grow * kn, m_new
            new_carry.append((state, n_row, m_prev))
            outs.append(jnp.concatenate(head_out, axis=0))
        for h in range(n_heads):
            o_ref[0, pl.ds(r0, MLSTM_GROUP * L), h * dv:(h + 1) * dv] = outs[h]
        return tuple(new_carry)

    init = tuple((jnp.zeros((dqk, dv), F32), jnp.zeros((1, dqk), F32), jnp.zeros((1, 1), F32))
                 for _ in range(n_heads))
    lax.fori_loop(0, n_chunks // MLSTM_GROUP, body, init)


def _mlstm_call(q, k, v, gates, b_igate, b_fgate, norm_gain):
    b, s, _ = q.shape
    nh = MLSTM_HEADS
    dqk, dv = q.shape[-1] // nh, v.shape[-1] // nh
    L = MLSTM_CHUNK
    nch = s // L
    g = jnp.transpose(gates[..., :2 * nh], (0, 2, 1)).reshape(b, 2 * nh, nch, L)
    smem = pl.BlockSpec(memory_space=pltpu.SMEM)
    per_b = lambda n: pl.BlockSpec((1, s, n), lambda i: (i, 0, 0))
    kern = functools.partial(_mlstm_kernel, n_chunks=nch, n_heads=nh, dqk=dqk, dv=dv)
    return pl.pallas_call(
        kern,
        out_shape=jax.ShapeDtypeStruct((b, s, nh * dv), BF16),
        grid=(b,),
        in_specs=[smem, smem, per_b(nh * dqk), per_b(nh * dqk), per_b(nh * dv),
                  pl.BlockSpec((1, 2 * nh, nch, L), lambda i: (i, 0, 0, 0)),
                  pl.BlockSpec((nh, 1, dv), lambda i: (0, 0, 0))],
        out_specs=per_b(nh * dv),
        scratch_shapes=[pltpu.VMEM((nh, nch, L), F32), pltpu.VMEM((nh, nch, L), F32)],
        compiler_params=_cparams(1),
        name="mlstm_chunk_scan",
    )(b_igate, b_fgate, q, k, v, g, norm_gain.reshape(nh, 1, dv))


def _route_tile(x, gain, sc, sh, w_hi, w_lo, router_bias):
    h = _norm_mod(x, gain, sc, sh)
    hi = h.astype(BF16)
    lo = (h - hi.astype(F32)).astype(BF16)
    logits = _dot(hi, w_hi) + (_dot(lo, w_hi) + _dot(hi, w_lo))
    tm = logits.shape[0]
    aff = jax.nn.sigmoid(logits.T[:N_EXPERTS])
    choice = (aff + router_bias).reshape(N_GROUPS, EXPERTS_PER_GROUP, tm)
    local = lax.broadcasted_iota(jnp.int32, choice.shape, 1)

    def first_max(v):
        m = jnp.max(v, axis=1, keepdims=True)
        return m, jnp.min(jnp.where(v == m, local, EXPERTS_PER_GROUP), axis=1, keepdims=True)

    m1, i1 = first_max(choice)
    m2, i2 = first_max(jnp.where(local == i1, -jnp.inf, choice))
    score = m1 + m2
    best, e0, e1 = score[0], i1[0], i2[0]
    for g in range(1, N_GROUPS):
        better = score[g] > best
        best = jnp.where(better, score[g], best)
        e0 = jnp.where(better, i1[g] + g * EXPERTS_PER_GROUP, e0)
        e1 = jnp.where(better, i2[g] + g * EXPERTS_PER_GROUP, e1)
    expert = lax.broadcasted_iota(jnp.int32, (N_EXPERTS, tm), 0)
    is0, is1 = expert == e0, expert == e1
    a0 = jnp.sum(jnp.where(is0, aff, 0.0), axis=0, keepdims=True)
    a1 = jnp.sum(jnp.where(is1, aff, 0.0), axis=0, keepdims=True)
    tot = a0 + a1
    onehot = jnp.where(is0 | is1, 1.0, 0.0).astype(BF16)
    r_i = lax.broadcasted_iota(jnp.int32, (tm, tm), 0)
    c_i = lax.broadcasted_iota(jnp.int32, (tm, tm), 1)
    running = _dot(onehot, jnp.where(r_i <= c_i, 1.0, 0.0).astype(BF16))
    r0 = jnp.sum(jnp.where(is0, running, 0.0), axis=0, keepdims=True) - 1.0
    r1 = jnp.sum(jnp.where(is1, running, 0.0), axis=0, keepdims=True) - 1.0
    rows = [e0.astype(F32), e1.astype(F32), r0, r1, a0 / tot, a1 / tot]
    packed = jnp.concatenate(rows + [jnp.zeros((LANES - len(rows), tm), F32)], axis=0)
    return hi, packed.T, jnp.broadcast_to(running[:, tm - 1:tm], (N_EXPERTS, LANES))


def _expert_kernel(be_ref, live_ref, x_ref, wg_ref, wu_ref, wd_ref, o_ref, wg_s, wu_s, wd_s):
    i = pl.program_id(0)
    fresh = (i == 0) | (be_ref[i] != be_ref[jnp.maximum(i - 1, 0)])

    @pl.when(fresh)
    def _():
        wg_s[...] = wg_ref[0, 0].astype(BF16)
        wu_s[...] = wu_ref[0, 0].astype(BF16)
        wd_s[...] = wd_ref[0, 0].astype(BF16)

    @pl.when(live_ref[i] == 1)
    def _():
        xb = x_ref[...]
        gate = _dot(xb, wg_s[...])
        hid = gate * jax.nn.sigmoid(gate) * _dot(xb, wu_s[...])
        o_ref[...] = _dot(hid.astype(BF16), wd_s[...]).astype(o_ref.dtype)

    @pl.when(live_ref[i] == 0)
    def _():
        o_ref[...] = jnp.zeros_like(o_ref)


def _expert_call(block_expert, block_live, xs, w_gate, w_up, w_down, layer):
    p, d = xs.shape
    de = w_gate.shape[-1]
    nb = p // MOE_ROWS
    grid_spec = pltpu.PrefetchScalarGridSpec(
        num_scalar_prefetch=2,
        grid=(nb,),
        in_specs=[pl.BlockSpec((MOE_ROWS, d), lambda i, be, lv: (i, 0)),
                  pl.BlockSpec((1, 1, d, de), lambda i, be, lv: (layer, be[i], 0, 0)),
                  pl.BlockSpec((1, 1, d, de), lambda i, be, lv: (layer, be[i], 0, 0)),
                  pl.BlockSpec((1, 1, de, d), lambda i, be, lv: (layer, be[i], 0, 0))],
        out_specs=pl.BlockSpec((MOE_ROWS, d), lambda i, be, lv: (i, 0)),
        scratch_shapes=[pltpu.VMEM((d, de), BF16), pltpu.VMEM((d, de), BF16), pltpu.VMEM((de, d), BF16)],
    )
    return pl.pallas_call(
        _expert_kernel,
        out_shape=jax.ShapeDtypeStruct((p, d), BF16),
        grid_spec=grid_spec,
        compiler_params=_cparams(1),
        name="moe_experts",
    )(block_expert, block_live, xs, w_gate, w_up, w_down)


def _combine_kernel(x_ref, g_ref, route_ref, ya_ref, yb_ref, o_ref):
    w = route_ref[0][:, 2 * TOP_K:3 * TOP_K]
    y = w[:, 0:1] * ya_ref[0].astype(F32) + w[:, 1:2] * yb_ref[0].astype(F32)
    o_ref[0] = x_ref[0] + g_ref[0] * y


def _combine_call(x, g, route, ya, yb, tm):
    b, s, d = x.shape
    row = lambda i, j: (i, j, 0)
    spec = pl.BlockSpec((1, tm, d), row)
    return pl.pallas_call(
        _combine_kernel,
        out_shape=jax.ShapeDtypeStruct((b, s, d), F32),
        grid=(b, s // tm),
        in_specs=[spec, pl.BlockSpec((1, 1, d), lambda i, j: (i, 0, 0)),
                  pl.BlockSpec((1, tm, LANES), row), spec, spec],
        out_specs=spec,
        compiler_params=_cparams(2),
        name="moe_combine",
    )(x, g, route, ya, yb)


def _row_layout(ri, cnt, tm):
    t = ri.shape[0]
    nt = t // tm
    lanes = jnp.arange(N_EXPERTS, dtype=jnp.int32)
    tile_cnt = cnt[:, :, 0].astype(jnp.int32)
    tile_off = jnp.cumsum(tile_cnt, axis=0) - tile_cnt
    counts = jnp.sum(tile_cnt, axis=0)
    padded = (counts + MOE_ROWS - 1) // MOE_ROWS * MOE_ROWS
    p_ends = jnp.cumsum(padded)
    base = (p_ends - padded)[None, :] + tile_off
    e = ri[:, 0:TOP_K].reshape(nt, tm, TOP_K)
    rank = ri[:, TOP_K:2 * TOP_K].reshape(nt, tm, TOP_K)
    dest = jnp.sum(jnp.where(e[..., None] == lanes, base[:, None, None, :], 0), axis=-1) + rank
    dest = dest.reshape(t, TOP_K)
    nb = (t * TOP_K) // MOE_ROWS + N_EXPERTS
    tok = jnp.repeat(jnp.arange(t, dtype=jnp.int32), TOP_K)
    buf_tok = (jnp.arange(nb * MOE_ROWS, dtype=jnp.int32) % t).at[dest.reshape(-1)].set(
        tok, mode="promise_in_bounds", unique_indices=True)
    block_start = jnp.arange(nb, dtype=jnp.int32) * MOE_ROWS
    block_expert = jnp.minimum(jnp.sum((p_ends[None, :] <= block_start[:, None]).astype(jnp.int32), axis=-1),
                               N_EXPERTS - 1)
    block_live = (block_start < p_ends[-1]).astype(jnp.int32)
    return dest, buf_tok, block_expert, block_live


def _moe_layer(x, hf, route, cnt, g, w_gate, w_up, w_down, layer, tm):
    b, s, d = x.shape
    t = b * s
    ri = route.reshape(t, LANES)[:, :2 * TOP_K].astype(jnp.int32)
    dest, buf_tok, block_expert, block_live = _row_layout(ri, cnt, tm)
    take = lambda a, idx: a.at[idx].get(mode="promise_in_bounds")
    xs = take(hf.reshape(t, d), buf_tok)
    out = _expert_call(block_expert, block_live, xs, w_gate, w_up, w_down, layer)
    ya = take(out, dest[:, 0]).reshape(b, s, d)
    yb = take(out, dest[:, 1]).reshape(b, s, d)
    return _combine_call(x, g, route, ya, yb, tm)


def _nsa_layer(x, gain, sc, sh, g, w_in, w_out, q_gain, k_gain, cmp_pe, cmp_w1, cmp_b1, cmp_w2, cmp_b2, ffn, tm):
    b, s, d = x.shape
    G, dh = NSA_KV_GROUPS, HEAD_DIM
    cos, sin = _rope_tables(jnp.arange(s, dtype=jnp.int32))
    q, cv, ks, vs, kw, vw, gates = _nsa_in_call(x, gain, sc, sh, w_in, q_gain, k_gain, cos, sin, tm)

    n_cmp = (s - CMP_BLOCK) // CMP_STRIDE + 1
    n_str = s // CMP_STRIDE
    cmp_pos = jnp.arange(n_str, dtype=jnp.int32) * CMP_STRIDE + (CMP_BLOCK - 1)
    ccos, csin = _rope_tables(cmp_pos)
    cmp = _compress_call(cv, cmp_pe, cmp_w1, cmp_b1, cmp_w2, cmp_b2, k_gain[0], ccos, csin)
    cmp = jnp.pad(cmp, ((0, 0), (0, 0), (0, LANES - n_str), (0, 0)))
    kc, vc = cmp[0], cmp[1]

    ns = s // SEL_BLOCK
    r_, u_ = SEL_BLOCK // CMP_STRIDE, CMP_BLOCK // CMP_STRIDE
    c_idx = (r_ * np.arange(ns)[:, None, None] + np.arange(r_)[None, :, None]
             + np.arange(u_)[None, None, :]).reshape(ns, -1)
    c2s = (c_idx[:, :, None] == np.arange(n_cmp)[None, None, :]).sum(1).astype(np.float32)
    c2s = jnp.asarray(np.pad(c2s, ((0, 0), (0, LANES - n_cmp))))

    o = _nsa_attn_call(q, gates, kc, vc, ks, vs, kw, vw, c2s, n_cmp)
    return _mixer_out_call(o, None, w_out[_head_pair_order(), :], x, g, ffn, tm)


def _mlstm_layer(x, gain, sc, sh, g, w_in, w_out, b_igate, b_fgate, norm_gain, ffn, tm):
    nh = MLSTM_HEADS
    dv = norm_gain.shape[-1]
    dqk = (w_in.shape[-1] - 2 * nh - 2 * nh * dv) // (2 * nh)
    q, k, v, og, gates = _mlstm_in_call(x, gain, sc, sh, w_in, dqk, dv, tm)
    hs = _mlstm_call(q, k, v, gates, b_igate, b_fgate, norm_gain)
    return _mixer_out_call(hs, og, w_out, x, g, ffn, tm)


def kernel(x, c, ada_w, ada_b, norm_mix_gain, norm_ffn_gain, nsa_w_in, nsa_w_out, nsa_q_gain, nsa_k_gain, nsa_cmp_pe, nsa_cmp_w1, nsa_cmp_b1, nsa_cmp_w2, nsa_cmp_b2, mlstm_w_in, mlstm_b_igate, mlstm_b_fgate, mlstm_norm_gain, mlstm_w_out, router_w, router_b, moe_w_gate, moe_w_up, moe_w_down):
    b, s, d = x.shape
    depth = ada_w.shape[0]
    tm = min(512, s)
    mod = _mod_call(c, ada_w, ada_b)
    for i in range(depth):
        sh_m, sc_m, g_m, sh_f, sc_f, g_f = [mod[i, :, None, k * d:(k + 1) * d] for k in range(6)]
        j = i // 2
        ffn = (norm_ffn_gain[i], sc_f, sh_f, router_w, router_b)
        if i % 2 == 0:
            mixed = _nsa_layer(x, norm_mix_gain[i], sc_m, sh_m, g_m, nsa_w_in[j], nsa_w_out[j], nsa_q_gain[j],
                               nsa_k_gain[j], nsa_cmp_pe[j], nsa_cmp_w1[j], nsa_cmp_b1[j], nsa_cmp_w2[j],
                               nsa_cmp_b2[j], ffn, tm)
        else:
            mixed = _mlstm_layer(x, norm_mix_gain[i], sc_m, sh_m, g_m, mlstm_w_in[j], mlstm_w_out[j],
                                 mlstm_b_igate[j], mlstm_b_fgate[j], mlstm_norm_gain[j], ffn, tm)
        x = _moe_layer(*mixed, g_f, moe_w_gate, moe_w_up, moe_w_down, i, tm)
    return x
```

```python
import functools

import numpy as np
import jax
import jax.numpy as jnp
from jax import lax
from jax.experimental import pallas as pl
from jax.experimental.pallas import tpu as pltpu

F32 = jnp.float32
BF16 = jnp.bfloat16
HIGHEST = lax.Precision.HIGHEST

EPS = 1e-6
NEG = -1e30
BIG = 1e9
ROPE_THETA = 500000.0
LOG2E = 1.4426950408889634

NSA_HEADS = 16
NSA_KV_GROUPS = 2
NSA_HEADS_PER_GROUP = NSA_HEADS // NSA_KV_GROUPS
HEAD_DIM = 64
ROT_DIM = HEAD_DIM // 4
CMP_BLOCK = 32
CMP_STRIDE = 16
SEL_BLOCK = 64
SEL_TOPN = 8
WINDOW = 512
NSA_Q_BLOCK = 64
NSA_BRANCHES = 3
NSA_CALL_KEYS = 512
NSA_BLOCKS_PER_STEP = 2

MLSTM_HEADS = 4
MLSTM_CHUNK = 256
MLSTM_GROUP = 1
GATE_SOFTCAP = 15.0

N_EXPERTS = 32
N_GROUPS = 4
EXPERTS_PER_GROUP = N_EXPERTS // N_GROUPS
TOP_K = 2
MOE_ROWS = 512

LANES = 128
VMEM_LIMIT = 48 * 1024 * 1024


def _cparams(n_axes):
    return pltpu.CompilerParams(dimension_semantics=("arbitrary",) * n_axes,
                                vmem_limit_bytes=VMEM_LIMIT)


def _dot(a, b):
    return jnp.dot(a, b, preferred_element_type=F32)


def _dot_nt(a, b):
    return lax.dot_general(a, b, (((1,), (1,)), ((), ())), preferred_element_type=F32)


def _dot_tn(a, b):
    return lax.dot_general(a, b, (((0,), (0,)), ((), ())), preferred_element_type=F32)


def _norm_mod(x, gain, sc, sh):
    y = x * lax.rsqrt(jnp.mean(x * x, axis=-1, keepdims=True) + EPS) * gain
    return y * (1.0 + sc) + sh


def _half_norm_rope(x, gain, cos, sin):
    lane = lax.broadcasted_iota(jnp.int32, x.shape, x.ndim - 1)
    x2 = x * x
    left = lane < HEAD_DIM
    ss_l = jnp.sum(jnp.where(left, x2, 0.0), axis=-1, keepdims=True)
    ss_r = jnp.sum(jnp.where(left, 0.0, x2), axis=-1, keepdims=True)
    ms = jnp.where(left, ss_l, ss_r) * (1.0 / HEAD_DIM)
    y = x * lax.rsqrt(ms + EPS) * gain
    half = ROT_DIM // 2
    src = lax.broadcasted_iota(jnp.int32, (LANES, LANES), 0)
    dst = lax.broadcasted_iota(jnp.int32, (LANES, LANES), 1)
    dst_in_head = dst % HEAD_DIM
    pair = jnp.where(dst_in_head < half, dst + half, jnp.where(dst_in_head < ROT_DIM, dst - half, -1))
    partner = _dot(y.astype(BF16), jnp.where(src == pair, 1.0, 0.0).astype(BF16))
    return y * cos + partner * sin


def _rope_tables(pos):
    half = ROT_DIM // 2
    inv_freq = ROPE_THETA ** (-jnp.arange(half, dtype=F32) / half)
    ang = pos.astype(F32)[:, None] * inv_freq[None, :]
    cos, sin = jnp.cos(ang), jnp.sin(ang)
    n = pos.shape[0]
    one = jnp.ones((n, HEAD_DIM - ROT_DIM), F32)
    cos_h = jnp.concatenate([cos, cos, one], axis=-1)
    sin_h = jnp.concatenate([-sin, sin, 0.0 * one], axis=-1)
    return jnp.tile(cos_h, (1, 2)), jnp.tile(sin_h, (1, 2))


def _mod_kernel(c_ref, w_ref, b_ref, o_ref):
    c = c_ref[...]
    cond = c * jax.nn.sigmoid(c)
    o_ref[0] = jnp.dot(cond, w_ref[0], preferred_element_type=F32, precision=HIGHEST) + b_ref[0]


def _mod_call(c, ada_w, ada_b):
    depth, d, n = ada_w.shape
    b = c.shape[0]
    tn = n // 4
    return pl.pallas_call(
        _mod_kernel,
        out_shape=jax.ShapeDtypeStruct((depth, b, n), F32),
        grid=(depth, n // tn),
        in_specs=[pl.BlockSpec((b, d), lambda i, j: (0, 0)),
                  pl.BlockSpec((1, d, tn), lambda i, j: (i, 0, j)),
                  pl.BlockSpec((1, 1, tn), lambda i, j: (i, 0, j))],
        out_specs=pl.BlockSpec((1, b, tn), lambda i, j: (i, 0, j)),
        compiler_params=_cparams(2),
        name="adaln_mod",
    )(c, ada_w, ada_b.reshape(depth, 1, n))


def _nsa_in_kernel(x_ref, gain_ref, sc_ref, sh_ref, wq_ref, wkv_ref, wg_ref, qg_ref, kg_ref, cos_ref, sin_ref,
                   q_ref, cv_ref, ks_ref, vs_ref, kw_ref, vw_ref, g_ref):
    h = _norm_mod(x_ref[0], gain_ref[...], sc_ref[0], sh_ref[0]).astype(BF16)
    g_ref[0] = _dot(h, wg_ref[...])
    kv = _dot(h, wkv_ref[...])
    cos, sin = cos_ref[...], sin_ref[...]
    q = _dot(h, wq_ref[...])
    for r in range(NSA_HEADS_PER_GROUP):
        slab = _half_norm_rope(q[:, r * LANES:(r + 1) * LANES], qg_ref[...], cos, sin)
        q_ref[0, :, r * LANES:(r + 1) * LANES] = (slab * (HEAD_DIM ** -0.5 * LOG2E)).astype(BF16)
    cv_ref[0] = kv[:, 0:2 * LANES]
    ks_ref[0] = _half_norm_rope(kv[:, 2 * LANES:3 * LANES], kg_ref[1:2, :], cos, sin).astype(BF16)
    vs_ref[0] = kv[:, 3 * LANES:4 * LANES].astype(BF16)
    kw_ref[0] = _half_norm_rope(kv[:, 4 * LANES:5 * LANES], kg_ref[2:3, :], cos, sin).astype(BF16)
    vw_ref[0] = kv[:, 5 * LANES:6 * LANES].astype(BF16)


def _head_pair_order():
    r, g, dd = np.meshgrid(np.arange(NSA_HEADS_PER_GROUP), np.arange(NSA_KV_GROUPS), np.arange(HEAD_DIM),
                           indexing="ij")
    return ((g * NSA_HEADS_PER_GROUP + r) * HEAD_DIM + dd).reshape(-1)


def _nsa_in_call(x, gain, sc, sh, w_in, q_gain, k_gain, cos, sin, tm):
    b, s, d = x.shape
    nq = NSA_HEADS * HEAD_DIM
    nkv = 6 * LANES
    wq = w_in[:, :nq][:, _head_pair_order()].astype(BF16)
    qg = jnp.tile(q_gain, 2).reshape(1, LANES)
    wkv = w_in[:, nq:nq + nkv].astype(BF16)
    ng = NSA_BRANCHES * NSA_HEADS
    wg = jnp.pad(w_in[:, nq + nkv:], ((0, 0), (0, LANES - ng))).astype(BF16)
    kg = jnp.tile(k_gain, (1, 2))
    row = lambda i, j: (i, j, 0)
    per_b = lambda i, j: (i, 0, 0)
    const = lambda i, j: (0, 0)
    kv_out = lambda dt: jax.ShapeDtypeStruct((b, s, LANES), dt)
    return pl.pallas_call(
        _nsa_in_kernel,
        out_shape=(jax.ShapeDtypeStruct((b, s, nq), BF16), jax.ShapeDtypeStruct((b, s, 2 * LANES), F32),
                   kv_out(BF16), kv_out(BF16), kv_out(BF16), kv_out(BF16), kv_out(F32)),
        grid=(b, s // tm),
        in_specs=[pl.BlockSpec((1, tm, d), row),
                  pl.BlockSpec((1, d), const),
                  pl.BlockSpec((1, 1, d), per_b),
                  pl.BlockSpec((1, 1, d), per_b),
                  pl.BlockSpec((d, nq), const),
                  pl.BlockSpec((d, nkv), const),
                  pl.BlockSpec((d, LANES), const),
                  pl.BlockSpec((1, LANES), const),
                  pl.BlockSpec((3, LANES), const),
                  pl.BlockSpec((tm, LANES), lambda i, j: (j, 0)),
                  pl.BlockSpec((tm, LANES), lambda i, j: (j, 0))],
        out_specs=(pl.BlockSpec((1, tm, nq), row), pl.BlockSpec((1, tm, 2 * LANES), row))
        + (pl.BlockSpec((1, tm, LANES), row),) * 5,
        compiler_params=_cparams(2),
        name="nsa_in_proj",
    )(x, gain.reshape(1, d), sc, sh, wq, wkv, wg, qg, kg, cos, sin)


def _compress_kernel(a_ref, pe_ref, w1_ref, b1_ref, w2_ref, b2_ref, kg_ref, cos_ref, sin_ref, o_ref, *, n_str):
    is_key = pl.program_id(0) == 0
    hid2 = w1_ref.shape[-1]
    first = jnp.zeros((n_str, hid2), F32)
    second = jnp.zeros((n_str, hid2), F32)
    pe_term = jnp.zeros((8, hid2), F32)
    for l in range(CMP_STRIDE):
        rows = a_ref[0, pl.ds(l, n_str, stride=CMP_STRIDE), :].astype(BF16)
        first = first + _dot(rows, w1_ref[0, l])
        second = second + _dot(rows, w1_ref[0, CMP_STRIDE + l])
    for l in range(CMP_BLOCK):
        pe_term = pe_term + _dot(pe_ref[0, l].astype(BF16), w1_ref[0, l])
    hid = first + pltpu.roll(second, n_str - 1, 0) + pe_term[0:1] + b1_ref[0]
    hid = 0.5 * hid * (1.0 + jnp.tanh(np.sqrt(2.0 / np.pi) * (hid + 0.044715 * hid * hid * hid)))
    out = _dot(hid.astype(BF16), w2_ref[0]) + b2_ref[0]
    normed = _half_norm_rope(out, kg_ref[...], cos_ref[...], sin_ref[...])
    o_ref[0, 0] = jnp.where(is_key, normed, out).astype(o_ref.dtype)


def _block_diag2(w):
    z = jnp.zeros_like(w)
    return jnp.concatenate([jnp.concatenate([w, z], axis=-1), jnp.concatenate([z, w], axis=-1)], axis=-2)


def _compress_call(cv, pe, w1, b1, w2, b2, k_gain0, cos, sin):
    b, s, _ = cv.shape
    n_str = s // CMP_STRIDE
    hid = w1.shape[-1]
    w1bd = _block_diag2(w1.reshape(2, CMP_BLOCK, HEAD_DIM, hid)).astype(BF16)
    w2bd = _block_diag2(w2).astype(BF16)
    pe2 = jnp.broadcast_to(jnp.tile(pe, (1, 1, 2))[:, :, None, :], (2, CMP_BLOCK, 8, LANES))
    b1t = jnp.tile(b1, (1, 2)).reshape(2, 1, 2 * hid)
    b2t = jnp.tile(b2, (1, 2)).reshape(2, 1, LANES)
    kg = jnp.tile(k_gain0, 2).reshape(1, LANES)
    sel3 = lambda i, j: (i, 0, 0)
    sel4 = lambda i, j: (i, 0, 0, 0)
    const = lambda i, j: (0, 0)
    return pl.pallas_call(
        functools.partial(_compress_kernel, n_str=n_str),
        out_shape=jax.ShapeDtypeStruct((2, b, n_str, LANES), BF16),
        grid=(2, b),
        in_specs=[pl.BlockSpec((1, s, LANES), lambda i, j: (j, 0, i)),
                  pl.BlockSpec((1, CMP_BLOCK, 8, LANES), sel4),
                  pl.BlockSpec((1, CMP_BLOCK, LANES, 2 * hid), sel4),
                  pl.BlockSpec((1, 1, 2 * hid), sel3),
                  pl.BlockSpec((1, 2 * hid, LANES), sel3),
                  pl.BlockSpec((1, 1, LANES), sel3),
                  pl.BlockSpec((1, LANES), const),
                  pl.BlockSpec((n_str, LANES), const),
                  pl.BlockSpec((n_str, LANES), const)],
        out_specs=pl.BlockSpec((1, 1, n_str, LANES), lambda i, j: (i, j, 0, 0)),
        compiler_params=_cparams(2),
        name="nsa_compress",
    )(cv, pe2, w1bd, b1t, w2bd, b2t, kg, cos, sin)


def _attend(qb, k_ref, v_ref, k0, spans, bias):
    rows = qb.shape[0]
    q_rows = bias.shape[0] // NSA_KV_GROUPS
    per_group = rows // (NSA_KV_GROUPS * q_rows)
    bias = bias.reshape(NSA_KV_GROUPS, 1, q_rows, bias.shape[-1])
    m = acc = None
    for off, size in spans:
        k = k_ref[0, pl.ds(k0 + off, size), :]
        v = v_ref[0, pl.ds(k0 + off, size), :]
        s = _dot_nt(qb, k).reshape(NSA_KV_GROUPS, per_group, q_rows, size) + bias[..., off:off + size]
        s = s.reshape(rows, size)
        m_span = jnp.max(s, axis=-1, keepdims=True)
        m_new = m_span if m is None else jnp.maximum(m, m_span)
        p = jnp.exp2(s - m_new).astype(BF16)
        pv = _dot(p, jnp.concatenate([v, jnp.ones_like(v)], axis=1))
        acc = pv if m is None else acc * jnp.exp2(m - m_new) + pv
        m = m_new
    return acc


def _nsa_attn_kernel(*refs, n_inputs, **static):
    _nsa_attn_block(pl.program_id(1), *refs[:n_inputs], refs[-1], **static)


def _nsa_attn_block(step, q_ref, g_ref, kc_ref, vc_ref, ks_ref, vs_ref, kw_ref, vw_ref,
                    c2s_ref, exp_ref, o_ref, *, seq, n_cmp, n_top, win_keys, sel_span, n_keys, q_block0, q_rows):
    G, R, QB = NSA_KV_GROUPS, NSA_HEADS_PER_GROUP, q_rows
    rows = R * QB
    n_sel = seq // SEL_BLOCK
    qi0 = step * (q_rows // NSA_Q_BLOCK) + q_block0
    s0 = qi0 * NSA_Q_BLOCK
    tok = slice(0, QB)
    gt = jax.nn.sigmoid(g_ref[0, tok, :])
    lane = lax.broadcasted_iota(jnp.int32, (QB, LANES), 1)
    tq = s0 + lax.broadcasted_iota(jnp.int32, (rows, 1), 0) % QB
    tq1 = s0 + lax.broadcasted_iota(jnp.int32, (QB, 1), 0)
    ones_sq = jnp.ones((LANES, LANES), BF16)

    w0 = pl.multiple_of(jnp.maximum(s0 + QB - win_keys, 0), SEL_BLOCK)
    wpos = w0 + lax.broadcasted_iota(jnp.int32, (1, win_keys), 1)
    bias_w = jnp.where((wpos <= tq1) & (wpos > tq1 - WINDOW), 0.0, NEG)
    win_spans = [(off, min(3 * LANES, win_keys - off)) for off in range(0, win_keys, 3 * LANES)]

    zero = jnp.zeros((QB, LANES), BF16)
    qb = jnp.concatenate([jnp.where((lane // HEAD_DIM) == g, q_ref[0, tok, r * LANES:(r + 1) * LANES], zero)
                          for g in range(G) for r in range(R)], axis=0)
    tq_all = jnp.concatenate([tq] * G, axis=0)
    tq_tok = jnp.concatenate([tq1] * G, axis=0)

    sc = _dot_nt(qb, kc_ref[0])
    cpos = lax.broadcasted_iota(jnp.int32, (1, LANES), 1) * CMP_STRIDE + (CMP_BLOCK - 1)
    valid_c = (cpos <= tq_all) & (lax.broadcasted_iota(jnp.int32, (1, LANES), 1) < n_cmp)
    sc = jnp.where(valid_c, sc, NEG)
    e_c = jnp.exp2(sc - jnp.max(sc, axis=-1, keepdims=True)).astype(BF16)
    p_c = jnp.where(valid_c, e_c.astype(F32) / _dot(e_c, ones_sq), 0.0)
    o_c = _dot(p_c.astype(BF16), vc_ref[0])

    acc_w = _attend(qb, kw_ref, vw_ref, w0, win_spans, jnp.concatenate([bias_w] * G, axis=0))

    psum = jnp.sum(p_c.reshape(G, R, QB, LANES), axis=1).reshape(G * QB, LANES)
    imp = lax.dot_general(c2s_ref[...], psum, (((1,), (1,)), ((), ())),
                          preferred_element_type=F32, precision=HIGHEST)
    blk = lax.broadcasted_iota(jnp.int32, (n_sel, G * QB), 0)
    qi = qi0 + (lax.broadcasted_iota(jnp.int32, (1, G * QB), 1) % QB) // NSA_Q_BLOCK
    forced = (blk == 0) | (blk == qi) | (blk == qi - 1)
    imp = jnp.where(blk <= qi, jnp.where(forced, BIG, imp), -BIG)
    beaten = jnp.zeros(imp.shape, F32)
    for k in range(1, n_sel):
        other = pltpu.roll(imp, k, 0)
        beats = (other > imp) | ((blk >= k) & (other == imp))
        beaten = beaten + jnp.where(beats, 1.0, 0.0)
    chosen = jnp.where(beaten < n_top, 1.0, 0.0).astype(BF16)
    picked = _dot_tn(chosen, exp_ref[:, :n_keys])
    kpos = lax.broadcasted_iota(jnp.int32, (1, n_keys), 1)
    sel_spans = [(off, min(sel_span, n_keys - off)) for off in range(0, n_keys, sel_span)]

    bias_s = jnp.where((picked > 0.5) & (kpos <= tq_tok), 0.0, NEG)
    acc_s = _attend(qb, ks_ref, vs_ref, 0, sel_spans, bias_s)
    heads = []
    for h in range(G * R):
        rs = slice(h * QB, (h + 1) * QB)
        a_s, a_w = acc_s[rs], acc_w[rs]
        g_s = gt[:, NSA_HEADS + h:NSA_HEADS + h + 1] / a_s[:, LANES:LANES + 1]
        g_w = gt[:, 2 * NSA_HEADS + h:2 * NSA_HEADS + h + 1] / a_w[:, LANES:LANES + 1]
        heads.append(gt[:, h:h + 1] * o_c[rs] + g_s * a_s[:, :LANES] + g_w * a_w[:, :LANES])
    for r in range(R):
        slab = jnp.where(lane < HEAD_DIM, heads[r], heads[R + r])
        o_ref[0, tok, r * LANES:(r + 1) * LANES] = slab.astype(o_ref.dtype)


def _nsa_attn_call(q, gates, kc, vc, ks, vs, kw, vw, cmp_to_sel, n_cmp):
    b, s, nq = q.shape
    qb = NSA_Q_BLOCK
    n_top = min(SEL_TOPN, s // SEL_BLOCK)
    call_rows = min(NSA_CALL_KEYS, s)
    qrows = min(NSA_BLOCKS_PER_STEP * qb, call_rows)
    win_keys = min(-(-(WINDOW + qrows) // LANES) * LANES, s)
    sel_span = min(512, s)
    n_sel = s // SEL_BLOCK
    expand = jnp.asarray(np.arange(n_sel)[:, None] == (np.arange(s)[None, :] // SEL_BLOCK), BF16)
    per_b = lambda i, j: (i, 0, 0)
    const = lambda i, j: (0, 0)
    call_keys = call_rows
    steps = call_keys // qrows
    out = None
    for n in range(1, s // call_keys + 1):
        q0 = (n - 1) * (call_keys // qb)
        n_keys = n * call_keys
        row = lambda i, j, t0=(n - 1) * steps: (i, j + t0, 0)
        kern = functools.partial(_nsa_attn_kernel, n_inputs=10, seq=s, n_cmp=n_cmp, n_top=n_top,
                                 win_keys=win_keys, sel_span=sel_span, n_keys=n_keys, q_block0=q0, q_rows=qrows)
        carried = [] if out is None else [out]
        out = pl.pallas_call(
            kern,
            out_shape=jax.ShapeDtypeStruct((b, s, nq), BF16),
            grid=(b, steps),
            in_specs=[pl.BlockSpec((1, qrows, nq), row),
                      pl.BlockSpec((1, qrows, LANES), row),
                      pl.BlockSpec((1, LANES, LANES), per_b),
                      pl.BlockSpec((1, LANES, LANES), per_b),
                      pl.BlockSpec((1, n_keys, LANES), per_b),
                      pl.BlockSpec((1, n_keys, LANES), per_b),
                      pl.BlockSpec((1, s, LANES), per_b),
                      pl.BlockSpec((1, s, LANES), per_b),
                      pl.BlockSpec((n_sel, LANES), const),
                      pl.BlockSpec((n_sel, s), const)] + [pl.BlockSpec(memory_space=pl.ANY)] * len(carried),
            out_specs=pl.BlockSpec((1, qrows, nq), row),
            input_output_aliases={10: 0} if carried else {},
            compiler_params=_cparams(2),
            name="nsa_attention",
        )(q, gates, kc, vc, ks, vs, kw, vw, cmp_to_sel, expand, *carried)
    return out


def _mixer_out_kernel(*refs, gated):
    a_ref, refs = refs[0], refs[1:]
    lhs = a_ref[0]
    if gated:
        og_ref, refs = refs[0], refs[1:]
        lhs = (jax.nn.sigmoid(og_ref[0].astype(F32)) * lhs.astype(F32)).astype(BF16)
    w_ref, x_ref, g_ref, gain_ref, sc_ref, sh_ref, wh_ref, wl_ref, rb_ref, xo_ref, h_ref, route_ref, cnt_ref = refs
    x_new = x_ref[0] + g_ref[0] * _dot(lhs, w_ref[...])
    xo_ref[0] = x_new
    h_ref[0], route_ref[0], cnt_ref[0] = _route_tile(x_new, gain_ref[...], sc_ref[0], sh_ref[0],
                                                     wh_ref[...], wl_ref[...], rb_ref[...])


def _mixer_out_call(a, og, w_out, x, g, ffn, tm):
    gain, sc, sh, router_w, router_b = ffn
    b, s, d = x.shape
    k = a.shape[-1]
    nt = s // tm
    wp = jnp.pad(router_w, ((0, 0), (0, LANES - N_EXPERTS)))
    wh = wp.astype(BF16)
    wl = (wp - wh.astype(F32)).astype(BF16)
    rb = router_b.astype(F32).reshape(N_EXPERTS, 1)
    row = lambda i, j: (i, j, 0)
    per_b = lambda i, j: (i, 0, 0)
    const = lambda i, j: (0, 0)
    a_spec = pl.BlockSpec((1, tm, k), row)
    acts = [a] if og is None else [a, og]
    return pl.pallas_call(
        functools.partial(_mixer_out_kernel, gated=og is not None),
        out_shape=(jax.ShapeDtypeStruct((b, s, d), F32), jax.ShapeDtypeStruct((b, s, d), BF16),
                   jax.ShapeDtypeStruct((b, s, LANES), F32),
                   jax.ShapeDtypeStruct((b * nt, N_EXPERTS, LANES), F32)),
        grid=(b, nt),
        in_specs=[a_spec] * len(acts) + [pl.BlockSpec((k, d), const),
                                         pl.BlockSpec((1, tm, d), row),
                                         pl.BlockSpec((1, 1, d), per_b),
                                         pl.BlockSpec((1, d), const),
                                         pl.BlockSpec((1, 1, d), per_b),
                                         pl.BlockSpec((1, 1, d), per_b),
                                         pl.BlockSpec((d, LANES), const),
                                         pl.BlockSpec((d, LANES), const),
                                         pl.BlockSpec((N_EXPERTS, 1), const)],
        out_specs=(pl.BlockSpec((1, tm, d), row), pl.BlockSpec((1, tm, d), row), pl.BlockSpec((1, tm, LANES), row),
                   pl.BlockSpec((1, N_EXPERTS, LANES), lambda i, j: (i * nt + j, 0, 0))),
        compiler_params=_cparams(2),
        name="mixer_out_router",
    )(*acts, w_out.astype(BF16), x, g, gain.reshape(1, d), sc, sh, wh, wl, rb)


def _mlstm_in_kernel(x_ref, gain_ref, sc_ref, sh_ref, wq_ref, wk_ref, wv_ref, wo_ref, wg_ref,
                     q_ref, k_ref, v_ref, o_ref, g_ref):
    h = _norm_mod(x_ref[0], gain_ref[...], sc_ref[0], sh_ref[0]).astype(BF16)
    q_ref[0] = _dot(h, wq_ref[...]).astype(BF16)
    k_ref[0] = _dot(h, wk_ref[...]).astype(BF16)
    v_ref[0] = _dot(h, wv_ref[...]).astype(BF16)
    o_ref[0] = _dot(h, wo_ref[...]).astype(BF16)
    g_ref[0] = _dot(h, wg_ref[...])


def _mlstm_in_call(x, gain, sc, sh, w_in, dqk, dv, tm):
    b, s, d = x.shape
    nh = MLSTM_HEADS
    sizes = [nh * dqk, nh * dqk, nh * dv, nh * dv]
    offs = np.cumsum([0] + sizes)
    ws = [w_in[:, offs[i]:offs[i + 1]].astype(BF16) for i in range(4)]
    wg = jnp.pad(w_in[:, offs[4]:], ((0, 0), (0, LANES - 2 * nh))).astype(BF16)
    row = lambda i, j: (i, j, 0)
    per_b = lambda i, j: (i, 0, 0)
    const = lambda i, j: (0, 0)
    widths = sizes + [LANES]
    return pl.pallas_call(
        _mlstm_in_kernel,
        out_shape=tuple(jax.ShapeDtypeStruct((b, s, n), BF16) for n in sizes)
        + (jax.ShapeDtypeStruct((b, s, LANES), F32),),
        grid=(b, s // tm),
        in_specs=[pl.BlockSpec((1, tm, d), row),
                  pl.BlockSpec((1, d), const),
                  pl.BlockSpec((1, 1, d), per_b),
                  pl.BlockSpec((1, 1, d), per_b)] + [pl.BlockSpec((d, n), const) for n in widths],
        out_specs=tuple(pl.BlockSpec((1, tm, n), row) for n in widths),
        compiler_params=_cparams(2),
        name="mlstm_in_proj",
    )(x, gain.reshape(1, d), sc, sh, *ws, wg)


def _softcap(a):
    return GATE_SOFTCAP * jnp.tanh(a / GATE_SOFTCAP)


def _mlstm_kernel(bi_ref, bf_ref, q_ref, k_ref, v_ref, g_ref, gain_ref, o_ref,
                  li_s, b_s, *, n_chunks, n_heads, dqk, dv):
    L = MLSTM_CHUNK
    r_i = lax.broadcasted_iota(jnp.int32, (L, L), 0)
    c_i = lax.broadcasted_iota(jnp.int32, (L, L), 1)
    upper = jnp.where(r_i <= c_i, 1.0, 0.0)
    for h in range(n_heads):
        li_s[h] = _softcap(g_ref[0, h] + bi_ref[h])
        fa = _softcap(g_ref[0, n_heads + h] + bf_ref[h])
        lf = jnp.minimum(fa, 0.0) - jnp.log1p(jnp.exp(-jnp.abs(fa)))
        b_s[h] = jnp.dot(lf, upper, preferred_element_type=F32, precision=HIGHEST)
    eye = r_i == c_i
    causal = r_i >= c_i
    k_scale = dqk ** -0.5

    def to_col(row):
        return jnp.sum(jnp.where(eye, jnp.broadcast_to(row, (L, L)), 0.0), axis=1, keepdims=True)

    def local_part(h, r0, c):
        qcb = q_ref[0, pl.ds(r0, L), h * dqk:(h + 1) * dqk]
        kc = k_ref[0, pl.ds(r0, L), h * dqk:(h + 1) * dqk].astype(F32) * k_scale
        vc = v_ref[0, pl.ds(r0, L), h * dv:(h + 1) * dv]
        b_row = b_s[h, pl.ds(c, 1), :]
        li_row = li_s[h, pl.ds(c, 1), :]
        b_col, li_col = to_col(b_row), to_col(li_row)
        b_last = b_row[:, L - 1:L]
        dmat = jnp.where(causal, b_col - b_row + li_row, NEG)
        m_loc = jnp.max(dmat, axis=-1, keepdims=True)
        a_loc = jnp.exp(dmat - m_loc) * _dot_nt(qcb, kc.astype(BF16))
        num_loc = _dot(a_loc.astype(BF16), vc)
        den_loc = jnp.sum(a_loc, axis=-1, keepdims=True)
        g_max = m_loc[L - 1:L, :]
        kw = kc * jnp.exp(b_last - b_col + li_col - g_max)
        kv = _dot_tn(kw.astype(BF16), vc)
        kn = jnp.sum(kw, axis=0, keepdims=True)
        return qcb, b_col, b_last, m_loc, num_loc, den_loc, g_max, kv, kn

    def body(grp, carry):
        r0 = pl.multiple_of(grp * (MLSTM_GROUP * L), MLSTM_GROUP * L)
        parts = [[local_part(h, r0 + j * L, grp * MLSTM_GROUP + j) for j in range(MLSTM_GROUP)]
                 for h in range(n_heads)]
        new_carry, outs = [], []
        for h in range(n_heads):
            state, n_row, m_prev = carry[h]
            gain = gain_ref[h]
            head_out = []
            for qcb, b_col, b_last, m_loc, num_loc, den_loc, g_max, kv, kn in parts[h]:
                m_inter = b_col + m_prev
                m_t = jnp.maximum(m_inter, m_loc)
                intra = jnp.exp(m_loc - m_t)
                inter = jnp.exp(m_inter - m_t)
                num = intra * num_loc + inter * _dot(qcb, state.astype(BF16))
                den = intra * den_loc + inter * jnp.sum(qcb.astype(F32) * n_row, axis=-1, keepdims=True)
                h_out = num / jnp.maximum(jnp.abs(den), jnp.exp(-m_t))
                hs = h_out * lax.rsqrt(jnp.mean(h_out * h_out, axis=-1, keepdims=True) + EPS) * gain
                head_out.append(hs.astype(o_ref.dtype))
                m_new = jnp.maximum(b_last + m_prev, g_max)
                decay = jnp.exp(b_last + m_prev - m_new)
                grow = jnp.exp(g_max - m_new)
                state, n_row, m_prev = decay * state + grow * kv, decay * n_row + grow * kn, m_new
            new_carry.append((state, n_row, m_prev))
            outs.append(jnp.concatenate(head_out, axis=0))
        for h in range(n_heads):
            o_ref[0, pl.ds(r0, MLSTM_GROUP * L), h * dv:(h + 1) * dv] = outs[h]
        return tuple(new_carry)

    init = tuple((jnp.zeros((dqk, dv), F32), jnp.zeros((1, dqk), F32), jnp.zeros((1, 1), F32))
                 for _ in range(n_heads))
    lax.fori_loop(0, n_chunks // MLSTM_GROUP, body, init)


def _mlstm_call(q, k, v, gates, b_igate, b_fgate, norm_gain):
    b, s, _ = q.shape
    nh = MLSTM_HEADS
    dqk, dv = q.shape[-1] // nh, v.shape[-1] // nh
    L = MLSTM_CHUNK
    nch = s // L
    g = jnp.transpose(gates[..., :2 * nh], (0, 2, 1)).reshape(b, 2 * nh, nch, L)
    smem = pl.BlockSpec(memory_space=pltpu.SMEM)
    per_b = lambda n: pl.BlockSpec((1, s, n), lambda i: (i, 0, 0))
    kern = functools.partial(_mlstm_kernel, n_chunks=nch, n_heads=nh, dqk=dqk, dv=dv)
    return pl.pallas_call(
        kern,
        out_shape=jax.ShapeDtypeStruct((b, s, nh * dv), BF16),
        grid=(b,),
        in_specs=[smem, smem, per_b(nh * dqk), per_b(nh * dqk), per_b(nh * dv),
                  pl.BlockSpec((1, 2 * nh, nch, L), lambda i: (i, 0, 0, 0)),
                  pl.BlockSpec((nh, 1, dv), lambda i: (0, 0, 0))],
        out_specs=per_b(nh * dv),
        scratch_shapes=[pltpu.VMEM((nh, nch, L), F32), pltpu.VMEM((nh, nch, L), F32)],
        compiler_params=_cparams(1),
        name="mlstm_chunk_scan",
    )(b_igate, b_fgate, q, k, v, g, norm_gain.reshape(nh, 1, dv))


def _route_tile(x, gain, sc, sh, w_hi, w_lo, router_bias):
    h = _norm_mod(x, gain, sc, sh)
    hi = h.astype(BF16)
    lo = (h - hi.astype(F32)).astype(BF16)
    logits = _dot(hi, w_hi) + (_dot(lo, w_hi) + _dot(hi, w_lo))
    tm = logits.shape[0]
    aff = jax.nn.sigmoid(logits.T[:N_EXPERTS])
    choice = (aff + router_bias).reshape(N_GROUPS, EXPERTS_PER_GROUP, tm)
    local = lax.broadcasted_iota(jnp.int32, choice.shape, 1)

    def first_max(v):
        m = jnp.max(v, axis=1, keepdims=True)
        return m, jnp.min(jnp.where(v == m, local, EXPERTS_PER_GROUP), axis=1, keepdims=True)

    m1, i1 = first_max(choice)
    m2, i2 = first_max(jnp.where(local == i1, -jnp.inf, choice))
    score = m1 + m2
    best, e0, e1 = score[0], i1[0], i2[0]
    for g in range(1, N_GROUPS):
        better = score[g] > best
        best = jnp.where(better, score[g], best)
        e0 = jnp.where(better, i1[g] + g * EXPERTS_PER_GROUP, e0)
        e1 = jnp.where(better, i2[g] + g * EXPERTS_PER_GROUP, e1)
    expert = lax.broadcasted_iota(jnp.int32, (N_EXPERTS, tm), 0)
    is0, is1 = expert == e0, expert == e1
    a0 = jnp.sum(jnp.where(is0, aff, 0.0), axis=0, keepdims=True)
    a1 = jnp.sum(jnp.where(is1, aff, 0.0), axis=0, keepdims=True)
    tot = a0 + a1
    onehot = jnp.where(is0 | is1, 1.0, 0.0).astype(BF16)
    r_i = lax.broadcasted_iota(jnp.int32, (tm, tm), 0)
    c_i = lax.broadcasted_iota(jnp.int32, (tm, tm), 1)
    running = _dot(onehot, jnp.where(r_i <= c_i, 1.0, 0.0).astype(BF16))
    r0 = jnp.sum(jnp.where(is0, running, 0.0), axis=0, keepdims=True) - 1.0
    r1 = jnp.sum(jnp.where(is1, running, 0.0), axis=0, keepdims=True) - 1.0
    rows = [e0.astype(F32), e1.astype(F32), r0, r1, a0 / tot, a1 / tot]
    packed = jnp.concatenate(rows + [jnp.zeros((LANES - len(rows), tm), F32)], axis=0)
    return hi, packed.T, jnp.broadcast_to(running[:, tm - 1:tm], (N_EXPERTS, LANES))


def _expert_kernel(be_ref, live_ref, x_ref, wg_ref, wu_ref, wd_ref, o_ref, wg_s, wu_s, wd_s):
    i = pl.program_id(0)
    fresh = (i == 0) | (be_ref[i] != be_ref[jnp.maximum(i - 1, 0)])

    @pl.when(fresh)
    def _():
        wg_s[...] = wg_ref[0, 0].astype(BF16)
        wu_s[...] = wu_ref[0, 0].astype(BF16)
        wd_s[...] = wd_ref[0, 0].astype(BF16)

    @pl.when(live_ref[i] == 1)
    def _():
        xb = x_ref[...]
        gate = _dot(xb, wg_s[...])
        hid = gate * jax.nn.sigmoid(gate) * _dot(xb, wu_s[...])
        o_ref[...] = _dot(hid.astype(BF16), wd_s[...]).astype(o_ref.dtype)

    @pl.when(live_ref[i] == 0)
    def _():
        o_ref[...] = jnp.zeros_like(o_ref)


def _expert_call(block_expert, block_live, xs, w_gate, w_up, w_down, layer):
    p, d = xs.shape
    de = w_gate.shape[-1]
    nb = p // MOE_ROWS
    grid_spec = pltpu.PrefetchScalarGridSpec(
        num_scalar_prefetch=2,
        grid=(nb,),
        in_specs=[pl.BlockSpec((MOE_ROWS, d), lambda i, be, lv: (i, 0)),
                  pl.BlockSpec((1, 1, d, de), lambda i, be, lv: (layer, be[i], 0, 0)),
                  pl.BlockSpec((1, 1, d, de), lambda i, be, lv: (layer, be[i], 0, 0)),
                  pl.BlockSpec((1, 1, de, d), lambda i, be, lv: (layer, be[i], 0, 0))],
        out_specs=pl.BlockSpec((MOE_ROWS, d), lambda i, be, lv: (i, 0)),
        scratch_shapes=[pltpu.VMEM((d, de), BF16), pltpu.VMEM((d, de), BF16), pltpu.VMEM((de, d), BF16)],
    )
    return pl.pallas_call(
        _expert_kernel,
        out_shape=jax.ShapeDtypeStruct((p, d), BF16),
        grid_spec=grid_spec,
        compiler_params=_cparams(1),
        name="moe_experts",
    )(block_expert, block_live, xs, w_gate, w_up, w_down)


def _combine_kernel(x_ref, g_ref, route_ref, ya_ref, yb_ref, o_ref):
    w = route_ref[0][:, 2 * TOP_K:3 * TOP_K]
    y = w[:, 0:1] * ya_ref[0].astype(F32) + w[:, 1:2] * yb_ref[0].astype(F32)
    o_ref[0] = x_ref[0] + g_ref[0] * y


def _combine_call(x, g, route, ya, yb, tm):
    b, s, d = x.shape
    row = lambda i, j: (i, j, 0)
    spec = pl.BlockSpec((1, tm, d), row)
    return pl.pallas_call(
        _combine_kernel,
        out_shape=jax.ShapeDtypeStruct((b, s, d), F32),
        grid=(b, s // tm),
        in_specs=[spec, pl.BlockSpec((1, 1, d), lambda i, j: (i, 0, 0)),
                  pl.BlockSpec((1, tm, LANES), row), spec, spec],
        out_specs=spec,
        compiler_params=_cparams(2),
        name="moe_combine",
    )(x, g, route, ya, yb)


def _row_layout(ri, cnt, tm):
    t = ri.shape[0]
    nt = t // tm
    lanes = jnp.arange(N_EXPERTS, dtype=jnp.int32)
    tile_cnt = cnt[:, :, 0].astype(jnp.int32)
    tile_off = jnp.cumsum(tile_cnt, axis=0) - tile_cnt
    counts = jnp.sum(tile_cnt, axis=0)
    padded = (counts + MOE_ROWS - 1) // MOE_ROWS * MOE_ROWS
    p_ends = jnp.cumsum(padded)
    base = (p_ends - padded)[None, :] + tile_off
    e = ri[:, 0:TOP_K].reshape(nt, tm, TOP_K)
    rank = ri[:, TOP_K:2 * TOP_K].reshape(nt, tm, TOP_K)
    dest = jnp.sum(jnp.where(e[..., None] == lanes, base[:, None, None, :], 0), axis=-1) + rank
    dest = dest.reshape(t, TOP_K)
    nb = (t * TOP_K) // MOE_ROWS + N_EXPERTS
    tok = jnp.repeat(jnp.arange(t, dtype=jnp.int32), TOP_K)
    buf_tok = (jnp.arange(nb * MOE_ROWS, dtype=jnp.int32) % t).at[dest.reshape(-1)].set(
        tok, mode="promise_in_bounds", unique_indices=True)
    block_start = jnp.arange(nb, dtype=jnp.int32) * MOE_ROWS
    block_expert = jnp.minimum(jnp.sum((p_ends[None, :] <= block_start[:, None]).astype(jnp.int32), axis=-1),
                               N_EXPERTS - 1)
    block_live = (block_start < p_ends[-1]).astype(jnp.int32)
    return dest, buf_tok, block_expert, block_live


def _moe_layer(x, hf, route, cnt, g, w_gate, w_up, w_down, layer, tm):
    b, s, d = x.shape
    t = b * s
    ri = route.reshape(t, LANES)[:, :2 * TOP_K].astype(jnp.int32)
    dest, buf_tok, block_expert, block_live = _row_layout(ri, cnt, tm)
    take = lambda a, idx: a.at[idx].get(mode="promise_in_bounds")
    xs = take(hf.reshape(t, d), buf_tok)
    out = _expert_call(block_expert, block_live, xs, w_gate, w_up, w_down, layer)
    ya = take(out, dest[:, 0]).reshape(b, s, d)
    yb = take(out, dest[:, 1]).reshape(b, s, d)
    return _combine_call(x, g, route, ya, yb, tm)


def _nsa_layer(x, gain, sc, sh, g, w_in, w_out, q_gain, k_gain, cmp_pe, cmp_w1, cmp_b1, cmp_w2, cmp_b2, ffn, tm):
    b, s, d = x.shape
    G, dh = NSA_KV_GROUPS, HEAD_DIM
    cos, sin = _rope_tables(jnp.arange(s, dtype=jnp.int32))
    q, cv, ks, vs, kw, vw, gates = _nsa_in_call(x, gain, sc, sh, w_in, q_gain, k_gain, cos, sin, tm)

    n_cmp = (s - CMP_BLOCK) // CMP_STRIDE + 1
    n_str = s // CMP_STRIDE
    cmp_pos = jnp.arange(n_str, dtype=jnp.int32) * CMP_STRIDE + (CMP_BLOCK - 1)
    ccos, csin = _rope_tables(cmp_pos)
    cmp = _compress_call(cv, cmp_pe, cmp_w1, cmp_b1, cmp_w2, cmp_b2, k_gain[0], ccos, csin)
    cmp = jnp.pad(cmp, ((0, 0), (0, 0), (0, LANES - n_str), (0, 0)))
    kc, vc = cmp[0], cmp[1]

    ns = s // SEL_BLOCK
    r_, u_ = SEL_BLOCK // CMP_STRIDE, CMP_BLOCK // CMP_STRIDE
    c_idx = (r_ * np.arange(ns)[:, None, None] + np.arange(r_)[None, :, None]
             + np.arange(u_)[None, None, :]).reshape(ns, -1)
    c2s = (c_idx[:, :, None] == np.arange(n_cmp)[None, None, :]).sum(1).astype(np.float32)
    c2s = jnp.asarray(np.pad(c2s, ((0, 0), (0, LANES - n_cmp))))

    o = _nsa_attn_call(q, gates, kc, vc, ks, vs, kw, vw, c2s, n_cmp)
    return _mixer_out_call(o, None, w_out[_head_pair_order(), :], x, g, ffn, tm)


def _mlstm_layer(x, gain, sc, sh, g, w_in, w_out, b_igate, b_fgate, norm_gain, ffn, tm):
    nh = MLSTM_HEADS
    dv = norm_gain.shape[-1]
    dqk = (w_in.shape[-1] - 2 * nh - 2 * nh * dv) // (2 * nh)
    q, k, v, og, gates = _mlstm_in_call(x, gain, sc, sh, w_in, dqk, dv, tm)
    hs = _mlstm_call(q, k, v, gates, b_igate, b_fgate, norm_gain)
    return _mixer_out_call(hs, og, w_out, x, g, ffn, tm)


def kernel(x, c, ada_w, ada_b, norm_mix_gain, norm_ffn_gain, nsa_w_in, nsa_w_out, nsa_q_gain, nsa_k_gain, nsa_cmp_pe, nsa_cmp_w1, nsa_cmp_b1, nsa_cmp_w2, nsa_cmp_b2, mlstm_w_in, mlstm_b_igate, mlstm_b_fgate, mlstm_norm_gain, mlstm_w_out, router_w, router_b, moe_w_gate, moe_w_up, moe_w_down):
    b, s, d = x.shape
    depth = ada_w.shape[0]
    tm = min(512, s)
    mod = _mod_call(c, ada_w, ada_b)
    for i in range(depth):
        sh_m, sc_m, g_m, sh_f, sc_f, g_f = [mod[i, :, None, k * d:(k + 1) * d] for k in range(6)]
        j = i // 2
        ffn = (norm_ffn_gain[i], sc_f, sh_f, router_w, router_b)
        if i % 2 == 0:
            mixed = _nsa_layer(x, norm_mix_gain[i], sc_m, sh_m, g_m, nsa_w_in[j], nsa_w_out[j], nsa_q_gain[j],
                               nsa_k_gain[j], nsa_cmp_pe[j], nsa_cmp_w1[j], nsa_cmp_b1[j], nsa_cmp_w2[j],
                               nsa_cmp_b2[j], ffn, tm)
        else:
            mixed = _mlstm_layer(x, norm_mix_gain[i], sc_m, sh_m, g_m, mlstm_w_in[j], mlstm_w_out[j],
                                 mlstm_b_igate[j], mlstm_b_fgate[j], mlstm_norm_gain[j], ffn, tm)
        x = _moe_layer(*mixed, g_f, moe_w_gate, moe_w_up, moe_w_down, i, tm)
    return x
```

```python
import functools

import numpy as np
import jax
import jax.numpy as jnp
from jax import lax
from jax.experimental import pallas as pl
from jax.experimental.pallas import tpu as pltpu

F32 = jnp.float32
BF16 = jnp.bfloat16
HIGHEST = lax.Precision.HIGHEST

EPS = 1e-6
NEG = -1e30
BIG = 1e9
ROPE_THETA = 500000.0
LOG2E = 1.4426950408889634

NSA_HEADS = 16
NSA_KV_GROUPS = 2
NSA_HEADS_PER_GROUP = NSA_HEADS // NSA_KV_GROUPS
HEAD_DIM = 64
ROT_DIM = HEAD_DIM // 4
CMP_BLOCK = 32
CMP_STRIDE = 16
SEL_BLOCK = 64
SEL_TOPN = 8
WINDOW = 512
NSA_Q_BLOCK = 64
NSA_BRANCHES = 3
NSA_CALL_KEYS = 512
NSA_BLOCKS_PER_STEP = 2

MLSTM_HEADS = 4
MLSTM_CHUNK = 256
MLSTM_GROUP = 1
GATE_SOFTCAP = 15.0

N_EXPERTS = 32
N_GROUPS = 4
EXPERTS_PER_GROUP = N_EXPERTS // N_GROUPS
TOP_K = 2
MOE_ROWS = 512

LANES = 128
VMEM_LIMIT = 48 * 1024 * 1024


def _cparams(n_axes):
    return pltpu.CompilerParams(dimension_semantics=("arbitrary",) * n_axes,
                                vmem_limit_bytes=VMEM_LIMIT)


def _dot(a, b):
    return jnp.dot(a, b, preferred_element_type=F32)


def _dot_nt(a, b):
    return lax.dot_general(a, b, (((1,), (1,)), ((), ())), preferred_element_type=F32)


def _dot_tn(a, b):
    return lax.dot_general(a, b, (((0,), (0,)), ((), ())), preferred_element_type=F32)


def _norm_mod(x, gain, sc, sh):
    y = x * lax.rsqrt(jnp.mean(x * x, axis=-1, keepdims=True) + EPS) * gain
    return y * (1.0 + sc) + sh


def _half_norm_rope(x, gain, cos, sin):
    lane = lax.broadcasted_iota(jnp.int32, x.shape, x.ndim - 1)
    x2 = x * x
    left = lane < HEAD_DIM
    ss_l = jnp.sum(jnp.where(left, x2, 0.0), axis=-1, keepdims=True)
    ss_r = jnp.sum(jnp.where(left, 0.0, x2), axis=-1, keepdims=True)
    ms = jnp.where(left, ss_l, ss_r) * (1.0 / HEAD_DIM)
    y = x * lax.rsqrt(ms + EPS) * gain
    half = ROT_DIM // 2
    src = lax.broadcasted_iota(jnp.int32, (LANES, LANES), 0)
    dst = lax.broadcasted_iota(jnp.int32, (LANES, LANES), 1)
    dst_in_head = dst % HEAD_DIM
    pair = jnp.where(dst_in_head < half, dst + half, jnp.where(dst_in_head < ROT_DIM, dst - half, -1))
    partner = _dot(y.astype(BF16), jnp.where(src == pair, 1.0, 0.0).astype(BF16))
    return y * cos + partner * sin


def _rope_tables(pos):
    half = ROT_DIM // 2
    inv_freq = ROPE_THETA ** (-jnp.arange(half, dtype=F32) / half)
    ang = pos.astype(F32)[:, None] * inv_freq[None, :]
    cos, sin = jnp.cos(ang), jnp.sin(ang)
    n = pos.shape[0]
    one = jnp.ones((n, HEAD_DIM - ROT_DIM), F32)
    cos_h = jnp.concatenate([cos, cos, one], axis=-1)
    sin_h = jnp.concatenate([-sin, sin, 0.0 * one], axis=-1)
    return jnp.tile(cos_h, (1, 2)), jnp.tile(sin_h, (1, 2))


def _mod_kernel(c_ref, w_ref, b_ref, o_ref):
    c = c_ref[...]
    cond = c * jax.nn.sigmoid(c)
    o_ref[0] = jnp.dot(cond, w_ref[0], preferred_element_type=F32, precision=HIGHEST) + b_ref[0]


def _mod_call(c, ada_w, ada_b):
    depth, d, n = ada_w.shape
    b = c.shape[0]
    tn = n // 4
    return pl.pallas_call(
        _mod_kernel,
        out_shape=jax.ShapeDtypeStruct((depth, b, n), F32),
        grid=(depth, n // tn),
        in_specs=[pl.BlockSpec((b, d), lambda i, j: (0, 0)),
                  pl.BlockSpec((1, d, tn), lambda i, j: (i, 0, j)),
                  pl.BlockSpec((1, 1, tn), lambda i, j: (i, 0, j))],
        out_specs=pl.BlockSpec((1, b, tn), lambda i, j: (i, 0, j)),
        compiler_params=_cparams(2),
        name="adaln_mod",
    )(c, ada_w, ada_b.reshape(depth, 1, n))


def _nsa_in_kernel(x_ref, gain_ref, sc_ref, sh_ref, wq_ref, wkv_ref, wg_ref, qg_ref, kg_ref, cos_ref, sin_ref,
                   q_ref, cv_ref, ks_ref, vs_ref, kw_ref, vw_ref, g_ref):
    h = _norm_mod(x_ref[0], gain_ref[...], sc_ref[0], sh_ref[0]).astype(BF16)
    g_ref[0] = _dot(h, wg_ref[...])
    kv = _dot(h, wkv_ref[...])
    cos, sin = cos_ref[...], sin_ref[...]
    q = _dot(h, wq_ref[...])
    for r in range(NSA_HEADS_PER_GROUP):
        slab = _half_norm_rope(q[:, r * LANES:(r + 1) * LANES], qg_ref[...], cos, sin)
        q_ref[0, :, r * LANES:(r + 1) * LANES] = (slab * (HEAD_DIM ** -0.5 * LOG2E)).astype(BF16)
    cv_ref[0] = kv[:, 0:2 * LANES]
    ks_ref[0] = _half_norm_rope(kv[:, 2 * LANES:3 * LANES], kg_ref[1:2, :], cos, sin).astype(BF16)
    vs_ref[0] = kv[:, 3 * LANES:4 * LANES].astype(BF16)
    kw_ref[0] = _half_norm_rope(kv[:, 4 * LANES:5 * LANES], kg_ref[2:3, :], cos, sin).astype(BF16)
    vw_ref[0] = kv[:, 5 * LANES:6 * LANES].astype(BF16)


def _head_pair_order():
    r, g, dd = np.meshgrid(np.arange(NSA_HEADS_PER_GROUP), np.arange(NSA_KV_GROUPS), np.arange(HEAD_DIM),
                           indexing="ij")
    return ((g * NSA_HEADS_PER_GROUP + r) * HEAD_DIM + dd).reshape(-1)


def _nsa_in_call(x, gain, sc, sh, w_in, q_gain, k_gain, cos, sin, tm):
    b, s, d = x.shape
    nq = NSA_HEADS * HEAD_DIM
    nkv = 6 * LANES
    wq = w_in[:, :nq][:, _head_pair_order()].astype(BF16)
    qg = jnp.tile(q_gain, 2).reshape(1, LANES)
    wkv = w_in[:, nq:nq + nkv].astype(BF16)
    ng = NSA_BRANCHES * NSA_HEADS
    wg = jnp.pad(w_in[:, nq + nkv:], ((0, 0), (0, LANES - ng))).astype(BF16)
    kg = jnp.tile(k_gain, (1, 2))
    row = lambda i, j: (i, j, 0)
    per_b = lambda i, j: (i, 0, 0)
    const = lambda i, j: (0, 0)
    kv_out = lambda dt: jax.ShapeDtypeStruct((b, s, LANES), dt)
    return pl.pallas_call(
        _nsa_in_kernel,
        out_shape=(jax.ShapeDtypeStruct((b, s, nq), BF16), jax.ShapeDtypeStruct((b, s, 2 * LANES), F32),
                   kv_out(BF16), kv_out(BF16), kv_out(BF16), kv_out(BF16), kv_out(F32)),
        grid=(b, s // tm),
        in_specs=[pl.BlockSpec((1, tm, d), row),
                  pl.BlockSpec((1, d), const),
                  pl.BlockSpec((1, 1, d), per_b),
                  pl.BlockSpec((1, 1, d), per_b),
                  pl.BlockSpec((d, nq), const),
                  pl.BlockSpec((d, nkv), const),
                  pl.BlockSpec((d, LANES), const),
                  pl.BlockSpec((1, LANES), const),
                  pl.BlockSpec((3, LANES), const),
                  pl.BlockSpec((tm, LANES), lambda i, j: (j, 0)),
                  pl.BlockSpec((tm, LANES), lambda i, j: (j, 0))],
        out_specs=(pl.BlockSpec((1, tm, nq), row), pl.BlockSpec((1, tm, 2 * LANES), row))
        + (pl.BlockSpec((1, tm, LANES), row),) * 5,
        compiler_params=_cparams(2),
        name="nsa_in_proj",
    )(x, gain.reshape(1, d), sc, sh, wq, wkv, wg, qg, kg, cos, sin)


def _compress_kernel(a_ref, pe_ref, w1_ref, b1_ref, w2_ref, b2_ref, kg_ref, cos_ref, sin_ref, o_ref, *, n_str):
    is_key = pl.program_id(0) == 0
    hid2 = w1_ref.shape[-1]
    first = jnp.zeros((n_str, hid2), F32)
    second = jnp.zeros((n_str, hid2), F32)
    pe_term = jnp.zeros((8, hid2), F32)
    for l in range(CMP_STRIDE):
        rows = a_ref[0, pl.ds(l, n_str, stride=CMP_STRIDE), :].astype(BF16)
        first = first + _dot(rows, w1_ref[0, l])
        second = second + _dot(rows, w1_ref[0, CMP_STRIDE + l])
    for l in range(CMP_BLOCK):
        pe_term = pe_term + _dot(pe_ref[0, l].astype(BF16), w1_ref[0, l])
    hid = first + pltpu.roll(second, n_str - 1, 0) + pe_term[0:1] + b1_ref[0]
    hid = 0.5 * hid * (1.0 + jnp.tanh(np.sqrt(2.0 / np.pi) * (hid + 0.044715 * hid * hid * hid)))
    out = _dot(hid.astype(BF16), w2_ref[0]) + b2_ref[0]
    normed = _half_norm_rope(out, kg_ref[...], cos_ref[...], sin_ref[...])
    o_ref[0, 0] = jnp.where(is_key, normed, out).astype(o_ref.dtype)


def _block_diag2(w):
    z = jnp.zeros_like(w)
    return jnp.concatenate([jnp.concatenate([w, z], axis=-1), jnp.concatenate([z, w], axis=-1)], axis=-2)


def _compress_call(cv, pe, w1, b1, w2, b2, k_gain0, cos, sin):
    b, s, _ = cv.shape
    n_str = s // CMP_STRIDE
    hid = w1.shape[-1]
    w1bd = _block_diag2(w1.reshape(2, CMP_BLOCK, HEAD_DIM, hid)).astype(BF16)
    w2bd = _block_diag2(w2).astype(BF16)
    pe2 = jnp.broadcast_to(jnp.tile(pe, (1, 1, 2))[:, :, None, :], (2, CMP_BLOCK, 8, LANES))
    b1t = jnp.tile(b1, (1, 2)).reshape(2, 1, 2 * hid)
    b2t = jnp.tile(b2, (1, 2)).reshape(2, 1, LANES)
    kg = jnp.tile(k_gain0, 2).reshape(1, LANES)
    sel3 = lambda i, j: (i, 0, 0)
    sel4 = lambda i, j: (i, 0, 0, 0)
    const = lambda i, j: (0, 0)
    return pl.pallas_call(
        functools.partial(_compress_kernel, n_str=n_str),
        out_shape=jax.ShapeDtypeStruct((2, b, n_str, LANES), BF16),
        grid=(2, b),
        in_specs=[pl.BlockSpec((1, s, LANES), lambda i, j: (j, 0, i)),
                  pl.BlockSpec((1, CMP_BLOCK, 8, LANES), sel4),
                  pl.BlockSpec((1, CMP_BLOCK, LANES, 2 * hid), sel4),
                  pl.BlockSpec((1, 1, 2 * hid), sel3),
                  pl.BlockSpec((1, 2 * hid, LANES), sel3),
                  pl.BlockSpec((1, 1, LANES), sel3),
                  pl.BlockSpec((1, LANES), const),
                  pl.BlockSpec((n_str, LANES), const),
                  pl.BlockSpec((n_str, LANES), const)],
        out_specs=pl.BlockSpec((1, 1, n_str, LANES), lambda i, j: (i, j, 0, 0)),
        compiler_params=_cparams(2),
        name="nsa_compress",
    )(cv, pe2, w1bd, b1t, w2bd, b2t, kg, cos, sin)


def _attend(qb, k_ref, v_ref, k0, spans, bias):
    rows = qb.shape[0]
    q_rows = bias.shape[0] // NSA_KV_GROUPS
    per_group = rows // (NSA_KV_GROUPS * q_rows)
    bias = bias.reshape(NSA_KV_GROUPS, 1, q_rows, bias.shape[-1])
    m = acc = None
    for off, size in spans:
        k = k_ref[0, pl.ds(k0 + off, size), :]
        v = v_ref[0, pl.ds(k0 + off, size), :]
        s = _dot_nt(qb, k).reshape(NSA_KV_GROUPS, per_group, q_rows, size) + bias[..., off:off + size]
        s = s.reshape(rows, size)
        m_span = jnp.max(s, axis=-1, keepdims=True)
        m_new = m_span if m is None else jnp.maximum(m, m_span)
        p = jnp.exp2(s - m_new).astype(BF16)
        pv = _dot(p, jnp.concatenate([v, jnp.ones_like(v)], axis=1))
        acc = pv if m is None else acc * jnp.exp2(m - m_new) + pv
        m = m_new
    return acc


def _nsa_attn_kernel(*refs, n_inputs, **static):
    _nsa_attn_block(pl.program_id(1), *refs[:n_inputs], refs[-1], **static)


def _nsa_attn_block(step, q_ref, g_ref, kc_ref, vc_ref, ks_ref, vs_ref, kw_ref, vw_ref,
                    c2s_ref, exp_ref, o_ref, *, seq, n_cmp, n_top, win_keys, sel_span, n_keys, q_block0, q_rows):
    G, R, QB = NSA_KV_GROUPS, NSA_HEADS_PER_GROUP, q_rows
    rows = R * QB
    n_sel = seq // SEL_BLOCK
    qi0 = step * (q_rows // NSA_Q_BLOCK) + q_block0
    s0 = qi0 * NSA_Q_BLOCK
    tok = slice(0, QB)
    gt = jax.nn.sigmoid(g_ref[0, tok, :])
    lane = lax.broadcasted_iota(jnp.int32, (QB, LANES), 1)
    tq = s0 + lax.broadcasted_iota(jnp.int32, (rows, 1), 0) % QB
    tq1 = s0 + lax.broadcasted_iota(jnp.int32, (QB, 1), 0)
    ones_sq = jnp.ones((LANES, LANES), BF16)

    w0 = pl.multiple_of(jnp.maximum(s0 + QB - win_keys, 0), SEL_BLOCK)
    wpos = w0 + lax.broadcasted_iota(jnp.int32, (1, win_keys), 1)
    bias_w = jnp.where((wpos <= tq1) & (wpos > tq1 - WINDOW), 0.0, NEG)
    win_spans = [(off, min(3 * LANES, win_keys - off)) for off in range(0, win_keys, 3 * LANES)]

    zero = jnp.zeros((QB, LANES), BF16)
    qb = jnp.concatenate([jnp.where((lane // HEAD_DIM) == g, q_ref[0, tok, r * LANES:(r + 1) * LANES], zero)
                          for g in range(G) for r in range(R)], axis=0)
    tq_all = jnp.concatenate([tq] * G, axis=0)
    tq_tok = jnp.concatenate([tq1] * G, axis=0)

    sc = _dot_nt(qb, kc_ref[0])
    cpos = lax.broadcasted_iota(jnp.int32, (1, LANES), 1) * CMP_STRIDE + (CMP_BLOCK - 1)
    valid_c = (cpos <= tq_all) & (lax.broadcasted_iota(jnp.int32, (1, LANES), 1) < n_cmp)
    sc = jnp.where(valid_c, sc, NEG)
    e_c = jnp.exp2(sc - jnp.max(sc, axis=-1, keepdims=True)).astype(BF16)
    p_c = jnp.where(valid_c, e_c.astype(F32) / _dot(e_c, ones_sq), 0.0)
    o_c = _dot(p_c.astype(BF16), vc_ref[0])

    acc_w = _attend(qb, kw_ref, vw_ref, w0, win_spans, jnp.concatenate([bias_w] * G, axis=0))

    psum = jnp.sum(p_c.reshape(G, R, QB, LANES), axis=1).reshape(G * QB, LANES)
    imp = lax.dot_general(c2s_ref[...], psum, (((1,), (1,)), ((), ())),
                          preferred_element_type=F32, precision=HIGHEST)
    blk = lax.broadcasted_iota(jnp.int32, (n_sel, G * QB), 0)
    qi = qi0 + (lax.broadcasted_iota(jnp.int32, (1, G * QB), 1) % QB) // NSA_Q_BLOCK
    forced = (blk == 0) | (blk == qi) | (blk == qi - 1)
    imp = jnp.where(blk <= qi, jnp.where(forced, BIG, imp), -BIG)
    beaten = jnp.zeros(imp.shape, F32)
    for k in range(1, n_sel):
        other = pltpu.roll(imp, k, 0)
        beats = (other > imp) | ((blk >= k) & (other == imp))
        beaten = beaten + jnp.where(beats, 1.0, 0.0)
    chosen = jnp.where(beaten < n_top, 1.0, 0.0).astype(BF16)
    picked = _dot_tn(chosen, exp_ref[:, :n_keys])
    kpos = lax.broadcasted_iota(jnp.int32, (1, n_keys), 1)
    sel_spans = [(off, min(sel_span, n_keys - off)) for off in range(0, n_keys, sel_span)]

    bias_s = jnp.where((picked > 0.5) & (kpos <= tq_tok), 0.0, NEG)
    acc_s = _attend(qb, ks_ref, vs_ref, 0, sel_spans, bias_s)
    heads = []
    for h in range(G * R):
        rs = slice(h * QB, (h + 1) * QB)
        a_s, a_w = acc_s[rs], acc_w[rs]
        g_s = gt[:, NSA_HEADS + h:NSA_HEADS + h + 1] / a_s[:, LANES:LANES + 1]
        g_w = gt[:, 2 * NSA_HEADS + h:2 * NSA_HEADS + h + 1] / a_w[:, LANES:LANES + 1]
        heads.append(gt[:, h:h + 1] * o_c[rs] + g_s * a_s[:, :LANES] + g_w * a_w[:, :LANES])
    for r in range(R):
        slab = jnp.where(lane < HEAD_DIM, heads[r], heads[R + r])
        o_ref[0, tok, r * LANES:(r + 1) * LANES] = slab.astype(o_ref.dtype)


def _nsa_attn_call(q, gates, kc, vc, ks, vs, kw, vw, cmp_to_sel, n_cmp):
    b, s, nq = q.shape
    qb = NSA_Q_BLOCK
    n_top = min(SEL_TOPN, s // SEL_BLOCK)
    call_rows = min(NSA_CALL_KEYS, s)
    qrows = min(NSA_BLOCKS_PER_STEP * qb, call_rows)
    win_keys = min(-(-(WINDOW + qrows) // LANES) * LANES, s)
    sel_span = min(512, s)
    n_sel = s // SEL_BLOCK
    expand = jnp.asarray(np.arange(n_sel)[:, None] == (np.arange(s)[None, :] // SEL_BLOCK), BF16)
    per_b = lambda i, j: (i, 0, 0)
    const = lambda i, j: (0, 0)
    call_keys = call_rows
    steps = call_keys // qrows
    out = None
    for n in range(1, s // call_keys + 1):
        q0 = (n - 1) * (call_keys // qb)
        n_keys = n * call_keys
        row = lambda i, j, t0=(n - 1) * steps: (i, j + t0, 0)
        kern = functools.partial(_nsa_attn_kernel, n_inputs=10, seq=s, n_cmp=n_cmp, n_top=n_top,
                                 win_keys=win_keys, sel_span=sel_span, n_keys=n_keys, q_block0=q0, q_rows=qrows)
        carried = [] if out is None else [out]
        out = pl.pallas_call(
            kern,
            out_shape=jax.ShapeDtypeStruct((b, s, nq), BF16),
            grid=(b, steps),
            in_specs=[pl.BlockSpec((1, qrows, nq), row),
                      pl.BlockSpec((1, qrows, LANES), row),
                      pl.BlockSpec((1, LANES, LANES), per_b),
                      pl.BlockSpec((1, LANES, LANES), per_b),
                      pl.BlockSpec((1, n_keys, LANES), per_b),
                      pl.BlockSpec((1, n_keys, LANES), per_b),
                      pl.BlockSpec((1, s, LANES), per_b),
                      pl.BlockSpec((1, s, LANES), per_b),
                      pl.BlockSpec((n_sel, LANES), const),
                      pl.BlockSpec((n_sel, s), const)] + [pl.BlockSpec(memory_space=pl.ANY)] * len(carried),
            out_specs=pl.BlockSpec((1, qrows, nq), row),
            input_output_aliases={10: 0} if carried else {},
            compiler_params=_cparams(2),
            name="nsa_attention",
        )(q, gates, kc, vc, ks, vs, kw, vw, cmp_to_sel, expand, *carried)
    return out


def _mixer_out_kernel(*refs, gated):
    a_ref, refs = refs[0], refs[1:]
    lhs = a_ref[0]
    if gated:
        og_ref, refs = refs[0], refs[1:]
        lhs = (jax.nn.sigmoid(og_ref[0].astype(F32)) * lhs.astype(F32)).astype(BF16)
    (w_ref, x_ref, g_ref, gain_ref, sc_ref, sh_ref, wh_ref, wl_ref, rb_ref,
     xo_ref, h_ref, route_ref, route_t_ref, cnt_ref) = refs
    x_new = x_ref[0] + g_ref[0] * _dot(lhs, w_ref[...])
    xo_ref[0] = x_new
    h_ref[0], route_ref[0], route_t_ref[...], cnt_ref[0] = _route_tile(
        x_new, gain_ref[...], sc_ref[0], sh_ref[0], wh_ref[...], wl_ref[...], rb_ref[...])


def _mixer_out_call(a, og, w_out, x, g, ffn, tm):
    gain, sc, sh, router_w, router_b = ffn
    b, s, d = x.shape
    k = a.shape[-1]
    nt = s // tm
    wp = jnp.pad(router_w, ((0, 0), (0, LANES - N_EXPERTS)))
    wh = wp.astype(BF16)
    wl = (wp - wh.astype(F32)).astype(BF16)
    rb = router_b.astype(F32).reshape(N_EXPERTS, 1)
    row = lambda i, j: (i, j, 0)
    per_b = lambda i, j: (i, 0, 0)
    const = lambda i, j: (0, 0)
    a_spec = pl.BlockSpec((1, tm, k), row)
    acts = [a] if og is None else [a, og]
    return pl.pallas_call(
        functools.partial(_mixer_out_kernel, gated=og is not None),
        out_shape=(jax.ShapeDtypeStruct((b, s, d), F32), jax.ShapeDtypeStruct((b, s, d), BF16),
                   jax.ShapeDtypeStruct((b, s, LANES), F32), jax.ShapeDtypeStruct((8, b * s), F32),
                   jax.ShapeDtypeStruct((b * nt, N_EXPERTS, LANES), F32)),
        grid=(b, nt),
        in_specs=[a_spec] * len(acts) + [pl.BlockSpec((k, d), const),
                                         pl.BlockSpec((1, tm, d), row),
                                         pl.BlockSpec((1, 1, d), per_b),
                                         pl.BlockSpec((1, d), const),
                                         pl.BlockSpec((1, 1, d), per_b),
                                         pl.BlockSpec((1, 1, d), per_b),
                                         pl.BlockSpec((d, LANES), const),
                                         pl.BlockSpec((d, LANES), const),
                                         pl.BlockSpec((N_EXPERTS, 1), const)],
        out_specs=(pl.BlockSpec((1, tm, d), row), pl.BlockSpec((1, tm, d), row), pl.BlockSpec((1, tm, LANES), row),
                   pl.BlockSpec((8, tm), lambda i, j: (0, i * nt + j)),
                   pl.BlockSpec((1, N_EXPERTS, LANES), lambda i, j: (i * nt + j, 0, 0))),
        compiler_params=_cparams(2),
        name="mixer_out_router",
    )(*acts, w_out.astype(BF16), x, g, gain.reshape(1, d), sc, sh, wh, wl, rb)


def _mlstm_in_kernel(x_ref, gain_ref, sc_ref, sh_ref, wq_ref, wk_ref, wv_ref, wo_ref, wg_ref,
                     q_ref, k_ref, v_ref, o_ref, g_ref):
    h = _norm_mod(x_ref[0], gain_ref[...], sc_ref[0], sh_ref[0]).astype(BF16)
    q_ref[0] = _dot(h, wq_ref[...]).astype(BF16)
    k_ref[0] = _dot(h, wk_ref[...]).astype(BF16)
    v_ref[0] = _dot(h, wv_ref[...]).astype(BF16)
    o_ref[0] = _dot(h, wo_ref[...]).astype(BF16)
    g_ref[0] = _dot(h, wg_ref[...])


def _mlstm_in_call(x, gain, sc, sh, w_in, dqk, dv, tm):
    b, s, d = x.shape
    nh = MLSTM_HEADS
    sizes = [nh * dqk, nh * dqk, nh * dv, nh * dv]
    offs = np.cumsum([0] + sizes)
    ws = [w_in[:, offs[i]:offs[i + 1]].astype(BF16) for i in range(4)]
    wg = jnp.pad(w_in[:, offs[4]:], ((0, 0), (0, LANES - 2 * nh))).astype(BF16)
    row = lambda i, j: (i, j, 0)
    per_b = lambda i, j: (i, 0, 0)
    const = lambda i, j: (0, 0)
    widths = sizes + [LANES]
    return pl.pallas_call(
        _mlstm_in_kernel,
        out_shape=tuple(jax.ShapeDtypeStruct((b, s, n), BF16) for n in sizes)
        + (jax.ShapeDtypeStruct((b, s, LANES), F32),),
        grid=(b, s // tm),
        in_specs=[pl.BlockSpec((1, tm, d), row),
                  pl.BlockSpec((1, d), const),
                  pl.BlockSpec((1, 1, d), per_b),
                  pl.BlockSpec((1, 1, d), per_b)] + [pl.BlockSpec((d, n), const) for n in widths],
        out_specs=tuple(pl.BlockSpec((1, tm, n), row) for n in widths),
        compiler_params=_cparams(2),
        name="mlstm_in_proj",
    )(x, gain.reshape(1, d), sc, sh, *ws, wg)


def _softcap(a):
    return GATE_SOFTCAP * jnp.tanh(a / GATE_SOFTCAP)


def _mlstm_kernel(bi_ref, bf_ref, q_ref, k_ref, v_ref, g_ref, gain_ref, o_ref,
                  li_s, b_s, *, n_chunks, n_heads, dqk, dv):
    L = MLSTM_CHUNK
    r_i = lax.broadcasted_iota(jnp.int32, (L, L), 0)
    c_i = lax.broadcasted_iota(jnp.int32, (L, L), 1)
    upper = jnp.where(r_i <= c_i, 1.0, 0.0)
    for h in range(n_heads):
        li_s[h] = _softcap(g_ref[0, h] + bi_ref[h])
        fa = _softcap(g_ref[0, n_heads + h] + bf_ref[h])
        lf = jnp.minimum(fa, 0.0) - jnp.log1p(jnp.exp(-jnp.abs(fa)))
        b_s[h] = jnp.dot(lf, upper, preferred_element_type=F32, precision=HIGHEST)
    eye = r_i == c_i
    causal = r_i >= c_i
    k_scale = dqk ** -0.5

    def to_col(row):
        return jnp.sum(jnp.where(eye, jnp.broadcast_to(row, (L, L)), 0.0), axis=1, keepdims=True)

    def local_part(h, r0, c):
        qcb = q_ref[0, pl.ds(r0, L), h * dqk:(h + 1) * dqk]
        kc = k_ref[0, pl.ds(r0, L), h * dqk:(h + 1) * dqk].astype(F32) * k_scale
        vc = v_ref[0, pl.ds(r0, L), h * dv:(h + 1) * dv]
        b_row = b_s[h, pl.ds(c, 1), :]
        li_row = li_s[h, pl.ds(c, 1), :]
        b_col, li_col = to_col(b_row), to_col(li_row)
        b_last = b_row[:, L - 1:L]
        dmat = jnp.where(causal, b_col - b_row + li_row, NEG)
        m_loc = jnp.max(dmat, axis=-1, keepdims=True)
        a_loc = jnp.exp(dmat - m_loc) * _dot_nt(qcb, kc.astype(BF16))
        num_loc = _dot(a_loc.astype(BF16), vc)
        den_loc = jnp.sum(a_loc, axis=-1, keepdims=True)
        g_max = m_loc[L - 1:L, :]
        kw = kc * jnp.exp(b_last - b_col + li_col - g_max)
        kv = _dot_tn(kw.astype(BF16), vc)
        kn = jnp.sum(kw, axis=0, keepdims=True)
        return qcb, b_col, b_last, m_loc, num_loc, den_loc, g_max, kv, kn

    def body(grp, carry):
        r0 = pl.multiple_of(grp * (MLSTM_GROUP * L), MLSTM_GROUP * L)
        parts = [[local_part(h, r0 + j * L, grp * MLSTM_GROUP + j) for j in range(MLSTM_GROUP)]
                 for h in range(n_heads)]
        new_carry, outs = [], []
        for h in range(n_heads):
            state, n_row, m_prev = carry[h]
            gain = gain_ref[h]
            head_out = []
            for qcb, b_col, b_last, m_loc, num_loc, den_loc, g_max, kv, kn in parts[h]:
                m_inter = b_col + m_prev
                m_t = jnp.maximum(m_inter, m_loc)
                intra = jnp.exp(m_loc - m_t)
                inter = jnp.exp(m_inter - m_t)
                num = intra * num_loc + inter * _dot(qcb, state.astype(BF16))
                den = intra * den_loc + inter * jnp.sum(qcb.astype(F32) * n_row, axis=-1, keepdims=True)
                h_out = num / jnp.maximum(jnp.abs(den), jnp.exp(-m_t))
                hs = h_out * lax.rsqrt(jnp.mean(h_out * h_out, axis=-1, keepdims=True) + EPS) * gain
                head_out.append(hs.astype(o_ref.dtype))
                m_new = jnp.maximum(b_last + m_prev, g_max)
                decay = jnp.exp(b_last + m_prev - m_new)
                grow = jnp.exp(g_max - m_new)
                state, n_row, m_prev = decay * state + grow * kv, decay * n_row + grow * kn, m_new
            new_carry.append((state, n_row, m_prev))
            outs.append(jnp.concatenate(head_out, axis=0))
        for h in range(n_heads):
            o_ref[0, pl.ds(r0, MLSTM_GROUP * L), h * dv:(h + 1) * dv] = outs[h]
        return tuple(new_carry)

    init = tuple((jnp.zeros((dqk, dv), F32), jnp.zeros((1, dqk), F32), jnp.zeros((1, 1), F32))
                 for _ in range(n_heads))
    lax.fori_loop(0, n_chunks // MLSTM_GROUP, body, init)


def _mlstm_call(q, k, v, gates, b_igate, b_fgate, norm_gain):
    b, s, _ = q.shape
    nh = MLSTM_HEADS
    dqk, dv = q.shape[-1] // nh, v.shape[-1] // nh
    L = MLSTM_CHUNK
    nch = s // L
    g = jnp.transpose(gates[..., :2 * nh], (0, 2, 1)).reshape(b, 2 * nh, nch, L)
    smem = pl.BlockSpec(memory_space=pltpu.SMEM)
    per_b = lambda n: pl.BlockSpec((1, s, n), lambda i: (i, 0, 0))
    kern = functools.partial(_mlstm_kernel, n_chunks=nch, n_heads=nh, dqk=dqk, dv=dv)
    return pl.pallas_call(
        kern,
        out_shape=jax.ShapeDtypeStruct((b, s, nh * dv), BF16),
        grid=(b,),
        in_specs=[smem, smem, per_b(nh * dqk), per_b(nh * dqk), per_b(nh * dv),
                  pl.BlockSpec((1, 2 * nh, nch, L), lambda i: (i, 0, 0, 0)),
                  pl.BlockSpec((nh, 1, dv), lambda i: (0, 0, 0))],
        out_specs=per_b(nh * dv),
        scratch_shapes=[pltpu.VMEM((nh, nch, L), F32), pltpu.VMEM((nh, nch, L), F32)],
        compiler_params=_cparams(1),
        name="mlstm_chunk_scan",
    )(b_igate, b_fgate, q, k, v, g, norm_gain.reshape(nh, 1, dv))


def _route_tile(x, gain, sc, sh, w_hi, w_lo, router_bias):
    h = _norm_mod(x, gain, sc, sh)
    hi = h.astype(BF16)
    lo = (h - hi.astype(F32)).astype(BF16)
    logits = _dot(hi, w_hi) + (_dot(lo, w_hi) + _dot(hi, w_lo))
    tm = logits.shape[0]
    aff = jax.nn.sigmoid(logits.T[:N_EXPERTS])
    choice = (aff + router_bias).reshape(N_GROUPS, EXPERTS_PER_GROUP, tm)
    local = lax.broadcasted_iota(jnp.int32, choice.shape, 1)

    def first_max(v):
        m = jnp.max(v, axis=1, keepdims=True)
        return m, jnp.min(jnp.where(v == m, local, EXPERTS_PER_GROUP), axis=1, keepdims=True)

    m1, i1 = first_max(choice)
    m2, i2 = first_max(jnp.where(local == i1, -jnp.inf, choice))
    score = m1 + m2
    best, e0, e1 = score[0], i1[0], i2[0]
    for g in range(1, N_GROUPS):
        better = score[g] > best
        best = jnp.where(better, score[g], best)
        e0 = jnp.where(better, i1[g] + g * EXPERTS_PER_GROUP, e0)
        e1 = jnp.where(better, i2[g] + g * EXPERTS_PER_GROUP, e1)
    expert = lax.broadcasted_iota(jnp.int32, (N_EXPERTS, tm), 0)
    is0, is1 = expert == e0, expert == e1
    a0 = jnp.sum(jnp.where(is0, aff, 0.0), axis=0, keepdims=True)
    a1 = jnp.sum(jnp.where(is1, aff, 0.0), axis=0, keepdims=True)
    tot = a0 + a1
    onehot = jnp.where(is0 | is1, 1.0, 0.0).astype(BF16)
    r_i = lax.broadcasted_iota(jnp.int32, (tm, tm), 0)
    c_i = lax.broadcasted_iota(jnp.int32, (tm, tm), 1)
    running = _dot(onehot, jnp.where(r_i <= c_i, 1.0, 0.0).astype(BF16))
    r0 = jnp.sum(jnp.where(is0, running, 0.0), axis=0, keepdims=True) - 1.0
    r1 = jnp.sum(jnp.where(is1, running, 0.0), axis=0, keepdims=True) - 1.0
    rows = [e0.astype(F32), e1.astype(F32), r0, r1, a0 / tot, a1 / tot]
    packed = jnp.concatenate(rows + [jnp.zeros((LANES - len(rows), tm), F32)], axis=0)
    return hi, packed.T, packed[:8], jnp.broadcast_to(running[:, tm - 1:tm], (N_EXPERTS, LANES))


def _expert_kernel(be_ref, live_ref, x_ref, wg_ref, wu_ref, wd_ref, o_ref, wg_s, wu_s, wd_s):
    i = pl.program_id(0)
    fresh = (i == 0) | (be_ref[i] != be_ref[jnp.maximum(i - 1, 0)])

    @pl.when(fresh)
    def _():
        wg_s[...] = wg_ref[0, 0].astype(BF16)
        wu_s[...] = wu_ref[0, 0].astype(BF16)
        wd_s[...] = wd_ref[0, 0].astype(BF16)

    @pl.when(live_ref[i] == 1)
    def _():
        xb = x_ref[...]
        gate = _dot(xb, wg_s[...])
        hid = gate * jax.nn.sigmoid(gate) * _dot(xb, wu_s[...])
        o_ref[...] = _dot(hid.astype(BF16), wd_s[...]).astype(o_ref.dtype)

    @pl.when(live_ref[i] == 0)
    def _():
        o_ref[...] = jnp.zeros_like(o_ref)


def _expert_call(block_expert, block_live, xs, w_gate, w_up, w_down, layer):
    p, d = xs.shape
    de = w_gate.shape[-1]
    nb = p // MOE_ROWS
    grid_spec = pltpu.PrefetchScalarGridSpec(
        num_scalar_prefetch=2,
        grid=(nb,),
        in_specs=[pl.BlockSpec((MOE_ROWS, d), lambda i, be, lv: (i, 0)),
                  pl.BlockSpec((1, 1, d, de), lambda i, be, lv: (layer, be[i], 0, 0)),
                  pl.BlockSpec((1, 1, d, de), lambda i, be, lv: (layer, be[i], 0, 0)),
                  pl.BlockSpec((1, 1, de, d), lambda i, be, lv: (layer, be[i], 0, 0))],
        out_specs=pl.BlockSpec((MOE_ROWS, d), lambda i, be, lv: (i, 0)),
        scratch_shapes=[pltpu.VMEM((d, de), BF16), pltpu.VMEM((d, de), BF16), pltpu.VMEM((de, d), BF16)],
    )
    return pl.pallas_call(
        _expert_kernel,
        out_shape=jax.ShapeDtypeStruct((p, d), BF16),
        grid_spec=grid_spec,
        compiler_params=_cparams(1),
        name="moe_experts",
    )(block_expert, block_live, xs, w_gate, w_up, w_down)


def _combine_kernel(x_ref, g_ref, route_ref, ya_ref, yb_ref, o_ref):
    w = route_ref[0][:, 2 * TOP_K:3 * TOP_K]
    y = w[:, 0:1] * ya_ref[0].astype(F32) + w[:, 1:2] * yb_ref[0].astype(F32)
    o_ref[0] = x_ref[0] + g_ref[0] * y


def _combine_call(x, g, route, ya, yb, tm):
    b, s, d = x.shape
    row = lambda i, j: (i, j, 0)
    spec = pl.BlockSpec((1, tm, d), row)
    return pl.pallas_call(
        _combine_kernel,
        out_shape=jax.ShapeDtypeStruct((b, s, d), F32),
        grid=(b, s // tm),
        in_specs=[spec, pl.BlockSpec((1, 1, d), lambda i, j: (i, 0, 0)),
                  pl.BlockSpec((1, tm, LANES), row), spec, spec],
        out_specs=spec,
        compiler_params=_cparams(2),
        name="moe_combine",
    )(x, g, route, ya, yb)


def _row_layout(route_t, cnt, tm):
    t = route_t.shape[1]
    nt = t // tm
    tile_cnt = cnt[:, :, 0].astype(jnp.int32)
    tile_off = jnp.cumsum(tile_cnt, axis=0) - tile_cnt
    counts = jnp.sum(tile_cnt, axis=0)
    padded = (counts + MOE_ROWS - 1) // MOE_ROWS * MOE_ROWS
    p_ends = jnp.cumsum(padded)
    base = (p_ends - padded)[None, :] + tile_off
    base_tok = jnp.broadcast_to(base.T[:, :, None], (N_EXPERTS, nt, tm)).reshape(N_EXPERTS, t)
    ids = route_t[:2 * TOP_K].astype(jnp.int32)
    experts = jnp.arange(N_EXPERTS, dtype=jnp.int32)[:, None]
    dest = [jnp.sum(jnp.where(ids[k][None, :] == experts, base_tok, 0), axis=0) + ids[TOP_K + k]
            for k in range(TOP_K)]
    nb = (t * TOP_K) // MOE_ROWS + N_EXPERTS
    tok = jnp.arange(t, dtype=jnp.int32)
    buf_tok = (jnp.arange(nb * MOE_ROWS, dtype=jnp.int32) % t).at[jnp.concatenate(dest)].set(
        jnp.concatenate([tok] * TOP_K), mode="promise_in_bounds", unique_indices=True)
    block_start = jnp.arange(nb, dtype=jnp.int32) * MOE_ROWS
    block_expert = jnp.minimum(jnp.sum((p_ends[None, :] <= block_start[:, None]).astype(jnp.int32), axis=-1),
                               N_EXPERTS - 1)
    block_live = (block_start < p_ends[-1]).astype(jnp.int32)
    return dest, buf_tok, block_expert, block_live


def _moe_layer(x, hf, route, route_t, cnt, g, w_gate, w_up, w_down, layer, tm):
    b, s, d = x.shape
    t = b * s
    dest, buf_tok, block_expert, block_live = _row_layout(route_t, cnt, tm)
    take = lambda a, idx: a.at[idx].get(mode="promise_in_bounds")
    xs = take(hf.reshape(t, d), buf_tok)
    out = _expert_call(block_expert, block_live, xs, w_gate, w_up, w_down, layer)
    ya = take(out, dest[0]).reshape(b, s, d)
    yb = take(out, dest[1]).reshape(b, s, d)
    return _combine_call(x, g, route, ya, yb, tm)


def _nsa_layer(x, gain, sc, sh, g, w_in, w_out, q_gain, k_gain, cmp_pe, cmp_w1, cmp_b1, cmp_w2, cmp_b2, ffn, tm):
    b, s, d = x.shape
    G, dh = NSA_KV_GROUPS, HEAD_DIM
    cos, sin = _rope_tables(jnp.arange(s, dtype=jnp.int32))
    q, cv, ks, vs, kw, vw, gates = _nsa_in_call(x, gain, sc, sh, w_in, q_gain, k_gain, cos, sin, tm)

    n_cmp = (s - CMP_BLOCK) // CMP_STRIDE + 1
    n_str = s // CMP_STRIDE
    cmp_pos = jnp.arange(n_str, dtype=jnp.int32) * CMP_STRIDE + (CMP_BLOCK - 1)
    ccos, csin = _rope_tables(cmp_pos)
    cmp = _compress_call(cv, cmp_pe, cmp_w1, cmp_b1, cmp_w2, cmp_b2, k_gain[0], ccos, csin)
    cmp = jnp.pad(cmp, ((0, 0), (0, 0), (0, LANES - n_str), (0, 0)))
    kc, vc = cmp[0], cmp[1]

    ns = s // SEL_BLOCK
    r_, u_ = SEL_BLOCK // CMP_STRIDE, CMP_BLOCK // CMP_STRIDE
    c_idx = (r_ * np.arange(ns)[:, None, None] + np.arange(r_)[None, :, None]
             + np.arange(u_)[None, None, :]).reshape(ns, -1)
    c2s = (c_idx[:, :, None] == np.arange(n_cmp)[None, None, :]).sum(1).astype(np.float32)
    c2s = jnp.asarray(np.pad(c2s, ((0, 0), (0, LANES - n_cmp))))

    o = _nsa_attn_call(q, gates, kc, vc, ks, vs, kw, vw, c2s, n_cmp)
    return _mixer_out_call(o, None, w_out[_head_pair_order(), :], x, g, ffn, tm)


def _mlstm_layer(x, gain, sc, sh, g, w_in, w_out, b_igate, b_fgate, norm_gain, ffn, tm):
    nh = MLSTM_HEADS
    dv = norm_gain.shape[-1]
    dqk = (w_in.shape[-1] - 2 * nh - 2 * nh * dv) // (2 * nh)
    q, k, v, og, gates = _mlstm_in_call(x, gain, sc, sh, w_in, dqk, dv, tm)
    hs = _mlstm_call(q, k, v, gates, b_igate, b_fgate, norm_gain)
    return _mixer_out_call(hs, og, w_out, x, g, ffn, tm)


def kernel(x, c, ada_w, ada_b, norm_mix_gain, norm_ffn_gain, nsa_w_in, nsa_w_out, nsa_q_gain, nsa_k_gain, nsa_cmp_pe, nsa_cmp_w1, nsa_cmp_b1, nsa_cmp_w2, nsa_cmp_b2, mlstm_w_in, mlstm_b_igate, mlstm_b_fgate, mlstm_norm_gain, mlstm_w_out, router_w, router_b, moe_w_gate, moe_w_up, moe_w_down):
    b, s, d = x.shape
    depth = ada_w.shape[0]
    tm = min(512, s)
    mod = _mod_call(c, ada_w, ada_b)
    for i in range(depth):
        sh_m, sc_m, g_m, sh_f, sc_f, g_f = [mod[i, :, None, k * d:(k + 1) * d] for k in range(6)]
        j = i // 2
        ffn = (norm_ffn_gain[i], sc_f, sh_f, router_w, router_b)
        if i % 2 == 0:
            mixed = _nsa_layer(x, norm_mix_gain[i], sc_m, sh_m, g_m, nsa_w_in[j], nsa_w_out[j], nsa_q_gain[j],
                               nsa_k_gain[j], nsa_cmp_pe[j], nsa_cmp_w1[j], nsa_cmp_b1[j], nsa_cmp_w2[j],
                               nsa_cmp_b2[j], ffn, tm)
        else:
            mixed = _mlstm_layer(x, norm_mix_gain[i], sc_m, sh_m, g_m, mlstm_w_in[j], mlstm_w_out[j],
                                 mlstm_b_igate[j], mlstm_b_fgate[j], mlstm_norm_gain[j], ffn, tm)
        x = _moe_layer(*mixed, g_f, moe_w_gate, moe_w_up, moe_w_down, i, tm)
    return x
```

```python
import functools

import numpy as np
import jax
import jax.numpy as jnp
from jax import lax
from jax.experimental import pallas as pl
from jax.experimental.pallas import tpu as pltpu

F32 = jnp.float32
BF16 = jnp.bfloat16
HIGHEST = lax.Precision.HIGHEST

EPS = 1e-6
NEG = -1e30
BIG = 1e9
ROPE_THETA = 500000.0
LOG2E = 1.4426950408889634

NSA_HEADS = 16
NSA_KV_GROUPS = 2
NSA_HEADS_PER_GROUP = NSA_HEADS // NSA_KV_GROUPS
HEAD_DIM = 64
ROT_DIM = HEAD_DIM // 4
CMP_BLOCK = 32
CMP_STRIDE = 16
SEL_BLOCK = 64
SEL_TOPN = 8
WINDOW = 512
NSA_Q_BLOCK = 64
NSA_BRANCHES = 3
NSA_CALL_KEYS = 512
NSA_BLOCKS_PER_STEP = 2

MLSTM_HEADS = 4
MLSTM_CHUNK = 256
MLSTM_GROUP = 1
GATE_SOFTCAP = 15.0

N_EXPERTS = 32
N_GROUPS = 4
EXPERTS_PER_GROUP = N_EXPERTS // N_GROUPS
TOP_K = 2
MOE_ROWS = 512

LANES = 128
VMEM_LIMIT = 48 * 1024 * 1024


def _cparams(n_axes):
    return pltpu.CompilerParams(dimension_semantics=("arbitrary",) * n_axes,
                                vmem_limit_bytes=VMEM_LIMIT)


def _dot(a, b):
    return jnp.dot(a, b, preferred_element_type=F32)


def _dot_nt(a, b):
    return lax.dot_general(a, b, (((1,), (1,)), ((), ())), preferred_element_type=F32)


def _dot_tn(a, b):
    return lax.dot_general(a, b, (((0,), (0,)), ((), ())), preferred_element_type=F32)


def _norm_mod(x, gain, sc, sh):
    y = x * lax.rsqrt(jnp.mean(x * x, axis=-1, keepdims=True) + EPS) * gain
    return y * (1.0 + sc) + sh


def _half_norm_rope(x, gain, cos, sin):
    lane = lax.broadcasted_iota(jnp.int32, x.shape, x.ndim - 1)
    x2 = x * x
    left = lane < HEAD_DIM
    ss_l = jnp.sum(jnp.where(left, x2, 0.0), axis=-1, keepdims=True)
    ss_r = jnp.sum(jnp.where(left, 0.0, x2), axis=-1, keepdims=True)
    ms = jnp.where(left, ss_l, ss_r) * (1.0 / HEAD_DIM)
    y = x * lax.rsqrt(ms + EPS) * gain
    half = ROT_DIM // 2
    src = lax.broadcasted_iota(jnp.int32, (LANES, LANES), 0)
    dst = lax.broadcasted_iota(jnp.int32, (LANES, LANES), 1)
    dst_in_head = dst % HEAD_DIM
    pair = jnp.where(dst_in_head < half, dst + half, jnp.where(dst_in_head < ROT_DIM, dst - half, -1))
    partner = _dot(y.astype(BF16), jnp.where(src == pair, 1.0, 0.0).astype(BF16))
    return y * cos + partner * sin


def _rope_tables(pos):
    half = ROT_DIM // 2
    inv_freq = ROPE_THETA ** (-jnp.arange(half, dtype=F32) / half)
    ang = pos.astype(F32)[:, None] * inv_freq[None, :]
    cos, sin = jnp.cos(ang), jnp.sin(ang)
    n = pos.shape[0]
    one = jnp.ones((n, HEAD_DIM - ROT_DIM), F32)
    cos_h = jnp.concatenate([cos, cos, one], axis=-1)
    sin_h = jnp.concatenate([-sin, sin, 0.0 * one], axis=-1)
    return jnp.tile(cos_h, (1, 2)), jnp.tile(sin_h, (1, 2))


def _mod_kernel(c_ref, w_ref, b_ref, o_ref):
    c = c_ref[...]
    cond = c * jax.nn.sigmoid(c)
    o_ref[0] = jnp.dot(cond, w_ref[0], preferred_element_type=F32, precision=HIGHEST) + b_ref[0]


def _mod_call(c, ada_w, ada_b):
    depth, d, n = ada_w.shape
    b = c.shape[0]
    tn = n // 4
    return pl.pallas_call(
        _mod_kernel,
        out_shape=jax.ShapeDtypeStruct((depth, b, n), F32),
        grid=(depth, n // tn),
        in_specs=[pl.BlockSpec((b, d), lambda i, j: (0, 0)),
                  pl.BlockSpec((1, d, tn), lambda i, j: (i, 0, j)),
                  pl.BlockSpec((1, 1, tn), lambda i, j: (i, 0, j))],
        out_specs=pl.BlockSpec((1, b, tn), lambda i, j: (i, 0, j)),
        compiler_params=_cparams(2),
        name="adaln_mod",
    )(c, ada_w, ada_b.reshape(depth, 1, n))


def _nsa_in_kernel(x_ref, gain_ref, sc_ref, sh_ref, wq_ref, wkv_ref, wg_ref, qg_ref, kg_ref, cos_ref, sin_ref,
                   q_ref, cv_ref, ks_ref, vs_ref, kw_ref, vw_ref, g_ref):
    h = _norm_mod(x_ref[0], gain_ref[...], sc_ref[0], sh_ref[0]).astype(BF16)
    g_ref[0] = _dot(h, wg_ref[...])
    kv = _dot(h, wkv_ref[...])
    cos, sin = cos_ref[...], sin_ref[...]
    q = _dot(h, wq_ref[...])
    for r in range(NSA_HEADS_PER_GROUP):
        slab = _half_norm_rope(q[:, r * LANES:(r + 1) * LANES], qg_ref[...], cos, sin)
        q_ref[0, :, r * LANES:(r + 1) * LANES] = (slab * (HEAD_DIM ** -0.5 * LOG2E)).astype(BF16)
    cv_ref[0] = kv[:, 0:2 * LANES]
    ks_ref[0] = _half_norm_rope(kv[:, 2 * LANES:3 * LANES], kg_ref[1:2, :], cos, sin).astype(BF16)
    vs_ref[0] = kv[:, 3 * LANES:4 * LANES].astype(BF16)
    kw_ref[0] = _half_norm_rope(kv[:, 4 * LANES:5 * LANES], kg_ref[2:3, :], cos, sin).astype(BF16)
    vw_ref[0] = kv[:, 5 * LANES:6 * LANES].astype(BF16)


def _head_pair_order():
    r, g, dd = np.meshgrid(np.arange(NSA_HEADS_PER_GROUP), np.arange(NSA_KV_GROUPS), np.arange(HEAD_DIM),
                           indexing="ij")
    return ((g * NSA_HEADS_PER_GROUP + r) * HEAD_DIM + dd).reshape(-1)


def _nsa_in_call(x, gain, sc, sh, w_in, q_gain, k_gain, cos, sin, tm):
    b, s, d = x.shape
    nq = NSA_HEADS * HEAD_DIM
    nkv = 6 * LANES
    wq = w_in[:, :nq][:, _head_pair_order()].astype(BF16)
    qg = jnp.tile(q_gain, 2).reshape(1, LANES)
    wkv = w_in[:, nq:nq + nkv].astype(BF16)
    ng = NSA_BRANCHES * NSA_HEADS
    wg = jnp.pad(w_in[:, nq + nkv:], ((0, 0), (0, LANES - ng))).astype(BF16)
    kg = jnp.tile(k_gain, (1, 2))
    row = lambda i, j: (i, j, 0)
    per_b = lambda i, j: (i, 0, 0)
    const = lambda i, j: (0, 0)
    kv_out = lambda dt: jax.ShapeDtypeStruct((b, s, LANES), dt)
    return pl.pallas_call(
        _nsa_in_kernel,
        out_shape=(jax.ShapeDtypeStruct((b, s, nq), BF16), jax.ShapeDtypeStruct((b, s, 2 * LANES), F32),
                   kv_out(BF16), kv_out(BF16), kv_out(BF16), kv_out(BF16), kv_out(F32)),
        grid=(b, s // tm),
        in_specs=[pl.BlockSpec((1, tm, d), row),
                  pl.BlockSpec((1, d), const),
                  pl.BlockSpec((1, 1, d), per_b),
                  pl.BlockSpec((1, 1, d), per_b),
                  pl.BlockSpec((d, nq), const),
                  pl.BlockSpec((d, nkv), const),
                  pl.BlockSpec((d, LANES), const),
                  pl.BlockSpec((1, LANES), const),
                  pl.BlockSpec((3, LANES), const),
                  pl.BlockSpec((tm, LANES), lambda i, j: (j, 0)),
                  pl.BlockSpec((tm, LANES), lambda i, j: (j, 0))],
        out_specs=(pl.BlockSpec((1, tm, nq), row), pl.BlockSpec((1, tm, 2 * LANES), row))
        + (pl.BlockSpec((1, tm, LANES), row),) * 5,
        compiler_params=_cparams(2),
        name="nsa_in_proj",
    )(x, gain.reshape(1, d), sc, sh, wq, wkv, wg, qg, kg, cos, sin)


def _compress_kernel(a_ref, pe_ref, w1_ref, b1_ref, w2_ref, b2_ref, kg_ref, cos_ref, sin_ref, o_ref, *, n_str):
    is_key = pl.program_id(0) == 0
    hid2 = w1_ref.shape[-1]
    first = jnp.zeros((n_str, hid2), F32)
    second = jnp.zeros((n_str, hid2), F32)
    pe_term = jnp.zeros((8, hid2), F32)
    for l in range(CMP_STRIDE):
        rows = a_ref[0, pl.ds(l, n_str, stride=CMP_STRIDE), :].astype(BF16)
        first = first + _dot(rows, w1_ref[0, l])
        second = second + _dot(rows, w1_ref[0, CMP_STRIDE + l])
    for l in range(CMP_BLOCK):
        pe_term = pe_term + _dot(pe_ref[0, l].astype(BF16), w1_ref[0, l])
    hid = first + pltpu.roll(second, n_str - 1, 0) + pe_term[0:1] + b1_ref[0]
    hid = 0.5 * hid * (1.0 + jnp.tanh(np.sqrt(2.0 / np.pi) * (hid + 0.044715 * hid * hid * hid)))
    out = _dot(hid.astype(BF16), w2_ref[0]) + b2_ref[0]
    normed = _half_norm_rope(out, kg_ref[...], cos_ref[...], sin_ref[...])
    o_ref[0, 0] = jnp.where(is_key, normed, out).astype(o_ref.dtype)


def _block_diag2(w):
    z = jnp.zeros_like(w)
    return jnp.concatenate([jnp.concatenate([w, z], axis=-1), jnp.concatenate([z, w], axis=-1)], axis=-2)


def _compress_call(cv, pe, w1, b1, w2, b2, k_gain0, cos, sin):
    b, s, _ = cv.shape
    n_str = s // CMP_STRIDE
    hid = w1.shape[-1]
    w1bd = _block_diag2(w1.reshape(2, CMP_BLOCK, HEAD_DIM, hid)).astype(BF16)
    w2bd = _block_diag2(w2).astype(BF16)
    pe2 = jnp.broadcast_to(jnp.tile(pe, (1, 1, 2))[:, :, None, :], (2, CMP_BLOCK, 8, LANES))
    b1t = jnp.tile(b1, (1, 2)).reshape(2, 1, 2 * hid)
    b2t = jnp.tile(b2, (1, 2)).reshape(2, 1, LANES)
    kg = jnp.tile(k_gain0, 2).reshape(1, LANES)
    sel3 = lambda i, j: (i, 0, 0)
    sel4 = lambda i, j: (i, 0, 0, 0)
    const = lambda i, j: (0, 0)
    return pl.pallas_call(
        functools.partial(_compress_kernel, n_str=n_str),
        out_shape=jax.ShapeDtypeStruct((2, b, n_str, LANES), BF16),
        grid=(2, b),
        in_specs=[pl.BlockSpec((1, s, LANES), lambda i, j: (j, 0, i)),
                  pl.BlockSpec((1, CMP_BLOCK, 8, LANES), sel4),
                  pl.BlockSpec((1, CMP_BLOCK, LANES, 2 * hid), sel4),
                  pl.BlockSpec((1, 1, 2 * hid), sel3),
                  pl.BlockSpec((1, 2 * hid, LANES), sel3),
                  pl.BlockSpec((1, 1, LANES), sel3),
                  pl.BlockSpec((1, LANES), const),
                  pl.BlockSpec((n_str, LANES), const),
                  pl.BlockSpec((n_str, LANES), const)],
        out_specs=pl.BlockSpec((1, 1, n_str, LANES), lambda i, j: (i, j, 0, 0)),
        compiler_params=_cparams(2),
        name="nsa_compress",
    )(cv, pe2, w1bd, b1t, w2bd, b2t, kg, cos, sin)


def _attend(qb, k_ref, v_ref, k0, spans, bias):
    rows = qb.shape[0]
    q_rows = bias.shape[0] // NSA_KV_GROUPS
    per_group = rows // (NSA_KV_GROUPS * q_rows)
    bias = bias.reshape(NSA_KV_GROUPS, 1, q_rows, bias.shape[-1])
    m = acc = None
    for off, size in spans:
        k = k_ref[0, pl.ds(k0 + off, size), :]
        v = v_ref[0, pl.ds(k0 + off, size), :]
        s = _dot_nt(qb, k).reshape(NSA_KV_GROUPS, per_group, q_rows, size) + bias[..., off:off + size]
        s = s.reshape(rows, size)
        m_span = jnp.max(s, axis=-1, keepdims=True)
        m_new = m_span if m is None else jnp.maximum(m, m_span)
        p = jnp.exp2(s - m_new).astype(BF16)
        pv = _dot(p, jnp.concatenate([v, jnp.ones_like(v)], axis=1))
        acc = pv if m is None else acc * jnp.exp2(m - m_new) + pv
        m = m_new
    return acc


def _nsa_attn_kernel(*refs, n_inputs, **static):
    _nsa_attn_block(pl.program_id(1), *refs[:n_inputs], refs[-1], **static)


def _nsa_attn_block(step, q_ref, g_ref, kc_ref, vc_ref, ks_ref, vs_ref, kw_ref, vw_ref,
                    c2s_ref, exp_ref, o_ref, *, seq, n_cmp, n_top, win_keys, sel_span, n_keys, q_block0, q_rows):
    G, R, QB = NSA_KV_GROUPS, NSA_HEADS_PER_GROUP, q_rows
    rows = R * QB
    n_sel = seq // SEL_BLOCK
    qi0 = step * (q_rows // NSA_Q_BLOCK) + q_block0
    s0 = qi0 * NSA_Q_BLOCK
    tok = slice(0, QB)
    gt = jax.nn.sigmoid(g_ref[0, tok, :])
    lane = lax.broadcasted_iota(jnp.int32, (QB, LANES), 1)
    tq = s0 + lax.broadcasted_iota(jnp.int32, (rows, 1), 0) % QB
    tq1 = s0 + lax.broadcasted_iota(jnp.int32, (QB, 1), 0)
    ones_sq = jnp.ones((LANES, LANES), BF16)

    w0 = pl.multiple_of(jnp.maximum(s0 + QB - win_keys, 0), SEL_BLOCK)
    wpos = w0 + lax.broadcasted_iota(jnp.int32, (1, win_keys), 1)
    bias_w = jnp.where((wpos <= tq1) & (wpos > tq1 - WINDOW), 0.0, NEG)
    win_spans = [(off, min(3 * LANES, win_keys - off)) for off in range(0, win_keys, 3 * LANES)]

    zero = jnp.zeros((QB, LANES), BF16)
    qb = jnp.concatenate([jnp.where((lane // HEAD_DIM) == g, q_ref[0, tok, r * LANES:(r + 1) * LANES], zero)
                          for g in range(G) for r in range(R)], axis=0)
    tq_all = jnp.concatenate([tq] * G, axis=0)
    tq_tok = jnp.concatenate([tq1] * G, axis=0)

    sc = _dot_nt(qb, kc_ref[0])
    cpos = lax.broadcasted_iota(jnp.int32, (1, LANES), 1) * CMP_STRIDE + (CMP_BLOCK - 1)
    valid_c = (cpos <= tq_all) & (lax.broadcasted_iota(jnp.int32, (1, LANES), 1) < n_cmp)
    sc = jnp.where(valid_c, sc, NEG)
    e_c = jnp.exp2(sc - jnp.max(sc, axis=-1, keepdims=True)).astype(BF16)
    p_c = jnp.where(valid_c, e_c.astype(F32) / _dot(e_c, ones_sq), 0.0)
    o_c = _dot(p_c.astype(BF16), vc_ref[0])

    acc_w = _attend(qb, kw_ref, vw_ref, w0, win_spans, jnp.concatenate([bias_w] * G, axis=0))

    psum = jnp.sum(p_c.reshape(G, R, QB, LANES), axis=1).reshape(G * QB, LANES)
    imp = lax.dot_general(c2s_ref[...], psum, (((1,), (1,)), ((), ())),
                          preferred_element_type=F32, precision=HIGHEST)
    blk = lax.broadcasted_iota(jnp.int32, (n_sel, G * QB), 0)
    qi = qi0 + (lax.broadcasted_iota(jnp.int32, (1, G * QB), 1) % QB) // NSA_Q_BLOCK
    forced = (blk == 0) | (blk == qi) | (blk == qi - 1)
    imp = jnp.where(blk <= qi, jnp.where(forced, BIG, imp), -BIG)
    beaten = jnp.zeros(imp.shape, F32)
    for k in range(1, n_sel):
        other = pltpu.roll(imp, k, 0)
        beats = (other > imp) | ((blk >= k) & (other == imp))
        beaten = beaten + jnp.where(beats, 1.0, 0.0)
    chosen = jnp.where(beaten < n_top, 1.0, 0.0).astype(BF16)
    picked = _dot_tn(chosen, exp_ref[:, :n_keys])
    kpos = lax.broadcasted_iota(jnp.int32, (1, n_keys), 1)
    sel_spans = [(off, min(sel_span, n_keys - off)) for off in range(0, n_keys, sel_span)]

    bias_s = jnp.where((picked > 0.5) & (kpos <= tq_tok), 0.0, NEG)
    acc_s = _attend(qb, ks_ref, vs_ref, 0, sel_spans, bias_s)
    heads = []
    for h in range(G * R):
        rs = slice(h * QB, (h + 1) * QB)
        a_s, a_w = acc_s[rs], acc_w[rs]
        g_s = gt[:, NSA_HEADS + h:NSA_HEADS + h + 1] / a_s[:, LANES:LANES + 1]
        g_w = gt[:, 2 * NSA_HEADS + h:2 * NSA_HEADS + h + 1] / a_w[:, LANES:LANES + 1]
        heads.append(gt[:, h:h + 1] * o_c[rs] + g_s * a_s[:, :LANES] + g_w * a_w[:, :LANES])
    for r in range(R):
        slab = jnp.where(lane < HEAD_DIM, heads[r], heads[R + r])
        o_ref[0, tok, r * LANES:(r + 1) * LANES] = slab.astype(o_ref.dtype)


def _nsa_attn_call(q, gates, kc, vc, ks, vs, kw, vw, cmp_to_sel, n_cmp):
    b, s, nq = q.shape
    qb = NSA_Q_BLOCK
    n_top = min(SEL_TOPN, s // SEL_BLOCK)
    call_rows = min(NSA_CALL_KEYS, s)
    qrows = min(NSA_BLOCKS_PER_STEP * qb, call_rows)
    win_keys = min(-(-(WINDOW + qrows) // LANES) * LANES, s)
    sel_span = min(512, s)
    n_sel = s // SEL_BLOCK
    expand = jnp.asarray(np.arange(n_sel)[:, None] == (np.arange(s)[None, :] // SEL_BLOCK), BF16)
    per_b = lambda i, j: (i, 0, 0)
    const = lambda i, j: (0, 0)
    call_keys = call_rows
    steps = call_keys // qrows
    out = None
    for n in range(1, s // call_keys + 1):
        q0 = (n - 1) * (call_keys // qb)
        n_keys = n * call_keys
        row = lambda i, j, t0=(n - 1) * steps: (i, j + t0, 0)
        kern = functools.partial(_nsa_attn_kernel, n_inputs=10, seq=s, n_cmp=n_cmp, n_top=n_top,
                                 win_keys=win_keys, sel_span=sel_span, n_keys=n_keys, q_block0=q0, q_rows=qrows)
        carried = [] if out is None else [out]
        out = pl.pallas_call(
            kern,
            out_shape=jax.ShapeDtypeStruct((b, s, nq), BF16),
            grid=(b, steps),
            in_specs=[pl.BlockSpec((1, qrows, nq), row),
                      pl.BlockSpec((1, qrows, LANES), row),
                      pl.BlockSpec((1, LANES, LANES), per_b),
                      pl.BlockSpec((1, LANES, LANES), per_b),
                      pl.BlockSpec((1, n_keys, LANES), per_b),
                      pl.BlockSpec((1, n_keys, LANES), per_b),
                      pl.BlockSpec((1, s, LANES), per_b),
                      pl.BlockSpec((1, s, LANES), per_b),
                      pl.BlockSpec((n_sel, LANES), const),
                      pl.BlockSpec((n_sel, s), const)] + [pl.BlockSpec(memory_space=pl.ANY)] * len(carried),
            out_specs=pl.BlockSpec((1, qrows, nq), row),
            input_output_aliases={10: 0} if carried else {},
            compiler_params=_cparams(2),
            name="nsa_attention",
        )(q, gates, kc, vc, ks, vs, kw, vw, cmp_to_sel, expand, *carried)
    return out


def _mixer_out_kernel(*refs, gated):
    a_ref, refs = refs[0], refs[1:]
    lhs = a_ref[0]
    if gated:
        og_ref, refs = refs[0], refs[1:]
        lhs = (jax.nn.sigmoid(og_ref[0].astype(F32)) * lhs.astype(F32)).astype(BF16)
    (w_ref, x_ref, g_ref, gain_ref, sc_ref, sh_ref, wh_ref, wl_ref, rb_ref,
     xo_ref, h_ref, route_ref, route_t_ref, cnt_ref) = refs
    x_new = x_ref[0] + g_ref[0] * _dot(lhs, w_ref[...])
    xo_ref[0] = x_new
    h_ref[0], route_ref[0], route_t_ref[...], cnt_ref[0] = _route_tile(
        x_new, gain_ref[...], sc_ref[0], sh_ref[0], wh_ref[...], wl_ref[...], rb_ref[...])


def _mixer_out_call(a, og, w_out, x, g, ffn, tm):
    gain, sc, sh, router_w, router_b = ffn
    b, s, d = x.shape
    k = a.shape[-1]
    nt = s // tm
    wp = jnp.pad(router_w, ((0, 0), (0, LANES - N_EXPERTS)))
    wh = wp.astype(BF16)
    wl = (wp - wh.astype(F32)).astype(BF16)
    rb = router_b.astype(F32).reshape(N_EXPERTS, 1)
    row = lambda i, j: (i, j, 0)
    per_b = lambda i, j: (i, 0, 0)
    const = lambda i, j: (0, 0)
    a_spec = pl.BlockSpec((1, tm, k), row)
    acts = [a] if og is None else [a, og]
    return pl.pallas_call(
        functools.partial(_mixer_out_kernel, gated=og is not None),
        out_shape=(jax.ShapeDtypeStruct((b, s, d), F32), jax.ShapeDtypeStruct((b, s, d), BF16),
                   jax.ShapeDtypeStruct((b, s, LANES), F32), jax.ShapeDtypeStruct((8, b * s), F32),
                   jax.ShapeDtypeStruct((b * nt, N_EXPERTS, LANES), F32)),
        grid=(b, nt),
        in_specs=[a_spec] * len(acts) + [pl.BlockSpec((k, d), const),
                                         pl.BlockSpec((1, tm, d), row),
                                         pl.BlockSpec((1, 1, d), per_b),
                                         pl.BlockSpec((1, d), const),
                                         pl.BlockSpec((1, 1, d), per_b),
                                         pl.BlockSpec((1, 1, d), per_b),
                                         pl.BlockSpec((d, LANES), const),
                                         pl.BlockSpec((d, LANES), const),
                                         pl.BlockSpec((N_EXPERTS, 1), const)],
        out_specs=(pl.BlockSpec((1, tm, d), row), pl.BlockSpec((1, tm, d), row), pl.BlockSpec((1, tm, LANES), row),
                   pl.BlockSpec((8, tm), lambda i, j: (0, i * nt + j)),
                   pl.BlockSpec((1, N_EXPERTS, LANES), lambda i, j: (i * nt + j, 0, 0))),
        compiler_params=_cparams(2),
        name="mixer_out_router",
    )(*acts, w_out.astype(BF16), x, g, gain.reshape(1, d), sc, sh, wh, wl, rb)


def _mlstm_in_kernel(x_ref, gain_ref, sc_ref, sh_ref, wq_ref, wk_ref, wv_ref, wo_ref, wg_ref,
                     q_ref, k_ref, v_ref, o_ref, g_ref):
    h = _norm_mod(x_ref[0], gain_ref[...], sc_ref[0], sh_ref[0]).astype(BF16)
    q_ref[0] = _dot(h, wq_ref[...]).astype(BF16)
    k_ref[0] = _dot(h, wk_ref[...]).astype(BF16)
    v_ref[0] = _dot(h, wv_ref[...]).astype(BF16)
    o_ref[0] = _dot(h, wo_ref[...]).astype(BF16)
    g_ref[0] = _dot(h, wg_ref[...])


def _mlstm_in_call(x, gain, sc, sh, w_in, dqk, dv, tm):
    b, s, d = x.shape
    nh = MLSTM_HEADS
    sizes = [nh * dqk, nh * dqk, nh * dv, nh * dv]
    offs = np.cumsum([0] + sizes)
    ws = [w_in[:, offs[i]:offs[i + 1]].astype(BF16) for i in range(4)]
    wg = jnp.pad(w_in[:, offs[4]:], ((0, 0), (0, LANES - 2 * nh))).astype(BF16)
    row = lambda i, j: (i, j, 0)
    per_b = lambda i, j: (i, 0, 0)
    const = lambda i, j: (0, 0)
    widths = sizes + [LANES]
    return pl.pallas_call(
        _mlstm_in_kernel,
        out_shape=tuple(jax.ShapeDtypeStruct((b, s, n), BF16) for n in sizes)
        + (jax.ShapeDtypeStruct((b, s, LANES), F32),),
        grid=(b, s // tm),
        in_specs=[pl.BlockSpec((1, tm, d), row),
                  pl.BlockSpec((1, d), const),
                  pl.BlockSpec((1, 1, d), per_b),
                  pl.BlockSpec((1, 1, d), per_b)] + [pl.BlockSpec((d, n), const) for n in widths],
        out_specs=tuple(pl.BlockSpec((1, tm, n), row) for n in widths),
        compiler_params=_cparams(2),
        name="mlstm_in_proj",
    )(x, gain.reshape(1, d), sc, sh, *ws, wg)


def _softcap(a):
    return GATE_SOFTCAP * jnp.tanh(a / GATE_SOFTCAP)


def _mlstm_kernel(bi_ref, bf_ref, q_ref, k_ref, v_ref, g_ref, gain_ref, o_ref,
                  li_s, b_s, *, n_chunks, n_heads, dqk, dv):
    L = MLSTM_CHUNK
    r_i = lax.broadcasted_iota(jnp.int32, (L, L), 0)
    c_i = lax.broadcasted_iota(jnp.int32, (L, L), 1)
    upper = jnp.where(r_i <= c_i, 1.0, 0.0)
    for h in range(n_heads):
        li_s[h] = _softcap(g_ref[0, h] + bi_ref[h])
        fa = _softcap(g_ref[0, n_heads + h] + bf_ref[h])
        lf = jnp.minimum(fa, 0.0) - jnp.log1p(jnp.exp(-jnp.abs(fa)))
        b_s[h] = jnp.dot(lf, upper, preferred_element_type=F32, precision=HIGHEST)
    eye = r_i == c_i
    causal = r_i >= c_i
    k_scale = dqk ** -0.5

    def to_col(row):
        return jnp.sum(jnp.where(eye, jnp.broadcast_to(row, (L, L)), 0.0), axis=1, keepdims=True)

    def local_part(h, r0, c):
        qcb = q_ref[0, pl.ds(r0, L), h * dqk:(h + 1) * dqk]
        kc = k_ref[0, pl.ds(r0, L), h * dqk:(h + 1) * dqk].astype(F32) * k_scale
        vc = v_ref[0, pl.ds(r0, L), h * dv:(h + 1) * dv]
        b_row = b_s[h, pl.ds(c, 1), :]
        li_row = li_s[h, pl.ds(c, 1), :]
        b_col, li_col = to_col(b_row), to_col(li_row)
        b_last = b_row[:, L - 1:L]
        dmat = jnp.where(causal, b_col - b_row + li_row, NEG)
        m_loc = jnp.max(dmat, axis=-1, keepdims=True)
        a_loc = jnp.exp(dmat - m_loc) * _dot_nt(qcb, kc.astype(BF16))
        num_loc = _dot(a_loc.astype(BF16), vc)
        den_loc = jnp.sum(a_loc, axis=-1, keepdims=True)
        g_max = m_loc[L - 1:L, :]
        kw = kc * jnp.exp(b_last - b_col + li_col - g_max)
        kv = _dot_tn(kw.astype(BF16), vc)
        kn = jnp.sum(kw, axis=0, keepdims=True)
        return qcb, b_col, b_last, m_loc, num_loc, den_loc, g_max, kv, kn

    def body(grp, carry):
        r0 = pl.multiple_of(grp * (MLSTM_GROUP * L), MLSTM_GROUP * L)
        parts = [[local_part(h, r0 + j * L, grp * MLSTM_GROUP + j) for j in range(MLSTM_GROUP)]
                 for h in range(n_heads)]
        new_carry, outs = [], []
        for h in range(n_heads):
            state, n_row, m_prev = carry[h]
            gain = gain_ref[h]
            head_out = []
            for qcb, b_col, b_last, m_loc, num_loc, den_loc, g_max, kv, kn in parts[h]:
                m_inter = b_col + m_prev
                m_t = jnp.maximum(m_inter, m_loc)
                intra = jnp.exp(m_loc - m_t)
                inter = jnp.exp(m_inter - m_t)
                num = intra * num_loc + inter * _dot(qcb, state.astype(BF16))
                den = intra * den_loc + inter * jnp.sum(qcb.astype(F32) * n_row, axis=-1, keepdims=True)
                h_out = num / jnp.maximum(jnp.abs(den), jnp.exp(-m_t))
                hs = h_out * lax.rsqrt(jnp.mean(h_out * h_out, axis=-1, keepdims=True) + EPS) * gain
                head_out.append(hs.astype(o_ref.dtype))
                m_new = jnp.maximum(b_last + m_prev, g_max)
                decay = jnp.exp(b_last + m_prev - m_new)
                grow = jnp.exp(g_max - m_new)
                state, n_row, m_prev = decay * state + grow * kv, decay * n_row + grow * kn, m_new
            new_carry.append((state, n_row, m_prev))
            outs.append(jnp.concatenate(head_out, axis=0))
        for h in range(n_heads):
            o_ref[0, pl.ds(r0, MLSTM_GROUP * L), h * dv:(h + 1) * dv] = outs[h]
        return tuple(new_carry)

    init = tuple((jnp.zeros((dqk, dv), F32), jnp.zeros((1, dqk), F32), jnp.zeros((1, 1), F32))
                 for _ in range(n_heads))
    lax.fori_loop(0, n_chunks // MLSTM_GROUP, body, init)


def _mlstm_call(q, k, v, gates, b_igate, b_fgate, norm_gain):
    b, s, _ = q.shape
    nh = MLSTM_HEADS
    dqk, dv = q.shape[-1] // nh, v.shape[-1] // nh
    L = MLSTM_CHUNK
    nch = s // L
    g = jnp.transpose(gates[..., :2 * nh], (0, 2, 1)).reshape(b, 2 * nh, nch, L)
    smem = pl.BlockSpec(memory_space=pltpu.SMEM)
    per_b = lambda n: pl.BlockSpec((1, s, n), lambda i: (i, 0, 0))
    kern = functools.partial(_mlstm_kernel, n_chunks=nch, n_heads=nh, dqk=dqk, dv=dv)
    return pl.pallas_call(
        kern,
        out_shape=jax.ShapeDtypeStruct((b, s, nh * dv), BF16),
        grid=(b,),
        in_specs=[smem, smem, per_b(nh * dqk), per_b(nh * dqk), per_b(nh * dv),
                  pl.BlockSpec((1, 2 * nh, nch, L), lambda i: (i, 0, 0, 0)),
                  pl.BlockSpec((nh, 1, dv), lambda i: (0, 0, 0))],
        out_specs=per_b(nh * dv),
        scratch_shapes=[pltpu.VMEM((nh, nch, L), F32), pltpu.VMEM((nh, nch, L), F32)],
        compiler_params=_cparams(1),
        name="mlstm_chunk_scan",
    )(b_igate, b_fgate, q, k, v, g, norm_gain.reshape(nh, 1, dv))


def _route_tile(x, gain, sc, sh, w_hi, w_lo, router_bias):
    h = _norm_mod(x, gain, sc, sh)
    hi = h.astype(BF16)
    lo = (h - hi.astype(F32)).astype(BF16)
    logits = _dot(hi, w_hi) + (_dot(lo, w_hi) + _dot(hi, w_lo))
    tm = logits.shape[0]
    aff = jax.nn.sigmoid(logits.T[:N_EXPERTS])
    choice = (aff + router_bias).reshape(N_GROUPS, EXPERTS_PER_GROUP, tm)
    local = lax.broadcasted_iota(jnp.int32, choice.shape, 1)

    def first_max(v):
        m = jnp.max(v, axis=1, keepdims=True)
        return m, jnp.min(jnp.where(v == m, local, EXPERTS_PER_GROUP), axis=1, keepdims=True)

    m1, i1 = first_max(choice)
    m2, i2 = first_max(jnp.where(local == i1, -jnp.inf, choice))
    score = m1 + m2
    best, e0, e1 = score[0], i1[0], i2[0]
    for g in range(1, N_GROUPS):
        better = score[g] > best
        best = jnp.where(better, score[g], best)
        e0 = jnp.where(better, i1[g] + g * EXPERTS_PER_GROUP, e0)
        e1 = jnp.where(better, i2[g] + g * EXPERTS_PER_GROUP, e1)
    expert = lax.broadcasted_iota(jnp.int32, (N_EXPERTS, tm), 0)
    is0, is1 = expert == e0, expert == e1
    a0 = jnp.sum(jnp.where(is0, aff, 0.0), axis=0, keepdims=True)
    a1 = jnp.sum(jnp.where(is1, aff, 0.0), axis=0, keepdims=True)
    tot = a0 + a1
    onehot = jnp.where(is0 | is1, 1.0, 0.0).astype(BF16)
    r_i = lax.broadcasted_iota(jnp.int32, (tm, tm), 0)
    c_i = lax.broadcasted_iota(jnp.int32, (tm, tm), 1)
    running = _dot(onehot, jnp.where(r_i <= c_i, 1.0, 0.0).astype(BF16))
    r0 = jnp.sum(jnp.where(is0, running, 0.0), axis=0, keepdims=True) - 1.0
    r1 = jnp.sum(jnp.where(is1, running, 0.0), axis=0, keepdims=True) - 1.0
    rows = [e0.astype(F32), e1.astype(F32), r0, r1, a0 / tot, a1 / tot]
    packed = jnp.concatenate(rows + [jnp.zeros((LANES - len(rows), tm), F32)], axis=0)
    return hi, packed.T, packed[:8], jnp.broadcast_to(running[:, tm - 1:tm], (N_EXPERTS, LANES))


def _expert_kernel(be_ref, live_ref, x_ref, wg_ref, wu_ref, wd_ref, o_ref, wg_s, wu_s, wd_s):
    i = pl.program_id(0)
    fresh = (i == 0) | (be_ref[i] != be_ref[jnp.maximum(i - 1, 0)])

    @pl.when(fresh)
    def _():
        wg_s[...] = wg_ref[0, 0].astype(BF16)
        wu_s[...] = wu_ref[0, 0].astype(BF16)
        wd_s[...] = wd_ref[0, 0].astype(BF16)

    @pl.when(live_ref[i] == 1)
    def _():
        xb = x_ref[...]
        gate = _dot(xb, wg_s[...])
        hid = gate * jax.nn.sigmoid(gate) * _dot(xb, wu_s[...])
        o_ref[...] = _dot(hid.astype(BF16), wd_s[...]).astype(o_ref.dtype)

    @pl.when(live_ref[i] == 0)
    def _():
        o_ref[...] = jnp.zeros_like(o_ref)


def _expert_call(block_expert, block_live, xs, w_gate, w_up, w_down, layer):
    p, d = xs.shape
    de = w_gate.shape[-1]
    nb = p // MOE_ROWS
    grid_spec = pltpu.PrefetchScalarGridSpec(
        num_scalar_prefetch=2,
        grid=(nb,),
        in_specs=[pl.BlockSpec((MOE_ROWS, d), lambda i, be, lv: (i, 0)),
                  pl.BlockSpec((1, 1, d, de), lambda i, be, lv: (layer, be[i], 0, 0)),
                  pl.BlockSpec((1, 1, d, de), lambda i, be, lv: (layer, be[i], 0, 0)),
                  pl.BlockSpec((1, 1, de, d), lambda i, be, lv: (layer, be[i], 0, 0))],
        out_specs=pl.BlockSpec((MOE_ROWS, d), lambda i, be, lv: (i, 0)),
        scratch_shapes=[pltpu.VMEM((d, de), BF16), pltpu.VMEM((d, de), BF16), pltpu.VMEM((de, d), BF16)],
    )
    return pl.pallas_call(
        _expert_kernel,
        out_shape=jax.ShapeDtypeStruct((p, d), BF16),
        grid_spec=grid_spec,
        compiler_params=_cparams(1),
        name="moe_experts",
    )(block_expert, block_live, xs, w_gate, w_up, w_down)


def _combine_kernel(x_ref, g_ref, route_ref, ya_ref, yb_ref, o_ref):
    w = route_ref[0][:, 2 * TOP_K:3 * TOP_K]
    y = w[:, 0:1] * ya_ref[0].astype(F32) + w[:, 1:2] * yb_ref[0].astype(F32)
    o_ref[0] = x_ref[0] + g_ref[0] * y


def _combine_call(x, g, route, ya, yb, tm):
    b, s, d = x.shape
    row = lambda i, j: (i, j, 0)
    spec = pl.BlockSpec((1, tm, d), row)
    return pl.pallas_call(
        _combine_kernel,
        out_shape=jax.ShapeDtypeStruct((b, s, d), F32),
        grid=(b, s // tm),
        in_specs=[spec, pl.BlockSpec((1, 1, d), lambda i, j: (i, 0, 0)),
                  pl.BlockSpec((1, tm, LANES), row), spec, spec],
        out_specs=spec,
        compiler_params=_cparams(2),
        name="moe_combine",
    )(x, g, route, ya, yb)


def _row_layout(route_t, cnt, tm):
    t = route_t.shape[1]
    nt = t // tm
    tile_cnt = cnt[:, :, 0].astype(jnp.int32)
    tile_off = jnp.cumsum(tile_cnt, axis=0) - tile_cnt
    counts = jnp.sum(tile_cnt, axis=0)
    padded = (counts + MOE_ROWS - 1) // MOE_ROWS * MOE_ROWS
    p_ends = jnp.cumsum(padded)
    base = (p_ends - padded)[None, :] + tile_off
    base_tok = jnp.broadcast_to(base.T[:, :, None], (N_EXPERTS, nt, tm)).reshape(N_EXPERTS, t)
    ids = route_t[:2 * TOP_K].astype(jnp.int32)
    experts = jnp.arange(N_EXPERTS, dtype=jnp.int32)[:, None]
    dest = [jnp.sum(jnp.where(ids[k][None, :] == experts, base_tok, 0), axis=0) + ids[TOP_K + k]
            for k in range(TOP_K)]
    nb = (t * TOP_K) // MOE_ROWS + N_EXPERTS
    tok = jnp.arange(t, dtype=jnp.int32)
    buf_tok = (jnp.arange(nb * MOE_ROWS, dtype=jnp.int32) % t).at[jnp.concatenate(dest)].set(
        jnp.concatenate([tok] * TOP_K), mode="promise_in_bounds", unique_indices=True)
    block_start = jnp.arange(nb, dtype=jnp.int32) * MOE_ROWS
    block_expert = jnp.minimum(jnp.sum((p_ends[None, :] <= block_start[:, None]).astype(jnp.int32), axis=-1),
                               N_EXPERTS - 1)
    block_live = (block_start < p_ends[-1]).astype(jnp.int32)
    return dest, buf_tok, block_expert, block_live


def _moe_layer(x, hf, route, route_t, cnt, g, w_gate, w_up, w_down, layer, tm):
    b, s, d = x.shape
    t = b * s
    dest, buf_tok, block_expert, block_live = _row_layout(route_t, cnt, tm)
    take = lambda a, idx: a.at[idx].get(mode="promise_in_bounds")
    xs = take(hf.reshape(t, d), buf_tok)
    out = _expert_call(block_expert, block_live, xs, w_gate, w_up, w_down, layer)
    ya = take(out, dest[0]).reshape(b, s, d)
    yb = take(out, dest[1]).reshape(b, s, d)
    return _combine_call(x, g, route, ya, yb, tm)


def _nsa_layer(x, gain, sc, sh, g, w_in, w_out, q_gain, k_gain, cmp_pe, cmp_w1, cmp_b1, cmp_w2, cmp_b2, ffn, tm):
    b, s, d = x.shape
    G, dh = NSA_KV_GROUPS, HEAD_DIM
    cos, sin = _rope_tables(jnp.arange(s, dtype=jnp.int32))
    q, cv, ks, vs, kw, vw, gates = _nsa_in_call(x, gain, sc, sh, w_in, q_gain, k_gain, cos, sin, tm)

    n_cmp = (s - CMP_BLOCK) // CMP_STRIDE + 1
    n_str = s // CMP_STRIDE
    cmp_pos = jnp.arange(n_str, dtype=jnp.int32) * CMP_STRIDE + (CMP_BLOCK - 1)
    ccos, csin = _rope_tables(cmp_pos)
    cmp = _compress_call(cv, cmp_pe, cmp_w1, cmp_b1, cmp_w2, cmp_b2, k_gain[0], ccos, csin)
    cmp = jnp.pad(cmp, ((0, 0), (0, 0), (0, LANES - n_str), (0, 0)))
    kc, vc = cmp[0], cmp[1]

    ns = s // SEL_BLOCK
    r_, u_ = SEL_BLOCK // CMP_STRIDE, CMP_BLOCK // CMP_STRIDE
    c_idx = (r_ * np.arange(ns)[:, None, None] + np.arange(r_)[None, :, None]
             + np.arange(u_)[None, None, :]).reshape(ns, -1)
    c2s = (c_idx[:, :, None] == np.arange(n_cmp)[None, None, :]).sum(1).astype(np.float32)
    c2s = jnp.asarray(np.pad(c2s, ((0, 0), (0, LANES - n_cmp))))

    o = _nsa_attn_call(q, gates, kc, vc, ks, vs, kw, vw, c2s, n_cmp)
    return _mixer_out_call(o, None, w_out[_head_pair_order(), :], x, g, ffn, tm)


def _mlstm_layer(x, gain, sc, sh, g, w_in, w_out, b_igate, b_fgate, norm_gain, ffn, tm):
    nh = MLSTM_HEADS
    dv = norm_gain.shape[-1]
    dqk = (w_in.shape[-1] - 2 * nh - 2 * nh * dv) // (2 * nh)
    q, k, v, og, gates = _mlstm_in_call(x, gain, sc, sh, w_in, dqk, dv, tm)
    hs = _mlstm_call(q, k, v, gates, b_igate, b_fgate, norm_gain)
    return _mixer_out_call(hs, og, w_out, x, g, ffn, tm)


def kernel(x, c, ada_w, ada_b, norm_mix_gain, norm_ffn_gain, nsa_w_in, nsa_w_out, nsa_q_gain, nsa_k_gain, nsa_cmp_pe, nsa_cmp_w1, nsa_cmp_b1, nsa_cmp_w2, nsa_cmp_b2, mlstm_w_in, mlstm_b_igate, mlstm_b_fgate, mlstm_norm_gain, mlstm_w_out, router_w, router_b, moe_w_gate, moe_w_up, moe_w_down):
    b, s, d = x.shape
    depth = ada_w.shape[0]
    tm = min(1024, s)
    mod = _mod_call(c, ada_w, ada_b)
    for i in range(depth):
        sh_m, sc_m, g_m, sh_f, sc_f, g_f = [mod[i, :, None, k * d:(k + 1) * d] for k in range(6)]
        j = i // 2
        ffn = (norm_ffn_gain[i], sc_f, sh_f, router_w, router_b)
        if i % 2 == 0:
            mixed = _nsa_layer(x, norm_mix_gain[i], sc_m, sh_m, g_m, nsa_w_in[j], nsa_w_out[j], nsa_q_gain[j],
                               nsa_k_gain[j], nsa_cmp_pe[j], nsa_cmp_w1[j], nsa_cmp_b1[j], nsa_cmp_w2[j],
                               nsa_cmp_b2[j], ffn, tm)
        else:
            mixed = _mlstm_layer(x, norm_mix_gain[i], sc_m, sh_m, g_m, mlstm_w_in[j], mlstm_w_out[j],
                                 mlstm_b_igate[j], mlstm_b_fgate[j], mlstm_norm_gain[j], ffn, tm)
        x = _moe_layer(*mixed, g_f, moe_w_gate, moe_w_up, moe_w_down, i, tm)
    return x
```

```python
import functools

import numpy as np
import jax
import jax.numpy as jnp
from jax import lax
from jax.experimental import pallas as pl
from jax.experimental.pallas import tpu as pltpu

F32 = jnp.float32
BF16 = jnp.bfloat16
HIGHEST = lax.Precision.HIGHEST

EPS = 1e-6
NEG = -1e30
BIG = 1e9
ROPE_THETA = 500000.0
LOG2E = 1.4426950408889634

NSA_HEADS = 16
NSA_KV_GROUPS = 2
NSA_HEADS_PER_GROUP = NSA_HEADS // NSA_KV_GROUPS
HEAD_DIM = 64
ROT_DIM = HEAD_DIM // 4
CMP_BLOCK = 32
CMP_STRIDE = 16
SEL_BLOCK = 64
SEL_TOPN = 8
WINDOW = 512
NSA_Q_BLOCK = 64
NSA_BRANCHES = 3
NSA_CALL_KEYS = 512
NSA_CHAINS_PER_STEP = 2
NSA_BLOCKS_PER_STEP = 2

MLSTM_HEADS = 4
MLSTM_CHUNK = 256
MLSTM_GROUP = 1
GATE_SOFTCAP = 15.0

N_EXPERTS = 32
N_GROUPS = 4
EXPERTS_PER_GROUP = N_EXPERTS // N_GROUPS
TOP_K = 2
MOE_ROWS = 512

LANES = 128
VMEM_LIMIT = 48 * 1024 * 1024


def _cparams(n_axes):
    return pltpu.CompilerParams(dimension_semantics=("arbitrary",) * n_axes,
                                vmem_limit_bytes=VMEM_LIMIT)


def _dot(a, b):
    return jnp.dot(a, b, preferred_element_type=F32)


def _dot_nt(a, b):
    return lax.dot_general(a, b, (((1,), (1,)), ((), ())), preferred_element_type=F32)


def _dot_tn(a, b):
    return lax.dot_general(a, b, (((0,), (0,)), ((), ())), preferred_element_type=F32)


def _norm_mod(x, gain, sc, sh):
    y = x * lax.rsqrt(jnp.mean(x * x, axis=-1, keepdims=True) + EPS) * gain
    return y * (1.0 + sc) + sh


def _half_norm_rope(x, gain, cos, sin):
    lane = lax.broadcasted_iota(jnp.int32, x.shape, x.ndim - 1)
    x2 = x * x
    left = lane < HEAD_DIM
    ss_l = jnp.sum(jnp.where(left, x2, 0.0), axis=-1, keepdims=True)
    ss_r = jnp.sum(jnp.where(left, 0.0, x2), axis=-1, keepdims=True)
    ms = jnp.where(left, ss_l, ss_r) * (1.0 / HEAD_DIM)
    y = x * lax.rsqrt(ms + EPS) * gain
    half = ROT_DIM // 2
    src = lax.broadcasted_iota(jnp.int32, (LANES, LANES), 0)
    dst = lax.broadcasted_iota(jnp.int32, (LANES, LANES), 1)
    dst_in_head = dst % HEAD_DIM
    pair = jnp.where(dst_in_head < half, dst + half, jnp.where(dst_in_head < ROT_DIM, dst - half, -1))
    partner = _dot(y.astype(BF16), jnp.where(src == pair, 1.0, 0.0).astype(BF16))
    return y * cos + partner * sin


def _rope_tables(pos):
    half = ROT_DIM // 2
    inv_freq = ROPE_THETA ** (-jnp.arange(half, dtype=F32) / half)
    ang = pos.astype(F32)[:, None] * inv_freq[None, :]
    cos, sin = jnp.cos(ang), jnp.sin(ang)
    n = pos.shape[0]
    one = jnp.ones((n, HEAD_DIM - ROT_DIM), F32)
    cos_h = jnp.concatenate([cos, cos, one], axis=-1)
    sin_h = jnp.concatenate([-sin, sin, 0.0 * one], axis=-1)
    return jnp.tile(cos_h, (1, 2)), jnp.tile(sin_h, (1, 2))


def _mod_kernel(c_ref, w_ref, b_ref, o_ref):
    c = c_ref[...]
    cond = c * jax.nn.sigmoid(c)
    o_ref[0] = jnp.dot(cond, w_ref[0], preferred_element_type=F32, precision=HIGHEST) + b_ref[0]


def _mod_call(c, ada_w, ada_b):
    depth, d, n = ada_w.shape
    b = c.shape[0]
    tn = n // 4
    return pl.pallas_call(
        _mod_kernel,
        out_shape=jax.ShapeDtypeStruct((depth, b, n), F32),
        grid=(depth, n // tn),
        in_specs=[pl.BlockSpec((b, d), lambda i, j: (0, 0)),
                  pl.BlockSpec((1, d, tn), lambda i, j: (i, 0, j)),
                  pl.BlockSpec((1, 1, tn), lambda i, j: (i, 0, j))],
        out_specs=pl.BlockSpec((1, b, tn), lambda i, j: (i, 0, j)),
        compiler_params=_cparams(2),
        name="adaln_mod",
    )(c, ada_w, ada_b.reshape(depth, 1, n))


def _nsa_in_kernel(x_ref, gain_ref, sc_ref, sh_ref, wq_ref, wkv_ref, wg_ref, qg_ref, kg_ref, cos_ref, sin_ref,
                   q_ref, cv_ref, ks_ref, vs_ref, kw_ref, vw_ref, g_ref):
    h = _norm_mod(x_ref[0], gain_ref[...], sc_ref[0], sh_ref[0]).astype(BF16)
    g_ref[0] = _dot(h, wg_ref[...])
    kv = _dot(h, wkv_ref[...])
    cos, sin = cos_ref[...], sin_ref[...]
    q = _dot(h, wq_ref[...])
    for r in range(NSA_HEADS_PER_GROUP):
        slab = _half_norm_rope(q[:, r * LANES:(r + 1) * LANES], qg_ref[...], cos, sin)
        q_ref[0, :, r * LANES:(r + 1) * LANES] = (slab * (HEAD_DIM ** -0.5 * LOG2E)).astype(BF16)
    cv_ref[0] = kv[:, 0:2 * LANES]
    ks_ref[0] = _half_norm_rope(kv[:, 2 * LANES:3 * LANES], kg_ref[1:2, :], cos, sin).astype(BF16)
    vs_ref[0] = kv[:, 3 * LANES:4 * LANES].astype(BF16)
    kw_ref[0] = _half_norm_rope(kv[:, 4 * LANES:5 * LANES], kg_ref[2:3, :], cos, sin).astype(BF16)
    vw_ref[0] = kv[:, 5 * LANES:6 * LANES].astype(BF16)


def _head_pair_order():
    r, g, dd = np.meshgrid(np.arange(NSA_HEADS_PER_GROUP), np.arange(NSA_KV_GROUPS), np.arange(HEAD_DIM),
                           indexing="ij")
    return ((g * NSA_HEADS_PER_GROUP + r) * HEAD_DIM + dd).reshape(-1)


def _nsa_in_call(x, gain, sc, sh, w_in, q_gain, k_gain, cos, sin, tm):
    b, s, d = x.shape
    nq = NSA_HEADS * HEAD_DIM
    nkv = 6 * LANES
    wq = w_in[:, :nq][:, _head_pair_order()].astype(BF16)
    qg = jnp.tile(q_gain, 2).reshape(1, LANES)
    wkv = w_in[:, nq:nq + nkv].astype(BF16)
    ng = NSA_BRANCHES * NSA_HEADS
    wg = jnp.pad(w_in[:, nq + nkv:], ((0, 0), (0, LANES - ng))).astype(BF16)
    kg = jnp.tile(k_gain, (1, 2))
    row = lambda i, j: (i, j, 0)
    per_b = lambda i, j: (i, 0, 0)
    const = lambda i, j: (0, 0)
    kv_out = lambda dt: jax.ShapeDtypeStruct((b, s, LANES), dt)
    return pl.pallas_call(
        _nsa_in_kernel,
        out_shape=(jax.ShapeDtypeStruct((b, s, nq), BF16), jax.ShapeDtypeStruct((b, s, 2 * LANES), F32),
                   kv_out(BF16), kv_out(BF16), kv_out(BF16), kv_out(BF16), kv_out(F32)),
        grid=(b, s // tm),
        in_specs=[pl.BlockSpec((1, tm, d), row),
                  pl.BlockSpec((1, d), const),
                  pl.BlockSpec((1, 1, d), per_b),
                  pl.BlockSpec((1, 1, d), per_b),
                  pl.BlockSpec((d, nq), const),
                  pl.BlockSpec((d, nkv), const),
                  pl.BlockSpec((d, LANES), const),
                  pl.BlockSpec((1, LANES), const),
                  pl.BlockSpec((3, LANES), const),
                  pl.BlockSpec((tm, LANES), lambda i, j: (j, 0)),
                  pl.BlockSpec((tm, LANES), lambda i, j: (j, 0))],
        out_specs=(pl.BlockSpec((1, tm, nq), row), pl.BlockSpec((1, tm, 2 * LANES), row))
        + (pl.BlockSpec((1, tm, LANES), row),) * 5,
        compiler_params=_cparams(2),
        name="nsa_in_proj",
    )(x, gain.reshape(1, d), sc, sh, wq, wkv, wg, qg, kg, cos, sin)


def _compress_kernel(a_ref, pe_ref, w1_ref, b1_ref, w2_ref, b2_ref, kg_ref, cos_ref, sin_ref, o_ref, *, n_str):
    is_key = pl.program_id(0) == 0
    hid2 = w1_ref.shape[-1]
    first = jnp.zeros((n_str, hid2), F32)
    second = jnp.zeros((n_str, hid2), F32)
    pe_term = jnp.zeros((8, hid2), F32)
    for l in range(CMP_STRIDE):
        rows = a_ref[0, pl.ds(l, n_str, stride=CMP_STRIDE), :].astype(BF16)
        first = first + _dot(rows, w1_ref[0, l])
        second = second + _dot(rows, w1_ref[0, CMP_STRIDE + l])
    for l in range(CMP_BLOCK):
        pe_term = pe_term + _dot(pe_ref[0, l].astype(BF16), w1_ref[0, l])
    hid = first + pltpu.roll(second, n_str - 1, 0) + pe_term[0:1] + b1_ref[0]
    hid = 0.5 * hid * (1.0 + jnp.tanh(np.sqrt(2.0 / np.pi) * (hid + 0.044715 * hid * hid * hid)))
    out = _dot(hid.astype(BF16), w2_ref[0]) + b2_ref[0]
    normed = _half_norm_rope(out, kg_ref[...], cos_ref[...], sin_ref[...])
    o_ref[0, 0] = jnp.where(is_key, normed, out).astype(o_ref.dtype)


def _block_diag2(w):
    z = jnp.zeros_like(w)
    return jnp.concatenate([jnp.concatenate([w, z], axis=-1), jnp.concatenate([z, w], axis=-1)], axis=-2)


def _compress_call(cv, pe, w1, b1, w2, b2, k_gain0, cos, sin):
    b, s, _ = cv.shape
    n_str = s // CMP_STRIDE
    hid = w1.shape[-1]
    w1bd = _block_diag2(w1.reshape(2, CMP_BLOCK, HEAD_DIM, hid)).astype(BF16)
    w2bd = _block_diag2(w2).astype(BF16)
    pe2 = jnp.broadcast_to(jnp.tile(pe, (1, 1, 2))[:, :, None, :], (2, CMP_BLOCK, 8, LANES))
    b1t = jnp.tile(b1, (1, 2)).reshape(2, 1, 2 * hid)
    b2t = jnp.tile(b2, (1, 2)).reshape(2, 1, LANES)
    kg = jnp.tile(k_gain0, 2).reshape(1, LANES)
    sel3 = lambda i, j: (i, 0, 0)
    sel4 = lambda i, j: (i, 0, 0, 0)
    const = lambda i, j: (0, 0)
    return pl.pallas_call(
        functools.partial(_compress_kernel, n_str=n_str),
        out_shape=jax.ShapeDtypeStruct((2, b, n_str, LANES), BF16),
        grid=(2, b),
        in_specs=[pl.BlockSpec((1, s, LANES), lambda i, j: (j, 0, i)),
                  pl.BlockSpec((1, CMP_BLOCK, 8, LANES), sel4),
                  pl.BlockSpec((1, CMP_BLOCK, LANES, 2 * hid), sel4),
                  pl.BlockSpec((1, 1, 2 * hid), sel3),
                  pl.BlockSpec((1, 2 * hid, LANES), sel3),
                  pl.BlockSpec((1, 1, LANES), sel3),
                  pl.BlockSpec((1, LANES), const),
                  pl.BlockSpec((n_str, LANES), const),
                  pl.BlockSpec((n_str, LANES), const)],
        out_specs=pl.BlockSpec((1, 1, n_str, LANES), lambda i, j: (i, j, 0, 0)),
        compiler_params=_cparams(2),
        name="nsa_compress",
    )(cv, pe2, w1bd, b1t, w2bd, b2t, kg, cos, sin)


def _attend(qb, k_ref, v_ref, k0, spans, bias):
    rows = qb.shape[0]
    q_rows = bias.shape[0] // NSA_KV_GROUPS
    per_group = rows // (NSA_KV_GROUPS * q_rows)
    bias = bias.reshape(NSA_KV_GROUPS, 1, q_rows, bias.shape[-1])
    m = acc = None
    for off, size in spans:
        k = k_ref[0, pl.ds(k0 + off, size), :]
        v = v_ref[0, pl.ds(k0 + off, size), :]
        s = _dot_nt(qb, k).reshape(NSA_KV_GROUPS, per_group, q_rows, size) + bias[..., off:off + size]
        s = s.reshape(rows, size)
        m_span = jnp.max(s, axis=-1, keepdims=True)
        m_new = m_span if m is None else jnp.maximum(m, m_span)
        p = jnp.exp2(s - m_new).astype(BF16)
        pv = _dot(p, jnp.concatenate([v, jnp.ones_like(v)], axis=1))
        acc = pv if m is None else acc * jnp.exp2(m - m_new) + pv
        m = m_new
    return acc


def _nsa_attn_kernel(*refs, n_inputs, **static):
    for sub in range(NSA_CHAINS_PER_STEP):
        _nsa_attn_block(sub, pl.program_id(1) * NSA_CHAINS_PER_STEP + sub, *refs[:n_inputs], refs[-1], **static)


def _nsa_attn_block(sub, step, q_ref, g_ref, kc_ref, vc_ref, ks_ref, vs_ref, kw_ref, vw_ref,
                    c2s_ref, exp_ref, o_ref, *, seq, n_cmp, n_top, win_keys, sel_span, n_keys, q_block0, q_rows):
    G, R, QB = NSA_KV_GROUPS, NSA_HEADS_PER_GROUP, q_rows
    rows = R * QB
    n_sel = seq // SEL_BLOCK
    qi0 = step * (q_rows // NSA_Q_BLOCK) + q_block0
    s0 = qi0 * NSA_Q_BLOCK
    tok = slice(sub * QB, (sub + 1) * QB)
    gt = jax.nn.sigmoid(g_ref[0, tok, :])
    lane = lax.broadcasted_iota(jnp.int32, (QB, LANES), 1)
    tq = s0 + lax.broadcasted_iota(jnp.int32, (rows, 1), 0) % QB
    tq1 = s0 + lax.broadcasted_iota(jnp.int32, (QB, 1), 0)
    ones_sq = jnp.ones((LANES, LANES), BF16)

    w0 = pl.multiple_of(jnp.maximum(s0 + QB - win_keys, 0), SEL_BLOCK)
    wpos = w0 + lax.broadcasted_iota(jnp.int32, (1, win_keys), 1)
    bias_w = jnp.where((wpos <= tq1) & (wpos > tq1 - WINDOW), 0.0, NEG)
    win_spans = [(off, min(3 * LANES, win_keys - off)) for off in range(0, win_keys, 3 * LANES)]

    zero = jnp.zeros((QB, LANES), BF16)
    qb = jnp.concatenate([jnp.where((lane // HEAD_DIM) == g, q_ref[0, tok, r * LANES:(r + 1) * LANES], zero)
                          for g in range(G) for r in range(R)], axis=0)
    tq_all = jnp.concatenate([tq] * G, axis=0)
    tq_tok = jnp.concatenate([tq1] * G, axis=0)

    sc = _dot_nt(qb, kc_ref[0])
    cpos = lax.broadcasted_iota(jnp.int32, (1, LANES), 1) * CMP_STRIDE + (CMP_BLOCK - 1)
    valid_c = (cpos <= tq_all) & (lax.broadcasted_iota(jnp.int32, (1, LANES), 1) < n_cmp)
    sc = jnp.where(valid_c, sc, NEG)
    e_c = jnp.exp2(sc - jnp.max(sc, axis=-1, keepdims=True)).astype(BF16)
    p_c = jnp.where(valid_c, e_c.astype(F32) / _dot(e_c, ones_sq), 0.0)
    o_c = _dot(p_c.astype(BF16), vc_ref[0])

    acc_w = _attend(qb, kw_ref, vw_ref, w0, win_spans, jnp.concatenate([bias_w] * G, axis=0))

    psum = jnp.sum(p_c.reshape(G, R, QB, LANES), axis=1).reshape(G * QB, LANES)
    imp = lax.dot_general(c2s_ref[...], psum, (((1,), (1,)), ((), ())),
                          preferred_element_type=F32, precision=HIGHEST)
    blk = lax.broadcasted_iota(jnp.int32, (n_sel, G * QB), 0)
    qi = qi0 + (lax.broadcasted_iota(jnp.int32, (1, G * QB), 1) % QB) // NSA_Q_BLOCK
    forced = (blk == 0) | (blk == qi) | (blk == qi - 1)
    imp = jnp.where(blk <= qi, jnp.where(forced, BIG, imp), -BIG)
    beaten = jnp.zeros(imp.shape, F32)
    for k in range(1, n_sel):
        other = pltpu.roll(imp, k, 0)
        beats = (other > imp) | ((blk >= k) & (other == imp))
        beaten = beaten + jnp.where(beats, 1.0, 0.0)
    chosen = jnp.where(beaten < n_top, 1.0, 0.0).astype(BF16)
    picked = _dot_tn(chosen, exp_ref[:, :n_keys])
    kpos = lax.broadcasted_iota(jnp.int32, (1, n_keys), 1)
    sel_spans = [(off, min(sel_span, n_keys - off)) for off in range(0, n_keys, sel_span)]

    bias_s = jnp.where((picked > 0.5) & (kpos <= tq_tok), 0.0, NEG)
    acc_s = _attend(qb, ks_ref, vs_ref, 0, sel_spans, bias_s)
    heads = []
    for h in range(G * R):
        rs = slice(h * QB, (h + 1) * QB)
        a_s, a_w = acc_s[rs], acc_w[rs]
        g_s = gt[:, NSA_HEADS + h:NSA_HEADS + h + 1] / a_s[:, LANES:LANES + 1]
        g_w = gt[:, 2 * NSA_HEADS + h:2 * NSA_HEADS + h + 1] / a_w[:, LANES:LANES + 1]
        heads.append(gt[:, h:h + 1] * o_c[rs] + g_s * a_s[:, :LANES] + g_w * a_w[:, :LANES])
    for r in range(R):
        slab = jnp.where(lane < HEAD_DIM, heads[r], heads[R + r])
        o_ref[0, tok, r * LANES:(r + 1) * LANES] = slab.astype(o_ref.dtype)


def _nsa_attn_call(q, gates, kc, vc, ks, vs, kw, vw, cmp_to_sel, n_cmp):
    b, s, nq = q.shape
    qb = NSA_Q_BLOCK
    n_top = min(SEL_TOPN, s // SEL_BLOCK)
    call_rows = min(NSA_CALL_KEYS, s)
    qrows = min(NSA_BLOCKS_PER_STEP * qb, call_rows)
    win_keys = min(-(-(WINDOW + qrows) // LANES) * LANES, s)
    sel_span = min(512, s)
    n_sel = s // SEL_BLOCK
    expand = jnp.asarray(np.arange(n_sel)[:, None] == (np.arange(s)[None, :] // SEL_BLOCK), BF16)
    per_b = lambda i, j: (i, 0, 0)
    const = lambda i, j: (0, 0)
    call_keys = call_rows
    step_rows = min(NSA_CHAINS_PER_STEP * qrows, call_keys)
    steps = call_keys // step_rows
    out = None
    for n in range(1, s // call_keys + 1):
        q0 = (n - 1) * (call_keys // qb)
        n_keys = n * call_keys
        row = lambda i, j, t0=(n - 1) * steps: (i, j + t0, 0)
        kern = functools.partial(_nsa_attn_kernel, n_inputs=10, seq=s, n_cmp=n_cmp, n_top=n_top,
                                 win_keys=win_keys, sel_span=sel_span, n_keys=n_keys, q_block0=q0, q_rows=qrows)
        carried = [] if out is None else [out]
        out = pl.pallas_call(
            kern,
            out_shape=jax.ShapeDtypeStruct((b, s, nq), BF16),
            grid=(b, steps),
            in_specs=[pl.BlockSpec((1, step_rows, nq), row),
                      pl.BlockSpec((1, step_rows, LANES), row),
                      pl.BlockSpec((1, LANES, LANES), per_b),
                      pl.BlockSpec((1, LANES, LANES), per_b),
                      pl.BlockSpec((1, n_keys, LANES), per_b),
                      pl.BlockSpec((1, n_keys, LANES), per_b),
                      pl.BlockSpec((1, s, LANES), per_b),
                      pl.BlockSpec((1, s, LANES), per_b),
                      pl.BlockSpec((n_sel, LANES), const),
                      pl.BlockSpec((n_sel, s), const)] + [pl.BlockSpec(memory_space=pl.ANY)] * len(carried),
            out_specs=pl.BlockSpec((1, step_rows, nq), row),
            input_output_aliases={10: 0} if carried else {},
            compiler_params=_cparams(2),
            name="nsa_attention",
        )(q, gates, kc, vc, ks, vs, kw, vw, cmp_to_sel, expand, *carried)
    return out


def _mixer_out_kernel(*refs, gated):
    a_ref, refs = refs[0], refs[1:]
    lhs = a_ref[0]
    if gated:
        og_ref, refs = refs[0], refs[1:]
        lhs = (jax.nn.sigmoid(og_ref[0].astype(F32)) * lhs.astype(F32)).astype(BF16)
    (w_ref, x_ref, g_ref, gain_ref, sc_ref, sh_ref, wh_ref, wl_ref, rb_ref,
     xo_ref, h_ref, route_ref, route_t_ref, cnt_ref) = refs
    x_new = x_ref[0] + g_ref[0] * _dot(lhs, w_ref[...])
    xo_ref[0] = x_new
    h_ref[0], route_ref[0], route_t_ref[...], cnt_ref[0] = _route_tile(
        x_new, gain_ref[...], sc_ref[0], sh_ref[0], wh_ref[...], wl_ref[...], rb_ref[...])


def _mixer_out_call(a, og, w_out, x, g, ffn, tm):
    gain, sc, sh, router_w, router_b = ffn
    b, s, d = x.shape
    k = a.shape[-1]
    nt = s // tm
    wp = jnp.pad(router_w, ((0, 0), (0, LANES - N_EXPERTS)))
    wh = wp.astype(BF16)
    wl = (wp - wh.astype(F32)).astype(BF16)
    rb = router_b.astype(F32).reshape(N_EXPERTS, 1)
    row = lambda i, j: (i, j, 0)
    per_b = lambda i, j: (i, 0, 0)
    const = lambda i, j: (0, 0)
    a_spec = pl.BlockSpec((1, tm, k), row)
    acts = [a] if og is None else [a, og]
    return pl.pallas_call(
        functools.partial(_mixer_out_kernel, gated=og is not None),
        out_shape=(jax.ShapeDtypeStruct((b, s, d), F32), jax.ShapeDtypeStruct((b, s, d), BF16),
                   jax.ShapeDtypeStruct((b, s, LANES), F32), jax.ShapeDtypeStruct((8, b * s), F32),
                   jax.ShapeDtypeStruct((b * nt, N_EXPERTS, LANES), F32)),
        grid=(b, nt),
        in_specs=[a_spec] * len(acts) + [pl.BlockSpec((k, d), const),
                                         pl.BlockSpec((1, tm, d), row),
                                         pl.BlockSpec((1, 1, d), per_b),
                                         pl.BlockSpec((1, d), const),
                                         pl.BlockSpec((1, 1, d), per_b),
                                         pl.BlockSpec((1, 1, d), per_b),
                                         pl.BlockSpec((d, LANES), const),
                                         pl.BlockSpec((d, LANES), const),
                                         pl.BlockSpec((N_EXPERTS, 1), const)],
        out_specs=(pl.BlockSpec((1, tm, d), row), pl.BlockSpec((1, tm, d), row), pl.BlockSpec((1, tm, LANES), row),
                   pl.BlockSpec((8, tm), lambda i, j: (0, i * nt + j)),
                   pl.BlockSpec((1, N_EXPERTS, LANES), lambda i, j: (i * nt + j, 0, 0))),
        compiler_params=_cparams(2),
        name="mixer_out_router",
    )(*acts, w_out.astype(BF16), x, g, gain.reshape(1, d), sc, sh, wh, wl, rb)


def _mlstm_in_kernel(x_ref, gain_ref, sc_ref, sh_ref, wq_ref, wk_ref, wv_ref, wo_ref, wg_ref,
                     q_ref, k_ref, v_ref, o_ref, g_ref):
    h = _norm_mod(x_ref[0], gain_ref[...], sc_ref[0], sh_ref[0]).astype(BF16)
    q_ref[0] = _dot(h, wq_ref[...]).astype(BF16)
    k_ref[0] = _dot(h, wk_ref[...]).astype(BF16)
    v_ref[0] = _dot(h, wv_ref[...]).astype(BF16)
    o_ref[0] = _dot(h, wo_ref[...]).astype(BF16)
    g_ref[0] = _dot(h, wg_ref[...])


def _mlstm_in_call(x, gain, sc, sh, w_in, dqk, dv, tm):
    b, s, d = x.shape
    nh = MLSTM_HEADS
    sizes = [nh * dqk, nh * dqk, nh * dv, nh * dv]
    offs = np.cumsum([0] + sizes)
    ws = [w_in[:, offs[i]:offs[i + 1]].astype(BF16) for i in range(4)]
    wg = jnp.pad(w_in[:, offs[4]:], ((0, 0), (0, LANES - 2 * nh))).astype(BF16)
    row = lambda i, j: (i, j, 0)
    per_b = lambda i, j: (i, 0, 0)
    const = lambda i, j: (0, 0)
    widths = sizes + [LANES]
    return pl.pallas_call(
        _mlstm_in_kernel,
        out_shape=tuple(jax.ShapeDtypeStruct((b, s, n), BF16) for n in sizes)
        + (jax.ShapeDtypeStruct((b, s, LANES), F32),),
        grid=(b, s // tm),
        in_specs=[pl.BlockSpec((1, tm, d), row),
                  pl.BlockSpec((1, d), const),
                  pl.BlockSpec((1, 1, d), per_b),
                  pl.BlockSpec((1, 1, d), per_b)] + [pl.BlockSpec((d, n), const) for n in widths],
        out_specs=tuple(pl.BlockSpec((1, tm, n), row) for n in widths),
        compiler_params=_cparams(2),
        name="mlstm_in_proj",
    )(x, gain.reshape(1, d), sc, sh, *ws, wg)


def _softcap(a):
    return GATE_SOFTCAP * jnp.tanh(a / GATE_SOFTCAP)


def _mlstm_kernel(bi_ref, bf_ref, q_ref, k_ref, v_ref, g_ref, gain_ref, o_ref,
                  li_s, b_s, *, n_chunks, n_heads, dqk, dv):
    L = MLSTM_CHUNK
    r_i = lax.broadcasted_iota(jnp.int32, (L, L), 0)
    c_i = lax.broadcasted_iota(jnp.int32, (L, L), 1)
    upper = jnp.where(r_i <= c_i, 1.0, 0.0)
    for h in range(n_heads):
        li_s[h] = _softcap(g_ref[0, h] + bi_ref[h])
        fa = _softcap(g_ref[0, n_heads + h] + bf_ref[h])
        lf = jnp.minimum(fa, 0.0) - jnp.log1p(jnp.exp(-jnp.abs(fa)))
        b_s[h] = jnp.dot(lf, upper, preferred_element_type=F32, precision=HIGHEST)
    eye = r_i == c_i
    causal = r_i >= c_i
    k_scale = dqk ** -0.5

    def to_col(row):
        return jnp.sum(jnp.where(eye, jnp.broadcast_to(row, (L, L)), 0.0), axis=1, keepdims=True)

    def local_part(h, r0, c):
        qcb = q_ref[0, pl.ds(r0, L), h * dqk:(h + 1) * dqk]
        kc = k_ref[0, pl.ds(r0, L), h * dqk:(h + 1) * dqk].astype(F32) * k_scale
        vc = v_ref[0, pl.ds(r0, L), h * dv:(h + 1) * dv]
        b_row = b_s[h, pl.ds(c, 1), :]
        li_row = li_s[h, pl.ds(c, 1), :]
        b_col, li_col = to_col(b_row), to_col(li_row)
        b_last = b_row[:, L - 1:L]
        dmat = jnp.where(causal, b_col - b_row + li_row, NEG)
        m_loc = jnp.max(dmat, axis=-1, keepdims=True)
        a_loc = jnp.exp(dmat - m_loc) * _dot_nt(qcb, kc.astype(BF16))
        num_loc = _dot(a_loc.astype(BF16), vc)
        den_loc = jnp.sum(a_loc, axis=-1, keepdims=True)
        g_max = m_loc[L - 1:L, :]
        kw = kc * jnp.exp(b_last - b_col + li_col - g_max)
        kv = _dot_tn(kw.astype(BF16), vc)
        kn = jnp.sum(kw, axis=0, keepdims=True)
        return qcb, b_col, b_last, m_loc, num_loc, den_loc, g_max, kv, kn

    def body(grp, carry):
        r0 = pl.multiple_of(grp * (MLSTM_GROUP * L), MLSTM_GROUP * L)
        parts = [[local_part(h, r0 + j * L, grp * MLSTM_GROUP + j) for j in range(MLSTM_GROUP)]
                 for h in range(n_heads)]
        new_carry, outs = [], []
        for h in range(n_heads):
            state, n_row, m_prev = carry[h]
            gain = gain_ref[h]
            head_out = []
            for qcb, b_col, b_last, m_loc, num_loc, den_loc, g_max, kv, kn in parts[h]:
                m_inter = b_col + m_prev
                m_t = jnp.maximum(m_inter, m_loc)
                intra = jnp.exp(m_loc - m_t)
                inter = jnp.exp(m_inter - m_t)
                num = intra * num_loc + inter * _dot(qcb, state.astype(BF16))
                den = intra * den_loc + inter * jnp.sum(qcb.astype(F32) * n_row, axis=-1, keepdims=True)
                h_out = num / jnp.maximum(jnp.abs(den), jnp.exp(-m_t))
                hs = h_out * lax.rsqrt(jnp.mean(h_out * h_out, axis=-1, keepdims=True) + EPS) * gain
                head_out.append(hs.astype(o_ref.dtype))
                m_new = jnp.maximum(b_last + m_prev, g_max)
                decay = jnp.exp(b_last + m_prev - m_new)
                grow = jnp.exp(g_max - m_new)
                state, n_row, m_prev = decay * state + grow * kv, decay * n_row + grow * kn, m_new
            new_carry.append((state, n_row, m_prev))
            outs.append(jnp.concatenate(head_out, axis=0))
        for h in range(n_heads):
            o_ref[0, pl.ds(r0, MLSTM_GROUP * L), h * dv:(h + 1) * dv] = outs[h]
        return tuple(new_carry)

    init = tuple((jnp.zeros((dqk, dv), F32), jnp.zeros((1, dqk), F32), jnp.zeros((1, 1), F32))
                 for _ in range(n_heads))
    lax.fori_loop(0, n_chunks // MLSTM_GROUP, body, init)


def _mlstm_call(q, k, v, gates, b_igate, b_fgate, norm_gain):
    b, s, _ = q.shape
    nh = MLSTM_HEADS
    dqk, dv = q.shape[-1] // nh, v.shape[-1] // nh
    L = MLSTM_CHUNK
    nch = s // L
    g = jnp.transpose(gates[..., :2 * nh], (0, 2, 1)).reshape(b, 2 * nh, nch, L)
    smem = pl.BlockSpec(memory_space=pltpu.SMEM)
    per_b = lambda n: pl.BlockSpec((1, s, n), lambda i: (i, 0, 0))
    kern = functools.partial(_mlstm_kernel, n_chunks=nch, n_heads=nh, dqk=dqk, dv=dv)
    return pl.pallas_call(
        kern,
        out_shape=jax.ShapeDtypeStruct((b, s, nh * dv), BF16),
        grid=(b,),
        in_specs=[smem, smem, per_b(nh * dqk), per_b(nh * dqk), per_b(nh * dv),
                  pl.BlockSpec((1, 2 * nh, nch, L), lambda i: (i, 0, 0, 0)),
                  pl.BlockSpec((nh, 1, dv), lambda i: (0, 0, 0))],
        out_specs=per_b(nh * dv),
        scratch_shapes=[pltpu.VMEM((nh, nch, L), F32), pltpu.VMEM((nh, nch, L), F32)],
        compiler_params=_cparams(1),
        name="mlstm_chunk_scan",
    )(b_igate, b_fgate, q, k, v, g, norm_gain.reshape(nh, 1, dv))


def _route_tile(x, gain, sc, sh, w_hi, w_lo, router_bias):
    h = _norm_mod(x, gain, sc, sh)
    hi = h.astype(BF16)
    lo = (h - hi.astype(F32)).astype(BF16)
    logits = _dot(hi, w_hi) + (_dot(lo, w_hi) + _dot(hi, w_lo))
    tm = logits.shape[0]
    aff = jax.nn.sigmoid(logits.T[:N_EXPERTS])
    choice = (aff + router_bias).reshape(N_GROUPS, EXPERTS_PER_GROUP, tm)
    local = lax.broadcasted_iota(jnp.int32, choice.shape, 1)

    def first_max(v):
        m = jnp.max(v, axis=1, keepdims=True)
        return m, jnp.min(jnp.where(v == m, local, EXPERTS_PER_GROUP), axis=1, keepdims=True)

    m1, i1 = first_max(choice)
    m2, i2 = first_max(jnp.where(local == i1, -jnp.inf, choice))
    score = m1 + m2
    best, e0, e1 = score[0], i1[0], i2[0]
    for g in range(1, N_GROUPS):
        better = score[g] > best
        best = jnp.where(better, score[g], best)
        e0 = jnp.where(better, i1[g] + g * EXPERTS_PER_GROUP, e0)
        e1 = jnp.where(better, i2[g] + g * EXPERTS_PER_GROUP, e1)
    expert = lax.broadcasted_iota(jnp.int32, (N_EXPERTS, tm), 0)
    is0, is1 = expert == e0, expert == e1
    a0 = jnp.sum(jnp.where(is0, aff, 0.0), axis=0, keepdims=True)
    a1 = jnp.sum(jnp.where(is1, aff, 0.0), axis=0, keepdims=True)
    tot = a0 + a1
    onehot = jnp.where(is0 | is1, 1.0, 0.0).astype(BF16)
    r_i = lax.broadcasted_iota(jnp.int32, (tm, tm), 0)
    c_i = lax.broadcasted_iota(jnp.int32, (tm, tm), 1)
    running = _dot(onehot, jnp.where(r_i <= c_i, 1.0, 0.0).astype(BF16))
    r0 = jnp.sum(jnp.where(is0, running, 0.0), axis=0, keepdims=True) - 1.0
    r1 = jnp.sum(jnp.where(is1, running, 0.0), axis=0, keepdims=True) - 1.0
    rows = [e0.astype(F32), e1.astype(F32), r0, r1, a0 / tot, a1 / tot]
    packed = jnp.concatenate(rows + [jnp.zeros((LANES - len(rows), tm), F32)], axis=0)
    return hi, packed.T, packed[:8], jnp.broadcast_to(running[:, tm - 1:tm], (N_EXPERTS, LANES))


def _expert_kernel(be_ref, live_ref, x_ref, wg_ref, wu_ref, wd_ref, o_ref, wg_s, wu_s, wd_s):
    i = pl.program_id(0)
    fresh = (i == 0) | (be_ref[i] != be_ref[jnp.maximum(i - 1, 0)])

    @pl.when(fresh)
    def _():
        wg_s[...] = wg_ref[0, 0].astype(BF16)
        wu_s[...] = wu_ref[0, 0].astype(BF16)
        wd_s[...] = wd_ref[0, 0].astype(BF16)

    @pl.when(live_ref[i] == 1)
    def _():
        xb = x_ref[...]
        gate = _dot(xb, wg_s[...])
        hid = gate * jax.nn.sigmoid(gate) * _dot(xb, wu_s[...])
        o_ref[...] = _dot(hid.astype(BF16), wd_s[...]).astype(o_ref.dtype)

    @pl.when(live_ref[i] == 0)
    def _():
        o_ref[...] = jnp.zeros_like(o_ref)


def _expert_call(block_expert, block_live, xs, w_gate, w_up, w_down, layer):
    p, d = xs.shape
    de = w_gate.shape[-1]
    nb = p // MOE_ROWS
    grid_spec = pltpu.PrefetchScalarGridSpec(
        num_scalar_prefetch=2,
        grid=(nb,),
        in_specs=[pl.BlockSpec((MOE_ROWS, d), lambda i, be, lv: (i, 0)),
                  pl.BlockSpec((1, 1, d, de), lambda i, be, lv: (layer, be[i], 0, 0)),
                  pl.BlockSpec((1, 1, d, de), lambda i, be, lv: (layer, be[i], 0, 0)),
                  pl.BlockSpec((1, 1, de, d), lambda i, be, lv: (layer, be[i], 0, 0))],
        out_specs=pl.BlockSpec((MOE_ROWS, d), lambda i, be, lv: (i, 0)),
        scratch_shapes=[pltpu.VMEM((d, de), BF16), pltpu.VMEM((d, de), BF16), pltpu.VMEM((de, d), BF16)],
    )
    return pl.pallas_call(
        _expert_kernel,
        out_shape=jax.ShapeDtypeStruct((p, d), BF16),
        grid_spec=grid_spec,
        compiler_params=_cparams(1),
        name="moe_experts",
    )(block_expert, block_live, xs, w_gate, w_up, w_down)


def _combine_kernel(x_ref, g_ref, route_ref, ya_ref, yb_ref, o_ref):
    w = route_ref[0][:, 2 * TOP_K:3 * TOP_K]
    y = w[:, 0:1] * ya_ref[0].astype(F32) + w[:, 1:2] * yb_ref[0].astype(F32)
    o_ref[0] = x_ref[0] + g_ref[0] * y


def _combine_call(x, g, route, ya, yb, tm):
    b, s, d = x.shape
    row = lambda i, j: (i, j, 0)
    spec = pl.BlockSpec((1, tm, d), row)
    return pl.pallas_call(
        _combine_kernel,
        out_shape=jax.ShapeDtypeStruct((b, s, d), F32),
        grid=(b, s // tm),
        in_specs=[spec, pl.BlockSpec((1, 1, d), lambda i, j: (i, 0, 0)),
                  pl.BlockSpec((1, tm, LANES), row), spec, spec],
        out_specs=spec,
        compiler_params=_cparams(2),
        name="moe_combine",
    )(x, g, route, ya, yb)


def _row_layout(route_t, cnt, tm):
    t = route_t.shape[1]
    nt = t // tm
    tile_cnt = cnt[:, :, 0].astype(jnp.int32)
    tile_off = jnp.cumsum(tile_cnt, axis=0) - tile_cnt
    counts = jnp.sum(tile_cnt, axis=0)
    padded = (counts + MOE_ROWS - 1) // MOE_ROWS * MOE_ROWS
    p_ends = jnp.cumsum(padded)
    base = (p_ends - padded)[None, :] + tile_off
    base_tok = jnp.broadcast_to(base.T[:, :, None], (N_EXPERTS, nt, tm)).reshape(N_EXPERTS, t)
    ids = route_t[:2 * TOP_K].astype(jnp.int32)
    experts = jnp.arange(N_EXPERTS, dtype=jnp.int32)[:, None]
    dest = [jnp.sum(jnp.where(ids[k][None, :] == experts, base_tok, 0), axis=0) + ids[TOP_K + k]
            for k in range(TOP_K)]
    nb = (t * TOP_K) // MOE_ROWS + N_EXPERTS
    tok = jnp.arange(t, dtype=jnp.int32)
    buf_tok = (jnp.arange(nb * MOE_ROWS, dtype=jnp.int32) % t).at[jnp.concatenate(dest)].set(
        jnp.concatenate([tok] * TOP_K), mode="promise_in_bounds", unique_indices=True)
    block_start = jnp.arange(nb, dtype=jnp.int32) * MOE_ROWS
    block_expert = jnp.minimum(jnp.sum((p_ends[None, :] <= block_start[:, None]).astype(jnp.int32), axis=-1),
                               N_EXPERTS - 1)
    block_live = (block_start < p_ends[-1]).astype(jnp.int32)
    return dest, buf_tok, block_expert, block_live


def _moe_layer(x, hf, route, route_t, cnt, g, w_gate, w_up, w_down, layer, tm):
    b, s, d = x.shape
    t = b * s
    dest, buf_tok, block_expert, block_live = _row_layout(route_t, cnt, tm)
    take = lambda a, idx: a.at[idx].get(mode="promise_in_bounds")
    xs = take(hf.reshape(t, d), buf_tok)
    out = _expert_call(block_expert, block_live, xs, w_gate, w_up, w_down, layer)
    ya = take(out, dest[0]).reshape(b, s, d)
    yb = take(out, dest[1]).reshape(b, s, d)
    return _combine_call(x, g, route, ya, yb, tm)


def _nsa_layer(x, gain, sc, sh, g, w_in, w_out, q_gain, k_gain, cmp_pe, cmp_w1, cmp_b1, cmp_w2, cmp_b2, ffn, tm):
    b, s, d = x.shape
    G, dh = NSA_KV_GROUPS, HEAD_DIM
    cos, sin = _rope_tables(jnp.arange(s, dtype=jnp.int32))
    q, cv, ks, vs, kw, vw, gates = _nsa_in_call(x, gain, sc, sh, w_in, q_gain, k_gain, cos, sin, tm)

    n_cmp = (s - CMP_BLOCK) // CMP_STRIDE + 1
    n_str = s // CMP_STRIDE
    cmp_pos = jnp.arange(n_str, dtype=jnp.int32) * CMP_STRIDE + (CMP_BLOCK - 1)
    ccos, csin = _rope_tables(cmp_pos)
    cmp = _compress_call(cv, cmp_pe, cmp_w1, cmp_b1, cmp_w2, cmp_b2, k_gain[0], ccos, csin)
    cmp = jnp.pad(cmp, ((0, 0), (0, 0), (0, LANES - n_str), (0, 0)))
    kc, vc = cmp[0], cmp[1]

    ns = s // SEL_BLOCK
    r_, u_ = SEL_BLOCK // CMP_STRIDE, CMP_BLOCK // CMP_STRIDE
    c_idx = (r_ * np.arange(ns)[:, None, None] + np.arange(r_)[None, :, None]
             + np.arange(u_)[None, None, :]).reshape(ns, -1)
    c2s = (c_idx[:, :, None] == np.arange(n_cmp)[None, None, :]).sum(1).astype(np.float32)
    c2s = jnp.asarray(np.pad(c2s, ((0, 0), (0, LANES - n_cmp))))

    o = _nsa_attn_call(q, gates, kc, vc, ks, vs, kw, vw, c2s, n_cmp)
    return _mixer_out_call(o, None, w_out[_head_pair_order(), :], x, g, ffn, tm)


def _mlstm_layer(x, gain, sc, sh, g, w_in, w_out, b_igate, b_fgate, norm_gain, ffn, tm):
    nh = MLSTM_HEADS
    dv = norm_gain.shape[-1]
    dqk = (w_in.shape[-1] - 2 * nh - 2 * nh * dv) // (2 * nh)
    q, k, v, og, gates = _mlstm_in_call(x, gain, sc, sh, w_in, dqk, dv, tm)
    hs = _mlstm_call(q, k, v, gates, b_igate, b_fgate, norm_gain)
    return _mixer_out_call(hs, og, w_out, x, g, ffn, tm)


def kernel(x, c, ada_w, ada_b, norm_mix_gain, norm_ffn_gain, nsa_w_in, nsa_w_out, nsa_q_gain, nsa_k_gain, nsa_cmp_pe, nsa_cmp_w1, nsa_cmp_b1, nsa_cmp_w2, nsa_cmp_b2, mlstm_w_in, mlstm_b_igate, mlstm_b_fgate, mlstm_norm_gain, mlstm_w_out, router_w, router_b, moe_w_gate, moe_w_up, moe_w_down):
    b, s, d = x.shape
    depth = ada_w.shape[0]
    tm = min(1024, s)
    mod = _mod_call(c, ada_w, ada_b)
    for i in range(depth):
        sh_m, sc_m, g_m, sh_f, sc_f, g_f = [mod[i, :, None, k * d:(k + 1) * d] for k in range(6)]
        j = i // 2
        ffn = (norm_ffn_gain[i], sc_f, sh_f, router_w, router_b)
        if i % 2 == 0:
            mixed = _nsa_layer(x, norm_mix_gain[i], sc_m, sh_m, g_m, nsa_w_in[j], nsa_w_out[j], nsa_q_gain[j],
                               nsa_k_gain[j], nsa_cmp_pe[j], nsa_cmp_w1[j], nsa_cmp_b1[j], nsa_cmp_w2[j],
                               nsa_cmp_b2[j], ffn, tm)
        else:
            mixed = _mlstm_layer(x, norm_mix_gain[i], sc_m, sh_m, g_m, mlstm_w_in[j], mlstm_w_out[j],
                                 mlstm_b_igate[j], mlstm_b_fgate[j], mlstm_norm_gain[j], ffn, tm)
        x = _moe_layer(*mixed, g_f, moe_w_gate, moe_w_up, moe_w_down, i, tm)
    return x
```

```python
import functools

import numpy as np
import jax
import jax.numpy as jnp
from jax import lax
from jax.experimental import pallas as pl
from jax.experimental.pallas import tpu as pltpu

F32 = jnp.float32
BF16 = jnp.bfloat16
HIGHEST = lax.Precision.HIGHEST

EPS = 1e-6
NEG = -1e30
BIG = 1e9
ROPE_THETA = 500000.0
LOG2E = 1.4426950408889634

NSA_HEADS = 16
NSA_KV_GROUPS = 2
NSA_HEADS_PER_GROUP = NSA_HEADS // NSA_KV_GROUPS
HEAD_DIM = 64
ROT_DIM = HEAD_DIM // 4
CMP_BLOCK = 32
CMP_STRIDE = 16
SEL_BLOCK = 64
SEL_TOPN = 8
WINDOW = 512
NSA_Q_BLOCK = 64
NSA_BRANCHES = 3
NSA_CALL_KEYS = 512
NSA_CHAINS_PER_STEP = 2
NSA_BLOCKS_PER_STEP = 2

MLSTM_HEADS = 4
MLSTM_CHUNK = 256
MLSTM_GROUP = 1
GATE_SOFTCAP = 15.0

N_EXPERTS = 32
N_GROUPS = 4
EXPERTS_PER_GROUP = N_EXPERTS // N_GROUPS
TOP_K = 2
MOE_ROWS = 512

LANES = 128
VMEM_LIMIT = 48 * 1024 * 1024


def _cparams(n_axes):
    return pltpu.CompilerParams(dimension_semantics=("arbitrary",) * n_axes,
                                vmem_limit_bytes=VMEM_LIMIT)


def _dot(a, b):
    return jnp.dot(a, b, preferred_element_type=F32)


def _dot_nt(a, b):
    return lax.dot_general(a, b, (((1,), (1,)), ((), ())), preferred_element_type=F32)


def _dot_tn(a, b):
    return lax.dot_general(a, b, (((0,), (0,)), ((), ())), preferred_element_type=F32)


def _norm_mod(x, gain, sc, sh):
    y = x * lax.rsqrt(jnp.mean(x * x, axis=-1, keepdims=True) + EPS) * gain
    return y * (1.0 + sc) + sh


def _half_norm_rope(x, gain, cos, sin):
    lane = lax.broadcasted_iota(jnp.int32, x.shape, x.ndim - 1)
    x2 = x * x
    left = lane < HEAD_DIM
    ss_l = jnp.sum(jnp.where(left, x2, 0.0), axis=-1, keepdims=True)
    ss_r = jnp.sum(jnp.where(left, 0.0, x2), axis=-1, keepdims=True)
    ms = jnp.where(left, ss_l, ss_r) * (1.0 / HEAD_DIM)
    y = x * lax.rsqrt(ms + EPS) * gain
    half = ROT_DIM // 2
    src = lax.broadcasted_iota(jnp.int32, (LANES, LANES), 0)
    dst = lax.broadcasted_iota(jnp.int32, (LANES, LANES), 1)
    dst_in_head = dst % HEAD_DIM
    pair = jnp.where(dst_in_head < half, dst + half, jnp.where(dst_in_head < ROT_DIM, dst - half, -1))
    partner = _dot(y.astype(BF16), jnp.where(src == pair, 1.0, 0.0).astype(BF16))
    return y * cos + partner * sin


def _rope_tables(pos):
    half = ROT_DIM // 2
    inv_freq = ROPE_THETA ** (-jnp.arange(half, dtype=F32) / half)
    ang = pos.astype(F32)[:, None] * inv_freq[None, :]
    cos, sin = jnp.cos(ang), jnp.sin(ang)
    n = pos.shape[0]
    one = jnp.ones((n, HEAD_DIM - ROT_DIM), F32)
    cos_h = jnp.concatenate([cos, cos, one], axis=-1)
    sin_h = jnp.concatenate([-sin, sin, 0.0 * one], axis=-1)
    return jnp.tile(cos_h, (1, 2)), jnp.tile(sin_h, (1, 2))


def _mod_kernel(c_ref, w_ref, b_ref, o_ref):
    c = c_ref[...]
    cond = c * jax.nn.sigmoid(c)
    o_ref[0] = jnp.dot(cond, w_ref[0], preferred_element_type=F32, precision=HIGHEST) + b_ref[0]


def _mod_call(c, ada_w, ada_b):
    depth, d, n = ada_w.shape
    b = c.shape[0]
    tn = n // 4
    return pl.pallas_call(
        _mod_kernel,
        out_shape=jax.ShapeDtypeStruct((depth, b, n), F32),
        grid=(depth, n // tn),
        in_specs=[pl.BlockSpec((b, d), lambda i, j: (0, 0)),
                  pl.BlockSpec((1, d, tn), lambda i, j: (i, 0, j)),
                  pl.BlockSpec((1, 1, tn), lambda i, j: (i, 0, j))],
        out_specs=pl.BlockSpec((1, b, tn), lambda i, j: (i, 0, j)),
        compiler_params=_cparams(2),
        name="adaln_mod",
    )(c, ada_w, ada_b.reshape(depth, 1, n))


def _nsa_in_kernel(x_ref, gain_ref, sc_ref, sh_ref, wq_ref, wkv_ref, wg_ref, qg_ref, kg_ref, cos_ref, sin_ref,
                   q_ref, cv_ref, ks_ref, vs_ref, kw_ref, vw_ref, g_ref):
    h = _norm_mod(x_ref[0], gain_ref[...], sc_ref[0], sh_ref[0]).astype(BF16)
    g_ref[0] = _dot(h, wg_ref[...])
    kv = _dot(h, wkv_ref[...])
    cos, sin = cos_ref[...], sin_ref[...]
    q = _dot(h, wq_ref[...])
    for r in range(NSA_HEADS_PER_GROUP):
        slab = _half_norm_rope(q[:, r * LANES:(r + 1) * LANES], qg_ref[...], cos, sin)
        q_ref[0, :, r * LANES:(r + 1) * LANES] = (slab * (HEAD_DIM ** -0.5 * LOG2E)).astype(BF16)
    cv_ref[0] = kv[:, 0:2 * LANES]
    ks_ref[0] = _half_norm_rope(kv[:, 2 * LANES:3 * LANES], kg_ref[1:2, :], cos, sin).astype(BF16)
    vs_ref[0] = kv[:, 3 * LANES:4 * LANES].astype(BF16)
    kw_ref[0] = _half_norm_rope(kv[:, 4 * LANES:5 * LANES], kg_ref[2:3, :], cos, sin).astype(BF16)
    vw_ref[0] = kv[:, 5 * LANES:6 * LANES].astype(BF16)


def _head_pair_order():
    r, g, dd = np.meshgrid(np.arange(NSA_HEADS_PER_GROUP), np.arange(NSA_KV_GROUPS), np.arange(HEAD_DIM),
                           indexing="ij")
    return ((g * NSA_HEADS_PER_GROUP + r) * HEAD_DIM + dd).reshape(-1)


def _nsa_in_call(x, gain, sc, sh, w_in, q_gain, k_gain, cos, sin, tm):
    b, s, d = x.shape
    nq = NSA_HEADS * HEAD_DIM
    nkv = 6 * LANES
    wq = w_in[:, :nq][:, _head_pair_order()].astype(BF16)
    qg = jnp.tile(q_gain, 2).reshape(1, LANES)
    wkv = w_in[:, nq:nq + nkv].astype(BF16)
    ng = NSA_BRANCHES * NSA_HEADS
    wg = jnp.pad(w_in[:, nq + nkv:], ((0, 0), (0, LANES - ng))).astype(BF16)
    kg = jnp.tile(k_gain, (1, 2))
    row = lambda i, j: (i, j, 0)
    per_b = lambda i, j: (i, 0, 0)
    const = lambda i, j: (0, 0)
    kv_out = lambda dt: jax.ShapeDtypeStruct((b, s, LANES), dt)
    return pl.pallas_call(
        _nsa_in_kernel,
        out_shape=(jax.ShapeDtypeStruct((b, s, nq), BF16), jax.ShapeDtypeStruct((b, s, 2 * LANES), F32),
                   kv_out(BF16), kv_out(BF16), kv_out(BF16), kv_out(BF16), kv_out(F32)),
        grid=(b, s // tm),
        in_specs=[pl.BlockSpec((1, tm, d), row),
                  pl.BlockSpec((1, d), const),
                  pl.BlockSpec((1, 1, d), per_b),
                  pl.BlockSpec((1, 1, d), per_b),
                  pl.BlockSpec((d, nq), const),
                  pl.BlockSpec((d, nkv), const),
                  pl.BlockSpec((d, LANES), const),
                  pl.BlockSpec((1, LANES), const),
                  pl.BlockSpec((3, LANES), const),
                  pl.BlockSpec((tm, LANES), lambda i, j: (j, 0)),
                  pl.BlockSpec((tm, LANES), lambda i, j: (j, 0))],
        out_specs=(pl.BlockSpec((1, tm, nq), row), pl.BlockSpec((1, tm, 2 * LANES), row))
        + (pl.BlockSpec((1, tm, LANES), row),) * 5,
        compiler_params=_cparams(2),
        name="nsa_in_proj",
    )(x, gain.reshape(1, d), sc, sh, wq, wkv, wg, qg, kg, cos, sin)


def _compress_kernel(a_ref, pe_ref, w1_ref, b1_ref, w2_ref, b2_ref, kg_ref, cos_ref, sin_ref, o_ref, *, n_str):
    is_key = pl.program_id(0) == 0
    hid2 = w1_ref.shape[-1]
    first = jnp.zeros((n_str, hid2), F32)
    second = jnp.zeros((n_str, hid2), F32)
    pe_term = jnp.zeros((8, hid2), F32)
    for l in range(CMP_STRIDE):
        rows = a_ref[0, pl.ds(l, n_str, stride=CMP_STRIDE), :].astype(BF16)
        first = first + _dot(rows, w1_ref[0, l])
        second = second + _dot(rows, w1_ref[0, CMP_STRIDE + l])
    for l in range(CMP_BLOCK):
        pe_term = pe_term + _dot(pe_ref[0, l].astype(BF16), w1_ref[0, l])
    hid = first + pltpu.roll(second, n_str - 1, 0) + pe_term[0:1] + b1_ref[0]
    hid = 0.5 * hid * (1.0 + jnp.tanh(np.sqrt(2.0 / np.pi) * (hid + 0.044715 * hid * hid * hid)))
    out = _dot(hid.astype(BF16), w2_ref[0]) + b2_ref[0]
    normed = _half_norm_rope(out, kg_ref[...], cos_ref[...], sin_ref[...])
    o_ref[0, 0] = jnp.where(is_key, normed, out).astype(o_ref.dtype)


def _block_diag2(w):
    z = jnp.zeros_like(w)
    return jnp.concatenate([jnp.concatenate([w, z], axis=-1), jnp.concatenate([z, w], axis=-1)], axis=-2)


def _compress_call(cv, pe, w1, b1, w2, b2, k_gain0, cos, sin):
    b, s, _ = cv.shape
    n_str = s // CMP_STRIDE
    hid = w1.shape[-1]
    w1bd = _block_diag2(w1.reshape(2, CMP_BLOCK, HEAD_DIM, hid)).astype(BF16)
    w2bd = _block_diag2(w2).astype(BF16)
    pe2 = jnp.broadcast_to(jnp.tile(pe, (1, 1, 2))[:, :, None, :], (2, CMP_BLOCK, 8, LANES))
    b1t = jnp.tile(b1, (1, 2)).reshape(2, 1, 2 * hid)
    b2t = jnp.tile(b2, (1, 2)).reshape(2, 1, LANES)
    kg = jnp.tile(k_gain0, 2).reshape(1, LANES)
    sel3 = lambda i, j: (i, 0, 0)
    sel4 = lambda i, j: (i, 0, 0, 0)
    const = lambda i, j: (0, 0)
    return pl.pallas_call(
        functools.partial(_compress_kernel, n_str=n_str),
        out_shape=jax.ShapeDtypeStruct((2, b, n_str, LANES), BF16),
        grid=(2, b),
        in_specs=[pl.BlockSpec((1, s, LANES), lambda i, j: (j, 0, i)),
                  pl.BlockSpec((1, CMP_BLOCK, 8, LANES), sel4),
                  pl.BlockSpec((1, CMP_BLOCK, LANES, 2 * hid), sel4),
                  pl.BlockSpec((1, 1, 2 * hid), sel3),
                  pl.BlockSpec((1, 2 * hid, LANES), sel3),
                  pl.BlockSpec((1, 1, LANES), sel3),
                  pl.BlockSpec((1, LANES), const),
                  pl.BlockSpec((n_str, LANES), const),
                  pl.BlockSpec((n_str, LANES), const)],
        out_specs=pl.BlockSpec((1, 1, n_str, LANES), lambda i, j: (i, j, 0, 0)),
        compiler_params=_cparams(2),
        name="nsa_compress",
    )(cv, pe2, w1bd, b1t, w2bd, b2t, kg, cos, sin)


def _attend(qb, k_ref, v_ref, k0, spans, bias):
    rows = qb.shape[0]
    q_rows = bias.shape[0] // NSA_KV_GROUPS
    per_group = rows // (NSA_KV_GROUPS * q_rows)
    bias = bias.reshape(NSA_KV_GROUPS, 1, q_rows, bias.shape[-1])
    m = acc = None
    for off, size in spans:
        k = k_ref[0, pl.ds(k0 + off, size), :]
        v = v_ref[0, pl.ds(k0 + off, size), :]
        s = _dot_nt(qb, k).reshape(NSA_KV_GROUPS, per_group, q_rows, size) + bias[..., off:off + size]
        s = s.reshape(rows, size)
        m_span = jnp.max(s, axis=-1, keepdims=True)
        m_new = m_span if m is None else jnp.maximum(m, m_span)
        p = jnp.exp2(s - m_new).astype(BF16)
        pv = _dot(p, jnp.concatenate([v, jnp.ones_like(v)], axis=1))
        acc = pv if m is None else acc * jnp.exp2(m - m_new) + pv
        m = m_new
    return acc


def _nsa_attn_kernel(*refs, n_inputs, **static):
    for sub in range(NSA_CHAINS_PER_STEP):
        _nsa_attn_block(sub, pl.program_id(1) * NSA_CHAINS_PER_STEP + sub, *refs[:n_inputs], refs[-1], **static)


def _nsa_attn_block(sub, step, q_ref, g_ref, kc_ref, vc_ref, ks_ref, vs_ref, kw_ref, vw_ref,
                    c2s_ref, exp_ref, o_ref, *, seq, n_cmp, n_top, win_keys, sel_span, n_keys, q_block0, q_rows):
    G, R, QB = NSA_KV_GROUPS, NSA_HEADS_PER_GROUP, q_rows
    rows = R * QB
    n_sel = seq // SEL_BLOCK
    qi0 = step * (q_rows // NSA_Q_BLOCK) + q_block0
    s0 = qi0 * NSA_Q_BLOCK
    tok = slice(sub * QB, (sub + 1) * QB)
    gt = jax.nn.sigmoid(g_ref[0, tok, :])
    lane = lax.broadcasted_iota(jnp.int32, (QB, LANES), 1)
    tq = s0 + lax.broadcasted_iota(jnp.int32, (rows, 1), 0) % QB
    tq1 = s0 + lax.broadcasted_iota(jnp.int32, (QB, 1), 0)
    ones_sq = jnp.ones((LANES, LANES), BF16)

    w0 = pl.multiple_of(jnp.maximum(s0 + QB - win_keys, 0), SEL_BLOCK)
    wpos = w0 + lax.broadcasted_iota(jnp.int32, (1, win_keys), 1)
    bias_w = jnp.where((wpos <= tq1) & (wpos > tq1 - WINDOW), 0.0, NEG)
    win_spans = [(off, min(3 * LANES, win_keys - off)) for off in range(0, win_keys, 3 * LANES)]

    zero = jnp.zeros((QB, LANES), BF16)
    qb = jnp.concatenate([jnp.where((lane // HEAD_DIM) == g, q_ref[0, tok, r * LANES:(r + 1) * LANES], zero)
                          for g in range(G) for r in range(R)], axis=0)
    tq_all = jnp.concatenate([tq] * G, axis=0)
    tq_tok = jnp.concatenate([tq1] * G, axis=0)

    sc = _dot_nt(qb, kc_ref[0])
    cpos = lax.broadcasted_iota(jnp.int32, (1, LANES), 1) * CMP_STRIDE + (CMP_BLOCK - 1)
    valid_c = (cpos <= tq_all) & (lax.broadcasted_iota(jnp.int32, (1, LANES), 1) < n_cmp)
    sc = jnp.where(valid_c, sc, NEG)
    e_c = jnp.exp2(sc - jnp.max(sc, axis=-1, keepdims=True)).astype(BF16)
    p_c = jnp.where(valid_c, e_c.astype(F32) / _dot(e_c, ones_sq), 0.0)
    o_c = _dot(p_c.astype(BF16), vc_ref[0])

    acc_w = _attend(qb, kw_ref, vw_ref, w0, win_spans, jnp.concatenate([bias_w] * G, axis=0))

    psum = jnp.sum(p_c.reshape(G, R, QB, LANES), axis=1).reshape(G * QB, LANES)
    imp = lax.dot_general(c2s_ref[...], psum, (((1,), (1,)), ((), ())),
                          preferred_element_type=F32, precision=HIGHEST)
    blk = lax.broadcasted_iota(jnp.int32, (n_sel, G * QB), 0)
    qi = qi0 + (lax.broadcasted_iota(jnp.int32, (1, G * QB), 1) % QB) // NSA_Q_BLOCK
    forced = (blk == 0) | (blk == qi) | (blk == qi - 1)
    imp = jnp.where(blk <= qi, jnp.where(forced, BIG, imp), -BIG)
    beaten = jnp.zeros(imp.shape, F32)
    for k in range(1, n_sel):
        other = pltpu.roll(imp, k, 0)
        beats = (other > imp) | ((blk >= k) & (other == imp))
        beaten = beaten + jnp.where(beats, 1.0, 0.0)
    chosen = jnp.where(beaten < n_top, 1.0, 0.0).astype(BF16)
    picked = _dot_tn(chosen, exp_ref[:, :n_keys])
    kpos = lax.broadcasted_iota(jnp.int32, (1, n_keys), 1)
    sel_spans = [(off, min(sel_span, n_keys - off)) for off in range(0, n_keys, sel_span)]

    bias_s = jnp.where((picked > 0.5) & (kpos <= tq_tok), 0.0, NEG)
    acc_s = _attend(qb, ks_ref, vs_ref, 0, sel_spans, bias_s)
    heads = []
    for h in range(G * R):
        rs = slice(h * QB, (h + 1) * QB)
        a_s, a_w = acc_s[rs], acc_w[rs]
        g_s = gt[:, NSA_HEADS + h:NSA_HEADS + h + 1] / a_s[:, LANES:LANES + 1]
        g_w = gt[:, 2 * NSA_HEADS + h:2 * NSA_HEADS + h + 1] / a_w[:, LANES:LANES + 1]
        heads.append(gt[:, h:h + 1] * o_c[rs] + g_s * a_s[:, :LANES] + g_w * a_w[:, :LANES])
    for r in range(R):
        slab = jnp.where(lane < HEAD_DIM, heads[r], heads[R + r])
        o_ref[0, tok, r * LANES:(r + 1) * LANES] = slab.astype(o_ref.dtype)


def _nsa_attn_call(q, gates, kc, vc, ks, vs, kw, vw, cmp_to_sel, n_cmp):
    b, s, nq = q.shape
    qb = NSA_Q_BLOCK
    n_top = min(SEL_TOPN, s // SEL_BLOCK)
    call_rows = min(NSA_CALL_KEYS, s)
    qrows = min(NSA_BLOCKS_PER_STEP * qb, call_rows)
    win_keys = min(-(-(WINDOW + qrows) // LANES) * LANES, s)
    sel_span = min(512, s)
    n_sel = s // SEL_BLOCK
    expand = jnp.asarray(np.arange(n_sel)[:, None] == (np.arange(s)[None, :] // SEL_BLOCK), BF16)
    per_b = lambda i, j: (i, 0, 0)
    const = lambda i, j: (0, 0)
    call_keys = call_rows
    step_rows = min(NSA_CHAINS_PER_STEP * qrows, call_keys)
    steps = call_keys // step_rows
    out = None
    for n in range(1, s // call_keys + 1):
        q0 = (n - 1) * (call_keys // qb)
        n_keys = n * call_keys
        row = lambda i, j, t0=(n - 1) * steps: (i, j + t0, 0)
        kern = functools.partial(_nsa_attn_kernel, n_inputs=10, seq=s, n_cmp=n_cmp, n_top=n_top,
                                 win_keys=win_keys, sel_span=sel_span, n_keys=n_keys, q_block0=q0, q_rows=qrows)
        carried = [] if out is None else [out]
        out = pl.pallas_call(
            kern,
            out_shape=jax.ShapeDtypeStruct((b, s, nq), BF16),
            grid=(b, steps),
            in_specs=[pl.BlockSpec((1, step_rows, nq), row),
                      pl.BlockSpec((1, step_rows, LANES), row),
                      pl.BlockSpec((1, LANES, LANES), per_b),
                      pl.BlockSpec((1, LANES, LANES), per_b),
                      pl.BlockSpec((1, n_keys, LANES), per_b),
                      pl.BlockSpec((1, n_keys, LANES), per_b),
                      pl.BlockSpec((1, s, LANES), per_b),
                      pl.BlockSpec((1, s, LANES), per_b),
                      pl.BlockSpec((n_sel, LANES), const),
                      pl.BlockSpec((n_sel, s), const)] + [pl.BlockSpec(memory_space=pl.ANY)] * len(carried),
            out_specs=pl.BlockSpec((1, step_rows, nq), row),
            input_output_aliases={10: 0} if carried else {},
            compiler_params=_cparams(2),
            name="nsa_attention",
        )(q, gates, kc, vc, ks, vs, kw, vw, cmp_to_sel, expand, *carried)
    return out


def _mixer_out_kernel(*refs, gated):
    a_ref, refs = refs[0], refs[1:]
    lhs = a_ref[0]
    if gated:
        og_ref, refs = refs[0], refs[1:]
        lhs = (jax.nn.sigmoid(og_ref[0].astype(F32)) * lhs.astype(F32)).astype(BF16)
    (w_ref, x_ref, g_ref, gain_ref, sc_ref, sh_ref, wh_ref, wl_ref, rb_ref,
     xo_ref, h_ref, route_ref, route_t_ref, cnt_ref) = refs
    x_new = x_ref[0] + g_ref[0] * _dot(lhs, w_ref[...])
    xo_ref[0] = x_new
    h_ref[0], route_ref[0], route_t_ref[...], cnt_ref[0] = _route_tile(
        x_new, gain_ref[...], sc_ref[0], sh_ref[0], wh_ref[...], wl_ref[...], rb_ref[...])


def _mixer_out_call(a, og, w_out, x, g, ffn, tm):
    gain, sc, sh, router_w, router_b = ffn
    b, s, d = x.shape
    k = a.shape[-1]
    nt = s // tm
    wp = jnp.pad(router_w, ((0, 0), (0, LANES - N_EXPERTS)))
    wh = wp.astype(BF16)
    wl = jnp.concatenate([wh, (wp - wh.astype(F32)).astype(BF16)], axis=1)
    rb = router_b.astype(F32).reshape(N_EXPERTS, 1)
    row = lambda i, j: (i, j, 0)
    per_b = lambda i, j: (i, 0, 0)
    const = lambda i, j: (0, 0)
    a_spec = pl.BlockSpec((1, tm, k), row)
    acts = [a] if og is None else [a, og]
    return pl.pallas_call(
        functools.partial(_mixer_out_kernel, gated=og is not None),
        out_shape=(jax.ShapeDtypeStruct((b, s, d), F32), jax.ShapeDtypeStruct((b, s, d), BF16),
                   jax.ShapeDtypeStruct((b, s, LANES), F32), jax.ShapeDtypeStruct((8, b * s), F32),
                   jax.ShapeDtypeStruct((b * nt, N_EXPERTS, LANES), F32)),
        grid=(b, nt),
        in_specs=[a_spec] * len(acts) + [pl.BlockSpec((k, d), const),
                                         pl.BlockSpec((1, tm, d), row),
                                         pl.BlockSpec((1, 1, d), per_b),
                                         pl.BlockSpec((1, d), const),
                                         pl.BlockSpec((1, 1, d), per_b),
                                         pl.BlockSpec((1, 1, d), per_b),
                                         pl.BlockSpec((d, LANES), const),
                                         pl.BlockSpec((d, 2 * LANES), const),
                                         pl.BlockSpec((N_EXPERTS, 1), const)],
        out_specs=(pl.BlockSpec((1, tm, d), row), pl.BlockSpec((1, tm, d), row), pl.BlockSpec((1, tm, LANES), row),
                   pl.BlockSpec((8, tm), lambda i, j: (0, i * nt + j)),
                   pl.BlockSpec((1, N_EXPERTS, LANES), lambda i, j: (i * nt + j, 0, 0))),
        compiler_params=_cparams(2),
        name="mixer_out_router",
    )(*acts, w_out.astype(BF16), x, g, gain.reshape(1, d), sc, sh, wh, wl, rb)


def _mlstm_in_kernel(x_ref, gain_ref, sc_ref, sh_ref, wq_ref, wk_ref, wv_ref, wo_ref, wg_ref,
                     q_ref, k_ref, v_ref, o_ref, g_ref):
    h = _norm_mod(x_ref[0], gain_ref[...], sc_ref[0], sh_ref[0]).astype(BF16)
    q_ref[0] = _dot(h, wq_ref[...]).astype(BF16)
    k_ref[0] = _dot(h, wk_ref[...]).astype(BF16)
    v_ref[0] = _dot(h, wv_ref[...]).astype(BF16)
    o_ref[0] = _dot(h, wo_ref[...]).astype(BF16)
    g_ref[0] = _dot(h, wg_ref[...])


def _mlstm_in_call(x, gain, sc, sh, w_in, dqk, dv, tm):
    b, s, d = x.shape
    nh = MLSTM_HEADS
    sizes = [nh * dqk, nh * dqk, nh * dv, nh * dv]
    offs = np.cumsum([0] + sizes)
    ws = [w_in[:, offs[i]:offs[i + 1]].astype(BF16) for i in range(4)]
    wg = jnp.pad(w_in[:, offs[4]:], ((0, 0), (0, LANES - 2 * nh))).astype(BF16)
    row = lambda i, j: (i, j, 0)
    per_b = lambda i, j: (i, 0, 0)
    const = lambda i, j: (0, 0)
    widths = sizes + [LANES]
    return pl.pallas_call(
        _mlstm_in_kernel,
        out_shape=tuple(jax.ShapeDtypeStruct((b, s, n), BF16) for n in sizes)
        + (jax.ShapeDtypeStruct((b, s, LANES), F32),),
        grid=(b, s // tm),
        in_specs=[pl.BlockSpec((1, tm, d), row),
                  pl.BlockSpec((1, d), const),
                  pl.BlockSpec((1, 1, d), per_b),
                  pl.BlockSpec((1, 1, d), per_b)] + [pl.BlockSpec((d, n), const) for n in widths],
        out_specs=tuple(pl.BlockSpec((1, tm, n), row) for n in widths),
        compiler_params=_cparams(2),
        name="mlstm_in_proj",
    )(x, gain.reshape(1, d), sc, sh, *ws, wg)


def _softcap(a):
    return GATE_SOFTCAP * jnp.tanh(a / GATE_SOFTCAP)


def _mlstm_kernel(bi_ref, bf_ref, q_ref, k_ref, v_ref, g_ref, gain_ref, o_ref,
                  li_s, b_s, *, n_chunks, n_heads, dqk, dv):
    L = MLSTM_CHUNK
    r_i = lax.broadcasted_iota(jnp.int32, (L, L), 0)
    c_i = lax.broadcasted_iota(jnp.int32, (L, L), 1)
    upper = jnp.where(r_i <= c_i, 1.0, 0.0)
    for h in range(n_heads):
        li_s[h] = _softcap(g_ref[0, h] + bi_ref[h])
        fa = _softcap(g_ref[0, n_heads + h] + bf_ref[h])
        lf = jnp.minimum(fa, 0.0) - jnp.log1p(jnp.exp(-jnp.abs(fa)))
        b_s[h] = jnp.dot(lf, upper, preferred_element_type=F32, precision=HIGHEST)
    eye = r_i == c_i
    causal = r_i >= c_i
    k_scale = dqk ** -0.5

    def to_col(row):
        return jnp.sum(jnp.where(eye, jnp.broadcast_to(row, (L, L)), 0.0), axis=1, keepdims=True)

    def local_part(h, r0, c):
        qcb = q_ref[0, pl.ds(r0, L), h * dqk:(h + 1) * dqk]
        kc = k_ref[0, pl.ds(r0, L), h * dqk:(h + 1) * dqk].astype(F32) * k_scale
        vc = v_ref[0, pl.ds(r0, L), h * dv:(h + 1) * dv]
        b_row = b_s[h, pl.ds(c, 1), :]
        li_row = li_s[h, pl.ds(c, 1), :]
        b_col, li_col = to_col(b_row), to_col(li_row)
        b_last = b_row[:, L - 1:L]
        dmat = jnp.where(causal, b_col - b_row + li_row, NEG)
        m_loc = jnp.max(dmat, axis=-1, keepdims=True)
        a_loc = jnp.exp(dmat - m_loc) * _dot_nt(qcb, kc.astype(BF16))
        num_loc = _dot(a_loc.astype(BF16), vc)
        den_loc = jnp.sum(a_loc, axis=-1, keepdims=True)
        g_max = m_loc[L - 1:L, :]
        kw = kc * jnp.exp(b_last - b_col + li_col - g_max)
        kv = _dot_tn(kw.astype(BF16), vc)
        kn = jnp.sum(kw, axis=0, keepdims=True)
        return qcb, b_col, b_last, m_loc, num_loc, den_loc, g_max, kv, kn

    def body(grp, carry):
        r0 = pl.multiple_of(grp * (MLSTM_GROUP * L), MLSTM_GROUP * L)
        parts = [[local_part(h, r0 + j * L, grp * MLSTM_GROUP + j) for j in range(MLSTM_GROUP)]
                 for h in range(n_heads)]
        new_carry, outs = [], []
        for h in range(n_heads):
            state, n_row, m_prev = carry[h]
            gain = gain_ref[h]
            head_out = []
            for qcb, b_col, b_last, m_loc, num_loc, den_loc, g_max, kv, kn in parts[h]:
                m_inter = b_col + m_prev
                m_t = jnp.maximum(m_inter, m_loc)
                intra = jnp.exp(m_loc - m_t)
                inter = jnp.exp(m_inter - m_t)
                num = intra * num_loc + inter * _dot(qcb, state.astype(BF16))
                den = intra * den_loc + inter * jnp.sum(qcb.astype(F32) * n_row, axis=-1, keepdims=True)
                h_out = num / jnp.maximum(jnp.abs(den), jnp.exp(-m_t))
                hs = h_out * lax.rsqrt(jnp.mean(h_out * h_out, axis=-1, keepdims=True) + EPS) * gain
                head_out.append(hs.astype(o_ref.dtype))
                m_new = jnp.maximum(b_last + m_prev, g_max)
                decay = jnp.exp(b_last + m_prev - m_new)
                grow = jnp.exp(g_max - m_new)
                state, n_row, m_prev = decay * state + grow * kv, decay * n_row + grow * kn, m_new
            new_carry.append((state, n_row, m_prev))
            outs.append(jnp.concatenate(head_out, axis=0))
        for h in range(n_heads):
            o_ref[0, pl.ds(r0, MLSTM_GROUP * L), h * dv:(h + 1) * dv] = outs[h]
        return tuple(new_carry)

    init = tuple((jnp.zeros((dqk, dv), F32), jnp.zeros((1, dqk), F32), jnp.zeros((1, 1), F32))
                 for _ in range(n_heads))
    lax.fori_loop(0, n_chunks // MLSTM_GROUP, body, init)


def _mlstm_call(q, k, v, gates, b_igate, b_fgate, norm_gain):
    b, s, _ = q.shape
    nh = MLSTM_HEADS
    dqk, dv = q.shape[-1] // nh, v.shape[-1] // nh
    L = MLSTM_CHUNK
    nch = s // L
    g = jnp.transpose(gates[..., :2 * nh], (0, 2, 1)).reshape(b, 2 * nh, nch, L)
    smem = pl.BlockSpec(memory_space=pltpu.SMEM)
    per_b = lambda n: pl.BlockSpec((1, s, n), lambda i: (i, 0, 0))
    kern = functools.partial(_mlstm_kernel, n_chunks=nch, n_heads=nh, dqk=dqk, dv=dv)
    return pl.pallas_call(
        kern,
        out_shape=jax.ShapeDtypeStruct((b, s, nh * dv), BF16),
        grid=(b,),
        in_specs=[smem, smem, per_b(nh * dqk), per_b(nh * dqk), per_b(nh * dv),
                  pl.BlockSpec((1, 2 * nh, nch, L), lambda i: (i, 0, 0, 0)),
                  pl.BlockSpec((nh, 1, dv), lambda i: (0, 0, 0))],
        out_specs=per_b(nh * dv),
        scratch_shapes=[pltpu.VMEM((nh, nch, L), F32), pltpu.VMEM((nh, nch, L), F32)],
        compiler_params=_cparams(1),
        name="mlstm_chunk_scan",
    )(b_igate, b_fgate, q, k, v, g, norm_gain.reshape(nh, 1, dv))


def _route_tile(x, gain, sc, sh, w_hi, w_hilo, router_bias):
    h = _norm_mod(x, gain, sc, sh)
    hi = h.astype(BF16)
    lo = (h - hi.astype(F32)).astype(BF16)
    hi_both = _dot(hi, w_hilo)
    logits = hi_both[:, :LANES] + (_dot(lo, w_hi) + hi_both[:, LANES:])
    tm = logits.shape[0]
    aff = jax.nn.sigmoid(logits.T[:N_EXPERTS])
    choice = (aff + router_bias).reshape(N_GROUPS, EXPERTS_PER_GROUP, tm)
    local = lax.broadcasted_iota(jnp.int32, choice.shape, 1)

    def first_max(v):
        m = jnp.max(v, axis=1, keepdims=True)
        return m, jnp.min(jnp.where(v == m, local, EXPERTS_PER_GROUP), axis=1, keepdims=True)

    m1, i1 = first_max(choice)
    m2, i2 = first_max(jnp.where(local == i1, -jnp.inf, choice))
    score = m1 + m2
    best, e0, e1 = score[0], i1[0], i2[0]
    for g in range(1, N_GROUPS):
        better = score[g] > best
        best = jnp.where(better, score[g], best)
        e0 = jnp.where(better, i1[g] + g * EXPERTS_PER_GROUP, e0)
        e1 = jnp.where(better, i2[g] + g * EXPERTS_PER_GROUP, e1)
    expert = lax.broadcasted_iota(jnp.int32, (N_EXPERTS, tm), 0)
    is0, is1 = expert == e0, expert == e1
    a0 = jnp.sum(jnp.where(is0, aff, 0.0), axis=0, keepdims=True)
    a1 = jnp.sum(jnp.where(is1, aff, 0.0), axis=0, keepdims=True)
    tot = a0 + a1
    onehot = jnp.where(is0 | is1, 1.0, 0.0).astype(BF16)
    r_i = lax.broadcasted_iota(jnp.int32, (tm, tm), 0)
    c_i = lax.broadcasted_iota(jnp.int32, (tm, tm), 1)
    running = _dot(onehot, jnp.where(r_i <= c_i, 1.0, 0.0).astype(BF16))
    r0 = jnp.sum(jnp.where(is0, running, 0.0), axis=0, keepdims=True) - 1.0
    r1 = jnp.sum(jnp.where(is1, running, 0.0), axis=0, keepdims=True) - 1.0
    rows = [e0.astype(F32), e1.astype(F32), r0, r1, a0 / tot, a1 / tot]
    packed = jnp.concatenate(rows + [jnp.zeros((LANES - len(rows), tm), F32)], axis=0)
    return hi, packed.T, packed[:8], jnp.broadcast_to(running[:, tm - 1:tm], (N_EXPERTS, LANES))


def _expert_kernel(be_ref, live_ref, x_ref, wg_ref, wu_ref, wd_ref, o_ref, wg_s, wu_s, wd_s):
    i = pl.program_id(0)
    fresh = (i == 0) | (be_ref[i] != be_ref[jnp.maximum(i - 1, 0)])

    @pl.when(fresh)
    def _():
        wg_s[...] = wg_ref[0, 0].astype(BF16)
        wu_s[...] = wu_ref[0, 0].astype(BF16)
        wd_s[...] = wd_ref[0, 0].astype(BF16)

    @pl.when(live_ref[i] == 1)
    def _():
        xb = x_ref[...]
        gate = _dot(xb, wg_s[...])
        hid = gate * jax.nn.sigmoid(gate) * _dot(xb, wu_s[...])
        o_ref[...] = _dot(hid.astype(BF16), wd_s[...]).astype(o_ref.dtype)

    @pl.when(live_ref[i] == 0)
    def _():
        o_ref[...] = jnp.zeros_like(o_ref)


def _expert_call(block_expert, block_live, xs, w_gate, w_up, w_down, layer):
    p, d = xs.shape
    de = w_gate.shape[-1]
    nb = p // MOE_ROWS
    grid_spec = pltpu.PrefetchScalarGridSpec(
        num_scalar_prefetch=2,
        grid=(nb,),
        in_specs=[pl.BlockSpec((MOE_ROWS, d), lambda i, be, lv: (i, 0)),
                  pl.BlockSpec((1, 1, d, de), lambda i, be, lv: (layer, be[i], 0, 0)),
                  pl.BlockSpec((1, 1, d, de), lambda i, be, lv: (layer, be[i], 0, 0)),
                  pl.BlockSpec((1, 1, de, d), lambda i, be, lv: (layer, be[i], 0, 0))],
        out_specs=pl.BlockSpec((MOE_ROWS, d), lambda i, be, lv: (i, 0)),
        scratch_shapes=[pltpu.VMEM((d, de), BF16), pltpu.VMEM((d, de), BF16), pltpu.VMEM((de, d), BF16)],
    )
    return pl.pallas_call(
        _expert_kernel,
        out_shape=jax.ShapeDtypeStruct((p, d), BF16),
        grid_spec=grid_spec,
        compiler_params=_cparams(1),
        name="moe_experts",
    )(block_expert, block_live, xs, w_gate, w_up, w_down)


def _combine_kernel(x_ref, g_ref, route_ref, ya_ref, yb_ref, o_ref):
    w = route_ref[0][:, 2 * TOP_K:3 * TOP_K]
    y = w[:, 0:1] * ya_ref[0].astype(F32) + w[:, 1:2] * yb_ref[0].astype(F32)
    o_ref[0] = x_ref[0] + g_ref[0] * y


def _combine_call(x, g, route, ya, yb, tm):
    b, s, d = x.shape
    row = lambda i, j: (i, j, 0)
    spec = pl.BlockSpec((1, tm, d), row)
    return pl.pallas_call(
        _combine_kernel,
        out_shape=jax.ShapeDtypeStruct((b, s, d), F32),
        grid=(b, s // tm),
        in_specs=[spec, pl.BlockSpec((1, 1, d), lambda i, j: (i, 0, 0)),
                  pl.BlockSpec((1, tm, LANES), row), spec, spec],
        out_specs=spec,
        compiler_params=_cparams(2),
        name="moe_combine",
    )(x, g, route, ya, yb)


def _row_layout(route_t, cnt, tm):
    t = route_t.shape[1]
    nt = t // tm
    tile_cnt = cnt[:, :, 0].astype(jnp.int32)
    tile_off = jnp.cumsum(tile_cnt, axis=0) - tile_cnt
    counts = jnp.sum(tile_cnt, axis=0)
    padded = (counts + MOE_ROWS - 1) // MOE_ROWS * MOE_ROWS
    p_ends = jnp.cumsum(padded)
    base = (p_ends - padded)[None, :] + tile_off
    base_tok = jnp.broadcast_to(base.T[:, :, None], (N_EXPERTS, nt, tm)).reshape(N_EXPERTS, t)
    ids = route_t[:2 * TOP_K].astype(jnp.int32)
    experts = jnp.arange(N_EXPERTS, dtype=jnp.int32)[:, None]
    dest = [jnp.sum(jnp.where(ids[k][None, :] == experts, base_tok, 0), axis=0) + ids[TOP_K + k]
            for k in range(TOP_K)]
    nb = (t * TOP_K) // MOE_ROWS + N_EXPERTS
    tok = jnp.arange(t, dtype=jnp.int32)
    buf_tok = (jnp.arange(nb * MOE_ROWS, dtype=jnp.int32) % t).at[jnp.concatenate(dest)].set(
        jnp.concatenate([tok] * TOP_K), mode="promise_in_bounds", unique_indices=True)
    block_start = jnp.arange(nb, dtype=jnp.int32) * MOE_ROWS
    block_expert = jnp.minimum(jnp.sum((p_ends[None, :] <= block_start[:, None]).astype(jnp.int32), axis=-1),
                               N_EXPERTS - 1)
    block_live = (block_start < p_ends[-1]).astype(jnp.int32)
    return dest, buf_tok, block_expert, block_live


def _moe_layer(x, hf, route, route_t, cnt, g, w_gate, w_up, w_down, layer, tm):
    b, s, d = x.shape
    t = b * s
    dest, buf_tok, block_expert, block_live = _row_layout(route_t, cnt, tm)
    take = lambda a, idx: a.at[idx].get(mode="promise_in_bounds")
    xs = take(hf.reshape(t, d), buf_tok)
    out = _expert_call(block_expert, block_live, xs, w_gate, w_up, w_down, layer)
    ya = take(out, dest[0]).reshape(b, s, d)
    yb = take(out, dest[1]).reshape(b, s, d)
    return _combine_call(x, g, route, ya, yb, tm)


def _nsa_layer(x, gain, sc, sh, g, w_in, w_out, q_gain, k_gain, cmp_pe, cmp_w1, cmp_b1, cmp_w2, cmp_b2, ffn, tm):
    b, s, d = x.shape
    G, dh = NSA_KV_GROUPS, HEAD_DIM
    cos, sin = _rope_tables(jnp.arange(s, dtype=jnp.int32))
    q, cv, ks, vs, kw, vw, gates = _nsa_in_call(x, gain, sc, sh, w_in, q_gain, k_gain, cos, sin, tm)

    n_cmp = (s - CMP_BLOCK) // CMP_STRIDE + 1
    n_str = s // CMP_STRIDE
    cmp_pos = jnp.arange(n_str, dtype=jnp.int32) * CMP_STRIDE + (CMP_BLOCK - 1)
    ccos, csin = _rope_tables(cmp_pos)
    cmp = _compress_call(cv, cmp_pe, cmp_w1, cmp_b1, cmp_w2, cmp_b2, k_gain[0], ccos, csin)
    cmp = jnp.pad(cmp, ((0, 0), (0, 0), (0, LANES - n_str), (0, 0)))
    kc, vc = cmp[0], cmp[1]

    ns = s // SEL_BLOCK
    r_, u_ = SEL_BLOCK // CMP_STRIDE, CMP_BLOCK // CMP_STRIDE
    c_idx = (r_ * np.arange(ns)[:, None, None] + np.arange(r_)[None, :, None]
             + np.arange(u_)[None, None, :]).reshape(ns, -1)
    c2s = (c_idx[:, :, None] == np.arange(n_cmp)[None, None, :]).sum(1).astype(np.float32)
    c2s = jnp.asarray(np.pad(c2s, ((0, 0), (0, LANES - n_cmp))))

    o = _nsa_attn_call(q, gates, kc, vc, ks, vs, kw, vw, c2s, n_cmp)
    return _mixer_out_call(o, None, w_out[_head_pair_order(), :], x, g, ffn, tm)


def _mlstm_layer(x, gain, sc, sh, g, w_in, w_out, b_igate, b_fgate, norm_gain, ffn, tm):
    nh = MLSTM_HEADS
    dv = norm_gain.shape[-1]
    dqk = (w_in.shape[-1] - 2 * nh - 2 * nh * dv) // (2 * nh)
    q, k, v, og, gates = _mlstm_in_call(x, gain, sc, sh, w_in, dqk, dv, tm)
    hs = _mlstm_call(q, k, v, gates, b_igate, b_fgate, norm_gain)
    return _mixer_out_call(hs, og, w_out, x, g, ffn, tm)


def kernel(x, c, ada_w, ada_b, norm_mix_gain, norm_ffn_gain, nsa_w_in, nsa_w_out, nsa_q_gain, nsa_k_gain, nsa_cmp_pe, nsa_cmp_w1, nsa_cmp_b1, nsa_cmp_w2, nsa_cmp_b2, mlstm_w_in, mlstm_b_igate, mlstm_b_fgate, mlstm_norm_gain, mlstm_w_out, router_w, router_b, moe_w_gate, moe_w_up, moe_w_down):
    b, s, d = x.shape
    depth = ada_w.shape[0]
    tm = min(1024, s)
    mod = _mod_call(c, ada_w, ada_b)
    for i in range(depth):
        sh_m, sc_m, g_m, sh_f, sc_f, g_f = [mod[i, :, None, k * d:(k + 1) * d] for k in range(6)]
        j = i // 2
        ffn = (norm_ffn_gain[i], sc_f, sh_f, router_w, router_b)
        if i % 2 == 0:
            mixed = _nsa_layer(x, norm_mix_gain[i], sc_m, sh_m, g_m, nsa_w_in[j], nsa_w_out[j], nsa_q_gain[j],
                               nsa_k_gain[j], nsa_cmp_pe[j], nsa_cmp_w1[j], nsa_cmp_b1[j], nsa_cmp_w2[j],
                               nsa_cmp_b2[j], ffn, tm)
        else:
            mixed = _mlstm_layer(x, norm_mix_gain[i], sc_m, sh_m, g_m, mlstm_w_in[j], mlstm_w_out[j],
                                 mlstm_b_igate[j], mlstm_b_fgate[j], mlstm_norm_gain[j], ffn, tm)
        x = _moe_layer(*mixed, g_f, moe_w_gate, moe_w_up, moe_w_down, i, tm)
    return x
```

```python
import functools

import numpy as np
import jax
import jax.numpy as jnp
from jax import lax
from jax.experimental import pallas as pl
from jax.experimental.pallas import tpu as pltpu

F32 = jnp.float32
BF16 = jnp.bfloat16
HIGHEST = lax.Precision.HIGHEST

EPS = 1e-6
NEG = -1e30
BIG = 1e9
ROPE_THETA = 500000.0
LOG2E = 1.4426950408889634

NSA_HEADS = 16
NSA_KV_GROUPS = 2
NSA_HEADS_PER_GROUP = NSA_HEADS // NSA_KV_GROUPS
HEAD_DIM = 64
ROT_DIM = HEAD_DIM // 4
CMP_BLOCK = 32
CMP_STRIDE = 16
SEL_BLOCK = 64
SEL_TOPN = 8
WINDOW = 512
NSA_Q_BLOCK = 64
NSA_BRANCHES = 3
NSA_CALL_KEYS = 512
NSA_CHAINS_PER_STEP = 2
NSA_BLOCKS_PER_STEP = 2

MLSTM_HEADS = 4
MLSTM_CHUNK = 256
MLSTM_GROUP = 1
GATE_SOFTCAP = 15.0

N_EXPERTS = 32
N_GROUPS = 4
EXPERTS_PER_GROUP = N_EXPERTS // N_GROUPS
TOP_K = 2
MOE_ROWS = 512

LANES = 128
VMEM_LIMIT = 48 * 1024 * 1024


def _cparams(n_axes):
    return pltpu.CompilerParams(dimension_semantics=("arbitrary",) * n_axes,
                                vmem_limit_bytes=VMEM_LIMIT)


def _dot(a, b):
    return jnp.dot(a, b, preferred_element_type=F32)


def _dot_nt(a, b):
    return lax.dot_general(a, b, (((1,), (1,)), ((), ())), preferred_element_type=F32)


def _dot_tn(a, b):
    return lax.dot_general(a, b, (((0,), (0,)), ((), ())), preferred_element_type=F32)


def _norm_mod(x, gain, sc, sh):
    y = x * lax.rsqrt(jnp.mean(x * x, axis=-1, keepdims=True) + EPS) * gain
    return y * (1.0 + sc) + sh


def _half_norm_rope(x, gain, cos, sin):
    lane = lax.broadcasted_iota(jnp.int32, x.shape, x.ndim - 1)
    x2 = x * x
    left = lane < HEAD_DIM
    ss_l = jnp.sum(jnp.where(left, x2, 0.0), axis=-1, keepdims=True)
    ss_r = jnp.sum(jnp.where(left, 0.0, x2), axis=-1, keepdims=True)
    ms = jnp.where(left, ss_l, ss_r) * (1.0 / HEAD_DIM)
    y = x * lax.rsqrt(ms + EPS) * gain
    half = ROT_DIM // 2
    src = lax.broadcasted_iota(jnp.int32, (LANES, LANES), 0)
    dst = lax.broadcasted_iota(jnp.int32, (LANES, LANES), 1)
    dst_in_head = dst % HEAD_DIM
    pair = jnp.where(dst_in_head < half, dst + half, jnp.where(dst_in_head < ROT_DIM, dst - half, -1))
    partner = _dot(y.astype(BF16), jnp.where(src == pair, 1.0, 0.0).astype(BF16))
    return y * cos + partner * sin


def _rope_tables(pos):
    half = ROT_DIM // 2
    inv_freq = ROPE_THETA ** (-jnp.arange(half, dtype=F32) / half)
    ang = pos.astype(F32)[:, None] * inv_freq[None, :]
    cos, sin = jnp.cos(ang), jnp.sin(ang)
    n = pos.shape[0]
    one = jnp.ones((n, HEAD_DIM - ROT_DIM), F32)
    cos_h = jnp.concatenate([cos, cos, one], axis=-1)
    sin_h = jnp.concatenate([-sin, sin, 0.0 * one], axis=-1)
    return jnp.tile(cos_h, (1, 2)), jnp.tile(sin_h, (1, 2))


def _mod_kernel(c_ref, w_ref, b_ref, o_ref):
    c = c_ref[...]
    cond = c * jax.nn.sigmoid(c)
    o_ref[0] = jnp.dot(cond, w_ref[0], preferred_element_type=F32, precision=HIGHEST) + b_ref[0]


def _mod_call(c, ada_w, ada_b):
    depth, d, n = ada_w.shape
    b = c.shape[0]
    tn = n // 4
    return pl.pallas_call(
        _mod_kernel,
        out_shape=jax.ShapeDtypeStruct((depth, b, n), F32),
        grid=(depth, n // tn),
        in_specs=[pl.BlockSpec((b, d), lambda i, j: (0, 0)),
                  pl.BlockSpec((1, d, tn), lambda i, j: (i, 0, j)),
                  pl.BlockSpec((1, 1, tn), lambda i, j: (i, 0, j))],
        out_specs=pl.BlockSpec((1, b, tn), lambda i, j: (i, 0, j)),
        compiler_params=_cparams(2),
        name="adaln_mod",
    )(c, ada_w, ada_b.reshape(depth, 1, n))


def _nsa_in_kernel(x_ref, gain_ref, sc_ref, sh_ref, wq_ref, wkv_ref, qg_ref, kg_ref, cos_ref, sin_ref,
                   q_ref, cv_ref, ks_ref, vs_ref, kw_ref, vw_ref, g_ref):
    h = _norm_mod(x_ref[0], gain_ref[...], sc_ref[0], sh_ref[0]).astype(BF16)
    kv = _dot(h, wkv_ref[...])
    g_ref[0] = kv[:, 6 * LANES:7 * LANES]
    cos, sin = cos_ref[...], sin_ref[...]
    q = _dot(h, wq_ref[...])
    for r in range(NSA_HEADS_PER_GROUP):
        slab = _half_norm_rope(q[:, r * LANES:(r + 1) * LANES], qg_ref[...], cos, sin)
        q_ref[0, :, r * LANES:(r + 1) * LANES] = (slab * (HEAD_DIM ** -0.5 * LOG2E)).astype(BF16)
    cv_ref[0] = kv[:, 0:2 * LANES]
    ks_ref[0] = _half_norm_rope(kv[:, 2 * LANES:3 * LANES], kg_ref[1:2, :], cos, sin).astype(BF16)
    vs_ref[0] = kv[:, 3 * LANES:4 * LANES].astype(BF16)
    kw_ref[0] = _half_norm_rope(kv[:, 4 * LANES:5 * LANES], kg_ref[2:3, :], cos, sin).astype(BF16)
    vw_ref[0] = kv[:, 5 * LANES:6 * LANES].astype(BF16)


def _head_pair_order():
    r, g, dd = np.meshgrid(np.arange(NSA_HEADS_PER_GROUP), np.arange(NSA_KV_GROUPS), np.arange(HEAD_DIM),
                           indexing="ij")
    return ((g * NSA_HEADS_PER_GROUP + r) * HEAD_DIM + dd).reshape(-1)


def _nsa_in_call(x, gain, sc, sh, w_in, q_gain, k_gain, cos, sin, tm):
    b, s, d = x.shape
    nq = NSA_HEADS * HEAD_DIM
    nkv = 6 * LANES
    wq = w_in[:, :nq][:, _head_pair_order()].astype(BF16)
    qg = jnp.tile(q_gain, 2).reshape(1, LANES)
    wkv = w_in[:, nq:nq + nkv].astype(BF16)
    ng = NSA_BRANCHES * NSA_HEADS
    wg = jnp.pad(w_in[:, nq + nkv:], ((0, 0), (0, LANES - ng))).astype(BF16)
    wkv = jnp.concatenate([wkv, wg], axis=1)
    kg = jnp.tile(k_gain, (1, 2))
    row = lambda i, j: (i, j, 0)
    per_b = lambda i, j: (i, 0, 0)
    const = lambda i, j: (0, 0)
    kv_out = lambda dt: jax.ShapeDtypeStruct((b, s, LANES), dt)
    return pl.pallas_call(
        _nsa_in_kernel,
        out_shape=(jax.ShapeDtypeStruct((b, s, nq), BF16), jax.ShapeDtypeStruct((b, s, 2 * LANES), F32),
                   kv_out(BF16), kv_out(BF16), kv_out(BF16), kv_out(BF16), kv_out(F32)),
        grid=(b, s // tm),
        in_specs=[pl.BlockSpec((1, tm, d), row),
                  pl.BlockSpec((1, d), const),
                  pl.BlockSpec((1, 1, d), per_b),
                  pl.BlockSpec((1, 1, d), per_b),
                  pl.BlockSpec((d, nq), const),
                  pl.BlockSpec((d, nkv + LANES), const),
                  pl.BlockSpec((1, LANES), const),
                  pl.BlockSpec((3, LANES), const),
                  pl.BlockSpec((tm, LANES), lambda i, j: (j, 0)),
                  pl.BlockSpec((tm, LANES), lambda i, j: (j, 0))],
        out_specs=(pl.BlockSpec((1, tm, nq), row), pl.BlockSpec((1, tm, 2 * LANES), row))
        + (pl.BlockSpec((1, tm, LANES), row),) * 5,
        compiler_params=_cparams(2),
        name="nsa_in_proj",
    )(x, gain.reshape(1, d), sc, sh, wq, wkv, qg, kg, cos, sin)


def _compress_kernel(a_ref, pe_ref, w1_ref, b1_ref, w2_ref, b2_ref, kg_ref, cos_ref, sin_ref, o_ref, *, n_str):
    is_key = pl.program_id(0) == 0
    hid2 = w1_ref.shape[-1]
    first = jnp.zeros((n_str, hid2), F32)
    second = jnp.zeros((n_str, hid2), F32)
    pe_term = jnp.zeros((8, hid2), F32)
    for l in range(CMP_STRIDE):
        rows = a_ref[0, pl.ds(l, n_str, stride=CMP_STRIDE), :].astype(BF16)
        first = first + _dot(rows, w1_ref[0, l])
        second = second + _dot(rows, w1_ref[0, CMP_STRIDE + l])
    for l in range(CMP_BLOCK):
        pe_term = pe_term + _dot(pe_ref[0, l].astype(BF16), w1_ref[0, l])
    hid = first + pltpu.roll(second, n_str - 1, 0) + pe_term[0:1] + b1_ref[0]
    hid = 0.5 * hid * (1.0 + jnp.tanh(np.sqrt(2.0 / np.pi) * (hid + 0.044715 * hid * hid * hid)))
    out = _dot(hid.astype(BF16), w2_ref[0]) + b2_ref[0]
    normed = _half_norm_rope(out, kg_ref[...], cos_ref[...], sin_ref[...])
    o_ref[0, 0] = jnp.where(is_key, normed, out).astype(o_ref.dtype)


def _block_diag2(w):
    z = jnp.zeros_like(w)
    return jnp.concatenate([jnp.concatenate([w, z], axis=-1), jnp.concatenate([z, w], axis=-1)], axis=-2)


def _compress_call(cv, pe, w1, b1, w2, b2, k_gain0, cos, sin):
    b, s, _ = cv.shape
    n_str = s // CMP_STRIDE
    hid = w1.shape[-1]
    w1bd = _block_diag2(w1.reshape(2, CMP_BLOCK, HEAD_DIM, hid)).astype(BF16)
    w2bd = _block_diag2(w2).astype(BF16)
    pe2 = jnp.broadcast_to(jnp.tile(pe, (1, 1, 2))[:, :, None, :], (2, CMP_BLOCK, 8, LANES))
    b1t = jnp.tile(b1, (1, 2)).reshape(2, 1, 2 * hid)
    b2t = jnp.tile(b2, (1, 2)).reshape(2, 1, LANES)
    kg = jnp.tile(k_gain0, 2).reshape(1, LANES)
    sel3 = lambda i, j: (i, 0, 0)
    sel4 = lambda i, j: (i, 0, 0, 0)
    const = lambda i, j: (0, 0)
    return pl.pallas_call(
        functools.partial(_compress_kernel, n_str=n_str),
        out_shape=jax.ShapeDtypeStruct((2, b, n_str, LANES), BF16),
        grid=(2, b),
        in_specs=[pl.BlockSpec((1, s, LANES), lambda i, j: (j, 0, i)),
                  pl.BlockSpec((1, CMP_BLOCK, 8, LANES), sel4),
                  pl.BlockSpec((1, CMP_BLOCK, LANES, 2 * hid), sel4),
                  pl.BlockSpec((1, 1, 2 * hid), sel3),
                  pl.BlockSpec((1, 2 * hid, LANES), sel3),
                  pl.BlockSpec((1, 1, LANES), sel3),
                  pl.BlockSpec((1, LANES), const),
                  pl.BlockSpec((n_str, LANES), const),
                  pl.BlockSpec((n_str, LANES), const)],
        out_specs=pl.BlockSpec((1, 1, n_str, LANES), lambda i, j: (i, j, 0, 0)),
        compiler_params=_cparams(2),
        name="nsa_compress",
    )(cv, pe2, w1bd, b1t, w2bd, b2t, kg, cos, sin)


def _attend(qb, k_ref, v_ref, k0, spans, bias):
    rows = qb.shape[0]
    q_rows = bias.shape[0] // NSA_KV_GROUPS
    per_group = rows // (NSA_KV_GROUPS * q_rows)
    bias = bias.reshape(NSA_KV_GROUPS, 1, q_rows, bias.shape[-1])
    m = acc = None
    for off, size in spans:
        k = k_ref[0, pl.ds(k0 + off, size), :]
        v = v_ref[0, pl.ds(k0 + off, size), :]
        s = _dot_nt(qb, k).reshape(NSA_KV_GROUPS, per_group, q_rows, size) + bias[..., off:off + size]
        s = s.reshape(rows, size)
        m_span = jnp.max(s, axis=-1, keepdims=True)
        m_new = m_span if m is None else jnp.maximum(m, m_span)
        p = jnp.exp2(s - m_new).astype(BF16)
        pv = _dot(p, jnp.concatenate([v, jnp.ones_like(v)], axis=1))
        acc = pv if m is None else acc * jnp.exp2(m - m_new) + pv
        m = m_new
    return acc


def _nsa_attn_kernel(*refs, n_inputs, **static):
    for sub in range(NSA_CHAINS_PER_STEP):
        _nsa_attn_block(sub, pl.program_id(1) * NSA_CHAINS_PER_STEP + sub, *refs[:n_inputs], refs[-1], **static)


def _nsa_attn_block(sub, step, q_ref, g_ref, kc_ref, vc_ref, ks_ref, vs_ref, kw_ref, vw_ref,
                    c2s_ref, exp_ref, o_ref, *, seq, n_cmp, n_top, win_keys, sel_span, n_keys, q_block0, q_rows):
    G, R, QB = NSA_KV_GROUPS, NSA_HEADS_PER_GROUP, q_rows
    rows = R * QB
    n_sel = seq // SEL_BLOCK
    qi0 = step * (q_rows // NSA_Q_BLOCK) + q_block0
    s0 = qi0 * NSA_Q_BLOCK
    tok = slice(sub * QB, (sub + 1) * QB)
    gt = jax.nn.sigmoid(g_ref[0, tok, :])
    lane = lax.broadcasted_iota(jnp.int32, (QB, LANES), 1)
    tq = s0 + lax.broadcasted_iota(jnp.int32, (rows, 1), 0) % QB
    tq1 = s0 + lax.broadcasted_iota(jnp.int32, (QB, 1), 0)
    ones_sq = jnp.ones((LANES, LANES), BF16)

    w0 = pl.multiple_of(jnp.maximum(s0 + QB - win_keys, 0), SEL_BLOCK)
    wpos = w0 + lax.broadcasted_iota(jnp.int32, (1, win_keys), 1)
    bias_w = jnp.where((wpos <= tq1) & (wpos > tq1 - WINDOW), 0.0, NEG)
    win_spans = [(off, min(3 * LANES, win_keys - off)) for off in range(0, win_keys, 3 * LANES)]

    zero = jnp.zeros((QB, LANES), BF16)
    qb = jnp.concatenate([jnp.where((lane // HEAD_DIM) == g, q_ref[0, tok, r * LANES:(r + 1) * LANES], zero)
                          for g in range(G) for r in range(R)], axis=0)
    tq_all = jnp.concatenate([tq] * G, axis=0)
    tq_tok = jnp.concatenate([tq1] * G, axis=0)

    sc = _dot_nt(qb, kc_ref[0])
    cpos = lax.broadcasted_iota(jnp.int32, (1, LANES), 1) * CMP_STRIDE + (CMP_BLOCK - 1)
    valid_c = (cpos <= tq_all) & (lax.broadcasted_iota(jnp.int32, (1, LANES), 1) < n_cmp)
    sc = jnp.where(valid_c, sc, NEG)
    e_c = jnp.exp2(sc - jnp.max(sc, axis=-1, keepdims=True)).astype(BF16)
    p_c = jnp.where(valid_c, e_c.astype(F32) / _dot(e_c, ones_sq), 0.0)
    o_c = _dot(p_c.astype(BF16), vc_ref[0])

    acc_w = _attend(qb, kw_ref, vw_ref, w0, win_spans, jnp.concatenate([bias_w] * G, axis=0))

    psum = jnp.sum(p_c.reshape(G, R, QB, LANES), axis=1).reshape(G * QB, LANES)
    imp = lax.dot_general(c2s_ref[...], psum, (((1,), (1,)), ((), ())),
                          preferred_element_type=F32, precision=HIGHEST)
    blk = lax.broadcasted_iota(jnp.int32, (n_sel, G * QB), 0)
    qi = qi0 + (lax.broadcasted_iota(jnp.int32, (1, G * QB), 1) % QB) // NSA_Q_BLOCK
    forced = (blk == 0) | (blk == qi) | (blk == qi - 1)
    imp = jnp.where(blk <= qi, jnp.where(forced, BIG, imp), -BIG)
    beaten = jnp.zeros(imp.shape, F32)
    for k in range(1, n_sel):
        other = pltpu.roll(imp, k, 0)
        beats = (other > imp) | ((blk >= k) & (other == imp))
        beaten = beaten + jnp.where(beats, 1.0, 0.0)
    chosen = jnp.where(beaten < n_top, 1.0, 0.0).astype(BF16)
    picked = _dot_tn(chosen, exp_ref[:, :n_keys])
    kpos = lax.broadcasted_iota(jnp.int32, (1, n_keys), 1)
    sel_spans = [(off, min(sel_span, n_keys - off)) for off in range(0, n_keys, sel_span)]

    bias_s = jnp.where((picked > 0.5) & (kpos <= tq_tok), 0.0, NEG)
    acc_s = _attend(qb, ks_ref, vs_ref, 0, sel_spans, bias_s)
    heads = []
    for h in range(G * R):
        rs = slice(h * QB, (h + 1) * QB)
        a_s, a_w = acc_s[rs], acc_w[rs]
        g_s = gt[:, NSA_HEADS + h:NSA_HEADS + h + 1] / a_s[:, LANES:LANES + 1]
        g_w = gt[:, 2 * NSA_HEADS + h:2 * NSA_HEADS + h + 1] / a_w[:, LANES:LANES + 1]
        heads.append(gt[:, h:h + 1] * o_c[rs] + g_s * a_s[:, :LANES] + g_w * a_w[:, :LANES])
    for r in range(R):
        slab = jnp.where(lane < HEAD_DIM, heads[r], heads[R + r])
        o_ref[0, tok, r * LANES:(r + 1) * LANES] = slab.astype(o_ref.dtype)


def _nsa_attn_call(q, gates, kc, vc, ks, vs, kw, vw, cmp_to_sel, n_cmp):
    b, s, nq = q.shape
    qb = NSA_Q_BLOCK
    n_top = min(SEL_TOPN, s // SEL_BLOCK)
    call_rows = min(NSA_CALL_KEYS, s)
    qrows = min(NSA_BLOCKS_PER_STEP * qb, call_rows)
    win_keys = min(-(-(WINDOW + qrows) // LANES) * LANES, s)
    sel_span = min(512, s)
    n_sel = s // SEL_BLOCK
    expand = jnp.asarray(np.arange(n_sel)[:, None] == (np.arange(s)[None, :] // SEL_BLOCK), BF16)
    per_b = lambda i, j: (i, 0, 0)
    const = lambda i, j: (0, 0)
    call_keys = call_rows
    step_rows = min(NSA_CHAINS_PER_STEP * qrows, call_keys)
    steps = call_keys // step_rows
    out = None
    for n in range(1, s // call_keys + 1):
        q0 = (n - 1) * (call_keys // qb)
        n_keys = n * call_keys
        row = lambda i, j, t0=(n - 1) * steps: (i, j + t0, 0)
        kern = functools.partial(_nsa_attn_kernel, n_inputs=10, seq=s, n_cmp=n_cmp, n_top=n_top,
                                 win_keys=win_keys, sel_span=sel_span, n_keys=n_keys, q_block0=q0, q_rows=qrows)
        carried = [] if out is None else [out]
        out = pl.pallas_call(
            kern,
            out_shape=jax.ShapeDtypeStruct((b, s, nq), BF16),
            grid=(b, steps),
            in_specs=[pl.BlockSpec((1, step_rows, nq), row),
                      pl.BlockSpec((1, step_rows, LANES), row),
                      pl.BlockSpec((1, LANES, LANES), per_b),
                      pl.BlockSpec((1, LANES, LANES), per_b),
                      pl.BlockSpec((1, n_keys, LANES), per_b),
                      pl.BlockSpec((1, n_keys, LANES), per_b),
                      pl.BlockSpec((1, s, LANES), per_b),
                      pl.BlockSpec((1, s, LANES), per_b),
                      pl.BlockSpec((n_sel, LANES), const),
                      pl.BlockSpec((n_sel, s), const)] + [pl.BlockSpec(memory_space=pl.ANY)] * len(carried),
            out_specs=pl.BlockSpec((1, step_rows, nq), row),
            input_output_aliases={10: 0} if carried else {},
            compiler_params=_cparams(2),
            name="nsa_attention",
        )(q, gates, kc, vc, ks, vs, kw, vw, cmp_to_sel, expand, *carried)
    return out


def _mixer_out_kernel(*refs, gated):
    a_ref, refs = refs[0], refs[1:]
    lhs = a_ref[0]
    if gated:
        og_ref, refs = refs[0], refs[1:]
        lhs = (jax.nn.sigmoid(og_ref[0].astype(F32)) * lhs.astype(F32)).astype(BF16)
    (w_ref, x_ref, g_ref, gain_ref, sc_ref, sh_ref, wh_ref, wl_ref, rb_ref,
     xo_ref, h_ref, route_ref, route_t_ref, cnt_ref) = refs
    x_new = x_ref[0] + g_ref[0] * _dot(lhs, w_ref[...])
    xo_ref[0] = x_new
    h_ref[0], route_ref[0], route_t_ref[...], cnt_ref[0] = _route_tile(
        x_new, gain_ref[...], sc_ref[0], sh_ref[0], wh_ref[...], wl_ref[...], rb_ref[...])


def _mixer_out_call(a, og, w_out, x, g, ffn, tm):
    gain, sc, sh, router_w, router_b = ffn
    b, s, d = x.shape
    k = a.shape[-1]
    nt = s // tm
    wp = jnp.pad(router_w, ((0, 0), (0, LANES - N_EXPERTS)))
    wh = wp.astype(BF16)
    wl = jnp.concatenate([wh, (wp - wh.astype(F32)).astype(BF16)], axis=1)
    rb = router_b.astype(F32).reshape(N_EXPERTS, 1)
    row = lambda i, j: (i, j, 0)
    per_b = lambda i, j: (i, 0, 0)
    const = lambda i, j: (0, 0)
    a_spec = pl.BlockSpec((1, tm, k), row)
    acts = [a] if og is None else [a, og]
    return pl.pallas_call(
        functools.partial(_mixer_out_kernel, gated=og is not None),
        out_shape=(jax.ShapeDtypeStruct((b, s, d), F32), jax.ShapeDtypeStruct((b, s, d), BF16),
                   jax.ShapeDtypeStruct((b, s, LANES), F32), jax.ShapeDtypeStruct((8, b * s), F32),
                   jax.ShapeDtypeStruct((b * nt, N_EXPERTS, LANES), F32)),
        grid=(b, nt),
        in_specs=[a_spec] * len(acts) + [pl.BlockSpec((k, d), const),
                                         pl.BlockSpec((1, tm, d), row),
                                         pl.BlockSpec((1, 1, d), per_b),
                                         pl.BlockSpec((1, d), const),
                                         pl.BlockSpec((1, 1, d), per_b),
                                         pl.BlockSpec((1, 1, d), per_b),
                                         pl.BlockSpec((d, LANES), const),
                                         pl.BlockSpec((d, 2 * LANES), const),
                                         pl.BlockSpec((N_EXPERTS, 1), const)],
        out_specs=(pl.BlockSpec((1, tm, d), row), pl.BlockSpec((1, tm, d), row), pl.BlockSpec((1, tm, LANES), row),
                   pl.BlockSpec((8, tm), lambda i, j: (0, i * nt + j)),
                   pl.BlockSpec((1, N_EXPERTS, LANES), lambda i, j: (i * nt + j, 0, 0))),
        compiler_params=_cparams(2),
        name="mixer_out_router",
    )(*acts, w_out.astype(BF16), x, g, gain.reshape(1, d), sc, sh, wh, wl, rb)


def _mlstm_in_kernel(x_ref, gain_ref, sc_ref, sh_ref, wqg_ref, wk_ref, wv_ref, wo_ref,
                     q_ref, k_ref, v_ref, o_ref, g_ref):
    h = _norm_mod(x_ref[0], gain_ref[...], sc_ref[0], sh_ref[0]).astype(BF16)
    nq = q_ref.shape[-1]
    qg = _dot(h, wqg_ref[...])
    q_ref[0] = qg[:, :nq].astype(BF16)
    g_ref[0] = qg[:, nq:]
    k_ref[0] = _dot(h, wk_ref[...]).astype(BF16)
    v_ref[0] = _dot(h, wv_ref[...]).astype(BF16)
    o_ref[0] = _dot(h, wo_ref[...]).astype(BF16)


def _mlstm_in_call(x, gain, sc, sh, w_in, dqk, dv, tm):
    b, s, d = x.shape
    nh = MLSTM_HEADS
    sizes = [nh * dqk, nh * dqk, nh * dv, nh * dv]
    offs = np.cumsum([0] + sizes)
    ws = [w_in[:, offs[i]:offs[i + 1]].astype(BF16) for i in range(4)]
    wg = jnp.pad(w_in[:, offs[4]:], ((0, 0), (0, LANES - 2 * nh))).astype(BF16)
    ws[0] = jnp.concatenate([ws[0], wg], axis=1)
    row = lambda i, j: (i, j, 0)
    per_b = lambda i, j: (i, 0, 0)
    const = lambda i, j: (0, 0)
    widths = sizes + [LANES]
    w_widths = [sizes[0] + LANES] + sizes[1:]
    return pl.pallas_call(
        _mlstm_in_kernel,
        out_shape=tuple(jax.ShapeDtypeStruct((b, s, n), BF16) for n in sizes)
        + (jax.ShapeDtypeStruct((b, s, LANES), F32),),
        grid=(b, s // tm),
        in_specs=[pl.BlockSpec((1, tm, d), row),
                  pl.BlockSpec((1, d), const),
                  pl.BlockSpec((1, 1, d), per_b),
                  pl.BlockSpec((1, 1, d), per_b)] + [pl.BlockSpec((d, n), const) for n in w_widths],
        out_specs=tuple(pl.BlockSpec((1, tm, n), row) for n in widths),
        compiler_params=_cparams(2),
        name="mlstm_in_proj",
    )(x, gain.reshape(1, d), sc, sh, *ws)


def _softcap(a):
    return GATE_SOFTCAP * jnp.tanh(a / GATE_SOFTCAP)


def _mlstm_kernel(bi_ref, bf_ref, q_ref, k_ref, v_ref, g_ref, gain_ref, o_ref,
                  li_s, b_s, *, n_chunks, n_heads, dqk, dv):
    L = MLSTM_CHUNK
    r_i = lax.broadcasted_iota(jnp.int32, (L, L), 0)
    c_i = lax.broadcasted_iota(jnp.int32, (L, L), 1)
    upper = jnp.where(r_i <= c_i, 1.0, 0.0)
    for h in range(n_heads):
        li_s[h] = _softcap(g_ref[0, h] + bi_ref[h])
        fa = _softcap(g_ref[0, n_heads + h] + bf_ref[h])
        lf = jnp.minimum(fa, 0.0) - jnp.log1p(jnp.exp(-jnp.abs(fa)))
        b_s[h] = jnp.dot(lf, upper, preferred_element_type=F32, precision=HIGHEST)
    eye = r_i == c_i
    causal = r_i >= c_i
    k_scale = dqk ** -0.5

    def to_col(row):
        return jnp.sum(jnp.where(eye, jnp.broadcast_to(row, (L, L)), 0.0), axis=1, keepdims=True)

    def local_part(h, r0, c):
        qcb = q_ref[0, pl.ds(r0, L), h * dqk:(h + 1) * dqk]
        kc = k_ref[0, pl.ds(r0, L), h * dqk:(h + 1) * dqk].astype(F32) * k_scale
        vc = v_ref[0, pl.ds(r0, L), h * dv:(h + 1) * dv]
        b_row = b_s[h, pl.ds(c, 1), :]
        li_row = li_s[h, pl.ds(c, 1), :]
        b_col, li_col = to_col(b_row), to_col(li_row)
        b_last = b_row[:, L - 1:L]
        dmat = jnp.where(causal, b_col - b_row + li_row, NEG)
        m_loc = jnp.max(dmat, axis=-1, keepdims=True)
        a_loc = jnp.exp(dmat - m_loc) * _dot_nt(qcb, kc.astype(BF16))
        num_loc = _dot(a_loc.astype(BF16), vc)
        den_loc = jnp.sum(a_loc, axis=-1, keepdims=True)
        g_max = m_loc[L - 1:L, :]
        kw = kc * jnp.exp(b_last - b_col + li_col - g_max)
        kv = _dot_tn(kw.astype(BF16), vc)
        kn = jnp.sum(kw, axis=0, keepdims=True)
        return qcb, b_col, b_last, m_loc, num_loc, den_loc, g_max, kv, kn

    def body(grp, carry):
        r0 = pl.multiple_of(grp * (MLSTM_GROUP * L), MLSTM_GROUP * L)
        parts = [[local_part(h, r0 + j * L, grp * MLSTM_GROUP + j) for j in range(MLSTM_GROUP)]
                 for h in range(n_heads)]
        new_carry, outs = [], []
        for h in range(n_heads):
            state, n_row, m_prev = carry[h]
            gain = gain_ref[h]
            head_out = []
            for qcb, b_col, b_last, m_loc, num_loc, den_loc, g_max, kv, kn in parts[h]:
                m_inter = b_col + m_prev
                m_t = jnp.maximum(m_inter, m_loc)
                intra = jnp.exp(m_loc - m_t)
                inter = jnp.exp(m_inter - m_t)
                num = intra * num_loc + inter * _dot(qcb, state.astype(BF16))
                den = intra * den_loc + inter * jnp.sum(qcb.astype(F32) * n_row, axis=-1, keepdims=True)
                h_out = num / jnp.maximum(jnp.abs(den), jnp.exp(-m_t))
                hs = h_out * lax.rsqrt(jnp.mean(h_out * h_out, axis=-1, keepdims=True) + EPS) * gain
                head_out.append(hs.astype(o_ref.dtype))
                m_new = jnp.maximum(b_last + m_prev, g_max)
                decay = jnp.exp(b_last + m_prev - m_new)
                grow = jnp.exp(g_max - m_new)
                state, n_row, m_prev = decay * state + grow * kv, decay * n_row + grow * kn, m_new
            new_carry.append((state, n_row, m_prev))
            outs.append(jnp.concatenate(head_out, axis=0))
        for h in range(n_heads):
            o_ref[0, pl.ds(r0, MLSTM_GROUP * L), h * dv:(h + 1) * dv] = outs[h]
        return tuple(new_carry)

    init = tuple((jnp.zeros((dqk, dv), F32), jnp.zeros((1, dqk), F32), jnp.zeros((1, 1), F32))
                 for _ in range(n_heads))
    lax.fori_loop(0, n_chunks // MLSTM_GROUP, body, init)


def _mlstm_call(q, k, v, gates, b_igate, b_fgate, norm_gain):
    b, s, _ = q.shape
    nh = MLSTM_HEADS
    dqk, dv = q.shape[-1] // nh, v.shape[-1] // nh
    L = MLSTM_CHUNK
    nch = s // L
    g = jnp.transpose(gates[..., :2 * nh], (0, 2, 1)).reshape(b, 2 * nh, nch, L)
    smem = pl.BlockSpec(memory_space=pltpu.SMEM)
    per_b = lambda n: pl.BlockSpec((1, s, n), lambda i: (i, 0, 0))
    kern = functools.partial(_mlstm_kernel, n_chunks=nch, n_heads=nh, dqk=dqk, dv=dv)
    return pl.pallas_call(
        kern,
        out_shape=jax.ShapeDtypeStruct((b, s, nh * dv), BF16),
        grid=(b,),
        in_specs=[smem, smem, per_b(nh * dqk), per_b(nh * dqk), per_b(nh * dv),
                  pl.BlockSpec((1, 2 * nh, nch, L), lambda i: (i, 0, 0, 0)),
                  pl.BlockSpec((nh, 1, dv), lambda i: (0, 0, 0))],
        out_specs=per_b(nh * dv),
        scratch_shapes=[pltpu.VMEM((nh, nch, L), F32), pltpu.VMEM((nh, nch, L), F32)],
        compiler_params=_cparams(1),
        name="mlstm_chunk_scan",
    )(b_igate, b_fgate, q, k, v, g, norm_gain.reshape(nh, 1, dv))


def _route_tile(x, gain, sc, sh, w_hi, w_hilo, router_bias):
    h = _norm_mod(x, gain, sc, sh)
    hi = h.astype(BF16)
    lo = (h - hi.astype(F32)).astype(BF16)
    hi_both = _dot(hi, w_hilo)
    logits = hi_both[:, :LANES] + (_dot(lo, w_hi) + hi_both[:, LANES:])
    tm = logits.shape[0]
    aff = jax.nn.sigmoid(logits.T[:N_EXPERTS])
    choice = (aff + router_bias).reshape(N_GROUPS, EXPERTS_PER_GROUP, tm)
    local = lax.broadcasted_iota(jnp.int32, choice.shape, 1)

    def first_max(v):
        m = jnp.max(v, axis=1, keepdims=True)
        return m, jnp.min(jnp.where(v == m, local, EXPERTS_PER_GROUP), axis=1, keepdims=True)

    m1, i1 = first_max(choice)
    m2, i2 = first_max(jnp.where(local == i1, -jnp.inf, choice))
    score = m1 + m2
    best, e0, e1 = score[0], i1[0], i2[0]
    for g in range(1, N_GROUPS):
        better = score[g] > best
        best = jnp.where(better, score[g], best)
        e0 = jnp.where(better, i1[g] + g * EXPERTS_PER_GROUP, e0)
        e1 = jnp.where(better, i2[g] + g * EXPERTS_PER_GROUP, e1)
    expert = lax.broadcasted_iota(jnp.int32, (N_EXPERTS, tm), 0)
    is0, is1 = expert == e0, expert == e1
    a0 = jnp.sum(jnp.where(is0, aff, 0.0), axis=0, keepdims=True)
    a1 = jnp.sum(jnp.where(is1, aff, 0.0), axis=0, keepdims=True)
    tot = a0 + a1
    onehot = jnp.where(is0 | is1, 1.0, 0.0).astype(BF16)
    r_i = lax.broadcasted_iota(jnp.int32, (tm, tm), 0)
    c_i = lax.broadcasted_iota(jnp.int32, (tm, tm), 1)
    running = _dot(onehot, jnp.where(r_i <= c_i, 1.0, 0.0).astype(BF16))
    r0 = jnp.sum(jnp.where(is0, running, 0.0), axis=0, keepdims=True) - 1.0
    r1 = jnp.sum(jnp.where(is1, running, 0.0), axis=0, keepdims=True) - 1.0
    rows = [e0.astype(F32), e1.astype(F32), r0, r1, a0 / tot, a1 / tot]
    packed = jnp.concatenate(rows + [jnp.zeros((LANES - len(rows), tm), F32)], axis=0)
    return hi, packed.T, packed[:8], jnp.broadcast_to(running[:, tm - 1:tm], (N_EXPERTS, LANES))


def _expert_kernel(be_ref, live_ref, x_ref, wg_ref, wu_ref, wd_ref, o_ref, wg_s, wu_s, wd_s):
    i = pl.program_id(0)
    fresh = (i == 0) | (be_ref[i] != be_ref[jnp.maximum(i - 1, 0)])

    @pl.when(fresh)
    def _():
        wg_s[...] = wg_ref[0, 0].astype(BF16)
        wu_s[...] = wu_ref[0, 0].astype(BF16)
        wd_s[...] = wd_ref[0, 0].astype(BF16)

    @pl.when(live_ref[i] == 1)
    def _():
        xb = x_ref[...]
        gate = _dot(xb, wg_s[...])
        hid = gate * jax.nn.sigmoid(gate) * _dot(xb, wu_s[...])
        o_ref[...] = _dot(hid.astype(BF16), wd_s[...]).astype(o_ref.dtype)

    @pl.when(live_ref[i] == 0)
    def _():
        o_ref[...] = jnp.zeros_like(o_ref)


def _expert_call(block_expert, block_live, xs, w_gate, w_up, w_down, layer):
    p, d = xs.shape
    de = w_gate.shape[-1]
    nb = p // MOE_ROWS
    grid_spec = pltpu.PrefetchScalarGridSpec(
        num_scalar_prefetch=2,
        grid=(nb,),
        in_specs=[pl.BlockSpec((MOE_ROWS, d), lambda i, be, lv: (i, 0)),
                  pl.BlockSpec((1, 1, d, de), lambda i, be, lv: (layer, be[i], 0, 0)),
                  pl.BlockSpec((1, 1, d, de), lambda i, be, lv: (layer, be[i], 0, 0)),
                  pl.BlockSpec((1, 1, de, d), lambda i, be, lv: (layer, be[i], 0, 0))],
        out_specs=pl.BlockSpec((MOE_ROWS, d), lambda i, be, lv: (i, 0)),
        scratch_shapes=[pltpu.VMEM((d, de), BF16), pltpu.VMEM((d, de), BF16), pltpu.VMEM((de, d), BF16)],
    )
    return pl.pallas_call(
        _expert_kernel,
        out_shape=jax.ShapeDtypeStruct((p, d), BF16),
        grid_spec=grid_spec,
        compiler_params=_cparams(1),
        name="moe_experts",
    )(block_expert, block_live, xs, w_gate, w_up, w_down)


def _combine_kernel(x_ref, g_ref, route_ref, ya_ref, yb_ref, o_ref):
    w = route_ref[0][:, 2 * TOP_K:3 * TOP_K]
    y = w[:, 0:1] * ya_ref[0].astype(F32) + w[:, 1:2] * yb_ref[0].astype(F32)
    o_ref[0] = x_ref[0] + g_ref[0] * y


def _combine_call(x, g, route, ya, yb, tm):
    b, s, d = x.shape
    row = lambda i, j: (i, j, 0)
    spec = pl.BlockSpec((1, tm, d), row)
    return pl.pallas_call(
        _combine_kernel,
        out_shape=jax.ShapeDtypeStruct((b, s, d), F32),
        grid=(b, s // tm),
        in_specs=[spec, pl.BlockSpec((1, 1, d), lambda i, j: (i, 0, 0)),
                  pl.BlockSpec((1, tm, LANES), row), spec, spec],
        out_specs=spec,
        compiler_params=_cparams(2),
        name="moe_combine",
    )(x, g, route, ya, yb)


def _row_layout(route_t, cnt, tm):
    t = route_t.shape[1]
    nt = t // tm
    tile_cnt = cnt[:, :, 0].astype(jnp.int32)
    tile_off = jnp.cumsum(tile_cnt, axis=0) - tile_cnt
    counts = jnp.sum(tile_cnt, axis=0)
    padded = (counts + MOE_ROWS - 1) // MOE_ROWS * MOE_ROWS
    p_ends = jnp.cumsum(padded)
    base = (p_ends - padded)[None, :] + tile_off
    base_tok = jnp.broadcast_to(base.T[:, :, None], (N_EXPERTS, nt, tm)).reshape(N_EXPERTS, t)
    ids = route_t[:2 * TOP_K].astype(jnp.int32)
    experts = jnp.arange(N_EXPERTS, dtype=jnp.int32)[:, None]
    dest = [jnp.sum(jnp.where(ids[k][None, :] == experts, base_tok, 0), axis=0) + ids[TOP_K + k]
            for k in range(TOP_K)]
    nb = (t * TOP_K) // MOE_ROWS + N_EXPERTS
    tok = jnp.arange(t, dtype=jnp.int32)
    buf_tok = (jnp.arange(nb * MOE_ROWS, dtype=jnp.int32) % t).at[jnp.concatenate(dest)].set(
        jnp.concatenate([tok] * TOP_K), mode="promise_in_bounds", unique_indices=True)
    block_start = jnp.arange(nb, dtype=jnp.int32) * MOE_ROWS
    block_expert = jnp.minimum(jnp.sum((p_ends[None, :] <= block_start[:, None]).astype(jnp.int32), axis=-1),
                               N_EXPERTS - 1)
    block_live = (block_start < p_ends[-1]).astype(jnp.int32)
    return dest, buf_tok, block_expert, block_live


def _moe_layer(x, hf, route, route_t, cnt, g, w_gate, w_up, w_down, layer, tm):
    b, s, d = x.shape
    t = b * s
    dest, buf_tok, block_expert, block_live = _row_layout(route_t, cnt, tm)
    take = lambda a, idx: a.at[idx].get(mode="promise_in_bounds")
    xs = take(hf.reshape(t, d), buf_tok)
    out = _expert_call(block_expert, block_live, xs, w_gate, w_up, w_down, layer)
    ya = take(out, dest[0]).reshape(b, s, d)
    yb = take(out, dest[1]).reshape(b, s, d)
    return _combine_call(x, g, route, ya, yb, tm)


def _nsa_layer(x, gain, sc, sh, g, w_in, w_out, q_gain, k_gain, cmp_pe, cmp_w1, cmp_b1, cmp_w2, cmp_b2, ffn, tm):
    b, s, d = x.shape
    G, dh = NSA_KV_GROUPS, HEAD_DIM
    cos, sin = _rope_tables(jnp.arange(s, dtype=jnp.int32))
    q, cv, ks, vs, kw, vw, gates = _nsa_in_call(x, gain, sc, sh, w_in, q_gain, k_gain, cos, sin, tm)

    n_cmp = (s - CMP_BLOCK) // CMP_STRIDE + 1
    n_str = s // CMP_STRIDE
    cmp_pos = jnp.arange(n_str, dtype=jnp.int32) * CMP_STRIDE + (CMP_BLOCK - 1)
    ccos, csin = _rope_tables(cmp_pos)
    cmp = _compress_call(cv, cmp_pe, cmp_w1, cmp_b1, cmp_w2, cmp_b2, k_gain[0], ccos, csin)
    cmp = jnp.pad(cmp, ((0, 0), (0, 0), (0, LANES - n_str), (0, 0)))
    kc, vc = cmp[0], cmp[1]

    ns = s // SEL_BLOCK
    r_, u_ = SEL_BLOCK // CMP_STRIDE, CMP_BLOCK // CMP_STRIDE
    c_idx = (r_ * np.arange(ns)[:, None, None] + np.arange(r_)[None, :, None]
             + np.arange(u_)[None, None, :]).reshape(ns, -1)
    c2s = (c_idx[:, :, None] == np.arange(n_cmp)[None, None, :]).sum(1).astype(np.float32)
    c2s = jnp.asarray(np.pad(c2s, ((0, 0), (0, LANES - n_cmp))))

    o = _nsa_attn_call(q, gates, kc, vc, ks, vs, kw, vw, c2s, n_cmp)
    return _mixer_out_call(o, None, w_out[_head_pair_order(), :], x, g, ffn, tm)


def _mlstm_layer(x, gain, sc, sh, g, w_in, w_out, b_igate, b_fgate, norm_gain, ffn, tm):
    nh = MLSTM_HEADS
    dv = norm_gain.shape[-1]
    dqk = (w_in.shape[-1] - 2 * nh - 2 * nh * dv) // (2 * nh)
    q, k, v, og, gates = _mlstm_in_call(x, gain, sc, sh, w_in, dqk, dv, tm)
    hs = _mlstm_call(q, k, v, gates, b_igate, b_fgate, norm_gain)
    return _mixer_out_call(hs, og, w_out, x, g, ffn, tm)


def kernel(x, c, ada_w, ada_b, norm_mix_gain, norm_ffn_gain, nsa_w_in, nsa_w_out, nsa_q_gain, nsa_k_gain, nsa_cmp_pe, nsa_cmp_w1, nsa_cmp_b1, nsa_cmp_w2, nsa_cmp_b2, mlstm_w_in, mlstm_b_igate, mlstm_b_fgate, mlstm_norm_gain, mlstm_w_out, router_w, router_b, moe_w_gate, moe_w_up, moe_w_down):
    b, s, d = x.shape
    depth = ada_w.shape[0]
    tm = min(1024, s)
    mod = _mod_call(c, ada_w, ada_b)
    for i in range(depth):
        sh_m, sc_m, g_m, sh_f, sc_f, g_f = [mod[i, :, None, k * d:(k + 1) * d] for k in range(6)]
        j = i // 2
        ffn = (norm_ffn_gain[i], sc_f, sh_f, router_w, router_b)
        if i % 2 == 0:
            mixed = _nsa_layer(x, norm_mix_gain[i], sc_m, sh_m, g_m, nsa_w_in[j], nsa_w_out[j], nsa_q_gain[j],
                               nsa_k_gain[j], nsa_cmp_pe[j], nsa_cmp_w1[j], nsa_cmp_b1[j], nsa_cmp_w2[j],
                               nsa_cmp_b2[j], ffn, tm)
        else:
            mixed = _mlstm_layer(x, norm_mix_gain[i], sc_m, sh_m, g_m, mlstm_w_in[j], mlstm_w_out[j],
                                 mlstm_b_igate[j], mlstm_b_fgate[j], mlstm_norm_gain[j], ffn, tm)
        x = _moe_layer(*mixed, g_f, moe_w_gate, moe_w_up, moe_w_down, i, tm)
    return x
```
